```python
import math
import jax
import jax.numpy as jnp
from jax import lax
import numpy as np

D_MODEL = 2048
BATCH = 1
SEQ = 16384
DEPTH = 4

GRID_W = 64
CTX_LEN = 256
HEAD_DIM = 64
ROPE_THETA = 10000.0
NORM_EPS = 1e-6
NEG_INF = -1e30
Q_BLOCK = 128
N_MIX_GROUPS = 4
GROUP_WIDTH = D_MODEL // N_MIX_GROUPS
MIX_WIDTH = N_MIX_GROUPS * GROUP_WIDTH

NA_HEADS = GROUP_WIDTH // HEAD_DIM
NA_KH = 8
NA_KW = 16
WA_HEADS = GROUP_WIDTH // HEAD_DIM
WA_KV_HEADS = 2
WINDOW = 128
DIFF_HEADS = GROUP_WIDTH // (2 * HEAD_DIM)
DIFF_DIM = HEAD_DIM
DIFF_V_DIM = 2 * HEAD_DIM
MLA_HEADS = GROUP_WIDTH // HEAD_DIM
MLA_NOPE = 64
MLA_ROPE = 32
MLA_V = GROUP_WIDTH // MLA_HEADS
MLA_Q_LORA = 384
MLA_KV_LORA = 128

NA_COLS = 3 * NA_HEADS * HEAD_DIM
WA_COLS = (WA_HEADS + 2 * WA_KV_HEADS) * HEAD_DIM
DIFF_COLS = 3 * DIFF_HEADS * 2 * DIFF_DIM
MLA_COLS = MLA_Q_LORA + MLA_KV_LORA + MLA_ROPE
NA_OFF = 0
WA_OFF = NA_OFF + NA_COLS
DIFF_OFF = WA_OFF + WA_COLS
MLA_OFF = DIFF_OFF + DIFF_COLS
IN_COLS = MLA_OFF + MLA_COLS

N_EXPERTS = 16
N_EXPERT_GROUPS = 4
TOP_K = 2
D_EXPERT = 1024
ROUTED_SCALE = 1.0
MOE_BLOCK = 256

kernel_name = 'hybrid_grid_diffusion_block'


def rms_norm(x, g):
    xf = x.astype(jnp.float32)
    y = xf * lax.rsqrt(jnp.mean(xf * xf, axis=-1, keepdims=True) + NORM_EPS)
    return y.astype(x.dtype) * g.astype(x.dtype)


def split_heads(t, n_heads, dim):
    b, s, _ = t.shape
    return t.reshape(b, s, n_heads, dim).transpose(0, 2, 1, 3)


def merge_heads(t):
    b, h, s, d = t.shape
    return t.transpose(0, 2, 1, 3).reshape(b, s, h * d)


def softmax_f32(s):
    return jax.nn.softmax(s.astype(jnp.float32), axis=-1)


def axial_rope_tables(seq_len, rot_dim):
    n = rot_dim // 4
    t = jnp.arange(seq_len)
    row = (t // GRID_W).astype(jnp.float32)[:, None]
    col = (t % GRID_W).astype(jnp.float32)[:, None]
    inv = ROPE_THETA ** (-jnp.arange(n, dtype=jnp.float32) / n)
    return (jnp.cos(row * inv), jnp.sin(row * inv), jnp.cos(col * inv), jnp.sin(col * inv))


def apply_axial_rope(x, tabs):
    cr, sr, cc, sc = (t.astype(x.dtype) for t in tabs)
    x1, x2, x3, x4 = jnp.split(x, 4, axis=-1)
    return jnp.concatenate([x1 * cr - x2 * sr, x2 * cr + x1 * sr,
                            x3 * cc - x4 * sc, x4 * cc + x3 * sc], axis=-1)


def dense_attention(q, k, v, scale):
    s = jnp.einsum('bhqd,bhkd->bhqk', q, k).astype(jnp.float32) * scale
    return jnp.einsum('bhqk,bhkd->bhqd', softmax_f32(s).astype(v.dtype), v)


def dense_block_attention(q, k, v, scale):
    b, h, s, dh = q.shape
    nb = s // Q_BLOCK
    qb = jnp.moveaxis(q.reshape(b, h, nb, Q_BLOCK, dh), 2, 0)
    o = lax.map(lambda qblk: dense_attention(qblk, k, v, scale), qb)
    return jnp.moveaxis(o, 0, 2).reshape(b, h, s, v.shape[-1])


def neighbourhood_mixer(p, pc, q_gain, k_gain, rpb, with_ctx):
    h, dh, kw = NA_HEADS, HEAD_DIM, NA_KW
    w = h * dh
    b, s, _ = p.shape
    rows = s // GRID_W
    kh = min(NA_KH, rows)
    scale = dh ** -0.5
    q = rms_norm(split_heads(p[..., :w], h, dh), q_gain)
    k = rms_norm(split_heads(p[..., w:2 * w], h, dh), k_gain)
    v = split_heads(p[..., 2 * w:], h, dh)
    kc = rms_norm(split_heads(pc[..., w:2 * w], h, dh), k_gain)
    vc = split_heads(pc[..., 2 * w:], h, dh)
    kg = k.reshape(b, h, rows, GRID_W, dh)
    vg = v.reshape(b, h, rows, GRID_W, dh)
    cols = jnp.arange(GRID_W)
    c0 = jnp.clip(cols - kw // 2, 0, GRID_W - kw)
    col_idx = c0[:, None] + jnp.arange(kw)[None, :]
    col_rel = col_idx - cols[:, None] + (NA_KW - 1)
    r_all = jnp.arange(rows)
    r0 = jnp.clip(r_all - kh // 2, 0, rows - kh)
    q_rows = jnp.moveaxis(q.reshape(b, h, rows, GRID_W, dh), 2, 0)
    n_loc = kh * kw

    def row_block(args):
        qr, r, r0r = args
        kr = lax.dynamic_slice_in_dim(kg, r0r, kh, axis=2)
        vr = lax.dynamic_slice_in_dim(vg, r0r, kh, axis=2)
        kwin = kr[:, :, :, col_idx]
        vwin = vr[:, :, :, col_idx]
        row_rel = r0r + jnp.arange(kh) - r + (NA_KH - 1)
        bias = rpb[:, row_rel[:, None, None], col_rel[None, :, :]]
        bias = jnp.transpose(bias, (0, 2, 1, 3)).astype(jnp.float32)
        s_loc = jnp.einsum('bhqd,bhrqwd->bhqrw', qr, kwin).astype(jnp.float32) * scale + bias
        s_ctx = jnp.einsum('bhqd,bhkd->bhqk', qr, kc).astype(jnp.float32) * scale
        pr = softmax_f32(jnp.concatenate([s_loc.reshape(b, h, GRID_W, n_loc), s_ctx], axis=-1)).astype(v.dtype)
        p_loc = pr[..., :n_loc].reshape(b, h, GRID_W, kh, kw)
        return (jnp.einsum('bhqrw,bhrqwd->bhqd', p_loc, vwin)
                + jnp.einsum('bhqk,bhkd->bhqd', pr[..., n_loc:], vc))

    o = lax.map(row_block, (q_rows, r_all, r0))
    o = jnp.moveaxis(o, 0, 2).reshape(b, h, s, dh)
    oc = None
    if with_ctx:
        qc = rms_norm(split_heads(pc[..., :w], h, dh), q_gain)
        oc = merge_heads(dense_attention(qc, kc, vc, scale))
    return merge_heads(o), oc


def window_mixer(p, pc, q_gain, k_gain, sink, rope, with_ctx):
    hq, hk, dh = WA_HEADS, WA_KV_HEADS, HEAD_DIM
    grp = hq // hk
    nq, nk = hq * dh, hk * dh
    b, s, _ = p.shape
    n_ctx = pc.shape[1]
    scale = dh ** -0.5
    q = apply_axial_rope(rms_norm(split_heads(p[..., :nq], hq, dh), q_gain), rope)
    k = apply_axial_rope(rms_norm(split_heads(p[..., nq:nq + nk], hk, dh), k_gain), rope)
    v = split_heads(p[..., nq + nk:], hk, dh)
    kc = rms_norm(split_heads(pc[..., nq:nq + nk], hk, dh), k_gain)
    vc = split_heads(pc[..., nq + nk:], hk, dh)
    sink_logit = sink.astype(jnp.float32).reshape(1, hk, grp, 1, 1)
    nb = s // Q_BLOCK
    n_loc = 3 * Q_BLOCK
    kp = jnp.pad(k, ((0, 0), (0, 0), (Q_BLOCK, Q_BLOCK), (0, 0)))
    vp = jnp.pad(v, ((0, 0), (0, 0), (Q_BLOCK, Q_BLOCK), (0, 0)))
    qb = jnp.moveaxis(q.reshape(b, hk, grp, nb, Q_BLOCK, dh), 3, 0)
    k_off = jnp.arange(n_loc) - Q_BLOCK
    q_off = jnp.arange(Q_BLOCK)

    def band_block(args):
        qblk, j = args
        start = j * Q_BLOCK
        kb = lax.dynamic_slice_in_dim(kp, start, n_loc, axis=2)
        vb = lax.dynamic_slice_in_dim(vp, start, n_loc, axis=2)
        kpos = start + k_off
        qpos = start + q_off
        valid = ((jnp.abs(qpos[:, None] - kpos[None, :]) <= WINDOW)
                 & (kpos >= 0)[None, :] & (kpos < s)[None, :])
        s_loc = jnp.einsum('bkgqd,bknd->bkgqn', qblk, kb).astype(jnp.float32) * scale
        s_loc = jnp.where(valid, s_loc, NEG_INF)
        s_ctx = jnp.einsum('bkgqd,bknd->bkgqn', qblk, kc).astype(jnp.float32) * scale
        sk = jnp.broadcast_to(sink_logit, s_ctx.shape[:-1] + (1,))
        pr = softmax_f32(jnp.concatenate([s_loc, s_ctx, sk], axis=-1)).astype(v.dtype)
        return (jnp.einsum('bkgqn,bknd->bkgqd', pr[..., :n_loc], vb)
                + jnp.einsum('bkgqn,bknd->bkgqd', pr[..., n_loc:n_loc + n_ctx], vc))

    o = lax.map(band_block, (qb, jnp.arange(nb)))
    o = jnp.moveaxis(o, 0, 3).reshape(b, hq, s, dh)
    oc = None
    if with_ctx:
        qc = rms_norm(split_heads(pc[..., :nq], hq, dh), q_gain).reshape(b, hk, grp, n_ctx, dh)
        s_c = jnp.einsum('bkgqd,bknd->bkgqn', qc, kc).astype(jnp.float32) * scale
        sk = jnp.broadcast_to(sink_logit, s_c.shape[:-1] + (1,))
        pr = softmax_f32(jnp.concatenate([s_c, sk], axis=-1)).astype(v.dtype)
        oc = merge_heads(jnp.einsum('bkgqn,bknd->bkgqd', pr[..., :n_ctx], vc).reshape(b, hq, n_ctx, dh))
    return merge_heads(o), oc


def diff_attend(q, k, v, lam, scale):
    s = jnp.einsum('bhcqd,bhckd->bhcqk', q, k).astype(jnp.float32) * scale
    pr = softmax_f32(s)
    a = (pr[:, :, 0] - lam * pr[:, :, 1]).astype(v.dtype)
    return jnp.einsum('bhqk,bhkd->bhqd', a, v)


def diff_mixer(p, pc, q_gain, k_gain, lq1, lk1, lq2, lk2, sub_gain, lambda_init, rope, with_ctx):
    h, d = DIFF_HEADS, DIFF_DIM
    w = h * 2 * d
    b, s, _ = p.shape
    scale = d ** -0.5

    def comps(t):
        bb, ss, _ = t.shape
        return t.reshape(bb, ss, h, 2, d).transpose(0, 2, 3, 1, 4)

    f32 = jnp.float32
    lam = (jnp.exp(jnp.sum(lq1.astype(f32) * lk1.astype(f32)))
           - jnp.exp(jnp.sum(lq2.astype(f32) * lk2.astype(f32))) + lambda_init)
    q = apply_axial_rope(rms_norm(comps(p[..., :w]), q_gain), rope)
    k = apply_axial_rope(rms_norm(comps(p[..., w:2 * w]), k_gain), rope)
    v = split_heads(p[..., 2 * w:], h, DIFF_V_DIM)
    kc = rms_norm(comps(pc[..., w:2 * w]), k_gain)
    vc = split_heads(pc[..., 2 * w:], h, DIFF_V_DIM)
    k_all = jnp.concatenate([kc, k], axis=3)
    v_all = jnp.concatenate([vc, v], axis=2)
    nb = s // Q_BLOCK
    qb = jnp.moveaxis(q.reshape(b, h, 2, nb, Q_BLOCK, d), 3, 0)
    o = lax.map(lambda qblk: diff_attend(qblk, k_all, v_all, lam, scale), qb)
    o = jnp.moveaxis(o, 0, 2).reshape(b, h, s, DIFF_V_DIM)
    o = rms_norm(o, sub_gain) * (1.0 - lambda_init)
    oc = None
    if with_ctx:
        qc = rms_norm(comps(pc[..., :w]), q_gain)
        oc = merge_heads(rms_norm(diff_attend(qc, kc, vc, lam, scale), sub_gain) * (1.0 - lambda_init))
    return merge_heads(o), oc


def mla_mixer(p, pc, qa_gain, kva_gain, w_uq, w_ukv, q_gain, k_gain, rope, with_ctx):
    scale = (MLA_NOPE + MLA_ROPE) ** -0.5

    def project(t, tabs):
        bb, ss, _ = t.shape
        cq = rms_norm(t[..., :MLA_Q_LORA], qa_gain)
        ckv = rms_norm(t[..., MLA_Q_LORA:MLA_Q_LORA + MLA_KV_LORA], kva_gain)
        k_rope = t[..., MLA_Q_LORA + MLA_KV_LORA:]
        q = split_heads(cq @ w_uq, MLA_HEADS, MLA_NOPE + MLA_ROPE)
        kv = split_heads(ckv @ w_ukv, MLA_HEADS, MLA_NOPE + MLA_V)
        k = jnp.concatenate([kv[..., :MLA_NOPE],
                             jnp.broadcast_to(k_rope[:, None], (bb, MLA_HEADS, ss, MLA_ROPE))], axis=-1)
        q = rms_norm(q, q_gain)
        k = rms_norm(k, k_gain)
        if tabs is not None:
            q = jnp.concatenate([q[..., :MLA_NOPE], apply_axial_rope(q[..., MLA_NOPE:], tabs)], axis=-1)
            k = jnp.concatenate([k[..., :MLA_NOPE], apply_axial_rope(k[..., MLA_NOPE:], tabs)], axis=-1)
        return q, k, kv[..., MLA_NOPE:]

    q, k, v = project(p, rope)
    qc, kc, vc = project(pc, None)
    o = dense_block_attention(q, jnp.concatenate([kc, k], axis=2), jnp.concatenate([vc, v], axis=2), scale)
    oc = merge_heads(dense_attention(qc, kc, vc, scale)) if with_ctx else None
    return merge_heads(o), oc


def grouped_router(h, w_router, b_router):
    n = h.shape[0]
    per = N_EXPERTS // N_EXPERT_GROUPS
    scores = jax.nn.sigmoid((h @ w_router).astype(jnp.float32))
    sel = scores + b_router.astype(jnp.float32)
    gscore = jnp.sum(lax.top_k(sel.reshape(n, N_EXPERT_GROUPS, per), 2)[0], axis=-1)
    gidx = lax.top_k(gscore, 1)[1]
    emask = jnp.repeat(jnp.arange(N_EXPERT_GROUPS)[None, :] == gidx, per, axis=1)
    eidx = lax.top_k(jnp.where(emask, sel, -jnp.inf), TOP_K)[1]
    wts = jnp.take_along_axis(scores, eidx, axis=1)
    wts = wts / jnp.sum(wts, axis=-1, keepdims=True) * ROUTED_SCALE
    return eidx, wts


def moe_ffn(h, w_router, b_router, w1, w3, w2, sw1, sw3, sw2):
    n, d = h.shape
    eidx, gate = grouped_router(h, w_router, b_router)
    n_assign = n * TOP_K
    e_flat = eidx.reshape(-1)
    tok = jnp.repeat(jnp.arange(n), TOP_K)
    g_flat = gate.reshape(-1).astype(h.dtype)
    order = jnp.argsort(e_flat, stable=True)
    e_s, tok_s, g_s = e_flat[order], tok[order], g_flat[order]
    counts = jnp.bincount(e_flat, length=N_EXPERTS)
    padded = (counts + MOE_BLOCK - 1) // MOE_BLOCK * MOE_BLOCK
    start = jnp.cumsum(counts) - counts
    pend = jnp.cumsum(padded)
    pstart = pend - padded
    dest = pstart[e_s] + jnp.arange(n_assign) - start[e_s]
    n_slots = (n_assign + N_EXPERTS * (MOE_BLOCK - 1) + MOE_BLOCK - 1) // MOE_BLOCK * MOE_BLOCK
    n_blk = n_slots // MOE_BLOCK
    slot_tok = jnp.full((n_slots,), n, dtype=tok.dtype).at[dest].set(tok_s)
    slot_gate = jnp.zeros((n_slots,), h.dtype).at[dest].set(g_s)
    blk_e = jnp.minimum(jnp.searchsorted(pend, jnp.arange(n_blk) * MOE_BLOCK, side='right'), N_EXPERTS - 1)
    h_pad = jnp.concatenate([h, jnp.zeros((1, d), h.dtype)], axis=0)
    xb = h_pad[slot_tok].reshape(n_blk, MOE_BLOCK, d)

    def expert_block(args):
        xblk, e = args
        return (jax.nn.silu(xblk @ w1[e]) * (xblk @ w3[e])) @ w2[e]

    yb = lax.map(expert_block, (xb, blk_e)).reshape(n_slots, d)
    routed = jnp.zeros((n + 1, d), h.dtype).at[slot_tok].add(yb * slot_gate[:, None])[:n]
    shared = (jax.nn.silu(h @ sw1) * (h @ sw3)) @ sw2
    return routed + shared


def ada_mod(cond, w, b):
    m = jax.nn.silu(cond) @ w + b
    return jnp.split(m[..., None, :], 6, axis=-1)


def setup_inputs(seed: int = 0) -> dict:
    key = jax.random.key(seed)
    ks = iter(jax.random.split(key, 40))
    D = D_MODEL

    def nrm(shape, scale):
        return jax.random.normal(next(ks), shape, jnp.float32) * scale

    def gain(shape):
        return 1.0 + 0.05 * jax.random.normal(next(ks), shape, jnp.float32)

    return {
        'x': nrm((BATCH, SEQ, D), 1.0),
        'c': nrm((BATCH, D), 1.0),
        'ctx': nrm((BATCH, CTX_LEN, D), 1.0),
        'c_ctx': nrm((D,), 1.0),
        'w_ada': nrm((DEPTH, D, 6 * D), 0.5 * D ** -0.5),
        'b_ada': nrm((DEPTH, 6 * D), 0.02),
        'g_attn': gain((DEPTH, D)),
        'g_ffn': gain((DEPTH, D)),
        'w_in': nrm((DEPTH, D, IN_COLS), D ** -0.5),
        'w_out': nrm((DEPTH, MIX_WIDTH, D), MIX_WIDTH ** -0.5),
        'na_q_gain': gain((DEPTH, HEAD_DIM)),
        'na_k_gain': gain((DEPTH, HEAD_DIM)),
        'na_rpb': nrm((DEPTH, NA_HEADS, 2 * NA_KH - 1, 2 * NA_KW - 1), 0.1),
        'wa_q_gain': gain((DEPTH, HEAD_DIM)),
        'wa_k_gain': gain((DEPTH, HEAD_DIM)),
        'wa_sink': nrm((DEPTH, WA_HEADS), 0.5),
        'diff_q_gain': gain((DEPTH, DIFF_DIM)),
        'diff_k_gain': gain((DEPTH, DIFF_DIM)),
        'diff_lq1': nrm((DEPTH, DIFF_DIM), 0.1),
        'diff_lk1': nrm((DEPTH, DIFF_DIM), 0.1),
        'diff_lq2': nrm((DEPTH, DIFF_DIM), 0.1),
        'diff_lk2': nrm((DEPTH, DIFF_DIM), 0.1),
        'diff_sub_gain': gain((DEPTH, DIFF_V_DIM)),
        'mla_qa_gain': gain((DEPTH, MLA_Q_LORA)),
        'mla_kva_gain': gain((DEPTH, MLA_KV_LORA)),
        'mla_w_uq': nrm((DEPTH, MLA_Q_LORA, MLA_HEADS * (MLA_NOPE + MLA_ROPE)), MLA_Q_LORA ** -0.5),
        'mla_w_ukv': nrm((DEPTH, MLA_KV_LORA, MLA_HEADS * (MLA_NOPE + MLA_V)), MLA_KV_LORA ** -0.5),
        'mla_q_gain': gain((DEPTH, MLA_NOPE + MLA_ROPE)),
        'mla_k_gain': gain((DEPTH, MLA_NOPE + MLA_ROPE)),
        'w_router': nrm((D, N_EXPERTS), D ** -0.5),
        'b_router': nrm((N_EXPERTS,), 0.01),
        'moe_w1': nrm((DEPTH, N_EXPERTS, D, D_EXPERT), D ** -0.5),
        'moe_w3': nrm((DEPTH, N_EXPERTS, D, D_EXPERT), D ** -0.5),
        'moe_w2': nrm((DEPTH, N_EXPERTS, D_EXPERT, D), D_EXPERT ** -0.5),
        'sh_w1': nrm((DEPTH, D, D_EXPERT), D ** -0.5),
        'sh_w3': nrm((DEPTH, D, D_EXPERT), D ** -0.5),
        'sh_w2': nrm((DEPTH, D_EXPERT, D), D_EXPERT ** -0.5),
    }


def reference(x, c, ctx, c_ctx, w_ada, b_ada, g_attn, g_ffn, w_in, w_out,
              na_q_gain, na_k_gain, na_rpb, wa_q_gain, wa_k_gain, wa_sink,
              diff_q_gain, diff_k_gain, diff_lq1, diff_lk1, diff_lq2, diff_lk2, diff_sub_gain,
              mla_qa_gain, mla_kva_gain, mla_w_uq, mla_w_ukv, mla_q_gain, mla_k_gain,
              w_router, b_router, moe_w1, moe_w3, moe_w2, sh_w1, sh_w3, sh_w2):
    b, s, d = x.shape
    n_ctx = ctx.shape[1]
    rope_head = axial_rope_tables(s, HEAD_DIM)
    rope_mla = axial_rope_tables(s, MLA_ROPE)
    for l in range(DEPTH):
        with_ctx = l < DEPTH - 1
        sh_a, sc_a, gt_a, sh_f, sc_f, gt_f = ada_mod(c, w_ada[l], b_ada[l])
        csh_a, csc_a, cgt_a, csh_f, csc_f, cgt_f = ada_mod(c_ctx, w_ada[l], b_ada[l])
        h = rms_norm(x, g_attn[l]) * (1 + sc_a) + sh_a
        hc = rms_norm(ctx, g_attn[l]) * (1 + csc_a) + csh_a
        p = h @ w_in[l]
        pc = hc @ w_in[l]
        o_na, oc_na = neighbourhood_mixer(p[..., NA_OFF:WA_OFF], pc[..., NA_OFF:WA_OFF],
                                          na_q_gain[l], na_k_gain[l], na_rpb[l], with_ctx)
        o_wa, oc_wa = window_mixer(p[..., WA_OFF:DIFF_OFF], pc[..., WA_OFF:DIFF_OFF],
                                   wa_q_gain[l], wa_k_gain[l], wa_sink[l], rope_head, with_ctx)
        o_df, oc_df = diff_mixer(p[..., DIFF_OFF:MLA_OFF], pc[..., DIFF_OFF:MLA_OFF],
                                 diff_q_gain[l], diff_k_gain[l], diff_lq1[l], diff_lk1[l],
                                 diff_lq2[l], diff_lk2[l], diff_sub_gain[l],
                                 0.8 - 0.6 * math.exp(-0.3 * l), rope_head, with_ctx)
        o_ml, oc_ml = mla_mixer(p[..., MLA_OFF:IN_COLS], pc[..., MLA_OFF:IN_COLS],
                                mla_qa_gain[l], mla_kva_gain[l], mla_w_uq[l], mla_w_ukv[l],
                                mla_q_gain[l], mla_k_gain[l], rope_mla, with_ctx)
        x = x + gt_a * (jnp.concatenate([o_na, o_wa, o_df, o_ml], axis=-1) @ w_out[l])
        hf = rms_norm(x, g_ffn[l]) * (1 + sc_f) + sh_f
        if with_ctx:
            ctx = ctx + cgt_a * (jnp.concatenate([oc_na, oc_wa, oc_df, oc_ml], axis=-1) @ w_out[l])
            hfc = rms_norm(ctx, g_ffn[l]) * (1 + csc_f) + csh_f
            tokens = jnp.concatenate([hfc, hf], axis=1)
        else:
            tokens = hf
        y = moe_ffn(tokens.reshape(-1, d), w_router, b_router, moe_w1[l], moe_w3[l], moe_w2[l],
                    sh_w1[l], sh_w3[l], sh_w2[l]).reshape(b, -1, d)
        if with_ctx:
            ctx = ctx + cgt_f * y[:, :n_ctx]
            y = y[:, n_ctx:]
        x = x + gt_f * y
    return x
```

```python
import functools
import math

import jax
import jax.numpy as jnp
from jax import lax
from jax.experimental import pallas as pl
from jax.experimental.pallas import tpu as pltpu

F32 = jnp.float32
BF16 = jnp.bfloat16

GRID_W = 64
HEAD_DIM = 64
ROPE_THETA = 10000.0
NORM_EPS = 1e-6
NEG_INF = -1e30
Q_BLOCK = 128
WINDOW = 128
NA_HEADS = 8
NA_KH = 8
NA_KW = 16
WA_HEADS = 8
WA_KV_HEADS = 2
DIFF_HEADS = 4
DIFF_DIM = 64
DIFF_V_DIM = 128
MLA_HEADS = 8
MLA_NOPE = 64
MLA_ROPE = 32
MLA_V = 64
MLA_Q_LORA = 384
MLA_KV_LORA = 128
N_EXPERTS = 16
N_EXPERT_GROUPS = 4
TOP_K = 2
ROUTED_SCALE = 1.0
MOE_BLOCK = 256

NA_COLS = 3 * NA_HEADS * HEAD_DIM
WA_COLS = (WA_HEADS + 2 * WA_KV_HEADS) * HEAD_DIM
DIFF_COLS = 3 * DIFF_HEADS * 2 * DIFF_DIM
MLA_COLS = MLA_Q_LORA + MLA_KV_LORA + MLA_ROPE
NA_OFF = 0
WA_OFF = NA_OFF + NA_COLS
DIFF_OFF = WA_OFF + WA_COLS
MLA_OFF = DIFF_OFF + DIFF_COLS
IN_COLS = MLA_OFF + MLA_COLS

LANE = 128
LOG2E = math.log2(math.e)
VMEM_LIMIT = 48 * 1024 * 1024


def _round_up(n, m):
    return (n + m - 1) // m * m


def _params(sem):
    return pltpu.CompilerParams(dimension_semantics=sem, vmem_limit_bytes=VMEM_LIMIT)


def _sigmoid(x):
    return 1.0 / (1.0 + jnp.exp(-x))


def _modvec_kernel(c_ref, w_ref, b_ref, o_ref):
    a = c_ref[...]
    a = a * _sigmoid(a)
    o_ref[0] = jnp.dot(a, w_ref[0], preferred_element_type=F32,
                       precision=lax.Precision.HIGHEST) + b_ref[0]


def modvec(cond8, w_ada, b_ada):
    depth, d, n = w_ada.shape
    tn = 1024
    return pl.pallas_call(
        _modvec_kernel,
        grid=(depth, n // tn),
        in_specs=[pl.BlockSpec((8, d), lambda l, j: (0, 0)),
                  pl.BlockSpec((1, d, tn), lambda l, j: (l, 0, j)),
                  pl.BlockSpec((1, 1, tn), lambda l, j: (l, 0, j))],
        out_specs=pl.BlockSpec((1, 8, tn), lambda l, j: (l, 0, j)),
        out_shape=jax.ShapeDtypeStruct((depth, 8, n), F32),
        compiler_params=_params(("parallel", "parallel")),
        name="modvec",
    )(cond8, w_ada, b_ada.reshape(depth, 1, n))


def _inproj_kernel(x_ref, g_ref, sc_ref, sh_ref, w_ref, o_ref, h_sc):
    @pl.when(pl.program_id(1) == 0)
    def _():
        x = x_ref[...]
        ms = jnp.mean(x * x, axis=-1, keepdims=True)
        y = x * lax.rsqrt(ms + NORM_EPS) * g_ref[...]
        h_sc[...] = (y * sc_ref[...] + sh_ref[...]).astype(BF16)

    o_ref[...] = jnp.dot(h_sc[...], w_ref[...], preferred_element_type=F32)


def inproj(x, g, sc1, sh, w, tn):
    r, d = x.shape
    n = w.shape[1]
    tm = min(r, 1024)
    vec = pl.BlockSpec((1, d), lambda i, j: (0, 0))
    return pl.pallas_call(
        _inproj_kernel,
        grid=(r // tm, n // tn),
        in_specs=[pl.BlockSpec((tm, d), lambda i, j: (i, 0)), vec, vec, vec,
                  pl.BlockSpec((d, tn), lambda i, j: (0, j))],
        out_specs=pl.BlockSpec((tm, tn), lambda i, j: (i, j)),
        out_shape=jax.ShapeDtypeStruct((r, n), F32),
        scratch_shapes=[pltpu.VMEM((tm, d), BF16)],
        compiler_params=_params(("parallel", "arbitrary")),
        name="inproj",
    )(x, g, sc1, sh, w)


def _mm_kernel(x_ref, w_ref, o_ref):
    o_ref[...] = jnp.dot(x_ref[...], w_ref[...], preferred_element_type=F32)


def matmul(x, w):
    r, k = x.shape
    n = w.shape[1]
    tm = min(r, 1024)
    return pl.pallas_call(
        _mm_kernel,
        grid=(r // tm,),
        in_specs=[pl.BlockSpec((tm, k), lambda i: (i, 0)),
                  pl.BlockSpec((k, n), lambda i: (0, 0))],
        out_specs=pl.BlockSpec((tm, n), lambda i: (i, 0)),
        out_shape=jax.ShapeDtypeStruct((r, n), F32),
        compiler_params=_params(("parallel",)),
        name="matmul",
    )(x, w)


def _flash_kernel(qT_ref, k_ref, vT_ref, o_ref, acc_sc, *, n_chain, bk, n_kb):
    bq = qT_ref.shape[-1]
    for c in range(n_chain):
        acc_sc[c] = jnp.zeros(acc_sc.shape[1:], F32)

    def body(j, carry):
        off = pl.multiple_of(j * bk, bk)
        kb = k_ref[0, pl.ds(off, bk), :]
        vb = vT_ref[0, :, pl.ds(off, bk)]
        out = []
        for c in range(n_chain):
            m, l = carry[c]
            sT = jnp.dot(kb, qT_ref[0, c], preferred_element_type=F32)
            m_new = jnp.maximum(m, jnp.max(sT, axis=0, keepdims=True))
            alpha = jnp.exp2(m - m_new)
            pT = jnp.exp2(sT - m_new)
            l_new = alpha * l + jnp.sum(pT, axis=0, keepdims=True)
            acc_sc[c] = alpha * acc_sc[c] + jnp.dot(vb, pT.astype(BF16),
                                                    preferred_element_type=F32)
            out.append((m_new, l_new))
        return tuple(out)

    init = tuple((jnp.full((1, bq), NEG_INF, F32), jnp.zeros((1, bq), F32))
                 for _ in range(n_chain))
    fin = lax.fori_loop(0, n_kb, body, init)
    for c in range(n_chain):
        o_ref[0, c] = acc_sc[c] * (1.0 / fin[c][1])


def _pick_bk(n):
    for bk in (1280, 1024, 768, 640, 512, 384, 256, 128):
        if n % bk == 0:
            return bk
    raise ValueError(f"key count {n} must be a multiple of {LANE}")


def flash(qT, k, vT, bq=512):
    g, c, dk, s = qT.shape
    n = k.shape[1]
    dv = vT.shape[1]
    bq = min(bq, s)
    bk = _pick_bk(n)
    kern = functools.partial(_flash_kernel, n_chain=c, bk=bk, n_kb=n // bk)
    return pl.pallas_call(
        kern,
        grid=(g, s // bq),
        in_specs=[pl.BlockSpec((1, c, dk, bq), lambda h, i: (h, 0, 0, i)),
                  pl.BlockSpec((1, n, dk), lambda h, i: (h, 0, 0)),
                  pl.BlockSpec((1, dv, n), lambda h, i: (h, 0, 0))],
        out_specs=pl.BlockSpec((1, c, dv, bq), lambda h, i: (h, 0, 0, i)),
        out_shape=jax.ShapeDtypeStruct((g, c, dv, s), F32),
        scratch_shapes=[pltpu.VMEM((c, dv, bq), F32)],
        compiler_params=_params(("parallel", "arbitrary")),
        name="flash",
    )(qT, k, vT)


def _ctx_kernel(q_ref, k_ref, v_ref, sink_ref, o_ref, *, use_sink):
    s = lax.dot_general(q_ref[0], k_ref[0], (((1,), (1,)), ((), ())),
                        preferred_element_type=F32)
    m = jnp.max(s, axis=-1, keepdims=True)
    if use_sink:
        sk = sink_ref[0, 0:1, 0:1]
        m = jnp.maximum(m, sk)
    p = jnp.exp2(s - m)
    l = jnp.sum(p, axis=-1, keepdims=True)
    if use_sink:
        l = l + jnp.exp2(sk - m)
    o = jnp.dot(p.astype(BF16), v_ref[0], preferred_element_type=F32)
    o_ref[0] = o * (1.0 / l)


def ctx_attn(q, k, v, sink=None):
    h, l, dk = q.shape
    hk, hv, dv = k.shape[0], v.shape[0], v.shape[2]
    use_sink = sink is not None
    if sink is None:
        sink = jnp.zeros((h,), F32)
    sink3 = jnp.broadcast_to(sink.astype(F32)[:, None, None], (h, 8, LANE))
    kern = functools.partial(_ctx_kernel, use_sink=use_sink)
    return pl.pallas_call(
        kern,
        grid=(h,),
        in_specs=[pl.BlockSpec((1, l, dk), lambda i: (i, 0, 0)),
                  pl.BlockSpec((1, l, dk), lambda i: (i // (h // hk), 0, 0)),
                  pl.BlockSpec((1, l, dv), lambda i: (i // (h // hv), 0, 0)),
                  pl.BlockSpec((1, 8, LANE), lambda i: (i, 0, 0))],
        out_specs=pl.BlockSpec((1, l, dv), lambda i: (i, 0, 0)),
        out_shape=jax.ShapeDtypeStruct((h, l, dv), F32),
        compiler_params=_params(("parallel",)),
        name="ctx_attn",
    )(q, k, v, sink3)


def _wa_kernel(q_ref, km_ref, k0_ref, kp_ref, vm_ref, v0_ref, vp_ref, kc_ref, vc_ref,
               sink_ref, o_ref, *, seq, n_heads, n_kv):
    j = pl.program_id(0)
    grp = n_heads // n_kv
    qpos = j * Q_BLOCK + lax.broadcasted_iota(jnp.int32, (Q_BLOCK, 3 * Q_BLOCK), 0)
    kpos = (j - 1) * Q_BLOCK + lax.broadcasted_iota(jnp.int32, (Q_BLOCK, 3 * Q_BLOCK), 1)
    valid = (jnp.abs(qpos - kpos) <= WINDOW) & (kpos >= 0) & (kpos < seq)
    nt = (((1,), (1,)), ((), ()))
    for g in range(n_kv):
        kw = jnp.concatenate([km_ref[g], k0_ref[g], kp_ref[g]], axis=0)
        vw = jnp.concatenate([vm_ref[g], v0_ref[g], vp_ref[g]], axis=0)
        kc = kc_ref[g]
        vc = vc_ref[g]
        for hh in range(grp):
            h = g * grp + hh
            q = q_ref[h]
            s_loc = lax.dot_general(q, kw, nt, preferred_element_type=F32)
            s_loc = jnp.where(valid, s_loc, NEG_INF)
            s_ctx = lax.dot_general(q, kc, nt, preferred_element_type=F32)
            sk = sink_ref[h, :, 0:1][0:1]
            m = jnp.maximum(jnp.max(s_loc, axis=-1, keepdims=True),
                            jnp.max(s_ctx, axis=-1, keepdims=True))
            m = jnp.maximum(m, sk)
            p_loc = jnp.exp2(s_loc - m)
            p_ctx = jnp.exp2(s_ctx - m)
            l = (jnp.sum(p_loc, axis=-1, keepdims=True) + jnp.sum(p_ctx, axis=-1, keepdims=True)
                 + jnp.exp2(sk - m))
            o = (jnp.dot(p_loc.astype(BF16), vw, preferred_element_type=F32)
                 + jnp.dot(p_ctx.astype(BF16), vc, preferred_element_type=F32))
            o_ref[:, h * HEAD_DIM:(h + 1) * HEAD_DIM] = o * (1.0 / l)


def wa_attn(q, k, v, kc, vc, sink):
    h, s, dh = q.shape
    hk = k.shape[0]
    l = kc.shape[1]
    nb = s // Q_BLOCK
    sink3 = jnp.broadcast_to(sink.astype(F32)[:, None, None], (h, 8, LANE))
    kvspec = lambda f: pl.BlockSpec((hk, Q_BLOCK, dh), f)
    prev = lambda j: (0, jnp.maximum(j - 1, 0), 0)
    cur = lambda j: (0, j, 0)
    nxt = lambda j: (0, jnp.minimum(j + 1, nb - 1), 0)
    full = lambda j: (0, 0, 0)
    kern = functools.partial(_wa_kernel, seq=s, n_heads=h, n_kv=hk)
    return pl.pallas_call(
        kern,
        grid=(nb,),
        in_specs=[pl.BlockSpec((h, Q_BLOCK, dh), cur),
                  kvspec(prev), kvspec(cur), kvspec(nxt),
                  kvspec(prev), kvspec(cur), kvspec(nxt),
                  pl.BlockSpec((hk, l, dh), full), pl.BlockSpec((hk, l, dh), full),
                  pl.BlockSpec((h, 8, LANE), full)],
        out_specs=pl.BlockSpec((Q_BLOCK, h * dh), lambda j: (j, 0)),
        out_shape=jax.ShapeDtypeStruct((s, h * dh), F32),
        compiler_params=_params(("parallel",)),
        name="wa_attn",
    )(q, k, k, k, v, v, v, kc, vc, sink3)


NA_ROWS_PER_STEP = 8


def _na_kernel(q_ref, k_ref, v_ref, kc_ref, vc_ref, bias_ref, o_ref, *, rows):
    blk = pl.program_id(1)
    nt = (((1,), (1,)), ((), ()))
    kc = kc_ref[0]
    vc = vc_ref[0]
    n_loc = NA_KH * GRID_W

    def row(i, carry):
        r = blk * NA_ROWS_PER_STEP + i
        r0 = jnp.clip(r - NA_KH // 2, 0, rows - NA_KH)
        qoff = pl.multiple_of(i * GRID_W, GRID_W)
        koff = pl.multiple_of(r0 * GRID_W, GRID_W)
        q = q_ref[0, pl.ds(qoff, GRID_W), :]
        kw = k_ref[0, pl.ds(koff, n_loc), :]
        vw = v_ref[0, pl.ds(koff, n_loc), :]
        s_loc = lax.dot_general(q, kw, nt, preferred_element_type=F32) + bias_ref[r - r0, 0]
        s_ctx = lax.dot_general(q, kc, nt, preferred_element_type=F32)
        m = jnp.maximum(jnp.max(s_loc, axis=-1, keepdims=True),
                        jnp.max(s_ctx, axis=-1, keepdims=True))
        p_loc = jnp.exp2(s_loc - m)
        p_ctx = jnp.exp2(s_ctx - m)
        l = jnp.sum(p_loc, axis=-1, keepdims=True) + jnp.sum(p_ctx, axis=-1, keepdims=True)
        o = (jnp.dot(p_loc.astype(BF16), vw, preferred_element_type=F32)
             + jnp.dot(p_ctx.astype(BF16), vc, preferred_element_type=F32))
        o_ref[0, pl.ds(qoff, GRID_W), :] = o * (1.0 / l)
        return carry

    lax.fori_loop(0, NA_ROWS_PER_STEP, row, 0, unroll=True)


def na_bias_table(rpb):
    cols = jnp.arange(GRID_W)
    c0 = jnp.clip(cols - NA_KW // 2, 0, GRID_W - NA_KW)
    kc = jnp.arange(GRID_W)
    in_win = (kc[None, :] >= c0[:, None]) & (kc[None, :] < c0[:, None] + NA_KW)
    col_rel = jnp.clip(kc[None, :] - cols[:, None] + (NA_KW - 1), 0, 2 * NA_KW - 2)
    d = jnp.arange(NA_KH)
    jj = jnp.arange(NA_KH)
    row_rel = jj[None, :] - d[:, None] + (NA_KH - 1)
    b = rpb[:, row_rel[:, :, None, None], col_rel[None, None, :, :]]
    b = jnp.where(in_win[None, None, None], b.astype(F32) * LOG2E, NEG_INF)
    b = jnp.transpose(b, (1, 0, 3, 2, 4))
    return b.reshape(NA_KH, rpb.shape[0], GRID_W, NA_KH * GRID_W)


def na_attn(q, k, v, kc, vc, bias):
    h, s, dh = q.shape
    l = kc.shape[1]
    rows = s // GRID_W
    assert rows >= NA_KH and rows % NA_ROWS_PER_STEP == 0
    tq = NA_ROWS_PER_STEP * GRID_W
    kern = functools.partial(_na_kernel, rows=rows)
    head = lambda hh, i: (hh, 0, 0)
    return pl.pallas_call(
        kern,
        grid=(h, rows // NA_ROWS_PER_STEP),
        in_specs=[pl.BlockSpec((1, tq, dh), lambda hh, i: (hh, i, 0)),
                  pl.BlockSpec((1, s, dh), head), pl.BlockSpec((1, s, dh), head),
                  pl.BlockSpec((1, l, dh), head), pl.BlockSpec((1, l, dh), head),
                  pl.BlockSpec((NA_KH, 1, GRID_W, NA_KH * GRID_W), lambda hh, i: (0, hh, 0, 0))],
        out_specs=pl.BlockSpec((1, tq, dh), lambda hh, i: (hh, i, 0)),
        out_shape=jax.ShapeDtypeStruct((h, s, dh), F32),
        compiler_params=_params(("parallel", "arbitrary")),
        name="na_attn",
    )(q, k, v, kc, vc, bias)


def _outproj_kernel(o_ref, x_ref, w_ref, gate_ref, g_ref, sc_ref, sh_ref, wr_ref,
                    xo_ref, hf_ref, lg_ref):
    acc = jnp.dot(o_ref[...], w_ref[...], preferred_element_type=F32)
    xn = x_ref[...] + gate_ref[...] * acc
    xo_ref[...] = xn
    ms = jnp.mean(xn * xn, axis=-1, keepdims=True)
    y = xn * lax.rsqrt(ms + NORM_EPS) * g_ref[...]
    hf = (y * sc_ref[...] + sh_ref[...]).astype(BF16)
    hf_ref[...] = hf
    lg_ref[...] = jnp.dot(hf, wr_ref[...], preferred_element_type=F32)


def outproj(o, x, w, gate, g, sc1, sh, wr):
    r, d = x.shape
    k = o.shape[1]
    tm = min(r, 256)
    vec = pl.BlockSpec((1, d), lambda i: (0, 0))
    row = lambda n: pl.BlockSpec((tm, n), lambda i: (i, 0))
    return pl.pallas_call(
        _outproj_kernel,
        grid=(r // tm,),
        in_specs=[row(k), row(d), pl.BlockSpec((k, d), lambda i: (0, 0)), vec, vec, vec, vec,
                  pl.BlockSpec((d, LANE), lambda i: (0, 0))],
        out_specs=[row(d), row(d), row(LANE)],
        out_shape=[jax.ShapeDtypeStruct((r, d), F32), jax.ShapeDtypeStruct((r, d), BF16),
                   jax.ShapeDtypeStruct((r, LANE), F32)],
        compiler_params=_params(("parallel",)),
        name="outproj",
    )(o, x, w, gate, g, sc1, sh, wr)


def _moe_kernel(be_ref, nv_ref, x_ref, w1_ref, w3_ref, w2_ref, o_ref):
    i = pl.program_id(0)

    @pl.when(i < nv_ref[0])
    def _():
        x = x_ref[...]
        a = jnp.dot(x, w1_ref[0], preferred_element_type=F32)
        b = jnp.dot(x, w3_ref[0], preferred_element_type=F32)
        hmid = (a * _sigmoid(a) * b).astype(BF16)
        o_ref[...] = jnp.dot(hmid, w2_ref[0], preferred_element_type=F32)

    @pl.when(i >= nv_ref[0])
    def _():
        o_ref[...] = jnp.zeros(o_ref.shape, o_ref.dtype)


def moe_blocks(x, w1, w3, w2, blk_e, n_valid):
    n, d = x.shape
    f = w1.shape[2]
    n_blk = n // MOE_BLOCK
    grid_spec = pltpu.PrefetchScalarGridSpec(
        num_scalar_prefetch=2,
        grid=(n_blk,),
        in_specs=[pl.BlockSpec((MOE_BLOCK, d), lambda i, be, nv: (i, 0)),
                  pl.BlockSpec((1, d, f), lambda i, be, nv: (be[i], 0, 0)),
                  pl.BlockSpec((1, d, f), lambda i, be, nv: (be[i], 0, 0)),
                  pl.BlockSpec((1, f, d), lambda i, be, nv: (be[i], 0, 0))],
        out_specs=pl.BlockSpec((MOE_BLOCK, d), lambda i, be, nv: (i, 0)),
    )
    return pl.pallas_call(
        _moe_kernel,
        grid_spec=grid_spec,
        out_shape=jax.ShapeDtypeStruct((n, d), F32),
        compiler_params=_params(("arbitrary",)),
        name="moe_blocks",
    )(blk_e, n_valid, x, w1, w3, w2)


def _rms(x, g):
    y = x * lax.rsqrt(jnp.mean(x * x, axis=-1, keepdims=True) + NORM_EPS)
    return y * g


def _rope_tables(seq_len, rot_dim):
    n = rot_dim // 4
    t = jnp.arange(seq_len)
    row = (t // GRID_W).astype(F32)[:, None]
    col = (t % GRID_W).astype(F32)[:, None]
    inv = ROPE_THETA ** (-jnp.arange(n, dtype=F32) / n)
    return (jnp.cos(row * inv), jnp.sin(row * inv), jnp.cos(col * inv), jnp.sin(col * inv))


def _rope(x, tabs):
    expand = (slice(None),) + (None,) * (x.ndim - 2) + (slice(None),)
    cr, sr, cc, sc = (t[expand] for t in tabs)
    x1, x2, x3, x4 = jnp.split(x, 4, axis=-1)
    return jnp.concatenate([x1 * cr - x2 * sr, x2 * cr + x1 * sr,
                            x3 * cc - x4 * sc, x4 * cc + x3 * sc], axis=-1)


def _heads_first(t):
    return jnp.transpose(t, (1, 0, 2))


def _route(logits, b_router):
    n = logits.shape[0]
    per = N_EXPERTS // N_EXPERT_GROUPS
    scores = jax.nn.sigmoid(logits.astype(F32))
    sel = scores + b_router.astype(F32)
    gscore = jnp.sum(lax.top_k(sel.reshape(n, N_EXPERT_GROUPS, per), 2)[0], axis=-1)
    gidx = lax.top_k(gscore, 1)[1]
    emask = jnp.repeat(jnp.arange(N_EXPERT_GROUPS)[None, :] == gidx, per, axis=1)
    eidx = lax.top_k(jnp.where(emask, sel, -jnp.inf), TOP_K)[1]
    wts = jnp.take_along_axis(scores, eidx, axis=1)
    wts = wts / jnp.sum(wts, axis=-1, keepdims=True) * ROUTED_SCALE
    return eidx, wts


def _dispatch(eidx):
    n = eidx.shape[0]
    n_assign = n * TOP_K
    e_flat = eidx.reshape(-1)
    tok = jnp.repeat(jnp.arange(n), TOP_K)
    order = jnp.argsort(e_flat, stable=True)
    e_s, tok_s = e_flat[order], tok[order]
    counts = jnp.bincount(e_flat, length=N_EXPERTS)
    padded = (counts + MOE_BLOCK - 1) // MOE_BLOCK * MOE_BLOCK
    start = jnp.cumsum(counts) - counts
    pend = jnp.cumsum(padded)
    pstart = pend - padded
    dest = pstart[e_s] + jnp.arange(n_assign) - start[e_s]
    n_slots = (n_assign + N_EXPERTS * (MOE_BLOCK - 1) + MOE_BLOCK - 1) // MOE_BLOCK * MOE_BLOCK
    n_blk = n_slots // MOE_BLOCK
    slot_tok = jnp.full((n_slots,), n, dtype=tok.dtype).at[dest].set(tok_s)
    slot_of = jnp.zeros((n_assign,), dest.dtype).at[order].set(dest).reshape(n, TOP_K)
    blk_e = jnp.minimum(jnp.searchsorted(pend, jnp.arange(n_blk) * MOE_BLOCK, side='right'),
                        N_EXPERTS - 1)
    n_valid = (pend[-1] // MOE_BLOCK).reshape(1)
    return slot_tok, slot_of, blk_e.astype(jnp.int32), n_valid.astype(jnp.int32)


def _na_prep(p, pc, q_gain, k_gain):
    h, dh = NA_HEADS, HEAD_DIM
    w = h * dh
    qs = dh ** -0.5 * LOG2E

    def split(t):
        return t.reshape(t.shape[0], h, dh)

    q = _heads_first(_rms(split(p[:, :w]), q_gain) * qs).astype(BF16)
    k = _heads_first(_rms(split(p[:, w:2 * w]), k_gain)).astype(BF16)
    v = _heads_first(split(p[:, 2 * w:])).astype(BF16)
    qc = _heads_first(_rms(split(pc[:, :w]), q_gain) * qs).astype(BF16)
    kc = _heads_first(_rms(split(pc[:, w:2 * w]), k_gain)).astype(BF16)
    vc = _heads_first(split(pc[:, 2 * w:])).astype(BF16)
    return q, k, v, qc, kc, vc


def _wa_prep(p, pc, q_gain, k_gain, rope):
    hq, hk, dh = WA_HEADS, WA_KV_HEADS, HEAD_DIM
    nq, nk = hq * dh, hk * dh
    qs = dh ** -0.5 * LOG2E
    s, l = p.shape[0], pc.shape[0]
    q = _heads_first(_rope(_rms(p[:, :nq].reshape(s, hq, dh), q_gain), rope) * qs).astype(BF16)
    k = _heads_first(_rope(_rms(p[:, nq:nq + nk].reshape(s, hk, dh), k_gain), rope)).astype(BF16)
    v = _heads_first(p[:, nq + nk:].reshape(s, hk, dh)).astype(BF16)
    qc = _heads_first(_rms(pc[:, :nq].reshape(l, hq, dh), q_gain) * qs).astype(BF16)
    kc = _heads_first(_rms(pc[:, nq:nq + nk].reshape(l, hk, dh), k_gain)).astype(BF16)
    vc = _heads_first(pc[:, nq + nk:].reshape(l, hk, dh)).astype(BF16)
    return q, k, v, qc, kc, vc


def _diff_prep(p, pc, q_gain, k_gain, rope):
    h, d = DIFF_HEADS, DIFF_DIM
    w = h * 2 * d
    qs = d ** -0.5 * LOG2E
    s, l = p.shape[0], pc.shape[0]
    q = _rope(_rms(p[:, :w].reshape(s, h, 2, d), q_gain), rope) * qs
    k = _rope(_rms(p[:, w:2 * w].reshape(s, h, 2, d), k_gain), rope)
    qc = _rms(pc[:, :w].reshape(l, h, 2, d), q_gain) * qs
    kc = _rms(pc[:, w:2 * w].reshape(l, h, 2, d), k_gain)
    v = p[:, 2 * w:].reshape(s, h, DIFF_V_DIM)
    vc = pc[:, 2 * w:].reshape(l, h, DIFF_V_DIM)
    eye = jnp.eye(2, dtype=q.dtype)[None, None, :, :, None]
    q_pad = (q[:, :, :, None, :] * eye).reshape(s, h, 2, 2 * d)
    qT = jnp.transpose(q_pad, (1, 2, 3, 0)).astype(BF16)
    k_all = jnp.concatenate([kc, k], axis=0).reshape(l + s, h, 2 * d)
    k_all = _heads_first(k_all).astype(BF16)
    vT = jnp.transpose(jnp.concatenate([vc, v], axis=0), (1, 2, 0)).astype(BF16)
    qc_h = jnp.transpose(qc, (1, 2, 0, 3)).reshape(2 * h, l, d).astype(BF16)
    kc_h = jnp.transpose(kc, (1, 2, 0, 3)).reshape(2 * h, l, d).astype(BF16)
    vc_h = _heads_first(vc).astype(BF16)
    return qT, k_all, vT, qc_h, kc_h, vc_h


def _mla_prep(p, pc, qa_gain, kva_gain, w_uq, w_ukv, q_gain, k_gain, rope):
    hh, dn, dr, dv = MLA_HEADS, MLA_NOPE, MLA_ROPE, MLA_V
    dq = dn + dr
    qs = dq ** -0.5 * LOG2E
    pad = LANE - dq

    def project(t, tabs):
        n = t.shape[0]
        cq = _rms(t[:, :MLA_Q_LORA], qa_gain).astype(BF16)
        ckv = _rms(t[:, MLA_Q_LORA:MLA_Q_LORA + MLA_KV_LORA], kva_gain).astype(BF16)
        k_rope = t[:, MLA_Q_LORA + MLA_KV_LORA:]
        q = matmul(cq, w_uq).reshape(n, hh, dq)
        kv = matmul(ckv, w_ukv).reshape(n, hh, dn + dv)
        k = jnp.concatenate([kv[..., :dn], jnp.broadcast_to(k_rope[:, None], (n, hh, dr))], -1)
        q = _rms(q, q_gain)
        k = _rms(k, k_gain)
        if tabs is not None:
            q = jnp.concatenate([q[..., :dn], _rope(q[..., dn:], tabs)], axis=-1)
            k = jnp.concatenate([k[..., :dn], _rope(k[..., dn:], tabs)], axis=-1)
        q = jnp.pad(q * qs, ((0, 0), (0, 0), (0, pad)))
        k = jnp.pad(k, ((0, 0), (0, 0), (0, pad)))
        return q, k, kv[..., dn:]

    q, k, v = project(p, rope)
    qc, kc, vc = project(pc, None)
    qT = jnp.transpose(q, (1, 2, 0))[:, None].astype(BF16)
    k_all = _heads_first(jnp.concatenate([kc, k], axis=0)).astype(BF16)
    vT = jnp.transpose(jnp.concatenate([vc, v], axis=0), (1, 2, 0)).astype(BF16)
    return (qT, k_all, vT, _heads_first(qc).astype(BF16), _heads_first(kc).astype(BF16),
            _heads_first(vc).astype(BF16))


def _merge(o):
    return jnp.transpose(o, (1, 0, 2)).reshape(o.shape[1], -1)


def _diff_post(o1, o2, lam, sub_gain, lambda_init):
    return _rms(o1 - lam * o2, sub_gain) * (1.0 - lambda_init)


def _moe(tokens_bf16, logits, b_router, w1, w3, w2, sw1, sw3, sw2):
    n, d = tokens_bf16.shape
    eidx, gate = _route(logits[:, :N_EXPERTS], b_router)
    slot_tok, slot_of, blk_e, n_valid = _dispatch(eidx)
    h_pad = jnp.concatenate([tokens_bf16, jnp.zeros((1, d), BF16)], axis=0)
    xb = h_pad[slot_tok]
    yb = moe_blocks(xb, w1, w3, w2, blk_e, n_valid)
    routed = (yb[slot_of[:, 0]] * gate[:, 0:1] + yb[slot_of[:, 1]] * gate[:, 1:2])
    n_sh = _round_up(n, MOE_BLOCK) // MOE_BLOCK
    shared = moe_blocks(tokens_bf16, sw1[None], sw3[None], sw2[None],
                        jnp.zeros((n_sh,), jnp.int32), jnp.full((1,), n_sh, jnp.int32))
    return routed + shared


def kernel(x, c, ctx, c_ctx, w_ada, b_ada, g_attn, g_ffn, w_in, w_out, na_q_gain, na_k_gain, na_rpb, wa_q_gain, wa_k_gain, wa_sink, diff_q_gain, diff_k_gain, diff_lq1, diff_lk1, diff_lq2, diff_lk2, diff_sub_gain, mla_qa_gain, mla_kva_gain, mla_w_uq, mla_w_ukv, mla_q_gain, mla_k_gain, w_router, b_router, moe_w1, moe_w3, moe_w2, sh_w1, sh_w3, sh_w2):
    b, s, d = x.shape
    assert b == 1
    n_ctx = ctx.shape[1]
    depth = w_ada.shape[0]
    xl = x[0]
    xc = ctx[0]
    rope_head = _rope_tables(s, HEAD_DIM)
    rope_mla = _rope_tables(s, MLA_ROPE)

    cond8 = jnp.zeros((8, d), F32).at[0].set(c[0]).at[1].set(c_ctx)
    mod = modvec(cond8, w_ada, b_ada)
    in_pad = _round_up(IN_COLS, 640)
    tn = 640
    wr = jnp.pad(w_router, ((0, 0), (0, LANE - N_EXPERTS))).astype(BF16)

    for l in range(depth):
        with_ctx = l < depth - 1
        m_lat = mod[l, 0].reshape(6, 1, d)
        m_ctx = mod[l, 1].reshape(6, 1, d)
        w_in_l = jnp.pad(w_in[l], ((0, 0), (0, in_pad - IN_COLS))).astype(BF16)
        w_out_l = w_out[l].astype(BF16)
        g_a = g_attn[l][None]
        g_f = g_ffn[l][None]

        p = inproj(xl, g_a, 1.0 + m_lat[1], m_lat[0], w_in_l, tn)
        pc = inproj(xc, g_a, 1.0 + m_ctx[1], m_ctx[0], w_in_l, tn)

        q, k, v, qc, kc, vc = _na_prep(p[:, NA_OFF:WA_OFF], pc[:, NA_OFF:WA_OFF],
                                       na_q_gain[l], na_k_gain[l])
        o_na = _merge(na_attn(q, k, v, kc, vc, na_bias_table(na_rpb[l])))
        oc_na = _merge(ctx_attn(qc, kc, vc)) if with_ctx else None

        q, k, v, qc, kc, vc = _wa_prep(p[:, WA_OFF:DIFF_OFF], pc[:, WA_OFF:DIFF_OFF],
                                       wa_q_gain[l], wa_k_gain[l], rope_head)
        sink2 = wa_sink[l].astype(F32) * LOG2E
        o_wa = wa_attn(q, k, v, kc, vc, sink2)
        oc_wa = _merge(ctx_attn(qc, kc, vc, sink2)) if with_ctx else None

        lambda_init = 0.8 - 0.6 * math.exp(-0.3 * l)
        lam = (jnp.exp(jnp.sum(diff_lq1[l].astype(F32) * diff_lk1[l].astype(F32)))
               - jnp.exp(jnp.sum(diff_lq2[l].astype(F32) * diff_lk2[l].astype(F32))) + lambda_init)
        qT, k_all, vT, qc, kc, vc = _diff_prep(p[:, DIFF_OFF:MLA_OFF], pc[:, DIFF_OFF:MLA_OFF],
                                               diff_q_gain[l], diff_k_gain[l], rope_head)
        oT = flash(qT, k_all, vT)
        o_df = _diff_post(jnp.transpose(oT[:, 0], (2, 0, 1)), jnp.transpose(oT[:, 1], (2, 0, 1)),
                          lam, diff_sub_gain[l], lambda_init).reshape(s, -1)
        oc_df = None
        if with_ctx:
            oc = ctx_attn(qc, kc, vc).reshape(DIFF_HEADS, 2, n_ctx, DIFF_V_DIM)
            oc_df = _diff_post(jnp.transpose(oc[:, 0], (1, 0, 2)), jnp.transpose(oc[:, 1], (1, 0, 2)),
                               lam, diff_sub_gain[l], lambda_init).reshape(n_ctx, -1)

        qT, k_all, vT, qc, kc, vc = _mla_prep(
            p[:, MLA_OFF:IN_COLS], pc[:, MLA_OFF:IN_COLS], mla_qa_gain[l], mla_kva_gain[l],
            mla_w_uq[l].astype(BF16), mla_w_ukv[l].astype(BF16), mla_q_gain[l], mla_k_gain[l],
            rope_mla)
        oT = flash(qT, k_all, vT)
        o_ml = jnp.transpose(oT[:, 0], (2, 0, 1)).reshape(s, -1)
        oc_ml = _merge(ctx_attn(qc, kc, vc)) if with_ctx else None

        o_cat = jnp.concatenate([o_na, o_wa, o_df, o_ml], axis=-1).astype(BF16)
        xl, hf, lg = outproj(o_cat, xl, w_out_l, m_lat[2], g_f, 1.0 + m_lat[4], m_lat[3], wr)
        if with_ctx:
            oc_cat = jnp.concatenate([oc_na, oc_wa, oc_df, oc_ml], axis=-1).astype(BF16)
            xc, hfc, lgc = outproj(oc_cat, xc, w_out_l, m_ctx[2], g_f, 1.0 + m_ctx[4], m_ctx[3], wr)
            tokens = jnp.concatenate([hfc, hf], axis=0)
            logits = jnp.concatenate([lgc, lg], axis=0)
        else:
            tokens, logits = hf, lg
        y = _moe(tokens, logits, b_router, moe_w1[l].astype(BF16), moe_w3[l].astype(BF16),
                 moe_w2[l].astype(BF16), sh_w1[l].astype(BF16), sh_w3[l].astype(BF16),
                 sh_w2[l].astype(BF16))
        if with_ctx:
            xc = xc + m_ctx[5] * y[:n_ctx]
            y = y[n_ctx:]
        xl = xl + m_lat[5] * y
    return xl[None]
```

```python
import functools
import math

import jax
import jax.numpy as jnp
from jax import lax
from jax.experimental import pallas as pl
from jax.experimental.pallas import tpu as pltpu

F32 = jnp.float32
BF16 = jnp.bfloat16

GRID_W = 64
HEAD_DIM = 64
ROPE_THETA = 10000.0
NORM_EPS = 1e-6
NEG_INF = -1e30
Q_BLOCK = 128
WINDOW = 128
NA_HEADS = 8
NA_KH = 8
NA_KW = 16
WA_HEADS = 8
WA_KV_HEADS = 2
DIFF_HEADS = 4
DIFF_DIM = 64
DIFF_V_DIM = 128
MLA_HEADS = 8
MLA_NOPE = 64
MLA_ROPE = 32
MLA_V = 64
MLA_Q_LORA = 384
MLA_KV_LORA = 128
N_EXPERTS = 16
N_EXPERT_GROUPS = 4
TOP_K = 2
ROUTED_SCALE = 1.0
MOE_BLOCK = 256

NA_COLS = 3 * NA_HEADS * HEAD_DIM
WA_COLS = (WA_HEADS + 2 * WA_KV_HEADS) * HEAD_DIM
DIFF_COLS = 3 * DIFF_HEADS * 2 * DIFF_DIM
MLA_COLS = MLA_Q_LORA + MLA_KV_LORA + MLA_ROPE
NA_OFF = 0
WA_OFF = NA_OFF + NA_COLS
DIFF_OFF = WA_OFF + WA_COLS
MLA_OFF = DIFF_OFF + DIFF_COLS
IN_COLS = MLA_OFF + MLA_COLS

LANE = 128
LOG2E = math.log2(math.e)
VMEM_LIMIT = 48 * 1024 * 1024


def _round_up(n, m):
    return (n + m - 1) // m * m


def _params(sem):
    return pltpu.CompilerParams(dimension_semantics=sem, vmem_limit_bytes=VMEM_LIMIT)


def _sigmoid(x):
    return 1.0 / (1.0 + jnp.exp(-x))


def _modvec_kernel(c_ref, w_ref, b_ref, o_ref):
    a = c_ref[...]
    a = a * _sigmoid(a)
    o_ref[0] = jnp.dot(a, w_ref[0], preferred_element_type=F32,
                       precision=lax.Precision.HIGHEST) + b_ref[0]


def modvec(cond8, w_ada, b_ada):
    depth, d, n = w_ada.shape
    tn = 1024
    return pl.pallas_call(
        _modvec_kernel,
        grid=(depth, n // tn),
        in_specs=[pl.BlockSpec((8, d), lambda l, j: (0, 0)),
                  pl.BlockSpec((1, d, tn), lambda l, j: (l, 0, j)),
                  pl.BlockSpec((1, 1, tn), lambda l, j: (l, 0, j))],
        out_specs=pl.BlockSpec((1, 8, tn), lambda l, j: (l, 0, j)),
        out_shape=jax.ShapeDtypeStruct((depth, 8, n), F32),
        compiler_params=_params(("parallel", "parallel")),
        name="modvec",
    )(cond8, w_ada, b_ada.reshape(depth, 1, n))


def _inproj_kernel(x_ref, g_ref, sc_ref, sh_ref, w_ref, o_ref, h_sc):
    @pl.when(pl.program_id(1) == 0)
    def _():
        x = x_ref[...]
        ms = jnp.mean(x * x, axis=-1, keepdims=True)
        y = x * lax.rsqrt(ms + NORM_EPS) * g_ref[...]
        h_sc[...] = (y * sc_ref[...] + sh_ref[...]).astype(BF16)

    o_ref[...] = jnp.dot(h_sc[...], w_ref[...], preferred_element_type=F32)


def inproj(x, g, sc1, sh, w, tn):
    r, d = x.shape
    n = w.shape[1]
    tm = min(r, 1024)
    vec = pl.BlockSpec((1, d), lambda i, j: (0, 0))
    return pl.pallas_call(
        _inproj_kernel,
        grid=(r // tm, n // tn),
        in_specs=[pl.BlockSpec((tm, d), lambda i, j: (i, 0)), vec, vec, vec,
                  pl.BlockSpec((d, tn), lambda i, j: (0, j))],
        out_specs=pl.BlockSpec((tm, tn), lambda i, j: (i, j)),
        out_shape=jax.ShapeDtypeStruct((r, n), F32),
        scratch_shapes=[pltpu.VMEM((tm, d), BF16)],
        compiler_params=_params(("parallel", "arbitrary")),
        name="inproj",
    )(x, g, sc1, sh, w)


def _mm_kernel(x_ref, w_ref, o_ref):
    o_ref[...] = jnp.dot(x_ref[...], w_ref[...], preferred_element_type=F32)


def matmul(x, w):
    r, k = x.shape
    n = w.shape[1]
    tm = min(r, 1024)
    return pl.pallas_call(
        _mm_kernel,
        grid=(r // tm,),
        in_specs=[pl.BlockSpec((tm, k), lambda i: (i, 0)),
                  pl.BlockSpec((k, n), lambda i: (0, 0))],
        out_specs=pl.BlockSpec((tm, n), lambda i: (i, 0)),
        out_shape=jax.ShapeDtypeStruct((r, n), F32),
        compiler_params=_params(("parallel",)),
        name="matmul",
    )(x, w)


FLASH_V_PAD = 16
FLASH_SAFE_SHIFT = 60.0


def _flash_kernel(qT_ref, k_ref, vT_ref, o_ref, acc_sc, *, n_comp, n_split, bk, n_kb, dv, online):
    bq = qT_ref.shape[-1] // n_split
    chains = [(c, h) for c in range(n_comp) for h in range(n_split)]
    for i in range(len(chains)):
        acc_sc[i] = jnp.zeros(acc_sc.shape[1:], F32)

    def body(j, carry):
        off = pl.multiple_of(j * bk, bk)
        vb = vT_ref[0, :, pl.ds(off, bk)]
        out = []
        scores = []
        for c, h in chains:
            kb = k_ref[0, c, pl.ds(off, bk), :]
            qT = qT_ref[0, c, :, h * bq:(h + 1) * bq]
            scores.append(jnp.dot(kb, qT, preferred_element_type=F32))
        for i, sT in enumerate(scores):
            if online:
                m = carry[i]
                m_new = jnp.maximum(m, jnp.max(sT, axis=0, keepdims=True))
                pT = jnp.exp2(sT - m_new).astype(BF16)
                acc_sc[i] = (jnp.exp2(m - m_new) * acc_sc[i]
                             + jnp.dot(vb, pT, preferred_element_type=F32))
                out.append(m_new)
            else:
                pT = jnp.exp2(sT).astype(BF16)
                acc_sc[i] += jnp.dot(vb, pT, preferred_element_type=F32)
                out.append(carry[i])
        return tuple(out)

    init = tuple(jnp.full((1, bq), NEG_INF, F32) for _ in chains)
    lax.fori_loop(0, n_kb, body, init)
    for i, (c, h) in enumerate(chains):
        acc = acc_sc[i]
        o_ref[0, c, :, h * bq:(h + 1) * bq] = acc[:dv] * (1.0 / acc[dv:dv + 1])


def _pick_bk(n):
    for bk in (1280, 1024, 768, 640, 512, 384, 256, 128):
        if n % bk == 0:
            return bk
    raise ValueError(f"key count {n} must be a multiple of {LANE}")


FLASH_BQ = 1024
FLASH_CHAIN_Q = 512


def _flash_call(qT, k, vT, online):
    g, c, dk, s = qT.shape
    n = k.shape[2]
    dvp = vT.shape[1]
    dv = dvp - FLASH_V_PAD
    bq = min(FLASH_BQ, s)
    n_split = max(1, bq // FLASH_CHAIN_Q)
    bk = _pick_bk(n)
    kern = functools.partial(_flash_kernel, n_comp=c, n_split=n_split, bk=bk, n_kb=n // bk,
                             dv=dv, online=online)
    return pl.pallas_call(
        kern,
        grid=(g, s // bq),
        in_specs=[pl.BlockSpec((1, c, dk, bq), lambda h, i: (h, 0, 0, i)),
                  pl.BlockSpec((1, c, n, dk), lambda h, i: (h, 0, 0, 0)),
                  pl.BlockSpec((1, dvp, n), lambda h, i: (h, 0, 0))],
        out_specs=pl.BlockSpec((1, c, dv, bq), lambda h, i: (h, 0, 0, i)),
        out_shape=jax.ShapeDtypeStruct((g, c, dv, s), F32),
        scratch_shapes=[pltpu.VMEM((c * n_split, dvp, bq // n_split), F32)],
        compiler_params=_params(("parallel", "arbitrary")),
        name="flash_online" if online else "flash",
    )(qT, k, vT)


def flash(qT, k, vT, u_max):
    return lax.cond(u_max <= FLASH_SAFE_SHIFT,
                    lambda a, b, cc: _flash_call(a, b, cc, False),
                    lambda a, b, cc: _flash_call(a, b, cc, True),
                    qT, k, vT)


def _augment_qk(q, k):
    d = q.shape[-1]
    qb = q.astype(BF16)
    kb = k.astype(BF16)
    qn = jnp.sqrt(jnp.sum(jnp.square(qb.astype(F32)), axis=-1))
    kn = jnp.sqrt(jnp.max(jnp.sum(jnp.square(kb.astype(F32)), axis=-1), axis=0))
    u = qn * kn * 1.01 + 0.01
    pad = LANE - d - 1
    q_aug = jnp.concatenate([qb, (-u)[..., None].astype(BF16),
                             jnp.zeros(q.shape[:-1] + (pad,), BF16)], axis=-1)
    k_aug = jnp.concatenate([kb, jnp.ones(k.shape[:-1] + (1,), BF16),
                             jnp.zeros(k.shape[:-1] + (pad,), BF16)], axis=-1)
    return (jnp.transpose(q_aug, (1, 2, 3, 0)), jnp.transpose(k_aug, (1, 2, 0, 3)), jnp.max(u))


def _augment_v(v):
    n, g, _ = v.shape
    v_aug = jnp.concatenate([v.astype(BF16), jnp.ones((n, g, 1), BF16),
                             jnp.zeros((n, g, FLASH_V_PAD - 1), BF16)], axis=-1)
    return jnp.transpose(v_aug, (1, 2, 0))


def _ctx_kernel(q_ref, k_ref, v_ref, sink_ref, o_ref, *, use_sink):
    s = lax.dot_general(q_ref[0], k_ref[0], (((1,), (1,)), ((), ())),
                        preferred_element_type=F32)
    m = jnp.max(s, axis=-1, keepdims=True)
    if use_sink:
        sk = sink_ref[0, 0:1, 0:1]
        m = jnp.maximum(m, sk)
    p = jnp.exp2(s - m)
    l = jnp.sum(p, axis=-1, keepdims=True)
    if use_sink:
        l = l + jnp.exp2(sk - m)
    o = jnp.dot(p.astype(BF16), v_ref[0], preferred_element_type=F32)
    o_ref[0] = o * (1.0 / l)


def ctx_attn(q, k, v, sink=None):
    h, l, dk = q.shape
    hk, hv, dv = k.shape[0], v.shape[0], v.shape[2]
    use_sink = sink is not None
    if sink is None:
        sink = jnp.zeros((h,), F32)
    sink3 = jnp.broadcast_to(sink.astype(F32)[:, None, None], (h, 8, LANE))
    kern = functools.partial(_ctx_kernel, use_sink=use_sink)
    return pl.pallas_call(
        kern,
        grid=(h,),
        in_specs=[pl.BlockSpec((1, l, dk), lambda i: (i, 0, 0)),
                  pl.BlockSpec((1, l, dk), lambda i: (i // (h // hk), 0, 0)),
                  pl.BlockSpec((1, l, dv), lambda i: (i // (h // hv), 0, 0)),
                  pl.BlockSpec((1, 8, LANE), lambda i: (i, 0, 0))],
        out_specs=pl.BlockSpec((1, l, dv), lambda i: (i, 0, 0)),
        out_shape=jax.ShapeDtypeStruct((h, l, dv), F32),
        compiler_params=_params(("parallel",)),
        name="ctx_attn",
    )(q, k, v, sink3)


def _wa_kernel(q_ref, km_ref, k0_ref, kp_ref, vm_ref, v0_ref, vp_ref, kc_ref, vc_ref,
               sink_ref, o_ref, *, seq, n_heads, n_kv):
    j = pl.program_id(0)
    grp = n_heads // n_kv
    qpos = j * Q_BLOCK + lax.broadcasted_iota(jnp.int32, (Q_BLOCK, 3 * Q_BLOCK), 0)
    kpos = (j - 1) * Q_BLOCK + lax.broadcasted_iota(jnp.int32, (Q_BLOCK, 3 * Q_BLOCK), 1)
    valid = (jnp.abs(qpos - kpos) <= WINDOW) & (kpos >= 0) & (kpos < seq)
    nt = (((1,), (1,)), ((), ()))
    for g in range(n_kv):
        kw = jnp.concatenate([km_ref[g], k0_ref[g], kp_ref[g]], axis=0)
        vw = jnp.concatenate([vm_ref[g], v0_ref[g], vp_ref[g]], axis=0)
        kc = kc_ref[g]
        vc = vc_ref[g]
        for hh in range(grp):
            h = g * grp + hh
            q = q_ref[h]
            s_loc = lax.dot_general(q, kw, nt, preferred_element_type=F32)
            s_loc = jnp.where(valid, s_loc, NEG_INF)
            s_ctx = lax.dot_general(q, kc, nt, preferred_element_type=F32)
            sk = sink_ref[h, :, 0:1][0:1]
            m = jnp.maximum(jnp.max(s_loc, axis=-1, keepdims=True),
                            jnp.max(s_ctx, axis=-1, keepdims=True))
            m = jnp.maximum(m, sk)
            p_loc = jnp.exp2(s_loc - m)
            p_ctx = jnp.exp2(s_ctx - m)
            l = (jnp.sum(p_loc, axis=-1, keepdims=True) + jnp.sum(p_ctx, axis=-1, keepdims=True)
                 + jnp.exp2(sk - m))
            o = (jnp.dot(p_loc.astype(BF16), vw, preferred_element_type=F32)
                 + jnp.dot(p_ctx.astype(BF16), vc, preferred_element_type=F32))
            o_ref[:, h * HEAD_DIM:(h + 1) * HEAD_DIM] = o * (1.0 / l)


def wa_attn(q, k, v, kc, vc, sink):
    h, s, dh = q.shape
    hk = k.shape[0]
    l = kc.shape[1]
    nb = s // Q_BLOCK
    sink3 = jnp.broadcast_to(sink.astype(F32)[:, None, None], (h, 8, LANE))
    kvspec = lambda f: pl.BlockSpec((hk, Q_BLOCK, dh), f)
    prev = lambda j: (0, jnp.maximum(j - 1, 0), 0)
    cur = lambda j: (0, j, 0)
    nxt = lambda j: (0, jnp.minimum(j + 1, nb - 1), 0)
    full = lambda j: (0, 0, 0)
    kern = functools.partial(_wa_kernel, seq=s, n_heads=h, n_kv=hk)
    return pl.pallas_call(
        kern,
        grid=(nb,),
        in_specs=[pl.BlockSpec((h, Q_BLOCK, dh), cur),
                  kvspec(prev), kvspec(cur), kvspec(nxt),
                  kvspec(prev), kvspec(cur), kvspec(nxt),
                  pl.BlockSpec((hk, l, dh), full), pl.BlockSpec((hk, l, dh), full),
                  pl.BlockSpec((h, 8, LANE), full)],
        out_specs=pl.BlockSpec((Q_BLOCK, h * dh), lambda j: (j, 0)),
        out_shape=jax.ShapeDtypeStruct((s, h * dh), F32),
        compiler_params=_params(("parallel",)),
        name="wa_attn",
    )(q, k, k, k, v, v, v, kc, vc, sink3)


NA_ROWS_PER_STEP = 8


def _na_kernel(q_ref, k_ref, v_ref, kc_ref, vc_ref, bias_ref, o_ref, *, rows):
    blk = pl.program_id(1)
    nt = (((1,), (1,)), ((), ()))
    kc = kc_ref[0]
    vc = vc_ref[0]
    n_loc = NA_KH * GRID_W

    def row(i, carry):
        r = blk * NA_ROWS_PER_STEP + i
        r0 = jnp.clip(r - NA_KH // 2, 0, rows - NA_KH)
        qoff = pl.multiple_of(i * GRID_W, GRID_W)
        koff = pl.multiple_of(r0 * GRID_W, GRID_W)
        q = q_ref[0, pl.ds(qoff, GRID_W), :]
        kw = k_ref[0, pl.ds(koff, n_loc), :]
        vw = v_ref[0, pl.ds(koff, n_loc), :]
        s_loc = lax.dot_general(q, kw, nt, preferred_element_type=F32) + bias_ref[r - r0, 0]
        s_ctx = lax.dot_general(q, kc, nt, preferred_element_type=F32)
        m = jnp.maximum(jnp.max(s_loc, axis=-1, keepdims=True),
                        jnp.max(s_ctx, axis=-1, keepdims=True))
        p_loc = jnp.exp2(s_loc - m)
        p_ctx = jnp.exp2(s_ctx - m)
        l = jnp.sum(p_loc, axis=-1, keepdims=True) + jnp.sum(p_ctx, axis=-1, keepdims=True)
        o = (jnp.dot(p_loc.astype(BF16), vw, preferred_element_type=F32)
             + jnp.dot(p_ctx.astype(BF16), vc, preferred_element_type=F32))
        o_ref[0, pl.ds(qoff, GRID_W), :] = o * (1.0 / l)
        return carry

    lax.fori_loop(0, NA_ROWS_PER_STEP, row, 0, unroll=True)


def na_bias_table(rpb):
    cols = jnp.arange(GRID_W)
    c0 = jnp.clip(cols - NA_KW // 2, 0, GRID_W - NA_KW)
    kc = jnp.arange(GRID_W)
    in_win = (kc[None, :] >= c0[:, None]) & (kc[None, :] < c0[:, None] + NA_KW)
    col_rel = jnp.clip(kc[None, :] - cols[:, None] + (NA_KW - 1), 0, 2 * NA_KW - 2)
    d = jnp.arange(NA_KH)
    jj = jnp.arange(NA_KH)
    row_rel = jj[None, :] - d[:, None] + (NA_KH - 1)
    b = rpb[:, row_rel[:, :, None, None], col_rel[None, None, :, :]]
    b = jnp.where(in_win[None, None, None], b.astype(F32) * LOG2E, NEG_INF)
    b = jnp.transpose(b, (1, 0, 3, 2, 4))
    return b.reshape(NA_KH, rpb.shape[0], GRID_W, NA_KH * GRID_W)


def na_attn(q, k, v, kc, vc, bias):
    h, s, dh = q.shape
    l = kc.shape[1]
    rows = s // GRID_W
    assert rows >= NA_KH and rows % NA_ROWS_PER_STEP == 0
    tq = NA_ROWS_PER_STEP * GRID_W
    kern = functools.partial(_na_kernel, rows=rows)
    head = lambda hh, i: (hh, 0, 0)
    return pl.pallas_call(
        kern,
        grid=(h, rows // NA_ROWS_PER_STEP),
        in_specs=[pl.BlockSpec((1, tq, dh), lambda hh, i: (hh, i, 0)),
                  pl.BlockSpec((1, s, dh), head), pl.BlockSpec((1, s, dh), head),
                  pl.BlockSpec((1, l, dh), head), pl.BlockSpec((1, l, dh), head),
                  pl.BlockSpec((NA_KH, 1, GRID_W, NA_KH * GRID_W), lambda hh, i: (0, hh, 0, 0))],
        out_specs=pl.BlockSpec((1, tq, dh), lambda hh, i: (hh, i, 0)),
        out_shape=jax.ShapeDtypeStruct((h, s, dh), F32),
        compiler_params=_params(("parallel", "arbitrary")),
        name="na_attn",
    )(q, k, v, kc, vc, bias)


def _outproj_kernel(o_ref, x_ref, w_ref, gate_ref, g_ref, sc_ref, sh_ref, wr_ref,
                    xo_ref, hf_ref, lg_ref):
    acc = jnp.dot(o_ref[...], w_ref[...], preferred_element_type=F32)
    xn = x_ref[...] + gate_ref[...] * acc
    xo_ref[...] = xn
    ms = jnp.mean(xn * xn, axis=-1, keepdims=True)
    y = xn * lax.rsqrt(ms + NORM_EPS) * g_ref[...]
    hf = (y * sc_ref[...] + sh_ref[...]).astype(BF16)
    hf_ref[...] = hf
    lg_ref[...] = jnp.dot(hf, wr_ref[...], preferred_element_type=F32)


def outproj(o, x, w, gate, g, sc1, sh, wr):
    r, d = x.shape
    k = o.shape[1]
    tm = min(r, 256)
    vec = pl.BlockSpec((1, d), lambda i: (0, 0))
    row = lambda n: pl.BlockSpec((tm, n), lambda i: (i, 0))
    return pl.pallas_call(
        _outproj_kernel,
        grid=(r // tm,),
        in_specs=[row(k), row(d), pl.BlockSpec((k, d), lambda i: (0, 0)), vec, vec, vec, vec,
                  pl.BlockSpec((d, LANE), lambda i: (0, 0))],
        out_specs=[row(d), row(d), row(LANE)],
        out_shape=[jax.ShapeDtypeStruct((r, d), F32), jax.ShapeDtypeStruct((r, d), BF16),
                   jax.ShapeDtypeStruct((r, LANE), F32)],
        compiler_params=_params(("parallel",)),
        name="outproj",
    )(o, x, w, gate, g, sc1, sh, wr)


def _moe_kernel(be_ref, nv_ref, x_ref, w1_ref, w3_ref, w2_ref, o_ref):
    i = pl.program_id(0)

    @pl.when(i < nv_ref[0])
    def _():
        x = x_ref[...]
        a = jnp.dot(x, w1_ref[0], preferred_element_type=F32)
        b = jnp.dot(x, w3_ref[0], preferred_element_type=F32)
        hmid = (a * _sigmoid(a) * b).astype(BF16)
        o_ref[...] = jnp.dot(hmid, w2_ref[0], preferred_element_type=F32)

    @pl.when(i >= nv_ref[0])
    def _():
        o_ref[...] = jnp.zeros(o_ref.shape, o_ref.dtype)


def moe_blocks(x, w1, w3, w2, blk_e, n_valid):
    n, d = x.shape
    f = w1.shape[2]
    n_blk = n // MOE_BLOCK
    grid_spec = pltpu.PrefetchScalarGridSpec(
        num_scalar_prefetch=2,
        grid=(n_blk,),
        in_specs=[pl.BlockSpec((MOE_BLOCK, d), lambda i, be, nv: (i, 0)),
                  pl.BlockSpec((1, d, f), lambda i, be, nv: (be[i], 0, 0)),
                  pl.BlockSpec((1, d, f), lambda i, be, nv: (be[i], 0, 0)),
                  pl.BlockSpec((1, f, d), lambda i, be, nv: (be[i], 0, 0))],
        out_specs=pl.BlockSpec((MOE_BLOCK, d), lambda i, be, nv: (i, 0)),
    )
    return pl.pallas_call(
        _moe_kernel,
        grid_spec=grid_spec,
        out_shape=jax.ShapeDtypeStruct((n, d), F32),
        compiler_params=_params(("arbitrary",)),
        name="moe_blocks",
    )(blk_e, n_valid, x, w1, w3, w2)


def _rms(x, g):
    y = x * lax.rsqrt(jnp.mean(x * x, axis=-1, keepdims=True) + NORM_EPS)
    return y * g


def _rope_tables(seq_len, rot_dim):
    n = rot_dim // 4
    t = jnp.arange(seq_len)
    row = (t // GRID_W).astype(F32)[:, None]
    col = (t % GRID_W).astype(F32)[:, None]
    inv = ROPE_THETA ** (-jnp.arange(n, dtype=F32) / n)
    return (jnp.cos(row * inv), jnp.sin(row * inv), jnp.cos(col * inv), jnp.sin(col * inv))


def _rope(x, tabs):
    expand = (slice(None),) + (None,) * (x.ndim - 2) + (slice(None),)
    cr, sr, cc, sc = (t[expand] for t in tabs)
    x1, x2, x3, x4 = jnp.split(x, 4, axis=-1)
    return jnp.concatenate([x1 * cr - x2 * sr, x2 * cr + x1 * sr,
                            x3 * cc - x4 * sc, x4 * cc + x3 * sc], axis=-1)


def _heads_first(t):
    return jnp.transpose(t, (1, 0, 2))


def _route(logits, b_router):
    n = logits.shape[0]
    per = N_EXPERTS // N_EXPERT_GROUPS
    scores = jax.nn.sigmoid(logits.astype(F32))
    sel = scores + b_router.astype(F32)
    grp = sel.reshape(n, N_EXPERT_GROUPS, per)
    gscore = None
    for a in range(per):
        for bb in range(a + 1, per):
            pair = grp[..., a] + grp[..., bb]
            gscore = pair if gscore is None else jnp.maximum(gscore, pair)
    gidx = jnp.argmax(gscore, axis=-1)
    eids = jnp.arange(N_EXPERTS)
    masked = jnp.where((eids // per)[None, :] == gidx[:, None], sel, -jnp.inf)
    e1 = jnp.argmax(masked, axis=-1)
    e2 = jnp.argmax(jnp.where(eids[None, :] == e1[:, None], -jnp.inf, masked), axis=-1)
    eidx = jnp.stack([e1, e2], axis=1).astype(jnp.int32)
    wts = jnp.take_along_axis(scores, eidx, axis=1)
    wts = wts / jnp.sum(wts, axis=-1, keepdims=True) * ROUTED_SCALE
    return eidx, wts


def _prefix_counts(onehot):
    n, e = onehot.shape
    ch = onehot.astype(F32).reshape(n // LANE, LANE, e)
    tri = jnp.tril(jnp.ones((LANE, LANE), F32))
    within = jnp.einsum('ij,cjk->cik', tri, ch)
    tot = within[:, -1, :]
    base = jnp.cumsum(tot, axis=0) - tot
    return (within + base[:, None, :]).reshape(n, e)


def _dispatch(eidx):
    n = eidx.shape[0]
    n_assign = n * TOP_K
    assert n_assign % LANE == 0
    e_flat = eidx.reshape(-1)
    tok = jnp.repeat(jnp.arange(n, dtype=jnp.int32), TOP_K)
    onehot = e_flat[:, None] == jnp.arange(N_EXPERTS)[None, :]
    csum = _prefix_counts(onehot)
    counts = csum[-1].astype(jnp.int32)
    rank = jnp.sum(jnp.where(onehot, csum, 0.0), axis=1).astype(jnp.int32) - 1
    padded = (counts + MOE_BLOCK - 1) // MOE_BLOCK * MOE_BLOCK
    pend = jnp.cumsum(padded)
    pstart = pend - padded
    dest = jnp.sum(jnp.where(onehot, pstart[None, :], 0), axis=1) + rank
    n_slots = (n_assign + N_EXPERTS * (MOE_BLOCK - 1) + MOE_BLOCK - 1) // MOE_BLOCK * MOE_BLOCK
    n_blk = n_slots // MOE_BLOCK
    slot_tok = jnp.full((n_slots,), n, dtype=jnp.int32).at[dest].set(tok)
    slot_of = dest.reshape(n, TOP_K)
    blk_start = jnp.arange(n_blk, dtype=jnp.int32) * MOE_BLOCK
    blk_e = jnp.minimum(jnp.sum(blk_start[:, None] >= pend[None, :], axis=1), N_EXPERTS - 1)
    n_valid = (pend[-1] // MOE_BLOCK).reshape(1)
    return slot_tok, slot_of, blk_e.astype(jnp.int32), n_valid.astype(jnp.int32)


def _na_prep(p, pc, q_gain, k_gain):
    h, dh = NA_HEADS, HEAD_DIM
    w = h * dh
    qs = dh ** -0.5 * LOG2E

    def split(t):
        return t.reshape(t.shape[0], h, dh)

    q = _heads_first(_rms(split(p[:, :w]), q_gain) * qs).astype(BF16)
    k = _heads_first(_rms(split(p[:, w:2 * w]), k_gain)).astype(BF16)
    v = _heads_first(split(p[:, 2 * w:])).astype(BF16)
    qc = _heads_first(_rms(split(pc[:, :w]), q_gain) * qs).astype(BF16)
    kc = _heads_first(_rms(split(pc[:, w:2 * w]), k_gain)).astype(BF16)
    vc = _heads_first(split(pc[:, 2 * w:])).astype(BF16)
    return q, k, v, qc, kc, vc


def _wa_prep(p, pc, q_gain, k_gain, rope):
    hq, hk, dh = WA_HEADS, WA_KV_HEADS, HEAD_DIM
    nq, nk = hq * dh, hk * dh
    qs = dh ** -0.5 * LOG2E
    s, l = p.shape[0], pc.shape[0]
    q = _heads_first(_rope(_rms(p[:, :nq].reshape(s, hq, dh), q_gain), rope) * qs).astype(BF16)
    k = _heads_first(_rope(_rms(p[:, nq:nq + nk].reshape(s, hk, dh), k_gain), rope)).astype(BF16)
    v = _heads_first(p[:, nq + nk:].reshape(s, hk, dh)).astype(BF16)
    qc = _heads_first(_rms(pc[:, :nq].reshape(l, hq, dh), q_gain) * qs).astype(BF16)
    kc = _heads_first(_rms(pc[:, nq:nq + nk].reshape(l, hk, dh), k_gain)).astype(BF16)
    vc = _heads_first(pc[:, nq + nk:].reshape(l, hk, dh)).astype(BF16)
    return q, k, v, qc, kc, vc


def _diff_prep(p, pc, q_gain, k_gain, rope):
    h, d = DIFF_HEADS, DIFF_DIM
    w = h * 2 * d
    qs = d ** -0.5 * LOG2E
    s, l = p.shape[0], pc.shape[0]
    q = _rope(_rms(p[:, :w].reshape(s, h, 2, d), q_gain), rope) * qs
    k = _rope(_rms(p[:, w:2 * w].reshape(s, h, 2, d), k_gain), rope)
    qc = _rms(pc[:, :w].reshape(l, h, 2, d), q_gain) * qs
    kc = _rms(pc[:, w:2 * w].reshape(l, h, 2, d), k_gain)
    v = p[:, 2 * w:].reshape(s, h, DIFF_V_DIM)
    vc = pc[:, 2 * w:].reshape(l, h, DIFF_V_DIM)
    qT, k_aug, u_max = _augment_qk(q, jnp.concatenate([kc, k], axis=0))
    vT = _augment_v(jnp.concatenate([vc, v], axis=0))
    qc_h = jnp.transpose(qc, (1, 2, 0, 3)).reshape(2 * h, l, d).astype(BF16)
    kc_h = jnp.transpose(kc, (1, 2, 0, 3)).reshape(2 * h, l, d).astype(BF16)
    vc_h = _heads_first(vc).astype(BF16)
    return qT, k_aug, vT, u_max, qc_h, kc_h, vc_h


def _mla_prep(p, pc, qa_gain, kva_gain, w_uq, w_ukv, q_gain, k_gain, rope):
    hh, dn, dr, dv = MLA_HEADS, MLA_NOPE, MLA_ROPE, MLA_V
    dq = dn + dr
    qs = dq ** -0.5 * LOG2E
    pad = LANE - dq

    def project(t, tabs):
        n = t.shape[0]
        cq = _rms(t[:, :MLA_Q_LORA], qa_gain).astype(BF16)
        ckv = _rms(t[:, MLA_Q_LORA:MLA_Q_LORA + MLA_KV_LORA], kva_gain).astype(BF16)
        k_rope = t[:, MLA_Q_LORA + MLA_KV_LORA:]
        q = matmul(cq, w_uq).reshape(n, hh, dq)
        kv = matmul(ckv, w_ukv).reshape(n, hh, dn + dv)
        k = jnp.concatenate([kv[..., :dn], jnp.broadcast_to(k_rope[:, None], (n, hh, dr))], -1)
        q = _rms(q, q_gain)
        k = _rms(k, k_gain)
        if tabs is not None:
            q = jnp.concatenate([q[..., :dn], _rope(q[..., dn:], tabs)], axis=-1)
            k = jnp.concatenate([k[..., :dn], _rope(k[..., dn:], tabs)], axis=-1)
        return q * qs, k, kv[..., dn:]

    q, k, v = project(p, rope)
    qc, kc, vc = project(pc, None)
    qT, k_aug, u_max = _augment_qk(q[:, :, None], jnp.concatenate([kc, k], axis=0)[:, :, None])
    vT = _augment_v(jnp.concatenate([vc, v], axis=0))
    padw = ((0, 0), (0, 0), (0, pad))
    return (qT, k_aug, vT, u_max, _heads_first(jnp.pad(qc, padw)).astype(BF16),
            _heads_first(jnp.pad(kc, padw)).astype(BF16), _heads_first(vc).astype(BF16))


def _merge(o):
    return jnp.transpose(o, (1, 0, 2)).reshape(o.shape[1], -1)


def _diff_post(o1, o2, lam, sub_gain, lambda_init):
    return _rms(o1 - lam * o2, sub_gain) * (1.0 - lambda_init)


def _moe(tokens_bf16, logits, b_router, w1, w3, w2, sw1, sw3, sw2):
    n, d = tokens_bf16.shape
    eidx, gate = _route(logits[:, :N_EXPERTS], b_router)
    slot_tok, slot_of, blk_e, n_valid = _dispatch(eidx)
    h_pad = jnp.concatenate([tokens_bf16, jnp.zeros((1, d), BF16)], axis=0)
    xb = h_pad[slot_tok]
    yb = moe_blocks(xb, w1, w3, w2, blk_e, n_valid)
    routed = (yb[slot_of[:, 0]] * gate[:, 0:1] + yb[slot_of[:, 1]] * gate[:, 1:2])
    n_sh = _round_up(n, MOE_BLOCK) // MOE_BLOCK
    shared = moe_blocks(tokens_bf16, sw1[None], sw3[None], sw2[None],
                        jnp.zeros((n_sh,), jnp.int32), jnp.full((1,), n_sh, jnp.int32))
    return routed + shared


def kernel(x, c, ctx, c_ctx, w_ada, b_ada, g_attn, g_ffn, w_in, w_out, na_q_gain, na_k_gain, na_rpb, wa_q_gain, wa_k_gain, wa_sink, diff_q_gain, diff_k_gain, diff_lq1, diff_lk1, diff_lq2, diff_lk2, diff_sub_gain, mla_qa_gain, mla_kva_gain, mla_w_uq, mla_w_ukv, mla_q_gain, mla_k_gain, w_router, b_router, moe_w1, moe_w3, moe_w2, sh_w1, sh_w3, sh_w2):
    b, s, d = x.shape
    assert b == 1
    n_ctx = ctx.shape[1]
    depth = w_ada.shape[0]
    xl = x[0]
    xc = ctx[0]
    rope_head = _rope_tables(s, HEAD_DIM)
    rope_mla = _rope_tables(s, MLA_ROPE)

    cond8 = jnp.zeros((8, d), F32).at[0].set(c[0]).at[1].set(c_ctx)
    mod = modvec(cond8, w_ada, b_ada)
    in_pad = _round_up(IN_COLS, 640)
    tn = 640
    wr = jnp.pad(w_router, ((0, 0), (0, LANE - N_EXPERTS))).astype(BF16)

    for l in range(depth):
        with_ctx = l < depth - 1
        m_lat = mod[l, 0].reshape(6, 1, d)
        m_ctx = mod[l, 1].reshape(6, 1, d)
        w_in_l = jnp.pad(w_in[l], ((0, 0), (0, in_pad - IN_COLS))).astype(BF16)
        w_out_l = w_out[l].astype(BF16)
        g_a = g_attn[l][None]
        g_f = g_ffn[l][None]

        p = inproj(xl, g_a, 1.0 + m_lat[1], m_lat[0], w_in_l, tn)
        pc = inproj(xc, g_a, 1.0 + m_ctx[1], m_ctx[0], w_in_l, tn)

        q, k, v, qc, kc, vc = _na_prep(p[:, NA_OFF:WA_OFF], pc[:, NA_OFF:WA_OFF],
                                       na_q_gain[l], na_k_gain[l])
        o_na = _merge(na_attn(q, k, v, kc, vc, na_bias_table(na_rpb[l])))
        oc_na = _merge(ctx_attn(qc, kc, vc)) if with_ctx else None

        q, k, v, qc, kc, vc = _wa_prep(p[:, WA_OFF:DIFF_OFF], pc[:, WA_OFF:DIFF_OFF],
                                       wa_q_gain[l], wa_k_gain[l], rope_head)
        sink2 = wa_sink[l].astype(F32) * LOG2E
        o_wa = wa_attn(q, k, v, kc, vc, sink2)
        oc_wa = _merge(ctx_attn(qc, kc, vc, sink2)) if with_ctx else None

        lambda_init = 0.8 - 0.6 * math.exp(-0.3 * l)
        lam = (jnp.exp(jnp.sum(diff_lq1[l].astype(F32) * diff_lk1[l].astype(F32)))
               - jnp.exp(jnp.sum(diff_lq2[l].astype(F32) * diff_lk2[l].astype(F32))) + lambda_init)
        qT, k_aug, vT, u_max, qc, kc, vc = _diff_prep(
            p[:, DIFF_OFF:MLA_OFF], pc[:, DIFF_OFF:MLA_OFF], diff_q_gain[l], diff_k_gain[l],
            rope_head)
        oT = flash(qT, k_aug, vT, u_max)
        o_df = _diff_post(jnp.transpose(oT[:, 0], (2, 0, 1)), jnp.transpose(oT[:, 1], (2, 0, 1)),
                          lam, diff_sub_gain[l], lambda_init).reshape(s, -1)
        oc_df = None
        if with_ctx:
            oc = ctx_attn(qc, kc, vc).reshape(DIFF_HEADS, 2, n_ctx, DIFF_V_DIM)
            oc_df = _diff_post(jnp.transpose(oc[:, 0], (1, 0, 2)), jnp.transpose(oc[:, 1], (1, 0, 2)),
                               lam, diff_sub_gain[l], lambda_init).reshape(n_ctx, -1)

        qT, k_aug, vT, u_max, qc, kc, vc = _mla_prep(
            p[:, MLA_OFF:IN_COLS], pc[:, MLA_OFF:IN_COLS], mla_qa_gain[l], mla_kva_gain[l],
            mla_w_uq[l].astype(BF16), mla_w_ukv[l].astype(BF16), mla_q_gain[l], mla_k_gain[l],
            rope_mla)
        oT = flash(qT, k_aug, vT, u_max)
        o_ml = jnp.transpose(oT[:, 0], (2, 0, 1)).reshape(s, -1)
        oc_ml = _merge(ctx_attn(qc, kc, vc)) if with_ctx else None

        o_cat = jnp.concatenate([o_na, o_wa, o_df, o_ml], axis=-1).astype(BF16)
        xl, hf, lg = outproj(o_cat, xl, w_out_l, m_lat[2], g_f, 1.0 + m_lat[4], m_lat[3], wr)
        if with_ctx:
            oc_cat = jnp.concatenate([oc_na, oc_wa, oc_df, oc_ml], axis=-1).astype(BF16)
            xc, hfc, lgc = outproj(oc_cat, xc, w_out_l, m_ctx[2], g_f, 1.0 + m_ctx[4], m_ctx[3], wr)
            tokens = jnp.concatenate([hfc, hf], axis=0)
            logits = jnp.concatenate([lgc, lg], axis=0)
        else:
            tokens, logits = hf, lg
        y = _moe(tokens, logits, b_router, moe_w1[l].astype(BF16), moe_w3[l].astype(BF16),
                 moe_w2[l].astype(BF16), sh_w1[l].astype(BF16), sh_w3[l].astype(BF16),
                 sh_w2[l].astype(BF16))
        if with_ctx:
            xc = xc + m_ctx[5] * y[:n_ctx]
            y = y[n_ctx:]
        xl = xl + m_lat[5] * y
    return xl[None]
```

```python
import functools
import math

import jax
import jax.numpy as jnp
from jax import lax
from jax.experimental import pallas as pl
from jax.experimental.pallas import tpu as pltpu

F32 = jnp.float32
BF16 = jnp.bfloat16

GRID_W = 64
HEAD_DIM = 64
ROPE_THETA = 10000.0
NORM_EPS = 1e-6
NEG_INF = -1e30
Q_BLOCK = 128
WINDOW = 128
NA_HEADS = 8
NA_KH = 8
NA_KW = 16
WA_HEADS = 8
WA_KV_HEADS = 2
DIFF_HEADS = 4
DIFF_DIM = 64
DIFF_V_DIM = 128
MLA_HEADS = 8
MLA_NOPE = 64
MLA_ROPE = 32
MLA_V = 64
MLA_Q_LORA = 384
MLA_KV_LORA = 128
N_EXPERTS = 16
N_EXPERT_GROUPS = 4
TOP_K = 2
ROUTED_SCALE = 1.0
MOE_BLOCK = 256

NA_COLS = 3 * NA_HEADS * HEAD_DIM
WA_COLS = (WA_HEADS + 2 * WA_KV_HEADS) * HEAD_DIM
DIFF_COLS = 3 * DIFF_HEADS * 2 * DIFF_DIM
MLA_COLS = MLA_Q_LORA + MLA_KV_LORA + MLA_ROPE
NA_OFF = 0
WA_OFF = NA_OFF + NA_COLS
DIFF_OFF = WA_OFF + WA_COLS
MLA_OFF = DIFF_OFF + DIFF_COLS
IN_COLS = MLA_OFF + MLA_COLS

LANE = 128
LOG2E = math.log2(math.e)
VMEM_LIMIT = 48 * 1024 * 1024


def _round_up(n, m):
    return (n + m - 1) // m * m


def _params(sem):
    return pltpu.CompilerParams(dimension_semantics=sem, vmem_limit_bytes=VMEM_LIMIT)


def _sigmoid(x):
    return 1.0 / (1.0 + jnp.exp(-x))


def _modvec_kernel(c_ref, w_ref, b_ref, o_ref):
    a = c_ref[...]
    a = a * _sigmoid(a)
    o_ref[0] = jnp.dot(a, w_ref[0], preferred_element_type=F32,
                       precision=lax.Precision.HIGHEST) + b_ref[0]


def modvec(cond8, w_ada, b_ada):
    depth, d, n = w_ada.shape
    tn = 1024
    return pl.pallas_call(
        _modvec_kernel,
        grid=(depth, n // tn),
        in_specs=[pl.BlockSpec((8, d), lambda l, j: (0, 0)),
                  pl.BlockSpec((1, d, tn), lambda l, j: (l, 0, j)),
                  pl.BlockSpec((1, 1, tn), lambda l, j: (l, 0, j))],
        out_specs=pl.BlockSpec((1, 8, tn), lambda l, j: (l, 0, j)),
        out_shape=jax.ShapeDtypeStruct((depth, 8, n), F32),
        compiler_params=_params(("parallel", "parallel")),
        name="modvec",
    )(cond8, w_ada, b_ada.reshape(depth, 1, n))


def _inproj_kernel(x_ref, g_ref, sc_ref, sh_ref, w_ref, o_ref, h_sc):
    @pl.when(pl.program_id(1) == 0)
    def _():
        x = x_ref[...]
        ms = jnp.mean(x * x, axis=-1, keepdims=True)
        y = x * lax.rsqrt(ms + NORM_EPS) * g_ref[...]
        h_sc[...] = (y * sc_ref[...] + sh_ref[...]).astype(BF16)

    o_ref[...] = jnp.dot(h_sc[...], w_ref[...], preferred_element_type=F32).astype(o_ref.dtype)


def inproj(x, g, sc1, sh, w, tn):
    r, d = x.shape
    n = w.shape[1]
    tm = min(r, 1024)
    vec = pl.BlockSpec((1, d), lambda i, j: (0, 0))
    return pl.pallas_call(
        _inproj_kernel,
        grid=(r // tm, n // tn),
        in_specs=[pl.BlockSpec((tm, d), lambda i, j: (i, 0)), vec, vec, vec,
                  pl.BlockSpec((d, tn), lambda i, j: (0, j))],
        out_specs=pl.BlockSpec((tm, tn), lambda i, j: (i, j)),
        out_shape=jax.ShapeDtypeStruct((r, n), BF16),
        scratch_shapes=[pltpu.VMEM((tm, d), BF16)],
        compiler_params=_params(("parallel", "arbitrary")),
        name="inproj",
    )(x, g, sc1, sh, w)


def _mm_kernel(x_ref, w_ref, o_ref):
    o_ref[...] = jnp.dot(x_ref[...], w_ref[...], preferred_element_type=F32)


def matmul(x, w):
    r, k = x.shape
    n = w.shape[1]
    tm = min(r, 1024)
    return pl.pallas_call(
        _mm_kernel,
        grid=(r // tm,),
        in_specs=[pl.BlockSpec((tm, k), lambda i: (i, 0)),
                  pl.BlockSpec((k, n), lambda i: (0, 0))],
        out_specs=pl.BlockSpec((tm, n), lambda i: (i, 0)),
        out_shape=jax.ShapeDtypeStruct((r, n), F32),
        compiler_params=_params(("parallel",)),
        name="matmul",
    )(x, w)


FLASH_V_PAD = 16
FLASH_SAFE_SHIFT = 60.0


def _flash_kernel(qT_ref, k_ref, vT_ref, o_ref, acc_sc, *, n_comp, n_split, bk, n_kb, dv, online):
    bq = qT_ref.shape[-1] // n_split
    chains = [(c, h) for c in range(n_comp) for h in range(n_split)]
    for i in range(len(chains)):
        acc_sc[i] = jnp.zeros(acc_sc.shape[1:], F32)

    def body(j, carry):
        off = pl.multiple_of(j * bk, bk)
        vb = vT_ref[0, :, pl.ds(off, bk)]
        out = []
        scores = []
        for c, h in chains:
            kb = k_ref[0, c, pl.ds(off, bk), :]
            qT = qT_ref[0, c, :, h * bq:(h + 1) * bq]
            scores.append(jnp.dot(kb, qT, preferred_element_type=F32))
        for i, sT in enumerate(scores):
            if online:
                m = carry[i]
                m_new = jnp.maximum(m, jnp.max(sT, axis=0, keepdims=True))
                pT = jnp.exp2(sT - m_new).astype(BF16)
                acc_sc[i] = (jnp.exp2(m - m_new) * acc_sc[i]
                             + jnp.dot(vb, pT, preferred_element_type=F32))
                out.append(m_new)
            else:
                pT = jnp.exp2(sT).astype(BF16)
                acc_sc[i] += jnp.dot(vb, pT, preferred_element_type=F32)
                out.append(carry[i])
        return tuple(out)

    init = tuple(jnp.full((1, bq), NEG_INF, F32) for _ in chains)
    lax.fori_loop(0, n_kb, body, init)
    for i, (c, h) in enumerate(chains):
        acc = acc_sc[i]
        o_ref[0, c, :, h * bq:(h + 1) * bq] = acc[:dv] * (1.0 / acc[dv:dv + 1])


FLASH_SCORE_BYTES = 14 * 1024 * 1024


def _pick_bk(n, n_chains, bq):
    for bk in (3328, 1280, 1024, 768, 640, 512, 384, 256, 128):
        if n % bk == 0 and n_chains * bk * bq * 4 <= FLASH_SCORE_BYTES:
            return bk
    raise ValueError(f"key count {n} must be a multiple of {LANE}")


FLASH_BQ = 1024
FLASH_CHAIN_Q = 512


def _flash_call(qT, k, vT, online):
    g, c, dk, s = qT.shape
    n = k.shape[2]
    dvp = vT.shape[1]
    dv = dvp - FLASH_V_PAD
    bq = min(FLASH_BQ, s)
    n_split = max(1, bq // FLASH_CHAIN_Q)
    bk = _pick_bk(n, c * n_split, bq // n_split)
    kern = functools.partial(_flash_kernel, n_comp=c, n_split=n_split, bk=bk, n_kb=n // bk,
                             dv=dv, online=online)
    return pl.pallas_call(
        kern,
        grid=(g, s // bq),
        in_specs=[pl.BlockSpec((1, c, dk, bq), lambda h, i: (h, 0, 0, i)),
                  pl.BlockSpec((1, c, n, dk), lambda h, i: (h, 0, 0, 0)),
                  pl.BlockSpec((1, dvp, n), lambda h, i: (h, 0, 0))],
        out_specs=pl.BlockSpec((1, c, dv, bq), lambda h, i: (h, 0, 0, i)),
        out_shape=jax.ShapeDtypeStruct((g, c, dv, s), F32),
        scratch_shapes=[pltpu.VMEM((c * n_split, dvp, bq // n_split), F32)],
        compiler_params=_params(("parallel", "arbitrary")),
        name="flash_online" if online else "flash",
    )(qT, k, vT)


def flash(qT, k, vT, u_max):
    return lax.cond(u_max <= FLASH_SAFE_SHIFT,
                    lambda a, b, cc: _flash_call(a, b, cc, False),
                    lambda a, b, cc: _flash_call(a, b, cc, True),
                    qT, k, vT)


def _score_bound(d, q_gain, k_gain, scale):
    return (d * scale * LOG2E * 1.02 * jnp.max(jnp.abs(q_gain.astype(F32)))
            * jnp.max(jnp.abs(k_gain.astype(F32))) + 0.01)


def _augment_qk(q, k, u):
    d = q.shape[-1]
    pad = LANE - d - 1
    neg_u = jnp.broadcast_to(-u, q.shape[:-1] + (1,))
    q_aug = jnp.concatenate([q, neg_u, jnp.zeros(q.shape[:-1] + (pad,), F32)], axis=-1)
    k_aug = jnp.concatenate([k, jnp.ones(k.shape[:-1] + (1,), F32),
                             jnp.zeros(k.shape[:-1] + (pad,), F32)], axis=-1)
    return (jnp.transpose(q_aug, (1, 2, 3, 0)).astype(BF16),
            jnp.transpose(k_aug, (1, 2, 0, 3)).astype(BF16))


def _augment_v(v):
    n, g, _ = v.shape
    v_aug = jnp.concatenate([v.astype(BF16), jnp.ones((n, g, 1), BF16),
                             jnp.zeros((n, g, FLASH_V_PAD - 1), BF16)], axis=-1)
    return jnp.transpose(v_aug, (1, 2, 0))


def _ctx_kernel(q_ref, k_ref, v_ref, sink_ref, o_ref, *, use_sink):
    s = lax.dot_general(q_ref[0], k_ref[0], (((1,), (1,)), ((), ())),
                        preferred_element_type=F32)
    m = jnp.max(s, axis=-1, keepdims=True)
    if use_sink:
        sk = sink_ref[0, 0:1, 0:1]
        m = jnp.maximum(m, sk)
    p = jnp.exp2(s - m)
    l = jnp.sum(p, axis=-1, keepdims=True)
    if use_sink:
        l = l + jnp.exp2(sk - m)
    o = jnp.dot(p.astype(BF16), v_ref[0], preferred_element_type=F32)
    o_ref[0] = o * (1.0 / l)


def ctx_attn(q, k, v, sink=None):
    h, l, dk = q.shape
    hk, hv, dv = k.shape[0], v.shape[0], v.shape[2]
    use_sink = sink is not None
    if sink is None:
        sink = jnp.zeros((h,), F32)
    sink3 = jnp.broadcast_to(sink.astype(F32)[:, None, None], (h, 8, LANE))
    kern = functools.partial(_ctx_kernel, use_sink=use_sink)
    return pl.pallas_call(
        kern,
        grid=(h,),
        in_specs=[pl.BlockSpec((1, l, dk), lambda i: (i, 0, 0)),
                  pl.BlockSpec((1, l, dk), lambda i: (i // (h // hk), 0, 0)),
                  pl.BlockSpec((1, l, dv), lambda i: (i // (h // hv), 0, 0)),
                  pl.BlockSpec((1, 8, LANE), lambda i: (i, 0, 0))],
        out_specs=pl.BlockSpec((1, l, dv), lambda i: (i, 0, 0)),
        out_shape=jax.ShapeDtypeStruct((h, l, dv), F32),
        compiler_params=_params(("parallel",)),
        name="ctx_attn",
    )(q, k, v, sink3)


LOCAL_BQ = 512


def _pad_rows(qT_h, slot, n_slots):
    z = jnp.zeros_like(qT_h)
    return jnp.concatenate([qT_h if s == slot else z for s in range(n_slots)], axis=0)


def _softmax_pv(s_list, v_list, extra, exact_max):
    if exact_max:
        m = functools.reduce(jnp.maximum, [jnp.max(s, axis=0, keepdims=True) for s in s_list])
        if extra is not None:
            m = jnp.maximum(m, extra)
            extra = extra - m
        s_list = [s - m for s in s_list]
    p_list = [jnp.exp2(s) for s in s_list]
    l = functools.reduce(jnp.add, [jnp.sum(p, axis=0, keepdims=True) for p in p_list])
    if extra is not None:
        l = l + jnp.exp2(extra)
    acc = functools.reduce(jnp.add, [jnp.dot(v, p.astype(BF16), preferred_element_type=F32)
                                     for v, p in zip(v_list, p_list)])
    return acc, l


def _wa_kernel(qT_ref, k_ref, vT_ref, kc_ref, vcT_ref, sh_ref, o_ref, *, seq, n_heads, n_kv,
               exact_max):
    i = pl.program_id(0)
    bq = qT_ref.shape[1]
    win = bq + 2 * WINDOW
    grp = n_heads // n_kv
    start = pl.multiple_of(jnp.clip(i * bq - WINDOW, 0, seq - win), WINDOW)
    kw = k_ref[pl.ds(start, win), :]
    vwT = vT_ref[:, pl.ds(start, win)]
    kc = kc_ref[...]
    vcT = vcT_ref[...]
    u = sh_ref[n_heads:n_heads + 1, 0:1]
    kpos = start + lax.broadcasted_iota(jnp.int32, (win, bq), 0)
    qpos = i * bq + lax.broadcasted_iota(jnp.int32, (win, bq), 1)
    mask_shift = jnp.where(jnp.abs(qpos - kpos) <= WINDOW, -u, NEG_INF)
    for h in range(n_heads):
        g = h // grp
        qTp = _pad_rows(qT_ref[h * HEAD_DIM:(h + 1) * HEAD_DIM, :], g, n_kv)
        s_loc = jnp.dot(kw, qTp, preferred_element_type=F32) + mask_shift
        s_ctx = jnp.dot(kc, qTp, preferred_element_type=F32) - u
        acc, l = _softmax_pv([s_loc, s_ctx], [vwT, vcT], sh_ref[h:h + 1, 0:1], exact_max)
        o_ref[h * HEAD_DIM:(h + 1) * HEAD_DIM, :] = (acc[g * HEAD_DIM:(g + 1) * HEAD_DIM]
                                                     * (1.0 / l))


def _wa_call(qT, k, vT, kc, vcT, sh, exact_max):
    hd, s = qT.shape
    kw = k.shape[1]
    l = kc.shape[0]
    bq = min(LOCAL_BQ, s)
    assert s % bq == 0 and s >= bq + 2 * WINDOW
    kern = functools.partial(_wa_kernel, seq=s, n_heads=hd // HEAD_DIM, n_kv=kw // HEAD_DIM,
                             exact_max=exact_max)
    full = lambda i: (0, 0)
    return pl.pallas_call(
        kern,
        grid=(s // bq,),
        in_specs=[pl.BlockSpec((hd, bq), lambda i: (0, i)),
                  pl.BlockSpec((s, kw), full), pl.BlockSpec((kw, s), full),
                  pl.BlockSpec((l, kw), full), pl.BlockSpec((kw, l), full),
                  pl.BlockSpec(sh.shape, full)],
        out_specs=pl.BlockSpec((hd, bq), lambda i: (0, i)),
        out_shape=jax.ShapeDtypeStruct((hd, s), F32),
        compiler_params=_params(("parallel",)),
        name="wa_attn_max" if exact_max else "wa_attn",
    )(qT, k, vT, kc, vcT, sh)


def wa_attn(qT, k, vT, kc, vcT, sink, u):
    n_heads = qT.shape[0] // HEAD_DIM
    rows = jnp.concatenate([sink.astype(F32) - u, jnp.reshape(u, (1,)).astype(F32),
                            jnp.zeros((2 * 8 - n_heads - 1,), F32)])
    sh = jnp.broadcast_to(rows[:, None], (rows.shape[0], LANE))
    return lax.cond(u <= FLASH_SAFE_SHIFT,
                    lambda *a: _wa_call(*a, False), lambda *a: _wa_call(*a, True),
                    qT, k, vT, kc, vcT, sh)


NA_KEY_ROWS = 3 * NA_KH


def _na_kernel(qT_ref, km_ref, k0_ref, kp_ref, vm_ref, v0_ref, vp_ref, kc_ref, vcT_ref,
               bias_ref, sh_ref, o_ref, *, exact_max):
    kwin = jnp.concatenate([km_ref[...], k0_ref[...], kp_ref[...]], axis=0)
    vwinT = jnp.concatenate([vm_ref[...], v0_ref[...], vp_ref[...]], axis=1)
    kc = kc_ref[...]
    vcT = vcT_ref[...]
    u = sh_ref[0:1, 0:1]
    for hh in range(2):
        rows = slice(hh * HEAD_DIM, (hh + 1) * HEAD_DIM)
        qTp = _pad_rows(qT_ref[rows, :], hh, 2)
        s_loc = jnp.dot(kwin, qTp, preferred_element_type=F32) + bias_ref[0, hh]
        s_ctx = jnp.dot(kc, qTp, preferred_element_type=F32) - u
        acc, l = _softmax_pv([s_loc, s_ctx], [vwinT, vcT], None, exact_max)
        o_ref[rows, :] = acc[rows] * (1.0 / l)


def na_bias_table(rpb, rows, shift):
    n_h = rpb.shape[0]
    a = jnp.arange(NA_KH)
    b = jnp.arange(NA_KEY_ROWS)
    c = jnp.arange(GRID_W)
    kc = jnp.arange(GRID_W)
    c0 = jnp.clip(c - NA_KW // 2, 0, GRID_W - NA_KW)
    col_ok = (kc[None, :] >= c0[:, None]) & (kc[None, :] < c0[:, None] + NA_KW)
    col_rel = jnp.clip(kc[None, :] - c[:, None] + (NA_KW - 1), 0, 2 * NA_KW - 2)
    row_rel = jnp.clip(b[None, :] - a[:, None] - 1, 0, 2 * NA_KH - 2)

    def edge_ok(r_base):
        r0 = jnp.clip(r_base + a - NA_KH // 2, 0, rows - NA_KH)
        key_row = r_base - NA_KH + b
        return (key_row[None, :] >= r0[:, None]) & (key_row[None, :] < r0[:, None] + NA_KH)

    inner_ok = ((b[None, :] >= a[:, None] + NA_KH // 2)
                & (b[None, :] < a[:, None] + NA_KH // 2 + NA_KH))
    row_ok = jnp.stack([edge_ok(0), inner_ok, edge_ok(rows - NA_KH)])
    e_c = jax.nn.one_hot(col_rel, 2 * NA_KW - 1, dtype=F32)
    e_r = jax.nn.one_hot(row_rel, 2 * NA_KH - 1, dtype=F32)
    hi = lax.Precision.HIGHEST
    t = jnp.einsum('hrx,ckx->hrck', rpb.astype(F32), e_c, precision=hi)
    t = jnp.einsum('abr,hrck->habck', e_r, t, precision=hi) * LOG2E - shift
    ok = row_ok[:, None, :, :, None, None] & col_ok[None, None, None, None]
    t = jnp.where(ok, t[None], NEG_INF)
    t = jnp.transpose(t, (0, 1, 3, 5, 2, 4))
    return t.reshape(3, n_h, NA_KEY_ROWS * GRID_W, NA_KH * GRID_W)


def _na_call(qT, k, vT, kc, vcT, bias, sh, exact_max):
    hd, s = qT.shape
    l = kc.shape[0]
    bq = NA_KH * GRID_W
    nb = s // bq
    pw = 2 * HEAD_DIM
    assert s % bq == 0 and bq == LOCAL_BQ
    prev = lambda i: jnp.maximum(i - 1, 0)
    nxt = lambda i: jnp.minimum(i + 1, nb - 1)
    var = lambda i: jnp.where(i == 0, 0, jnp.where(i == nb - 1, 2, 1))
    kspec = lambda f: pl.BlockSpec((bq, pw), lambda p, i: (f(i), p))
    vspec = lambda f: pl.BlockSpec((pw, bq), lambda p, i: (p, f(i)))
    cur = lambda i: i
    kern = functools.partial(_na_kernel, exact_max=exact_max)
    return pl.pallas_call(
        kern,
        grid=(hd // pw, nb),
        in_specs=[pl.BlockSpec((pw, bq), lambda p, i: (p, i)),
                  kspec(prev), kspec(cur), kspec(nxt), vspec(prev), vspec(cur), vspec(nxt),
                  pl.BlockSpec((l, pw), lambda p, i: (0, p)),
                  pl.BlockSpec((pw, l), lambda p, i: (p, 0)),
                  pl.BlockSpec((1, 2, NA_KEY_ROWS * GRID_W, bq), lambda p, i: (var(i), p, 0, 0)),
                  pl.BlockSpec(sh.shape, lambda p, i: (0, 0))],
        out_specs=pl.BlockSpec((pw, bq), lambda p, i: (p, i)),
        out_shape=jax.ShapeDtypeStruct((hd, s), F32),
        compiler_params=_params(("parallel", "arbitrary")),
        name="na_attn_max" if exact_max else "na_attn",
    )(qT, k, k, k, vT, vT, vT, kc, vcT, bias, sh)


def na_attn(qT, k, vT, kc, vcT, rpb, u_qk):
    s = qT.shape[1]
    u = u_qk + jnp.maximum(jnp.max(rpb.astype(F32)) * LOG2E, 0.0)
    bias = na_bias_table(rpb, s // GRID_W, u)
    sh = jnp.full((8, LANE), u, F32)
    return lax.cond(u <= FLASH_SAFE_SHIFT,
                    lambda *a: _na_call(*a, False), lambda *a: _na_call(*a, True),
                    qT, k, vT, kc, vcT, bias, sh)


def _outproj_kernel(o_ref, x_ref, w_ref, gate_ref, g_ref, sc_ref, sh_ref, wr_ref,
                    xo_ref, hf_ref, lg_ref):
    acc = jnp.dot(o_ref[...], w_ref[...], preferred_element_type=F32)
    xn = x_ref[...] + gate_ref[...] * acc
    xo_ref[...] = xn
    ms = jnp.mean(xn * xn, axis=-1, keepdims=True)
    y = xn * lax.rsqrt(ms + NORM_EPS) * g_ref[...]
    hf = (y * sc_ref[...] + sh_ref[...]).astype(BF16)
    hf_ref[...] = hf
    lg_ref[...] = jnp.dot(hf, wr_ref[...], preferred_element_type=F32)


def outproj(o, x, w, gate, g, sc1, sh, wr):
    r, d = x.shape
    k = o.shape[1]
    tm = min(r, 256)
    vec = pl.BlockSpec((1, d), lambda i: (0, 0))
    row = lambda n: pl.BlockSpec((tm, n), lambda i: (i, 0))
    return pl.pallas_call(
        _outproj_kernel,
        grid=(r // tm,),
        in_specs=[row(k), row(d), pl.BlockSpec((k, d), lambda i: (0, 0)), vec, vec, vec, vec,
                  pl.BlockSpec((d, LANE), lambda i: (0, 0))],
        out_specs=[row(d), row(d), row(LANE)],
        out_shape=[jax.ShapeDtypeStruct((r, d), F32), jax.ShapeDtypeStruct((r, d), BF16),
                   jax.ShapeDtypeStruct((r, LANE), F32)],
        compiler_params=_params(("parallel",)),
        name="outproj",
    )(o, x, w, gate, g, sc1, sh, wr)


def _moe_kernel(be_ref, nv_ref, x_ref, w1_ref, w3_ref, w2_ref, o_ref):
    i = pl.program_id(0)

    @pl.when(i < nv_ref[0])
    def _():
        x = x_ref[...]
        a = jnp.dot(x, w1_ref[0], preferred_element_type=F32)
        b = jnp.dot(x, w3_ref[0], preferred_element_type=F32)
        hmid = (a * _sigmoid(a) * b).astype(BF16)
        o_ref[...] = jnp.dot(hmid, w2_ref[0], preferred_element_type=F32)

    @pl.when(i >= nv_ref[0])
    def _():
        o_ref[...] = jnp.zeros(o_ref.shape, o_ref.dtype)


def moe_blocks(x, w1, w3, w2, blk_e, n_valid):
    n, d = x.shape
    f = w1.shape[2]
    n_blk = n // MOE_BLOCK
    grid_spec = pltpu.PrefetchScalarGridSpec(
        num_scalar_prefetch=2,
        grid=(n_blk,),
        in_specs=[pl.BlockSpec((MOE_BLOCK, d), lambda i, be, nv: (i, 0)),
                  pl.BlockSpec((1, d, f), lambda i, be, nv: (be[i], 0, 0)),
                  pl.BlockSpec((1, d, f), lambda i, be, nv: (be[i], 0, 0)),
                  pl.BlockSpec((1, f, d), lambda i, be, nv: (be[i], 0, 0))],
        out_specs=pl.BlockSpec((MOE_BLOCK, d), lambda i, be, nv: (i, 0)),
    )
    return pl.pallas_call(
        _moe_kernel,
        grid_spec=grid_spec,
        out_shape=jax.ShapeDtypeStruct((n, d), F32),
        compiler_params=_params(("arbitrary",)),
        name="moe_blocks",
    )(blk_e, n_valid, x, w1, w3, w2)


def _rms(x, g):
    y = x * lax.rsqrt(jnp.mean(x * x, axis=-1, keepdims=True) + NORM_EPS)
    return y * g


def _rope_tables(seq_len, rot_dim):
    n = rot_dim // 4
    t = jnp.arange(seq_len)
    row = (t // GRID_W).astype(F32)[:, None]
    col = (t % GRID_W).astype(F32)[:, None]
    inv = ROPE_THETA ** (-jnp.arange(n, dtype=F32) / n)
    return (jnp.cos(row * inv), jnp.sin(row * inv), jnp.cos(col * inv), jnp.sin(col * inv))


def _rope(x, tabs):
    expand = (slice(None),) + (None,) * (x.ndim - 2) + (slice(None),)
    cr, sr, cc, sc = (t[expand] for t in tabs)
    x1, x2, x3, x4 = jnp.split(x, 4, axis=-1)
    return jnp.concatenate([x1 * cr - x2 * sr, x2 * cr + x1 * sr,
                            x3 * cc - x4 * sc, x4 * cc + x3 * sc], axis=-1)


def _heads_first(t):
    return jnp.transpose(t, (1, 0, 2))


def _route(logits, b_router):
    n = logits.shape[0]
    per = N_EXPERTS // N_EXPERT_GROUPS
    scores = jax.nn.sigmoid(logits.astype(F32))
    sel = scores + b_router.astype(F32)
    grp = sel.reshape(n, N_EXPERT_GROUPS, per)
    gscore = None
    for a in range(per):
        for bb in range(a + 1, per):
            pair = grp[..., a] + grp[..., bb]
            gscore = pair if gscore is None else jnp.maximum(gscore, pair)
    gidx = jnp.argmax(gscore, axis=-1)
    eids = jnp.arange(N_EXPERTS)
    masked = jnp.where((eids // per)[None, :] == gidx[:, None], sel, -jnp.inf)
    e1 = jnp.argmax(masked, axis=-1)
    e2 = jnp.argmax(jnp.where(eids[None, :] == e1[:, None], -jnp.inf, masked), axis=-1)
    eidx = jnp.stack([e1, e2], axis=1).astype(jnp.int32)
    wts = jnp.take_along_axis(scores, eidx, axis=1)
    wts = wts / jnp.sum(wts, axis=-1, keepdims=True) * ROUTED_SCALE
    return eidx, wts


def _prefix_counts(onehot):
    n, e = onehot.shape
    ch = onehot.astype(F32).reshape(n // LANE, LANE, e)
    tri = jnp.tril(jnp.ones((LANE, LANE), F32))
    within = jnp.einsum('ij,cjk->cik', tri, ch)
    tot = within[:, -1, :]
    base = jnp.cumsum(tot, axis=0) - tot
    return (within + base[:, None, :]).reshape(n, e)


def _dispatch(eidx):
    n = eidx.shape[0]
    n_assign = n * TOP_K
    assert n_assign % LANE == 0
    e_flat = eidx.reshape(-1)
    tok = jnp.repeat(jnp.arange(n, dtype=jnp.int32), TOP_K)
    onehot = e_flat[:, None] == jnp.arange(N_EXPERTS)[None, :]
    csum = _prefix_counts(onehot)
    counts = csum[-1].astype(jnp.int32)
    rank = jnp.sum(jnp.where(onehot, csum, 0.0), axis=1).astype(jnp.int32) - 1
    padded = (counts + MOE_BLOCK - 1) // MOE_BLOCK * MOE_BLOCK
    pend = jnp.cumsum(padded)
    pstart = pend - padded
    dest = jnp.sum(jnp.where(onehot, pstart[None, :], 0), axis=1) + rank
    n_slots = (n_assign + N_EXPERTS * (MOE_BLOCK - 1) + MOE_BLOCK - 1) // MOE_BLOCK * MOE_BLOCK
    n_blk = n_slots // MOE_BLOCK
    slot_tok = jnp.full((n_slots,), n, dtype=jnp.int32).at[dest].set(tok)
    slot_of = dest.reshape(n, TOP_K)
    blk_start = jnp.arange(n_blk, dtype=jnp.int32) * MOE_BLOCK
    blk_e = jnp.minimum(jnp.sum(blk_start[:, None] >= pend[None, :], axis=1), N_EXPERTS - 1)
    n_valid = (pend[-1] // MOE_BLOCK).reshape(1)
    return slot_tok, slot_of, blk_e.astype(jnp.int32), n_valid.astype(jnp.int32)


def _na_prep(p, pc, q_gain, k_gain):
    h, dh = NA_HEADS, HEAD_DIM
    w = h * dh
    qs = dh ** -0.5 * LOG2E

    def nrm(t, g):
        return _rms(t.astype(F32).reshape(t.shape[0], h, dh), g)

    q = (nrm(p[:, :w], q_gain) * qs).reshape(-1, w).astype(BF16)
    k = nrm(p[:, w:2 * w], k_gain).reshape(-1, w).astype(BF16)
    qc = (nrm(pc[:, :w], q_gain) * qs).astype(BF16)
    kc = nrm(pc[:, w:2 * w], k_gain).astype(BF16)
    vc = pc[:, 2 * w:]
    lat = (q.T, k, p[:, 2 * w:].T, kc.reshape(-1, w), vc.T)
    ctx = (_heads_first(qc), _heads_first(kc), _heads_first(vc.reshape(-1, h, dh)))
    return lat, ctx


def _wa_prep(p, pc, q_gain, k_gain, rope):
    hq, hk, dh = WA_HEADS, WA_KV_HEADS, HEAD_DIM
    nq, nk = hq * dh, hk * dh
    qs = dh ** -0.5 * LOG2E
    s, l = p.shape[0], pc.shape[0]
    q = (_rope(_rms(p[:, :nq].astype(F32).reshape(s, hq, dh), q_gain), rope) * qs)
    k = _rope(_rms(p[:, nq:nq + nk].astype(F32).reshape(s, hk, dh), k_gain), rope)
    qc = (_rms(pc[:, :nq].astype(F32).reshape(l, hq, dh), q_gain) * qs).astype(BF16)
    kc = _rms(pc[:, nq:nq + nk].astype(F32).reshape(l, hk, dh), k_gain).astype(BF16)
    vc = pc[:, nq + nk:]
    lat = (q.reshape(s, nq).astype(BF16).T, k.reshape(s, nk).astype(BF16), p[:, nq + nk:].T,
           kc.reshape(l, nk), vc.T)
    ctx = (_heads_first(qc), _heads_first(kc), _heads_first(vc.reshape(l, hk, dh)))
    return lat, ctx


def _diff_prep(p, pc, q_gain, k_gain, rope):
    h, d = DIFF_HEADS, DIFF_DIM
    w = h * 2 * d
    qs = d ** -0.5 * LOG2E
    s, l = p.shape[0], pc.shape[0]
    q = _rope(_rms(p[:, :w].astype(F32).reshape(s, h, 2, d), q_gain), rope) * qs
    k = _rope(_rms(p[:, w:2 * w].astype(F32).reshape(s, h, 2, d), k_gain), rope)
    qc = _rms(pc[:, :w].astype(F32).reshape(l, h, 2, d), q_gain) * qs
    kc = _rms(pc[:, w:2 * w].astype(F32).reshape(l, h, 2, d), k_gain)
    v = p[:, 2 * w:].reshape(s, h, DIFF_V_DIM)
    vc = pc[:, 2 * w:].reshape(l, h, DIFF_V_DIM)
    u_max = _score_bound(d, q_gain, k_gain, d ** -0.5)
    qT, k_aug = _augment_qk(q, jnp.concatenate([kc, k], axis=0), u_max)
    vT = _augment_v(jnp.concatenate([vc, v], axis=0))
    qc_h = jnp.transpose(qc, (1, 2, 0, 3)).reshape(2 * h, l, d).astype(BF16)
    kc_h = jnp.transpose(kc, (1, 2, 0, 3)).reshape(2 * h, l, d).astype(BF16)
    vc_h = _heads_first(vc).astype(BF16)
    return qT, k_aug, vT, u_max, qc_h, kc_h, vc_h


def _mla_prep(p, pc, qa_gain, kva_gain, w_uq, w_ukv, q_gain, k_gain, rope):
    hh, dn, dr, dv = MLA_HEADS, MLA_NOPE, MLA_ROPE, MLA_V
    dq = dn + dr
    qs = dq ** -0.5 * LOG2E
    pad = LANE - dq

    def project(t, tabs):
        n = t.shape[0]
        t = t.astype(F32)
        cq = _rms(t[:, :MLA_Q_LORA], qa_gain).astype(BF16)
        ckv = _rms(t[:, MLA_Q_LORA:MLA_Q_LORA + MLA_KV_LORA], kva_gain).astype(BF16)
        k_rope = t[:, MLA_Q_LORA + MLA_KV_LORA:]
        q = matmul(cq, w_uq).reshape(n, hh, dq)
        kv = matmul(ckv, w_ukv).reshape(n, hh, dn + dv)
        k = jnp.concatenate([kv[..., :dn], jnp.broadcast_to(k_rope[:, None], (n, hh, dr))], -1)
        q = _rms(q, q_gain)
        k = _rms(k, k_gain)
        if tabs is not None:
            q = jnp.concatenate([q[..., :dn], _rope(q[..., dn:], tabs)], axis=-1)
            k = jnp.concatenate([k[..., :dn], _rope(k[..., dn:], tabs)], axis=-1)
        return q * qs, k, kv[..., dn:]

    q, k, v = project(p, rope)
    qc, kc, vc = project(pc, None)
    u_max = _score_bound(dq, q_gain, k_gain, dq ** -0.5)
    qT, k_aug = _augment_qk(q[:, :, None], jnp.concatenate([kc, k], axis=0)[:, :, None], u_max)
    vT = _augment_v(jnp.concatenate([vc, v], axis=0))
    padw = ((0, 0), (0, 0), (0, pad))
    return (qT, k_aug, vT, u_max, _heads_first(jnp.pad(qc, padw)).astype(BF16),
            _heads_first(jnp.pad(kc, padw)).astype(BF16), _heads_first(vc).astype(BF16))


def _merge(o):
    return jnp.transpose(o, (1, 0, 2)).reshape(o.shape[1], -1)


def _diff_post(o1, o2, lam, sub_gain, lambda_init):
    return _rms(o1 - lam * o2, sub_gain) * (1.0 - lambda_init)


def _moe(tokens_bf16, logits, b_router, w1, w3, w2, sw1, sw3, sw2):
    n, d = tokens_bf16.shape
    eidx, gate = _route(logits[:, :N_EXPERTS], b_router)
    slot_tok, slot_of, blk_e, n_valid = _dispatch(eidx)
    h_pad = jnp.concatenate([tokens_bf16, jnp.zeros((1, d), BF16)], axis=0)
    xb = h_pad[slot_tok]
    yb = moe_blocks(xb, w1, w3, w2, blk_e, n_valid)
    y0, y1 = lax.optimization_barrier((yb[slot_of[:, 0]], yb[slot_of[:, 1]]))
    routed = y0 * gate[:, 0:1] + y1 * gate[:, 1:2]
    n_sh = _round_up(n, MOE_BLOCK) // MOE_BLOCK
    shared = moe_blocks(tokens_bf16, sw1[None], sw3[None], sw2[None],
                        jnp.zeros((n_sh,), jnp.int32), jnp.full((1,), n_sh, jnp.int32))
    return routed + shared


def kernel(x, c, ctx, c_ctx, w_ada, b_ada, g_attn, g_ffn, w_in, w_out, na_q_gain, na_k_gain, na_rpb, wa_q_gain, wa_k_gain, wa_sink, diff_q_gain, diff_k_gain, diff_lq1, diff_lk1, diff_lq2, diff_lk2, diff_sub_gain, mla_qa_gain, mla_kva_gain, mla_w_uq, mla_w_ukv, mla_q_gain, mla_k_gain, w_router, b_router, moe_w1, moe_w3, moe_w2, sh_w1, sh_w3, sh_w2):
    b, s, d = x.shape
    assert b == 1
    n_ctx = ctx.shape[1]
    depth = w_ada.shape[0]
    xl = x[0]
    xc = ctx[0]
    rope_head = _rope_tables(s, HEAD_DIM)
    rope_mla = _rope_tables(s, MLA_ROPE)

    cond8 = jnp.zeros((8, d), F32).at[0].set(c[0]).at[1].set(c_ctx)
    mod = modvec(cond8, w_ada, b_ada)
    in_pad = _round_up(IN_COLS, 640)
    tn = 640
    wr = jnp.pad(w_router, ((0, 0), (0, LANE - N_EXPERTS))).astype(BF16)

    for l in range(depth):
        with_ctx = l < depth - 1
        m_lat = mod[l, 0].reshape(6, 1, d)
        m_ctx = mod[l, 1].reshape(6, 1, d)
        w_in_l = jnp.pad(w_in[l], ((0, 0), (0, in_pad - IN_COLS))).astype(BF16)
        w_out_l = w_out[l].astype(BF16)
        g_a = g_attn[l][None]
        g_f = g_ffn[l][None]

        p = inproj(xl, g_a, 1.0 + m_lat[1], m_lat[0], w_in_l, tn)
        pc = inproj(xc, g_a, 1.0 + m_ctx[1], m_ctx[0], w_in_l, tn)

        lat, cq = _na_prep(p[:, NA_OFF:WA_OFF], pc[:, NA_OFF:WA_OFF], na_q_gain[l], na_k_gain[l])
        u_na = _score_bound(HEAD_DIM, na_q_gain[l], na_k_gain[l], HEAD_DIM ** -0.5)
        o_na = na_attn(*lat, na_rpb[l], u_na).T
        oc_na = _merge(ctx_attn(*cq)) if with_ctx else None

        lat, cq = _wa_prep(p[:, WA_OFF:DIFF_OFF], pc[:, WA_OFF:DIFF_OFF],
                           wa_q_gain[l], wa_k_gain[l], rope_head)
        u_wa = _score_bound(HEAD_DIM, wa_q_gain[l], wa_k_gain[l], HEAD_DIM ** -0.5)
        sink2 = wa_sink[l].astype(F32) * LOG2E
        o_wa = wa_attn(*lat, sink2, u_wa).T
        oc_wa = _merge(ctx_attn(*cq, sink2)) if with_ctx else None

        lambda_init = 0.8 - 0.6 * math.exp(-0.3 * l)
        lam = (jnp.exp(jnp.sum(diff_lq1[l].astype(F32) * diff_lk1[l].astype(F32)))
               - jnp.exp(jnp.sum(diff_lq2[l].astype(F32) * diff_lk2[l].astype(F32))) + lambda_init)
        qT, k_aug, vT, u_max, qc, kc, vc = _diff_prep(
            p[:, DIFF_OFF:MLA_OFF], pc[:, DIFF_OFF:MLA_OFF], diff_q_gain[l], diff_k_gain[l],
            rope_head)
        oT = flash(qT, k_aug, vT, u_max)
        o_df = _diff_post(jnp.transpose(oT[:, 0], (2, 0, 1)), jnp.transpose(oT[:, 1], (2, 0, 1)),
                          lam, diff_sub_gain[l], lambda_init).reshape(s, -1)
        oc_df = None
        if with_ctx:
            oc = ctx_attn(qc, kc, vc).reshape(DIFF_HEADS, 2, n_ctx, DIFF_V_DIM)
            oc_df = _diff_post(jnp.transpose(oc[:, 0], (1, 0, 2)), jnp.transpose(oc[:, 1], (1, 0, 2)),
                               lam, diff_sub_gain[l], lambda_init).reshape(n_ctx, -1)

        qT, k_aug, vT, u_max, qc, kc, vc = _mla_prep(
            p[:, MLA_OFF:IN_COLS], pc[:, MLA_OFF:IN_COLS], mla_qa_gain[l], mla_kva_gain[l],
            mla_w_uq[l].astype(BF16), mla_w_ukv[l].astype(BF16), mla_q_gain[l], mla_k_gain[l],
            rope_mla)
        oT = flash(qT, k_aug, vT, u_max)
        o_ml = jnp.transpose(oT[:, 0], (2, 0, 1)).reshape(s, -1)
        oc_ml = _merge(ctx_attn(qc, kc, vc)) if with_ctx else None

        o_cat = jnp.concatenate([o_na, o_wa, o_df, o_ml], axis=-1).astype(BF16)
        xl, hf, lg = outproj(o_cat, xl, w_out_l, m_lat[2], g_f, 1.0 + m_lat[4], m_lat[3], wr)
        if with_ctx:
            oc_cat = jnp.concatenate([oc_na, oc_wa, oc_df, oc_ml], axis=-1).astype(BF16)
            xc, hfc, lgc = outproj(oc_cat, xc, w_out_l, m_ctx[2], g_f, 1.0 + m_ctx[4], m_ctx[3], wr)
            tokens = jnp.concatenate([hfc, hf], axis=0)
            logits = jnp.concatenate([lgc, lg], axis=0)
        else:
            tokens, logits = hf, lg
        y = _moe(tokens, logits, b_router, moe_w1[l].astype(BF16), moe_w3[l].astype(BF16),
                 moe_w2[l].astype(BF16), sh_w1[l].astype(BF16), sh_w3[l].astype(BF16),
                 sh_w2[l].astype(BF16))
        if with_ctx:
            xc = xc + m_ctx[5] * y[:n_ctx]
            y = y[n_ctx:]
        xl = xl + m_lat[5] * y
    return xl[None]
```

```python
import functools
import math

import jax
import jax.numpy as jnp
from jax import lax
from jax.experimental import pallas as pl
from jax.experimental.pallas import tpu as pltpu

F32 = jnp.float32
BF16 = jnp.bfloat16

GRID_W = 64
HEAD_DIM = 64
ROPE_THETA = 10000.0
NORM_EPS = 1e-6
NEG_INF = -1e30
Q_BLOCK = 128
WINDOW = 128
NA_HEADS = 8
NA_KH = 8
NA_KW = 16
WA_HEADS = 8
WA_KV_HEADS = 2
DIFF_HEADS = 4
DIFF_DIM = 64
DIFF_V_DIM = 128
MLA_HEADS = 8
MLA_NOPE = 64
MLA_ROPE = 32
MLA_V = 64
MLA_Q_LORA = 384
MLA_KV_LORA = 128
N_EXPERTS = 16
N_EXPERT_GROUPS = 4
TOP_K = 2
ROUTED_SCALE = 1.0
MOE_BLOCK = 256

NA_COLS = 3 * NA_HEADS * HEAD_DIM
WA_COLS = (WA_HEADS + 2 * WA_KV_HEADS) * HEAD_DIM
DIFF_COLS = 3 * DIFF_HEADS * 2 * DIFF_DIM
MLA_COLS = MLA_Q_LORA + MLA_KV_LORA + MLA_ROPE
NA_OFF = 0
WA_OFF = NA_OFF + NA_COLS
DIFF_OFF = WA_OFF + WA_COLS
MLA_OFF = DIFF_OFF + DIFF_COLS
IN_COLS = MLA_OFF + MLA_COLS

LANE = 128
INPROJ_TN = 768
LOG2E = math.log2(math.e)
VMEM_LIMIT = 48 * 1024 * 1024


def _round_up(n, m):
    return (n + m - 1) // m * m


def _params(sem):
    return pltpu.CompilerParams(dimension_semantics=sem, vmem_limit_bytes=VMEM_LIMIT)


def _sigmoid(x):
    return 1.0 / (1.0 + jnp.exp(-x))


def _modvec_kernel(c_ref, w_ref, b_ref, o_ref):
    a = c_ref[...]
    a = a * _sigmoid(a)
    o_ref[0] = jnp.dot(a, w_ref[0], preferred_element_type=F32,
                       precision=lax.Precision.HIGHEST) + b_ref[0]


def modvec(cond8, w_ada, b_ada):
    depth, d, n = w_ada.shape
    tn = 1024
    return pl.pallas_call(
        _modvec_kernel,
        grid=(depth, n // tn),
        in_specs=[pl.BlockSpec((8, d), lambda l, j: (0, 0)),
                  pl.BlockSpec((1, d, tn), lambda l, j: (l, 0, j)),
                  pl.BlockSpec((1, 1, tn), lambda l, j: (l, 0, j))],
        out_specs=pl.BlockSpec((1, 8, tn), lambda l, j: (l, 0, j)),
        out_shape=jax.ShapeDtypeStruct((depth, 8, n), F32),
        compiler_params=_params(("parallel", "parallel")),
        name="modvec",
    )(cond8, w_ada, b_ada.reshape(depth, 1, n))


def _inproj_kernel(x_ref, g_ref, sc_ref, sh_ref, w_ref, o_ref, h_sc):
    @pl.when(pl.program_id(1) == 0)
    def _():
        x = x_ref[...]
        ms = jnp.mean(x * x, axis=-1, keepdims=True)
        y = x * lax.rsqrt(ms + NORM_EPS) * g_ref[...]
        h_sc[...] = (y * sc_ref[...] + sh_ref[...]).astype(BF16)

    o_ref[...] = jnp.dot(h_sc[...], w_ref[...], preferred_element_type=F32).astype(o_ref.dtype)


def inproj(x, g, sc1, sh, w, tn):
    r, d = x.shape
    n = w.shape[1]
    tm = min(r, 1024)
    vec = pl.BlockSpec((1, d), lambda i, j: (0, 0))
    return pl.pallas_call(
        _inproj_kernel,
        grid=(r // tm, n // tn),
        in_specs=[pl.BlockSpec((tm, d), lambda i, j: (i, 0)), vec, vec, vec,
                  pl.BlockSpec((d, tn), lambda i, j: (0, j))],
        out_specs=pl.BlockSpec((tm, tn), lambda i, j: (i, j)),
        out_shape=jax.ShapeDtypeStruct((r, n), BF16),
        scratch_shapes=[pltpu.VMEM((tm, d), BF16)],
        compiler_params=_params(("parallel", "arbitrary")),
        name="inproj",
    )(x, g, sc1, sh, w)


def _mm_kernel(x_ref, w_ref, o_ref):
    o_ref[...] = jnp.dot(x_ref[...], w_ref[...], preferred_element_type=F32)


def matmul(x, w):
    r, k = x.shape
    n = w.shape[1]
    tm = min(r, 1024)
    return pl.pallas_call(
        _mm_kernel,
        grid=(r // tm,),
        in_specs=[pl.BlockSpec((tm, k), lambda i: (i, 0)),
                  pl.BlockSpec((k, n), lambda i: (0, 0))],
        out_specs=pl.BlockSpec((tm, n), lambda i: (i, 0)),
        out_shape=jax.ShapeDtypeStruct((r, n), F32),
        compiler_params=_params(("parallel",)),
        name="matmul",
    )(x, w)


FLASH_V_PAD = 16
FLASH_SAFE_SHIFT = 60.0


def _flash_kernel(qT_ref, k_ref, vT_ref, o_ref, acc_sc, *, n_comp, n_split, bk, n_kb, dv, online):
    bq = qT_ref.shape[-1] // n_split
    chains = [(c, h) for c in range(n_comp) for h in range(n_split)]
    for i in range(len(chains)):
        acc_sc[i] = jnp.zeros(acc_sc.shape[1:], F32)

    def body(j, carry):
        off = pl.multiple_of(j * bk, bk)
        vb = vT_ref[0, :, pl.ds(off, bk)]
        out = []
        scores = []
        for c, h in chains:
            kb = k_ref[0, c, pl.ds(off, bk), :]
            qT = qT_ref[0, c, :, h * bq:(h + 1) * bq]
            scores.append(jnp.dot(kb, qT, preferred_element_type=F32))
        for i, sT in enumerate(scores):
            if online:
                m = carry[i]
                m_new = jnp.maximum(m, jnp.max(sT, axis=0, keepdims=True))
                pT = jnp.exp2(sT - m_new).astype(BF16)
                acc_sc[i] = (jnp.exp2(m - m_new) * acc_sc[i]
                             + jnp.dot(vb, pT, preferred_element_type=F32))
                out.append(m_new)
            else:
                pT = jnp.exp2(sT).astype(BF16)
                acc_sc[i] += jnp.dot(vb, pT, preferred_element_type=F32)
                out.append(carry[i])
        return tuple(out)

    init = tuple(jnp.full((1, bq), NEG_INF, F32) for _ in chains)
    lax.fori_loop(0, n_kb, body, init)
    for i, (c, h) in enumerate(chains):
        acc = acc_sc[i]
        o_ref[0, c, :, h * bq:(h + 1) * bq] = acc[:dv] * (1.0 / acc[dv:dv + 1])


FLASH_SCORE_BYTES = 14 * 1024 * 1024


def _pick_bk(n, n_chains, bq):
    for bk in (3328, 1280, 1024, 768, 640, 512, 384, 256, 128):
        if n % bk == 0 and n_chains * bk * bq * 4 <= FLASH_SCORE_BYTES:
            return bk
    raise ValueError(f"key count {n} must be a multiple of {LANE}")


FLASH_BQ = 1024
FLASH_CHAIN_Q = 512


def _flash_call(qT, k, vT, online):
    g, c, dk, s = qT.shape
    n = k.shape[2]
    dvp = vT.shape[1]
    dv = dvp - FLASH_V_PAD
    bq = min(FLASH_BQ, s)
    n_split = max(1, bq // FLASH_CHAIN_Q)
    bk = _pick_bk(n, c * n_split, bq // n_split)
    kern = functools.partial(_flash_kernel, n_comp=c, n_split=n_split, bk=bk, n_kb=n // bk,
                             dv=dv, online=online)
    return pl.pallas_call(
        kern,
        grid=(g, s // bq),
        in_specs=[pl.BlockSpec((1, c, dk, bq), lambda h, i: (h, 0, 0, i)),
                  pl.BlockSpec((1, c, n, dk), lambda h, i: (h, 0, 0, 0)),
                  pl.BlockSpec((1, dvp, n), lambda h, i: (h, 0, 0))],
        out_specs=pl.BlockSpec((1, c, dv, bq), lambda h, i: (h, 0, 0, i)),
        out_shape=jax.ShapeDtypeStruct((g, c, dv, s), F32),
        scratch_shapes=[pltpu.VMEM((c * n_split, dvp, bq // n_split), F32)],
        compiler_params=_params(("parallel", "arbitrary")),
        name="flash_online" if online else "flash",
    )(qT, k, vT)


def flash(qT, k, vT, u_max):
    return lax.cond(u_max <= FLASH_SAFE_SHIFT,
                    lambda a, b, cc: _flash_call(a, b, cc, False),
                    lambda a, b, cc: _flash_call(a, b, cc, True),
                    qT, k, vT)


def _score_bound(d, q_gain, k_gain, scale):
    return (d * scale * LOG2E * 1.02 * jnp.max(jnp.abs(q_gain.astype(F32)))
            * jnp.max(jnp.abs(k_gain.astype(F32))) + 0.01)


def _augment_qk(q, k, u):
    d = q.shape[-1]
    pad = LANE - d - 1
    neg_u = jnp.broadcast_to(-u, q.shape[:-1] + (1,))
    q_aug = jnp.concatenate([q, neg_u, jnp.zeros(q.shape[:-1] + (pad,), F32)], axis=-1)
    k_aug = jnp.concatenate([k, jnp.ones(k.shape[:-1] + (1,), F32),
                             jnp.zeros(k.shape[:-1] + (pad,), F32)], axis=-1)
    return (jnp.transpose(q_aug, (1, 2, 3, 0)).astype(BF16),
            jnp.transpose(k_aug, (1, 2, 0, 3)).astype(BF16))


def _augment_v(v):
    n, g, _ = v.shape
    v_aug = jnp.concatenate([v.astype(BF16), jnp.ones((n, g, 1), BF16),
                             jnp.zeros((n, g, FLASH_V_PAD - 1), BF16)], axis=-1)
    return jnp.transpose(v_aug, (1, 2, 0))


def _ctx_kernel(q_ref, k_ref, v_ref, sink_ref, o_ref, *, use_sink):
    s = lax.dot_general(q_ref[0], k_ref[0], (((1,), (1,)), ((), ())),
                        preferred_element_type=F32)
    m = jnp.max(s, axis=-1, keepdims=True)
    if use_sink:
        sk = sink_ref[0, 0:1, 0:1]
        m = jnp.maximum(m, sk)
    p = jnp.exp2(s - m)
    l = jnp.sum(p, axis=-1, keepdims=True)
    if use_sink:
        l = l + jnp.exp2(sk - m)
    o = jnp.dot(p.astype(BF16), v_ref[0], preferred_element_type=F32)
    o_ref[0] = o * (1.0 / l)


def ctx_attn(q, k, v, sink=None):
    h, l, dk = q.shape
    hk, hv, dv = k.shape[0], v.shape[0], v.shape[2]
    use_sink = sink is not None
    if sink is None:
        sink = jnp.zeros((h,), F32)
    sink3 = jnp.broadcast_to(sink.astype(F32)[:, None, None], (h, 8, LANE))
    kern = functools.partial(_ctx_kernel, use_sink=use_sink)
    return pl.pallas_call(
        kern,
        grid=(h,),
        in_specs=[pl.BlockSpec((1, l, dk), lambda i: (i, 0, 0)),
                  pl.BlockSpec((1, l, dk), lambda i: (i // (h // hk), 0, 0)),
                  pl.BlockSpec((1, l, dv), lambda i: (i // (h // hv), 0, 0)),
                  pl.BlockSpec((1, 8, LANE), lambda i: (i, 0, 0))],
        out_specs=pl.BlockSpec((1, l, dv), lambda i: (i, 0, 0)),
        out_shape=jax.ShapeDtypeStruct((h, l, dv), F32),
        compiler_params=_params(("parallel",)),
        name="ctx_attn",
    )(q, k, v, sink3)


LOCAL_BQ = 512


def _pad_rows(qT_h, slot, n_slots):
    z = jnp.zeros_like(qT_h)
    return jnp.concatenate([qT_h if s == slot else z for s in range(n_slots)], axis=0)


def _softmax_pv(s_list, v_list, extra, exact_max):
    if exact_max:
        m = functools.reduce(jnp.maximum, [jnp.max(s, axis=0, keepdims=True) for s in s_list])
        if extra is not None:
            m = jnp.maximum(m, extra)
            extra = extra - m
        s_list = [s - m for s in s_list]
    p_list = [jnp.exp2(s) for s in s_list]
    l = functools.reduce(jnp.add, [jnp.sum(p, axis=0, keepdims=True) for p in p_list])
    if extra is not None:
        l = l + jnp.exp2(extra)
    acc = functools.reduce(jnp.add, [jnp.dot(v, p.astype(BF16), preferred_element_type=F32)
                                     for v, p in zip(v_list, p_list)])
    return acc, l


def _wa_kernel(qT_ref, k_ref, vT_ref, kc_ref, vcT_ref, sh_ref, o_ref, *, seq, n_heads, n_kv,
               exact_max):
    i = pl.program_id(0)
    bq = qT_ref.shape[1]
    win = bq + 2 * WINDOW
    grp = n_heads // n_kv
    start = pl.multiple_of(jnp.clip(i * bq - WINDOW, 0, seq - win), WINDOW)
    kw = k_ref[pl.ds(start, win), :]
    vwT = vT_ref[:, pl.ds(start, win)]
    kc = kc_ref[...]
    vcT = vcT_ref[...]
    u = sh_ref[n_heads:n_heads + 1, 0:1]
    kpos = start + lax.broadcasted_iota(jnp.int32, (win, bq), 0)
    qpos = i * bq + lax.broadcasted_iota(jnp.int32, (win, bq), 1)
    mask_shift = jnp.where(jnp.abs(qpos - kpos) <= WINDOW, -u, NEG_INF)
    for h in range(n_heads):
        g = h // grp
        qTp = _pad_rows(qT_ref[h * HEAD_DIM:(h + 1) * HEAD_DIM, :], g, n_kv)
        s_loc = jnp.dot(kw, qTp, preferred_element_type=F32) + mask_shift
        s_ctx = jnp.dot(kc, qTp, preferred_element_type=F32) - u
        acc, l = _softmax_pv([s_loc, s_ctx], [vwT, vcT], sh_ref[h:h + 1, 0:1], exact_max)
        o_ref[h * HEAD_DIM:(h + 1) * HEAD_DIM, :] = (acc[g * HEAD_DIM:(g + 1) * HEAD_DIM]
                                                     * (1.0 / l))


def _wa_call(qT, k, vT, kc, vcT, sh, exact_max):
    hd, s = qT.shape
    kw = k.shape[1]
    l = kc.shape[0]
    bq = min(LOCAL_BQ, s)
    assert s % bq == 0 and s >= bq + 2 * WINDOW
    kern = functools.partial(_wa_kernel, seq=s, n_heads=hd // HEAD_DIM, n_kv=kw // HEAD_DIM,
                             exact_max=exact_max)
    full = lambda i: (0, 0)
    return pl.pallas_call(
        kern,
        grid=(s // bq,),
        in_specs=[pl.BlockSpec((hd, bq), lambda i: (0, i)),
                  pl.BlockSpec((s, kw), full), pl.BlockSpec((kw, s), full),
                  pl.BlockSpec((l, kw), full), pl.BlockSpec((kw, l), full),
                  pl.BlockSpec(sh.shape, full)],
        out_specs=pl.BlockSpec((hd, bq), lambda i: (0, i)),
        out_shape=jax.ShapeDtypeStruct((hd, s), F32),
        compiler_params=_params(("parallel",)),
        name="wa_attn_max" if exact_max else "wa_attn",
    )(qT, k, vT, kc, vcT, sh)


def wa_attn(qT, k, vT, kc, vcT, sink, u):
    n_heads = qT.shape[0] // HEAD_DIM
    rows = jnp.concatenate([sink.astype(F32) - u, jnp.reshape(u, (1,)).astype(F32),
                            jnp.zeros((2 * 8 - n_heads - 1,), F32)])
    sh = jnp.broadcast_to(rows[:, None], (rows.shape[0], LANE))
    return lax.cond(u <= FLASH_SAFE_SHIFT,
                    lambda *a: _wa_call(*a, False), lambda *a: _wa_call(*a, True),
                    qT, k, vT, kc, vcT, sh)


NA_KEY_ROWS = 3 * NA_KH


def _na_kernel(qT_ref, km_ref, k0_ref, kp_ref, vm_ref, v0_ref, vp_ref, kc_ref, vcT_ref,
               bias_ref, sh_ref, o_ref, *, exact_max):
    kwin = jnp.concatenate([km_ref[...], k0_ref[...], kp_ref[...]], axis=0)
    vwinT = jnp.concatenate([vm_ref[...], v0_ref[...], vp_ref[...]], axis=1)
    kc = kc_ref[...]
    vcT = vcT_ref[...]
    u = sh_ref[0:1, 0:1]
    for hh in range(2):
        rows = slice(hh * HEAD_DIM, (hh + 1) * HEAD_DIM)
        qTp = _pad_rows(qT_ref[rows, :], hh, 2)
        s_loc = jnp.dot(kwin, qTp, preferred_element_type=F32) + bias_ref[0, hh]
        s_ctx = jnp.dot(kc, qTp, preferred_element_type=F32) - u
        acc, l = _softmax_pv([s_loc, s_ctx], [vwinT, vcT], None, exact_max)
        o_ref[rows, :] = acc[rows] * (1.0 / l)


def na_bias_table(rpb, rows, shift):
    n_h = rpb.shape[0]
    a = jnp.arange(NA_KH)
    b = jnp.arange(NA_KEY_ROWS)
    c = jnp.arange(GRID_W)
    kc = jnp.arange(GRID_W)
    c0 = jnp.clip(c - NA_KW // 2, 0, GRID_W - NA_KW)
    col_ok = (kc[None, :] >= c0[:, None]) & (kc[None, :] < c0[:, None] + NA_KW)
    col_rel = jnp.clip(kc[None, :] - c[:, None] + (NA_KW - 1), 0, 2 * NA_KW - 2)
    row_rel = jnp.clip(b[None, :] - a[:, None] - 1, 0, 2 * NA_KH - 2)

    def edge_ok(r_base):
        r0 = jnp.clip(r_base + a - NA_KH // 2, 0, rows - NA_KH)
        key_row = r_base - NA_KH + b
        return (key_row[None, :] >= r0[:, None]) & (key_row[None, :] < r0[:, None] + NA_KH)

    inner_ok = ((b[None, :] >= a[:, None] + NA_KH // 2)
                & (b[None, :] < a[:, None] + NA_KH // 2 + NA_KH))
    row_ok = jnp.stack([edge_ok(0), inner_ok, edge_ok(rows - NA_KH)])
    e_c = jax.nn.one_hot(col_rel, 2 * NA_KW - 1, dtype=F32)
    e_r = jax.nn.one_hot(row_rel, 2 * NA_KH - 1, dtype=F32)
    hi = lax.Precision.HIGHEST
    t = jnp.einsum('hrx,ckx->hrck', rpb.astype(F32), e_c, precision=hi)
    t = jnp.einsum('abr,hrck->habck', e_r, t, precision=hi) * LOG2E - shift
    ok = row_ok[:, None, :, :, None, None] & col_ok[None, None, None, None]
    t = jnp.where(ok, t[None], NEG_INF)
    t = jnp.transpose(t, (0, 1, 3, 5, 2, 4))
    return t.reshape(3, n_h, NA_KEY_ROWS * GRID_W, NA_KH * GRID_W)


def _na_call(qT, k, vT, kc, vcT, bias, sh, exact_max):
    hd, s = qT.shape
    l = kc.shape[0]
    bq = NA_KH * GRID_W
    nb = s // bq
    pw = 2 * HEAD_DIM
    assert s % bq == 0 and bq == LOCAL_BQ
    prev = lambda i: jnp.maximum(i - 1, 0)
    nxt = lambda i: jnp.minimum(i + 1, nb - 1)
    var = lambda i: jnp.where(i == 0, 0, jnp.where(i == nb - 1, 2, 1))
    kspec = lambda f: pl.BlockSpec((bq, pw), lambda p, i: (f(i), p))
    vspec = lambda f: pl.BlockSpec((pw, bq), lambda p, i: (p, f(i)))
    cur = lambda i: i
    kern = functools.partial(_na_kernel, exact_max=exact_max)
    return pl.pallas_call(
        kern,
        grid=(hd // pw, nb),
        in_specs=[pl.BlockSpec((pw, bq), lambda p, i: (p, i)),
                  kspec(prev), kspec(cur), kspec(nxt), vspec(prev), vspec(cur), vspec(nxt),
                  pl.BlockSpec((l, pw), lambda p, i: (0, p)),
                  pl.BlockSpec((pw, l), lambda p, i: (p, 0)),
                  pl.BlockSpec((1, 2, NA_KEY_ROWS * GRID_W, bq), lambda p, i: (var(i), p, 0, 0)),
                  pl.BlockSpec(sh.shape, lambda p, i: (0, 0))],
        out_specs=pl.BlockSpec((pw, bq), lambda p, i: (p, i)),
        out_shape=jax.ShapeDtypeStruct((hd, s), F32),
        compiler_params=_params(("parallel", "arbitrary")),
        name="na_attn_max" if exact_max else "na_attn",
    )(qT, k, k, k, vT, vT, vT, kc, vcT, bias, sh)


def na_attn(qT, k, vT, kc, vcT, rpb, u_qk):
    s = qT.shape[1]
    u = u_qk + jnp.maximum(jnp.max(rpb.astype(F32)) * LOG2E, 0.0)
    bias = na_bias_table(rpb, s // GRID_W, u)
    sh = jnp.full((8, LANE), u, F32)
    return lax.cond(u <= FLASH_SAFE_SHIFT,
                    lambda *a: _na_call(*a, False), lambda *a: _na_call(*a, True),
                    qT, k, vT, kc, vcT, bias, sh)


def _outproj_kernel(o_ref, x_ref, w_ref, gate_ref, g_ref, sc_ref, sh_ref, wr_ref,
                    xo_ref, hf_ref, lg_ref):
    acc = jnp.dot(o_ref[...], w_ref[...], preferred_element_type=F32)
    xn = x_ref[...] + gate_ref[...] * acc
    xo_ref[...] = xn
    ms = jnp.mean(xn * xn, axis=-1, keepdims=True)
    y = xn * lax.rsqrt(ms + NORM_EPS) * g_ref[...]
    hf = (y * sc_ref[...] + sh_ref[...]).astype(BF16)
    hf_ref[...] = hf
    lg_ref[...] = jnp.dot(hf, wr_ref[...], preferred_element_type=F32)


def _outproj_t_kernel(na_ref, wa_ref, df_ref, ml_ref, dfg_ref, lam_ref, x_ref, w_ref, gate_ref,
                      g_ref, sc_ref, sh_ref, wr_ref, xo_ref, hf_ref, lg_ref):
    lam = lam_ref[0:1, 0:1]
    parts = [na_ref[...], wa_ref[...]]
    for h in range(df_ref.shape[0]):
        dd = df_ref[h, 0] - lam * df_ref[h, 1]
        ms = jnp.mean(dd * dd, axis=0, keepdims=True)
        parts.append(dd * lax.rsqrt(ms + NORM_EPS) * dfg_ref[...])
    parts.append(ml_ref[...])
    o = jnp.concatenate([jnp.transpose(t).astype(BF16) for t in parts], axis=1)
    acc = jnp.dot(o, w_ref[...], preferred_element_type=F32)
    xn = x_ref[...] + gate_ref[...] * acc
    xo_ref[...] = xn
    ms = jnp.mean(xn * xn, axis=-1, keepdims=True)
    y = xn * lax.rsqrt(ms + NORM_EPS) * g_ref[...]
    hf = (y * sc_ref[...] + sh_ref[...]).astype(BF16)
    hf_ref[...] = hf
    lg_ref[...] = jnp.dot(hf, wr_ref[...], preferred_element_type=F32)


def outproj_t(oT_na, oT_wa, oT_df, oT_ml, df_gain, lam, x, w, gate, g, sc1, sh, wr):
    r, d = x.shape
    tm = min(r, 256)
    nh, _, dv2, _ = oT_df.shape
    wdt = oT_na.shape[0]
    dfg = jnp.broadcast_to(df_gain.astype(F32)[:, None], (dv2, tm))
    lam8 = jnp.full((8, LANE), lam, F32)
    vec = pl.BlockSpec((1, d), lambda i: (0, 0))
    row = lambda n: pl.BlockSpec((tm, n), lambda i: (i, 0))
    colblk = pl.BlockSpec((wdt, tm), lambda i: (0, i))
    const = lambda shape: pl.BlockSpec(shape, lambda i: tuple(0 for _ in shape))
    return pl.pallas_call(
        _outproj_t_kernel,
        grid=(r // tm,),
        in_specs=[colblk, colblk, pl.BlockSpec((nh, 2, dv2, tm), lambda i: (0, 0, 0, i)), colblk,
                  const((dv2, tm)), const((8, LANE)), row(d), const(w.shape), vec, vec, vec, vec,
                  const((d, LANE))],
        out_specs=[row(d), row(d), row(LANE)],
        out_shape=[jax.ShapeDtypeStruct((r, d), F32), jax.ShapeDtypeStruct((r, d), BF16),
                   jax.ShapeDtypeStruct((r, LANE), F32)],
        compiler_params=_params(("parallel",)),
        name="outproj_t",
    )(oT_na, oT_wa, oT_df, oT_ml, dfg, lam8, x, w, gate, g, sc1, sh, wr)


def outproj(o, x, w, gate, g, sc1, sh, wr):
    r, d = x.shape
    k = o.shape[1]
    tm = min(r, 256)
    vec = pl.BlockSpec((1, d), lambda i: (0, 0))
    row = lambda n: pl.BlockSpec((tm, n), lambda i: (i, 0))
    return pl.pallas_call(
        _outproj_kernel,
        grid=(r // tm,),
        in_specs=[row(k), row(d), pl.BlockSpec((k, d), lambda i: (0, 0)), vec, vec, vec, vec,
                  pl.BlockSpec((d, LANE), lambda i: (0, 0))],
        out_specs=[row(d), row(d), row(LANE)],
        out_shape=[jax.ShapeDtypeStruct((r, d), F32), jax.ShapeDtypeStruct((r, d), BF16),
                   jax.ShapeDtypeStruct((r, LANE), F32)],
        compiler_params=_params(("parallel",)),
        name="outproj",
    )(o, x, w, gate, g, sc1, sh, wr)


def _moe_kernel(be_ref, nv_ref, x_ref, w1_ref, w3_ref, w2_ref, o_ref):
    i = pl.program_id(0)

    @pl.when(i < nv_ref[0])
    def _():
        x = x_ref[...]
        a = jnp.dot(x, w1_ref[0], preferred_element_type=F32)
        b = jnp.dot(x, w3_ref[0], preferred_element_type=F32)
        hmid = (a * _sigmoid(a) * b).astype(BF16)
        o_ref[...] = jnp.dot(hmid, w2_ref[0], preferred_element_type=F32)

    @pl.when(i >= nv_ref[0])
    def _():
        o_ref[...] = jnp.zeros(o_ref.shape, o_ref.dtype)


def moe_blocks(x, w1, w3, w2, blk_e, n_valid):
    n, d = x.shape
    f = w1.shape[2]
    n_blk = n // MOE_BLOCK
    grid_spec = pltpu.PrefetchScalarGridSpec(
        num_scalar_prefetch=2,
        grid=(n_blk,),
        in_specs=[pl.BlockSpec((MOE_BLOCK, d), lambda i, be, nv: (i, 0)),
                  pl.BlockSpec((1, d, f), lambda i, be, nv: (be[i], 0, 0)),
                  pl.BlockSpec((1, d, f), lambda i, be, nv: (be[i], 0, 0)),
                  pl.BlockSpec((1, f, d), lambda i, be, nv: (be[i], 0, 0))],
        out_specs=pl.BlockSpec((MOE_BLOCK, d), lambda i, be, nv: (i, 0)),
    )
    return pl.pallas_call(
        _moe_kernel,
        grid_spec=grid_spec,
        out_shape=jax.ShapeDtypeStruct((n, d), F32),
        compiler_params=_params(("arbitrary",)),
        name="moe_blocks",
    )(blk_e, n_valid, x, w1, w3, w2)


def _rms(x, g):
    y = x * lax.rsqrt(jnp.mean(x * x, axis=-1, keepdims=True) + NORM_EPS)
    return y * g


def _rope_tables(seq_len, rot_dim):
    n = rot_dim // 4
    t = jnp.arange(seq_len)
    row = (t // GRID_W).astype(F32)[:, None]
    col = (t % GRID_W).astype(F32)[:, None]
    inv = ROPE_THETA ** (-jnp.arange(n, dtype=F32) / n)
    return (jnp.cos(row * inv), jnp.sin(row * inv), jnp.cos(col * inv), jnp.sin(col * inv))


def _rope(x, tabs):
    expand = (slice(None),) + (None,) * (x.ndim - 2) + (slice(None),)
    cr, sr, cc, sc = (t[expand] for t in tabs)
    x1, x2, x3, x4 = jnp.split(x, 4, axis=-1)
    return jnp.concatenate([x1 * cr - x2 * sr, x2 * cr + x1 * sr,
                            x3 * cc - x4 * sc, x4 * cc + x3 * sc], axis=-1)


def _heads_first(t):
    return jnp.transpose(t, (1, 0, 2))


def _route(logits, b_router):
    n = logits.shape[0]
    per = N_EXPERTS // N_EXPERT_GROUPS
    scores = jax.nn.sigmoid(logits.astype(F32))
    sel = scores + b_router.astype(F32)
    grp = sel.reshape(n, N_EXPERT_GROUPS, per)
    gscore = None
    for a in range(per):
        for bb in range(a + 1, per):
            pair = grp[..., a] + grp[..., bb]
            gscore = pair if gscore is None else jnp.maximum(gscore, pair)
    gidx = jnp.argmax(gscore, axis=-1)
    eids = jnp.arange(N_EXPERTS)
    masked = jnp.where((eids // per)[None, :] == gidx[:, None], sel, -jnp.inf)
    e1 = jnp.argmax(masked, axis=-1)
    e2 = jnp.argmax(jnp.where(eids[None, :] == e1[:, None], -jnp.inf, masked), axis=-1)
    eidx = jnp.stack([e1, e2], axis=1).astype(jnp.int32)
    wts = jnp.take_along_axis(scores, eidx, axis=1)
    wts = wts / jnp.sum(wts, axis=-1, keepdims=True) * ROUTED_SCALE
    return eidx, wts


def _prefix_counts(onehot):
    n, e = onehot.shape
    ch = onehot.astype(F32).reshape(n // LANE, LANE, e)
    tri = jnp.tril(jnp.ones((LANE, LANE), F32))
    within = jnp.einsum('ij,cjk->cik', tri, ch)
    tot = within[:, -1, :]
    base = jnp.cumsum(tot, axis=0) - tot
    return (within + base[:, None, :]).reshape(n, e)


def _dispatch(eidx):
    n = eidx.shape[0]
    n_assign = n * TOP_K
    assert n_assign % LANE == 0
    e_flat = eidx.reshape(-1)
    tok = jnp.repeat(jnp.arange(n, dtype=jnp.int32), TOP_K)
    onehot = e_flat[:, None] == jnp.arange(N_EXPERTS)[None, :]
    csum = _prefix_counts(onehot)
    counts = csum[-1].astype(jnp.int32)
    rank = jnp.sum(jnp.where(onehot, csum, 0.0), axis=1).astype(jnp.int32) - 1
    padded = (counts + MOE_BLOCK - 1) // MOE_BLOCK * MOE_BLOCK
    pend = jnp.cumsum(padded)
    pstart = pend - padded
    dest = jnp.sum(jnp.where(onehot, pstart[None, :], 0), axis=1) + rank
    n_slots = (n_assign + N_EXPERTS * (MOE_BLOCK - 1) + MOE_BLOCK - 1) // MOE_BLOCK * MOE_BLOCK
    n_blk = n_slots // MOE_BLOCK
    slot_tok = jnp.full((n_slots,), n, dtype=jnp.int32).at[dest].set(tok)
    slot_of = dest.reshape(n, TOP_K)
    blk_start = jnp.arange(n_blk, dtype=jnp.int32) * MOE_BLOCK
    blk_e = jnp.minimum(jnp.sum(blk_start[:, None] >= pend[None, :], axis=1), N_EXPERTS - 1)
    n_valid = (pend[-1] // MOE_BLOCK).reshape(1)
    return slot_tok, slot_of, blk_e.astype(jnp.int32), n_valid.astype(jnp.int32)


def _na_prep(p, pc, q_gain, k_gain):
    h, dh = NA_HEADS, HEAD_DIM
    w = h * dh
    qs = dh ** -0.5 * LOG2E

    def nrm(t, g):
        return _rms(t.astype(F32).reshape(t.shape[0], h, dh), g)

    q = (nrm(p[:, :w], q_gain) * qs).reshape(-1, w).astype(BF16)
    k = nrm(p[:, w:2 * w], k_gain).reshape(-1, w).astype(BF16)
    qc = (nrm(pc[:, :w], q_gain) * qs).astype(BF16)
    kc = nrm(pc[:, w:2 * w], k_gain).astype(BF16)
    vc = pc[:, 2 * w:]
    lat = (q.T, k, p[:, 2 * w:].T, kc.reshape(-1, w), vc.T)
    ctx = (_heads_first(qc), _heads_first(kc), _heads_first(vc.reshape(-1, h, dh)))
    return lat, ctx


def _wa_prep(p, pc, q_gain, k_gain, rope):
    hq, hk, dh = WA_HEADS, WA_KV_HEADS, HEAD_DIM
    nq, nk = hq * dh, hk * dh
    qs = dh ** -0.5 * LOG2E
    s, l = p.shape[0], pc.shape[0]
    q = (_rope(_rms(p[:, :nq].astype(F32).reshape(s, hq, dh), q_gain), rope) * qs)
    k = _rope(_rms(p[:, nq:nq + nk].astype(F32).reshape(s, hk, dh), k_gain), rope)
    qc = (_rms(pc[:, :nq].astype(F32).reshape(l, hq, dh), q_gain) * qs).astype(BF16)
    kc = _rms(pc[:, nq:nq + nk].astype(F32).reshape(l, hk, dh), k_gain).astype(BF16)
    vc = pc[:, nq + nk:]
    lat = (q.reshape(s, nq).astype(BF16).T, k.reshape(s, nk).astype(BF16), p[:, nq + nk:].T,
           kc.reshape(l, nk), vc.T)
    ctx = (_heads_first(qc), _heads_first(kc), _heads_first(vc.reshape(l, hk, dh)))
    return lat, ctx


def _diff_prep(p, pc, q_gain, k_gain, rope):
    h, d = DIFF_HEADS, DIFF_DIM
    w = h * 2 * d
    qs = d ** -0.5 * LOG2E
    s, l = p.shape[0], pc.shape[0]
    q = _rope(_rms(p[:, :w].astype(F32).reshape(s, h, 2, d), q_gain), rope) * qs
    k = _rope(_rms(p[:, w:2 * w].astype(F32).reshape(s, h, 2, d), k_gain), rope)
    qc = _rms(pc[:, :w].astype(F32).reshape(l, h, 2, d), q_gain) * qs
    kc = _rms(pc[:, w:2 * w].astype(F32).reshape(l, h, 2, d), k_gain)
    v = p[:, 2 * w:].reshape(s, h, DIFF_V_DIM)
    vc = pc[:, 2 * w:].reshape(l, h, DIFF_V_DIM)
    u_max = _score_bound(d, q_gain, k_gain, d ** -0.5)
    qT, k_aug = _augment_qk(q, jnp.concatenate([kc, k], axis=0), u_max)
    vT = _augment_v(jnp.concatenate([vc, v], axis=0))
    qc_h = jnp.transpose(qc, (1, 2, 0, 3)).reshape(2 * h, l, d).astype(BF16)
    kc_h = jnp.transpose(kc, (1, 2, 0, 3)).reshape(2 * h, l, d).astype(BF16)
    vc_h = _heads_first(vc).astype(BF16)
    return qT, k_aug, vT, u_max, qc_h, kc_h, vc_h


def _mla_prep(p, pc, qa_gain, kva_gain, w_uq, w_ukv, q_gain, k_gain, rope):
    hh, dn, dr, dv = MLA_HEADS, MLA_NOPE, MLA_ROPE, MLA_V
    dq = dn + dr
    qs = dq ** -0.5 * LOG2E
    pad = LANE - dq

    def project(t, tabs):
        n = t.shape[0]
        t = t.astype(F32)
        cq = _rms(t[:, :MLA_Q_LORA], qa_gain).astype(BF16)
        ckv = _rms(t[:, MLA_Q_LORA:MLA_Q_LORA + MLA_KV_LORA], kva_gain).astype(BF16)
        k_rope = t[:, MLA_Q_LORA + MLA_KV_LORA:]
        q = matmul(cq, w_uq).reshape(n, hh, dq)
        kv = matmul(ckv, w_ukv).reshape(n, hh, dn + dv)
        k = jnp.concatenate([kv[..., :dn], jnp.broadcast_to(k_rope[:, None], (n, hh, dr))], -1)
        q = _rms(q, q_gain)
        k = _rms(k, k_gain)
        if tabs is not None:
            q = jnp.concatenate([q[..., :dn], _rope(q[..., dn:], tabs)], axis=-1)
            k = jnp.concatenate([k[..., :dn], _rope(k[..., dn:], tabs)], axis=-1)
        return q * qs, k, kv[..., dn:]

    q, k, v = project(p, rope)
    qc, kc, vc = project(pc, None)
    u_max = _score_bound(dq, q_gain, k_gain, dq ** -0.5)
    qT, k_aug = _augment_qk(q[:, :, None], jnp.concatenate([kc, k], axis=0)[:, :, None], u_max)
    vT = _augment_v(jnp.concatenate([vc, v], axis=0))
    padw = ((0, 0), (0, 0), (0, pad))
    return (qT, k_aug, vT, u_max, _heads_first(jnp.pad(qc, padw)).astype(BF16),
            _heads_first(jnp.pad(kc, padw)).astype(BF16), _heads_first(vc).astype(BF16))


def _merge(o):
    return jnp.transpose(o, (1, 0, 2)).reshape(o.shape[1], -1)


def _diff_post(o1, o2, lam, sub_gain, lambda_init):
    return _rms(o1 - lam * o2, sub_gain) * (1.0 - lambda_init)


def _moe(tokens_bf16, logits, b_router, w1, w3, w2, sw1, sw3, sw2):
    n, d = tokens_bf16.shape
    eidx, gate = _route(logits[:, :N_EXPERTS], b_router)
    slot_tok, slot_of, blk_e, n_valid = _dispatch(eidx)
    h_pad = jnp.concatenate([tokens_bf16, jnp.zeros((1, d), BF16)], axis=0)
    xb = h_pad[slot_tok]
    yb = moe_blocks(xb, w1, w3, w2, blk_e, n_valid)
    y0, y1 = lax.optimization_barrier((yb[slot_of[:, 0]], yb[slot_of[:, 1]]))
    routed = y0 * gate[:, 0:1] + y1 * gate[:, 1:2]
    n_sh = _round_up(n, MOE_BLOCK) // MOE_BLOCK
    shared = moe_blocks(tokens_bf16, sw1[None], sw3[None], sw2[None],
                        jnp.zeros((n_sh,), jnp.int32), jnp.full((1,), n_sh, jnp.int32))
    return routed + shared


def kernel(x, c, ctx, c_ctx, w_ada, b_ada, g_attn, g_ffn, w_in, w_out, na_q_gain, na_k_gain, na_rpb, wa_q_gain, wa_k_gain, wa_sink, diff_q_gain, diff_k_gain, diff_lq1, diff_lk1, diff_lq2, diff_lk2, diff_sub_gain, mla_qa_gain, mla_kva_gain, mla_w_uq, mla_w_ukv, mla_q_gain, mla_k_gain, w_router, b_router, moe_w1, moe_w3, moe_w2, sh_w1, sh_w3, sh_w2):
    b, s, d = x.shape
    assert b == 1
    n_ctx = ctx.shape[1]
    depth = w_ada.shape[0]
    xl = x[0]
    xc = ctx[0]
    rope_head = _rope_tables(s, HEAD_DIM)
    rope_mla = _rope_tables(s, MLA_ROPE)

    cond8 = jnp.zeros((8, d), F32).at[0].set(c[0]).at[1].set(c_ctx)
    mod = modvec(cond8, w_ada, b_ada)
    tn = INPROJ_TN
    in_pad = _round_up(IN_COLS, tn)
    wr = jnp.pad(w_router, ((0, 0), (0, LANE - N_EXPERTS))).astype(BF16)

    for l in range(depth):
        with_ctx = l < depth - 1
        m_lat = mod[l, 0].reshape(6, 1, d)
        m_ctx = mod[l, 1].reshape(6, 1, d)
        w_in_l = jnp.pad(w_in[l], ((0, 0), (0, in_pad - IN_COLS))).astype(BF16)
        w_out_l = w_out[l].astype(BF16)
        g_a = g_attn[l][None]
        g_f = g_ffn[l][None]

        p = inproj(xl, g_a, 1.0 + m_lat[1], m_lat[0], w_in_l, tn)
        pc = inproj(xc, g_a, 1.0 + m_ctx[1], m_ctx[0], w_in_l, tn)

        lat, cq = _na_prep(p[:, NA_OFF:WA_OFF], pc[:, NA_OFF:WA_OFF], na_q_gain[l], na_k_gain[l])
        u_na = _score_bound(HEAD_DIM, na_q_gain[l], na_k_gain[l], HEAD_DIM ** -0.5)
        oT_na = na_attn(*lat, na_rpb[l], u_na)
        oc_na = _merge(ctx_attn(*cq)) if with_ctx else None

        lat, cq = _wa_prep(p[:, WA_OFF:DIFF_OFF], pc[:, WA_OFF:DIFF_OFF],
                           wa_q_gain[l], wa_k_gain[l], rope_head)
        u_wa = _score_bound(HEAD_DIM, wa_q_gain[l], wa_k_gain[l], HEAD_DIM ** -0.5)
        sink2 = wa_sink[l].astype(F32) * LOG2E
        oT_wa = wa_attn(*lat, sink2, u_wa)
        oc_wa = _merge(ctx_attn(*cq, sink2)) if with_ctx else None

        lambda_init = 0.8 - 0.6 * math.exp(-0.3 * l)
        lam = (jnp.exp(jnp.sum(diff_lq1[l].astype(F32) * diff_lk1[l].astype(F32)))
               - jnp.exp(jnp.sum(diff_lq2[l].astype(F32) * diff_lk2[l].astype(F32))) + lambda_init)
        qT, k_aug, vT, u_max, qc, kc, vc = _diff_prep(
            p[:, DIFF_OFF:MLA_OFF], pc[:, DIFF_OFF:MLA_OFF], diff_q_gain[l], diff_k_gain[l],
            rope_head)
        oT_df = flash(qT, k_aug, vT, u_max)
        oc_df = None
        if with_ctx:
            oc = ctx_attn(qc, kc, vc).reshape(DIFF_HEADS, 2, n_ctx, DIFF_V_DIM)
            oc_df = _diff_post(jnp.transpose(oc[:, 0], (1, 0, 2)), jnp.transpose(oc[:, 1], (1, 0, 2)),
                               lam, diff_sub_gain[l], lambda_init).reshape(n_ctx, -1)

        qT, k_aug, vT, u_max, qc, kc, vc = _mla_prep(
            p[:, MLA_OFF:IN_COLS], pc[:, MLA_OFF:IN_COLS], mla_qa_gain[l], mla_kva_gain[l],
            mla_w_uq[l].astype(BF16), mla_w_ukv[l].astype(BF16), mla_q_gain[l], mla_k_gain[l],
            rope_mla)
        oT_ml = flash(qT, k_aug, vT, u_max).reshape(MLA_HEADS * MLA_V, s)
        oc_ml = _merge(ctx_attn(qc, kc, vc)) if with_ctx else None

        xl, hf, lg = outproj_t(oT_na, oT_wa, oT_df, oT_ml,
                               diff_sub_gain[l].astype(F32) * (1.0 - lambda_init), lam,
                               xl, w_out_l, m_lat[2], g_f, 1.0 + m_lat[4], m_lat[3], wr)
        if with_ctx:
            oc_cat = jnp.concatenate([oc_na, oc_wa, oc_df, oc_ml], axis=-1).astype(BF16)
            xc, hfc, lgc = outproj(oc_cat, xc, w_out_l, m_ctx[2], g_f, 1.0 + m_ctx[4], m_ctx[3], wr)
            tokens = jnp.concatenate([hfc, hf], axis=0)
            logits = jnp.concatenate([lgc, lg], axis=0)
        else:
            tokens, logits = hf, lg
        y = _moe(tokens, logits, b_router, moe_w1[l].astype(BF16), moe_w3[l].astype(BF16),
                 moe_w2[l].astype(BF16), sh_w1[l].astype(BF16), sh_w3[l].astype(BF16),
                 sh_w2[l].astype(BF16))
        if with_ctx:
            xc = xc + m_ctx[5] * y[:n_ctx]
            y = y[n_ctx:]
        xl = xl + m_lat[5] * y
    return xl[None]
```

```python
import functools
import math

import jax
import jax.numpy as jnp
from jax import lax
from jax.experimental import pallas as pl
from jax.experimental.pallas import tpu as pltpu

F32 = jnp.float32
BF16 = jnp.bfloat16

GRID_W = 64
HEAD_DIM = 64
ROPE_THETA = 10000.0
NORM_EPS = 1e-6
NEG_INF = -1e30
WINDOW = 128
NA_HEADS = 8
NA_KH = 8
NA_KW = 16
WA_HEADS = 8
WA_KV_HEADS = 2
DIFF_HEADS = 4
DIFF_DIM = 64
DIFF_V_DIM = 128
MLA_HEADS = 8
MLA_NOPE = 64
MLA_ROPE = 32
MLA_V = 64
MLA_Q_LORA = 384
MLA_KV_LORA = 128
N_EXPERTS = 16
N_EXPERT_GROUPS = 4
TOP_K = 2
ROUTED_SCALE = 1.0
MOE_BLOCK = 256

NA_COLS = 3 * NA_HEADS * HEAD_DIM
WA_COLS = (WA_HEADS + 2 * WA_KV_HEADS) * HEAD_DIM
DIFF_COLS = 3 * DIFF_HEADS * 2 * DIFF_DIM
MLA_COLS = MLA_Q_LORA + MLA_KV_LORA + MLA_ROPE
NA_OFF = 0
WA_OFF = NA_OFF + NA_COLS
DIFF_OFF = WA_OFF + WA_COLS
MLA_OFF = DIFF_OFF + DIFF_COLS
IN_COLS = MLA_OFF + MLA_COLS

LANE = 128
INPROJ_TN = 768
LOG2E = math.log2(math.e)
VMEM_LIMIT = 48 * 1024 * 1024


def _round_up(n, m):
    return (n + m - 1) // m * m


def _params(sem):
    return pltpu.CompilerParams(dimension_semantics=sem, vmem_limit_bytes=VMEM_LIMIT)


def _sigmoid(x):
    return 1.0 / (1.0 + jnp.exp(-x))


def _modvec_kernel(c_ref, w_ref, b_ref, o_ref):
    a = c_ref[...]
    a = a * _sigmoid(a)
    o_ref[0] = jnp.dot(a, w_ref[0], preferred_element_type=F32,
                       precision=lax.Precision.HIGHEST) + b_ref[0]


def modvec(cond8, w_ada, b_ada):
    depth, d, n = w_ada.shape
    tn = 1024
    return pl.pallas_call(
        _modvec_kernel,
        grid=(depth, n // tn),
        in_specs=[pl.BlockSpec((8, d), lambda l, j: (0, 0)),
                  pl.BlockSpec((1, d, tn), lambda l, j: (l, 0, j)),
                  pl.BlockSpec((1, 1, tn), lambda l, j: (l, 0, j))],
        out_specs=pl.BlockSpec((1, 8, tn), lambda l, j: (l, 0, j)),
        out_shape=jax.ShapeDtypeStruct((depth, 8, n), F32),
        compiler_params=_params(("parallel", "parallel")),
        name="modvec",
    )(cond8, w_ada, b_ada.reshape(depth, 1, n))


def _inproj_kernel(x_ref, g_ref, sc_ref, sh_ref, w_ref, o_ref, h_sc):
    @pl.when(pl.program_id(1) == 0)
    def _():
        x = x_ref[...]
        ms = jnp.mean(x * x, axis=-1, keepdims=True)
        y = x * lax.rsqrt(ms + NORM_EPS) * g_ref[...]
        h_sc[...] = (y * sc_ref[...] + sh_ref[...]).astype(BF16)

    o_ref[...] = jnp.dot(h_sc[...], w_ref[...], preferred_element_type=F32).astype(o_ref.dtype)


def inproj(x, g, sc1, sh, w, tn):
    r, d = x.shape
    n = w.shape[1]
    tm = min(r, 1024)
    vec = pl.BlockSpec((1, d), lambda i, j: (0, 0))
    return pl.pallas_call(
        _inproj_kernel,
        grid=(r // tm, n // tn),
        in_specs=[pl.BlockSpec((tm, d), lambda i, j: (i, 0)), vec, vec, vec,
                  pl.BlockSpec((d, tn), lambda i, j: (0, j))],
        out_specs=pl.BlockSpec((tm, tn), lambda i, j: (i, j)),
        out_shape=jax.ShapeDtypeStruct((r, n), BF16),
        scratch_shapes=[pltpu.VMEM((tm, d), BF16)],
        compiler_params=_params(("parallel", "arbitrary")),
        name="inproj",
    )(x, g, sc1, sh, w)


def _mm_kernel(x_ref, w_ref, o_ref):
    o_ref[...] = jnp.dot(x_ref[...], w_ref[...], preferred_element_type=F32)


def matmul(x, w):
    r, k = x.shape
    n = w.shape[1]
    tm = min(r, 1024)
    return pl.pallas_call(
        _mm_kernel,
        grid=(r // tm,),
        in_specs=[pl.BlockSpec((tm, k), lambda i: (i, 0)),
                  pl.BlockSpec((k, n), lambda i: (0, 0))],
        out_specs=pl.BlockSpec((tm, n), lambda i: (i, 0)),
        out_shape=jax.ShapeDtypeStruct((r, n), F32),
        compiler_params=_params(("parallel",)),
        name="matmul",
    )(x, w)


FLASH_V_PAD = 16
FLASH_SAFE_SHIFT = 60.0
FLASH_BQ = 1024
FLASH_CHAIN_Q = 512
FLASH_SCORE_BYTES = 17 * 1024 * 1024


def _flash_kernel(qT_ref, k_ref, kc_ref, vT_ref, vcT_ref, u_ref, o_ref, acc_sc, *, n_comp, n_split,
                  bk, n_kb, online):
    bq = qT_ref.shape[1] // n_split
    dv = vT_ref.shape[0]
    comp_rows = LANE // n_comp
    chains = [(c, h) for c in range(n_comp) for h in range(n_split)]
    u = u_ref[0:1, 0:1]
    row = lax.broadcasted_iota(jnp.int32, (LANE, bq), 0)
    shift_rows = jnp.where(row == 0, -u, 0.0).astype(BF16)
    q_ops = []
    for c, h in chains:
        qT = qT_ref[:, h * bq:(h + 1) * bq]
        if n_comp > 1:
            qT = jnp.where((row >= c * comp_rows) & (row < (c + 1) * comp_rows), qT,
                           jnp.zeros_like(qT))
        q_ops.append(jnp.concatenate([qT, shift_rows], axis=0))
    for i in range(len(chains)):
        acc_sc[i] = jnp.zeros(acc_sc.shape[1:], F32)

    def step(kb, vb, carry):
        n = kb.shape[0]
        ones_col = jnp.where(lax.broadcasted_iota(jnp.int32, (n, LANE), 1) == 0, 1.0, 0.0)
        ones_row = jnp.where(lax.broadcasted_iota(jnp.int32, (FLASH_V_PAD, n), 0) == 0, 1.0, 0.0)
        ka = jnp.concatenate([kb, ones_col.astype(BF16)], axis=1)
        va = jnp.concatenate([vb, ones_row.astype(BF16)], axis=0)
        scores = [jnp.dot(ka, q, preferred_element_type=F32) for q in q_ops]
        out = []
        for i, sT in enumerate(scores):
            if online:
                m = carry[i]
                m_new = jnp.maximum(m, jnp.max(sT, axis=0, keepdims=True))
                pT = jnp.exp2(sT - m_new).astype(BF16)
                acc_sc[i] = (jnp.exp2(m - m_new) * acc_sc[i]
                             + jnp.dot(va, pT, preferred_element_type=F32))
                out.append(m_new)
            else:
                pT = jnp.exp2(sT).astype(BF16)
                acc_sc[i] += jnp.dot(va, pT, preferred_element_type=F32)
                out.append(carry[i])
        return tuple(out)

    def body(j, carry):
        off = pl.multiple_of(j * bk, bk)
        return step(k_ref[pl.ds(off, bk), :], vT_ref[:, pl.ds(off, bk)], carry)

    init = tuple(jnp.full((1, bq), NEG_INF, F32) for _ in chains)
    lax.fori_loop(0, n_kb, body, step(kc_ref[...], vcT_ref[...], init))
    for i, (c, h) in enumerate(chains):
        acc = acc_sc[i]
        o_ref[0, c, :, h * bq:(h + 1) * bq] = acc[:dv] * (1.0 / acc[dv:dv + 1])


def _pick_bk(n, n_chains, bq):
    for bk in (4096, 2048, 1024, 512, 256, 128):
        if n % bk == 0 and n_chains * bk * bq * 4 <= FLASH_SCORE_BYTES:
            return bk
    raise ValueError(f"key count {n} must be a multiple of {LANE}")


def _flash_call(qT, k, kc, vT, vcT, u8, n_comp, online):
    hw, s = qT.shape
    n_heads = hw // LANE
    dv = vT.shape[0] // n_heads
    l = kc.shape[0]
    bq = min(FLASH_BQ, s)
    n_split = max(1, bq // FLASH_CHAIN_Q)
    bk = _pick_bk(s, n_comp * n_split, bq // n_split)
    kern = functools.partial(_flash_kernel, n_comp=n_comp, n_split=n_split, bk=bk, n_kb=s // bk,
                             online=online)
    return pl.pallas_call(
        kern,
        grid=(n_heads, s // bq),
        in_specs=[pl.BlockSpec((LANE, bq), lambda h, i: (h, i)),
                  pl.BlockSpec((s, LANE), lambda h, i: (0, h)),
                  pl.BlockSpec((l, LANE), lambda h, i: (0, h)),
                  pl.BlockSpec((dv, s), lambda h, i: (h, 0)),
                  pl.BlockSpec((dv, l), lambda h, i: (h, 0)),
                  pl.BlockSpec(u8.shape, lambda h, i: (0, 0))],
        out_specs=pl.BlockSpec((1, n_comp, dv, bq), lambda h, i: (h, 0, 0, i)),
        out_shape=jax.ShapeDtypeStruct((n_heads, n_comp, dv, s), F32),
        scratch_shapes=[pltpu.VMEM((n_comp * n_split, dv + FLASH_V_PAD, bq // n_split), F32)],
        compiler_params=_params(("parallel", "arbitrary")),
        name="flash_online" if online else "flash",
    )(qT, k, kc, vT, vcT, u8)


def flash(qT, k, kc, vT, vcT, u, n_comp):
    u8 = jnp.full((8, LANE), u, F32)
    return lax.cond(u <= FLASH_SAFE_SHIFT,
                    lambda *a: _flash_call(*a, n_comp, False),
                    lambda *a: _flash_call(*a, n_comp, True),
                    qT, k, kc, vT, vcT, u8)


def _score_bound(d, q_gain, k_gain, scale):
    return (d * scale * LOG2E * 1.02 * jnp.max(jnp.abs(q_gain.astype(F32)))
            * jnp.max(jnp.abs(k_gain.astype(F32))) + 0.01)


def _ctx_kernel(q_ref, k_ref, v_ref, sink_ref, o_ref, *, use_sink):
    s = lax.dot_general(q_ref[0], k_ref[0], (((1,), (1,)), ((), ())),
                        preferred_element_type=F32)
    m = jnp.max(s, axis=-1, keepdims=True)
    if use_sink:
        sk = sink_ref[0, 0:1, 0:1]
        m = jnp.maximum(m, sk)
    p = jnp.exp2(s - m)
    l = jnp.sum(p, axis=-1, keepdims=True)
    if use_sink:
        l = l + jnp.exp2(sk - m)
    o = jnp.dot(p.astype(BF16), v_ref[0], preferred_element_type=F32)
    o_ref[0] = o * (1.0 / l)


def ctx_attn(q, k, v, sink=None):
    h, l, dk = q.shape
    hk, hv, dv = k.shape[0], v.shape[0], v.shape[2]
    use_sink = sink is not None
    if sink is None:
        sink = jnp.zeros((h,), F32)
    sink3 = jnp.broadcast_to(sink.astype(F32)[:, None, None], (h, 8, LANE))
    kern = functools.partial(_ctx_kernel, use_sink=use_sink)
    return pl.pallas_call(
        kern,
        grid=(h,),
        in_specs=[pl.BlockSpec((1, l, dk), lambda i: (i, 0, 0)),
                  pl.BlockSpec((1, l, dk), lambda i: (i // (h // hk), 0, 0)),
                  pl.BlockSpec((1, l, dv), lambda i: (i // (h // hv), 0, 0)),
                  pl.BlockSpec((1, 8, LANE), lambda i: (i, 0, 0))],
        out_specs=pl.BlockSpec((1, l, dv), lambda i: (i, 0, 0)),
        out_shape=jax.ShapeDtypeStruct((h, l, dv), F32),
        compiler_params=_params(("parallel",)),
        name="ctx_attn",
    )(q, k, v, sink3)


LOCAL_BQ = 512


def _pad_rows(qT_h, slot, n_slots):
    z = jnp.zeros_like(qT_h)
    return jnp.concatenate([qT_h if s == slot else z for s in range(n_slots)], axis=0)


def _softmax_pv(s_list, v_list, extra, exact_max):
    if exact_max:
        m = functools.reduce(jnp.maximum, [jnp.max(s, axis=0, keepdims=True) for s in s_list])
        if extra is not None:
            m = jnp.maximum(m, extra)
            extra = extra - m
        s_list = [s - m for s in s_list]
    p_list = [jnp.exp2(s) for s in s_list]
    l = functools.reduce(jnp.add, [jnp.sum(p, axis=0, keepdims=True) for p in p_list])
    if extra is not None:
        l = l + jnp.exp2(extra)
    acc = functools.reduce(jnp.add, [jnp.dot(v, p.astype(BF16), preferred_element_type=F32)
                                     for v, p in zip(v_list, p_list)])
    return acc, l


def _wa_kernel(qT_ref, k_ref, vT_ref, kc_ref, vcT_ref, sh_ref, o_ref, *, seq, n_heads, n_kv,
               exact_max):
    i = pl.program_id(0)
    bq = qT_ref.shape[1]
    win = bq + 2 * WINDOW
    grp = n_heads // n_kv
    start = pl.multiple_of(jnp.clip(i * bq - WINDOW, 0, seq - win), WINDOW)
    kw = k_ref[pl.ds(start, win), :]
    vwT = vT_ref[:, pl.ds(start, win)]
    kc = kc_ref[...]
    vcT = vcT_ref[...]
    u = sh_ref[n_heads:n_heads + 1, 0:1]
    kpos = start + lax.broadcasted_iota(jnp.int32, (win, bq), 0)
    qpos = i * bq + lax.broadcasted_iota(jnp.int32, (win, bq), 1)
    mask_shift = jnp.where(jnp.abs(qpos - kpos) <= WINDOW, -u, NEG_INF)
    for h in range(n_heads):
        g = h // grp
        qTp = _pad_rows(qT_ref[h * HEAD_DIM:(h + 1) * HEAD_DIM, :], g, n_kv)
        s_loc = jnp.dot(kw, qTp, preferred_element_type=F32) + mask_shift
        s_ctx = jnp.dot(kc, qTp, preferred_element_type=F32) - u
        acc, l = _softmax_pv([s_loc, s_ctx], [vwT, vcT], sh_ref[h:h + 1, 0:1], exact_max)
        o_ref[h * HEAD_DIM:(h + 1) * HEAD_DIM, :] = (acc[g * HEAD_DIM:(g + 1) * HEAD_DIM]
                                                     * (1.0 / l))


def _wa_call(qT, k, vT, kc, vcT, sh, exact_max):
    hd, s = qT.shape
    kw = k.shape[1]
    l = kc.shape[0]
    bq = min(LOCAL_BQ, s)
    assert s % bq == 0 and s >= bq + 2 * WINDOW
    kern = functools.partial(_wa_kernel, seq=s, n_heads=hd // HEAD_DIM, n_kv=kw // HEAD_DIM,
                             exact_max=exact_max)
    full = lambda i: (0, 0)
    return pl.pallas_call(
        kern,
        grid=(s // bq,),
        in_specs=[pl.BlockSpec((hd, bq), lambda i: (0, i)),
                  pl.BlockSpec((s, kw), full), pl.BlockSpec((kw, s), full),
                  pl.BlockSpec((l, kw), full), pl.BlockSpec((kw, l), full),
                  pl.BlockSpec(sh.shape, full)],
        out_specs=pl.BlockSpec((hd, bq), lambda i: (0, i)),
        out_shape=jax.ShapeDtypeStruct((hd, s), F32),
        compiler_params=_params(("parallel",)),
        name="wa_attn_max" if exact_max else "wa_attn",
    )(qT, k, vT, kc, vcT, sh)


def wa_attn(qT, k, vT, kc, vcT, sink, u):
    n_heads = qT.shape[0] // HEAD_DIM
    rows = jnp.concatenate([sink.astype(F32) - u, jnp.reshape(u, (1,)).astype(F32),
                            jnp.zeros((2 * 8 - n_heads - 1,), F32)])
    sh = jnp.broadcast_to(rows[:, None], (rows.shape[0], LANE))
    return lax.cond(u <= FLASH_SAFE_SHIFT,
                    lambda *a: _wa_call(*a, False), lambda *a: _wa_call(*a, True),
                    qT, k, vT, kc, vcT, sh)


NA_KEY_ROWS = 3 * NA_KH


def _na_kernel(qT_ref, km_ref, k0_ref, kp_ref, vm_ref, v0_ref, vp_ref, kc_ref, vcT_ref,
               bias_ref, sh_ref, o_ref, *, exact_max):
    kwin = jnp.concatenate([km_ref[...], k0_ref[...], kp_ref[...]], axis=0)
    vwinT = jnp.concatenate([vm_ref[...], v0_ref[...], vp_ref[...]], axis=1)
    kc = kc_ref[...]
    vcT = vcT_ref[...]
    u = sh_ref[0:1, 0:1]
    for hh in range(2):
        rows = slice(hh * HEAD_DIM, (hh + 1) * HEAD_DIM)
        qTp = _pad_rows(qT_ref[rows, :], hh, 2)
        s_loc = jnp.dot(kwin, qTp, preferred_element_type=F32) + bias_ref[0, hh]
        s_ctx = jnp.dot(kc, qTp, preferred_element_type=F32) - u
        acc, l = _softmax_pv([s_loc, s_ctx], [vwinT, vcT], None, exact_max)
        o_ref[rows, :] = acc[rows] * (1.0 / l)


def na_bias_table(rpb, rows, shift):
    n_h = rpb.shape[0]
    a = jnp.arange(NA_KH)
    b = jnp.arange(NA_KEY_ROWS)
    c = jnp.arange(GRID_W)
    kc = jnp.arange(GRID_W)
    c0 = jnp.clip(c - NA_KW // 2, 0, GRID_W - NA_KW)
    col_ok = (kc[None, :] >= c0[:, None]) & (kc[None, :] < c0[:, None] + NA_KW)
    col_rel = jnp.clip(kc[None, :] - c[:, None] + (NA_KW - 1), 0, 2 * NA_KW - 2)
    row_rel = jnp.clip(b[None, :] - a[:, None] - 1, 0, 2 * NA_KH - 2)

    def edge_ok(r_base):
        r0 = jnp.clip(r_base + a - NA_KH // 2, 0, rows - NA_KH)
        key_row = r_base - NA_KH + b
        return (key_row[None, :] >= r0[:, None]) & (key_row[None, :] < r0[:, None] + NA_KH)

    inner_ok = ((b[None, :] >= a[:, None] + NA_KH // 2)
                & (b[None, :] < a[:, None] + NA_KH // 2 + NA_KH))
    row_ok = jnp.stack([edge_ok(0), inner_ok, edge_ok(rows - NA_KH)])
    e_c = jax.nn.one_hot(col_rel, 2 * NA_KW - 1, dtype=F32)
    e_r = jax.nn.one_hot(row_rel, 2 * NA_KH - 1, dtype=F32)
    hi = lax.Precision.HIGHEST
    t = jnp.einsum('hrx,ckx->hrck', rpb.astype(F32), e_c, precision=hi)
    t = jnp.einsum('abr,hrck->habck', e_r, t, precision=hi) * LOG2E - shift
    ok = row_ok[:, None, :, :, None, None] & col_ok[None, None, None, None]
    t = jnp.where(ok, t[None], NEG_INF)
    t = jnp.transpose(t, (0, 1, 3, 5, 2, 4))
    return t.reshape(3, n_h, NA_KEY_ROWS * GRID_W, NA_KH * GRID_W)


def _na_call(qT, k, vT, kc, vcT, bias, sh, exact_max):
    hd, s = qT.shape
    l = kc.shape[0]
    bq = NA_KH * GRID_W
    nb = s // bq
    pw = 2 * HEAD_DIM
    assert s % bq == 0 and bq == LOCAL_BQ
    prev = lambda i: jnp.maximum(i - 1, 0)
    nxt = lambda i: jnp.minimum(i + 1, nb - 1)
    var = lambda i: jnp.where(i == 0, 0, jnp.where(i == nb - 1, 2, 1))
    kspec = lambda f: pl.BlockSpec((bq, pw), lambda p, i: (f(i), p))
    vspec = lambda f: pl.BlockSpec((pw, bq), lambda p, i: (p, f(i)))
    cur = lambda i: i
    kern = functools.partial(_na_kernel, exact_max=exact_max)
    return pl.pallas_call(
        kern,
        grid=(hd // pw, nb),
        in_specs=[pl.BlockSpec((pw, bq), lambda p, i: (p, i)),
                  kspec(prev), kspec(cur), kspec(nxt), vspec(prev), vspec(cur), vspec(nxt),
                  pl.BlockSpec((l, pw), lambda p, i: (0, p)),
                  pl.BlockSpec((pw, l), lambda p, i: (p, 0)),
                  pl.BlockSpec((1, 2, NA_KEY_ROWS * GRID_W, bq), lambda p, i: (var(i), p, 0, 0)),
                  pl.BlockSpec(sh.shape, lambda p, i: (0, 0))],
        out_specs=pl.BlockSpec((pw, bq), lambda p, i: (p, i)),
        out_shape=jax.ShapeDtypeStruct((hd, s), F32),
        compiler_params=_params(("parallel", "arbitrary")),
        name="na_attn_max" if exact_max else "na_attn",
    )(qT, k, k, k, vT, vT, vT, kc, vcT, bias, sh)


def na_attn(qT, k, vT, kc, vcT, rpb, u_qk):
    s = qT.shape[1]
    u = u_qk + jnp.maximum(jnp.max(rpb.astype(F32)) * LOG2E, 0.0)
    bias = na_bias_table(rpb, s // GRID_W, u)
    sh = jnp.full((8, LANE), u, F32)
    return lax.cond(u <= FLASH_SAFE_SHIFT,
                    lambda *a: _na_call(*a, False), lambda *a: _na_call(*a, True),
                    qT, k, vT, kc, vcT, bias, sh)


def _outproj_kernel(o_ref, x_ref, w_ref, gate_ref, g_ref, sc_ref, sh_ref, wr_ref,
                    xo_ref, hf_ref, lg_ref):
    acc = jnp.dot(o_ref[...], w_ref[...], preferred_element_type=F32)
    xn = x_ref[...] + gate_ref[...] * acc
    xo_ref[...] = xn
    ms = jnp.mean(xn * xn, axis=-1, keepdims=True)
    y = xn * lax.rsqrt(ms + NORM_EPS) * g_ref[...]
    hf = (y * sc_ref[...] + sh_ref[...]).astype(BF16)
    hf_ref[...] = hf
    lg_ref[...] = jnp.dot(hf, wr_ref[...], preferred_element_type=F32)


def _outproj_t_kernel(na_ref, wa_ref, df_ref, ml_ref, dfg_ref, lam_ref, x_ref, w_ref, gate_ref,
                      g_ref, sc_ref, sh_ref, wr_ref, xo_ref, hf_ref, lg_ref):
    lam = lam_ref[0:1, 0:1]
    parts = [na_ref[...], wa_ref[...]]
    for h in range(df_ref.shape[0]):
        dd = df_ref[h, 0] - lam * df_ref[h, 1]
        ms = jnp.mean(dd * dd, axis=0, keepdims=True)
        parts.append(dd * lax.rsqrt(ms + NORM_EPS) * dfg_ref[...])
    parts.append(ml_ref[...])
    o = jnp.concatenate([jnp.transpose(t).astype(BF16) for t in parts], axis=1)
    acc = jnp.dot(o, w_ref[...], preferred_element_type=F32)
    xn = x_ref[...] + gate_ref[...] * acc
    xo_ref[...] = xn
    ms = jnp.mean(xn * xn, axis=-1, keepdims=True)
    y = xn * lax.rsqrt(ms + NORM_EPS) * g_ref[...]
    hf = (y * sc_ref[...] + sh_ref[...]).astype(BF16)
    hf_ref[...] = hf
    lg_ref[...] = jnp.dot(hf, wr_ref[...], preferred_element_type=F32)


def outproj_t(oT_na, oT_wa, oT_df, oT_ml, df_gain, lam, x, w, gate, g, sc1, sh, wr):
    r, d = x.shape
    tm = min(r, 256)
    nh, _, dv2, _ = oT_df.shape
    wdt = oT_na.shape[0]
    dfg = jnp.broadcast_to(df_gain.astype(F32)[:, None], (dv2, tm))
    lam8 = jnp.full((8, LANE), lam, F32)
    vec = pl.BlockSpec((1, d), lambda i: (0, 0))
    row = lambda n: pl.BlockSpec((tm, n), lambda i: (i, 0))
    colblk = pl.BlockSpec((wdt, tm), lambda i: (0, i))
    const = lambda shape: pl.BlockSpec(shape, lambda i: tuple(0 for _ in shape))
    return pl.pallas_call(
        _outproj_t_kernel,
        grid=(r // tm,),
        in_specs=[colblk, colblk, pl.BlockSpec((nh, 2, dv2, tm), lambda i: (0, 0, 0, i)), colblk,
                  const((dv2, tm)), const((8, LANE)), row(d), const(w.shape), vec, vec, vec, vec,
                  const((d, LANE))],
        out_specs=[row(d), row(d), row(LANE)],
        out_shape=[jax.ShapeDtypeStruct((r, d), F32), jax.ShapeDtypeStruct((r, d), BF16),
                   jax.ShapeDtypeStruct((r, LANE), F32)],
        compiler_params=_params(("parallel",)),
        name="outproj_t",
    )(oT_na, oT_wa, oT_df, oT_ml, dfg, lam8, x, w, gate, g, sc1, sh, wr)


def outproj(o, x, w, gate, g, sc1, sh, wr):
    r, d = x.shape
    k = o.shape[1]
    tm = min(r, 256)
    vec = pl.BlockSpec((1, d), lambda i: (0, 0))
    row = lambda n: pl.BlockSpec((tm, n), lambda i: (i, 0))
    return pl.pallas_call(
        _outproj_kernel,
        grid=(r // tm,),
        in_specs=[row(k), row(d), pl.BlockSpec((k, d), lambda i: (0, 0)), vec, vec, vec, vec,
                  pl.BlockSpec((d, LANE), lambda i: (0, 0))],
        out_specs=[row(d), row(d), row(LANE)],
        out_shape=[jax.ShapeDtypeStruct((r, d), F32), jax.ShapeDtypeStruct((r, d), BF16),
                   jax.ShapeDtypeStruct((r, LANE), F32)],
        compiler_params=_params(("parallel",)),
        name="outproj",
    )(o, x, w, gate, g, sc1, sh, wr)


def _moe_kernel(be_ref, nv_ref, x_ref, w1_ref, w3_ref, w2_ref, o_ref):
    i = pl.program_id(0)

    @pl.when(i < nv_ref[0])
    def _():
        x = x_ref[...]
        a = jnp.dot(x, w1_ref[0], preferred_element_type=F32)
        b = jnp.dot(x, w3_ref[0], preferred_element_type=F32)
        hmid = (a * _sigmoid(a) * b).astype(BF16)
        o_ref[...] = jnp.dot(hmid, w2_ref[0], preferred_element_type=F32)

    @pl.when(i >= nv_ref[0])
    def _():
        o_ref[...] = jnp.zeros(o_ref.shape, o_ref.dtype)


def moe_blocks(x, w1, w3, w2, blk_e, n_valid):
    n, d = x.shape
    f = w1.shape[2]
    n_blk = n // MOE_BLOCK
    grid_spec = pltpu.PrefetchScalarGridSpec(
        num_scalar_prefetch=2,
        grid=(n_blk,),
        in_specs=[pl.BlockSpec((MOE_BLOCK, d), lambda i, be, nv: (i, 0)),
                  pl.BlockSpec((1, d, f), lambda i, be, nv: (be[i], 0, 0)),
                  pl.BlockSpec((1, d, f), lambda i, be, nv: (be[i], 0, 0)),
                  pl.BlockSpec((1, f, d), lambda i, be, nv: (be[i], 0, 0))],
        out_specs=pl.BlockSpec((MOE_BLOCK, d), lambda i, be, nv: (i, 0)),
    )
    return pl.pallas_call(
        _moe_kernel,
        grid_spec=grid_spec,
        out_shape=jax.ShapeDtypeStruct((n, d), F32),
        compiler_params=_params(("arbitrary",)),
        name="moe_blocks",
    )(blk_e, n_valid, x, w1, w3, w2)


def _rms(x, g):
    y = x * lax.rsqrt(jnp.mean(x * x, axis=-1, keepdims=True) + NORM_EPS)
    return y * g


def _rope_tables(seq_len, rot_dim):
    n = rot_dim // 4
    t = jnp.arange(seq_len)
    row = (t // GRID_W).astype(F32)[:, None]
    col = (t % GRID_W).astype(F32)[:, None]
    inv = ROPE_THETA ** (-jnp.arange(n, dtype=F32) / n)
    return (jnp.cos(row * inv), jnp.sin(row * inv), jnp.cos(col * inv), jnp.sin(col * inv))


def _heads_first(t):
    return jnp.transpose(t, (1, 0, 2))


def _route(logits, b_router):
    n = logits.shape[0]
    per = N_EXPERTS // N_EXPERT_GROUPS
    scores = jax.nn.sigmoid(logits.astype(F32))
    sel = scores + b_router.astype(F32)
    grp = sel.reshape(n, N_EXPERT_GROUPS, per)
    gscore = None
    for a in range(per):
        for bb in range(a + 1, per):
            pair = grp[..., a] + grp[..., bb]
            gscore = pair if gscore is None else jnp.maximum(gscore, pair)
    gidx = jnp.argmax(gscore, axis=-1)
    eids = jnp.arange(N_EXPERTS)
    masked = jnp.where((eids // per)[None, :] == gidx[:, None], sel, -jnp.inf)
    e1 = jnp.argmax(masked, axis=-1)
    e2 = jnp.argmax(jnp.where(eids[None, :] == e1[:, None], -jnp.inf, masked), axis=-1)
    eidx = jnp.stack([e1, e2], axis=1).astype(jnp.int32)
    wts = jnp.take_along_axis(scores, eidx, axis=1)
    wts = wts / jnp.sum(wts, axis=-1, keepdims=True) * ROUTED_SCALE
    return eidx, wts


def _prefix_counts(onehot):
    n, e = onehot.shape
    ch = onehot.astype(F32).reshape(n // LANE, LANE, e)
    tri = jnp.tril(jnp.ones((LANE, LANE), F32))
    within = jnp.einsum('ij,cjk->cik', tri, ch)
    tot = within[:, -1, :]
    base = jnp.cumsum(tot, axis=0) - tot
    return (within + base[:, None, :]).reshape(n, e)


def _dispatch(eidx):
    n = eidx.shape[0]
    n_assign = n * TOP_K
    assert n_assign % LANE == 0
    e_flat = eidx.reshape(-1)
    tok = jnp.repeat(jnp.arange(n, dtype=jnp.int32), TOP_K)
    onehot = e_flat[:, None] == jnp.arange(N_EXPERTS)[None, :]
    csum = _prefix_counts(onehot)
    counts = csum[-1].astype(jnp.int32)
    rank = jnp.sum(jnp.where(onehot, csum, 0.0), axis=1).astype(jnp.int32) - 1
    padded = (counts + MOE_BLOCK - 1) // MOE_BLOCK * MOE_BLOCK
    pend = jnp.cumsum(padded)
    pstart = pend - padded
    dest = jnp.sum(jnp.where(onehot, pstart[None, :], 0), axis=1) + rank
    n_slots = (n_assign + N_EXPERTS * (MOE_BLOCK - 1) + MOE_BLOCK - 1) // MOE_BLOCK * MOE_BLOCK
    n_blk = n_slots // MOE_BLOCK
    slot_tok = jnp.full((n_slots,), n, dtype=jnp.int32).at[dest].set(tok)
    slot_of = dest.reshape(n, TOP_K)
    blk_start = jnp.arange(n_blk, dtype=jnp.int32) * MOE_BLOCK
    blk_e = jnp.minimum(jnp.sum(blk_start[:, None] >= pend[None, :], axis=1), N_EXPERTS - 1)
    n_valid = (pend[-1] // MOE_BLOCK).reshape(1)
    return slot_tok, slot_of, blk_e.astype(jnp.int32), n_valid.astype(jnp.int32)


def _slabs(x):
    return x.astype(F32).reshape(x.shape[0], x.shape[1] // LANE, LANE)


def _seg_rms(x, gain, seg, count):
    sq = x * x
    hi = sq.astype(BF16)
    lo = (sq - hi.astype(F32)).astype(BF16)
    lane = jnp.arange(LANE)
    blk = (lane[:, None] // seg == lane[None, :] // seg).astype(BF16)
    ssum = (jnp.einsum('rnl,lm->rnm', hi, blk, preferred_element_type=F32)
            + jnp.einsum('rnl,lm->rnm', lo, blk, preferred_element_type=F32))
    return x * lax.rsqrt(ssum * (1.0 / count) + NORM_EPS) * gain


def _rope_slab_tables(seq_len, width, offset, reps):
    n = width // 4
    cr, sr, cc, sc = _rope_tables(seq_len, width)
    one = jnp.ones((seq_len, offset), F32)
    zero = jnp.zeros((seq_len, offset), F32)
    zq = jnp.zeros((seq_len, n), F32)
    rest = LANE // reps - offset - width
    pad1 = jnp.ones((seq_len, rest), F32)
    pad0 = jnp.zeros((seq_len, rest), F32)
    cos = jnp.concatenate([one, cr, cr, cc, cc, pad1] * reps, axis=1)
    s_next = jnp.concatenate([zero, -sr, zq, -sc, zq, pad0] * reps, axis=1)
    s_prev = jnp.concatenate([zero, zq, sr, zq, sc, pad0] * reps, axis=1)
    return cos[:, None, :], s_next[:, None, :], s_prev[:, None, :]


def _rope_slab(x, tabs, q):
    cos, s_next, s_prev = tabs
    return x * cos + jnp.roll(x, -q, axis=-1) * s_next + jnp.roll(x, q, axis=-1) * s_prev


def _flat(x):
    return x.reshape(x.shape[0], -1)


def _gain2(g):
    return jnp.concatenate([g, g]).astype(F32)


def _ctx_heads(t, n_heads):
    return _heads_first(t.reshape(t.shape[0], n_heads, -1)).astype(BF16)


def _qk_prep(p, pc, w, q_gain, k_gain, rope, wk=None):
    wk = w if wk is None else wk
    qs = HEAD_DIM ** -0.5 * LOG2E
    gq, gk = _gain2(q_gain), _gain2(k_gain)
    q = _seg_rms(_slabs(p[:, :w]), gq, HEAD_DIM, HEAD_DIM)
    k = _seg_rms(_slabs(p[:, w:w + wk]), gk, HEAD_DIM, HEAD_DIM)
    if rope is not None:
        q = _rope_slab(q, rope, HEAD_DIM // 4)
        k = _rope_slab(k, rope, HEAD_DIM // 4)
    qc = _seg_rms(_slabs(pc[:, :w]), gq, HEAD_DIM, HEAD_DIM) * qs
    kc = _seg_rms(_slabs(pc[:, w:w + wk]), gk, HEAD_DIM, HEAD_DIM)
    return (_flat(q * qs).astype(BF16).T, _flat(k).astype(BF16),
            _flat(qc).astype(BF16), _flat(kc).astype(BF16))


def _mla_prep(p, pc, qa_gain, kva_gain, w_uq, w_ukv, q_gain, k_gain, rope):
    hh, dn, dr, dv = MLA_HEADS, MLA_NOPE, MLA_ROPE, MLA_V
    dq = dn + dr
    qs = dq ** -0.5 * LOG2E
    padh = LANE - dq
    w_q = jnp.pad(w_uq.reshape(MLA_Q_LORA, hh, dq), ((0, 0), (0, 0), (0, padh)))
    w_kv = w_ukv.reshape(MLA_KV_LORA, hh, dn + dv)
    w_k = jnp.pad(w_kv[..., :dn], ((0, 0), (0, 0), (0, LANE - dn)))
    w_all = jnp.concatenate([w_k.reshape(MLA_KV_LORA, hh * LANE),
                             w_kv[..., dn:].reshape(MLA_KV_LORA, hh * dv)], axis=1).astype(BF16)
    w_q = w_q.reshape(MLA_Q_LORA, hh * LANE).astype(BF16)
    gq = jnp.pad(q_gain.astype(F32), (0, padh))
    gk = jnp.pad(k_gain.astype(F32), (0, padh))

    def project(t, tabs):
        t = t.astype(F32)
        cq = _rms(t[:, :MLA_Q_LORA], qa_gain).astype(BF16)
        ckv = _rms(t[:, MLA_Q_LORA:MLA_Q_LORA + MLA_KV_LORA], kva_gain).astype(BF16)
        k_rope = jnp.pad(t[:, MLA_Q_LORA + MLA_KV_LORA:], ((0, 0), (dn, padh)))
        q = _slabs(matmul(cq, w_q))
        kv = matmul(ckv, w_all)
        k = _slabs(kv[:, :hh * LANE]) + k_rope[:, None, :]
        q = _seg_rms(q, gq, LANE, dq)
        k = _seg_rms(k, gk, LANE, dq)
        if tabs is not None:
            q = _rope_slab(q, tabs, dr // 4)
            k = _rope_slab(k, tabs, dr // 4)
        return _flat(q * qs).astype(BF16), _flat(k).astype(BF16), kv[:, hh * LANE:].astype(BF16)

    q, k, v = project(p, rope)
    qc, kc, vc = project(pc, None)
    return q.T, k, kc, v.T, vc.T, qc, vc


def _merge(o):
    return jnp.transpose(o, (1, 0, 2)).reshape(o.shape[1], -1)


def _diff_post(o1, o2, lam, sub_gain, lambda_init):
    return _rms(o1 - lam * o2, sub_gain) * (1.0 - lambda_init)


def _moe(tokens_bf16, logits, b_router, w1, w3, w2, sw1, sw3, sw2):
    n, d = tokens_bf16.shape
    eidx, gate = _route(logits[:, :N_EXPERTS], b_router)
    slot_tok, slot_of, blk_e, n_valid = _dispatch(eidx)
    h_pad = jnp.concatenate([tokens_bf16, jnp.zeros((1, d), BF16)], axis=0)
    xb = h_pad[slot_tok]
    yb = moe_blocks(xb, w1, w3, w2, blk_e, n_valid)
    y0, y1 = lax.optimization_barrier((yb[slot_of[:, 0]], yb[slot_of[:, 1]]))
    routed = y0 * gate[:, 0:1] + y1 * gate[:, 1:2]
    n_sh = _round_up(n, MOE_BLOCK) // MOE_BLOCK
    shared = moe_blocks(tokens_bf16, sw1[None], sw3[None], sw2[None],
                        jnp.zeros((n_sh,), jnp.int32), jnp.full((1,), n_sh, jnp.int32))
    return routed + shared


def kernel(x, c, ctx, c_ctx, w_ada, b_ada, g_attn, g_ffn, w_in, w_out, na_q_gain, na_k_gain, na_rpb, wa_q_gain, wa_k_gain, wa_sink, diff_q_gain, diff_k_gain, diff_lq1, diff_lk1, diff_lq2, diff_lk2, diff_sub_gain, mla_qa_gain, mla_kva_gain, mla_w_uq, mla_w_ukv, mla_q_gain, mla_k_gain, w_router, b_router, moe_w1, moe_w3, moe_w2, sh_w1, sh_w3, sh_w2):
    b, s, d = x.shape
    assert b == 1
    n_ctx = ctx.shape[1]
    depth = w_ada.shape[0]
    xl = x[0]
    xc = ctx[0]
    rope_head = _rope_slab_tables(s, HEAD_DIM, 0, 2)
    rope_mla = _rope_slab_tables(s, MLA_ROPE, MLA_NOPE, 1)

    cond8 = jnp.zeros((8, d), F32).at[0].set(c[0]).at[1].set(c_ctx)
    mod = modvec(cond8, w_ada, b_ada)
    tn = INPROJ_TN
    in_pad = _round_up(IN_COLS, tn)
    wr = jnp.pad(w_router, ((0, 0), (0, LANE - N_EXPERTS))).astype(BF16)
    inv_sqrt_d = HEAD_DIM ** -0.5

    for l in range(depth):
        with_ctx = l < depth - 1
        m_lat = mod[l, 0].reshape(6, 1, d)
        m_ctx = mod[l, 1].reshape(6, 1, d)
        w_in_l = jnp.pad(w_in[l], ((0, 0), (0, in_pad - IN_COLS))).astype(BF16)
        w_out_l = w_out[l].astype(BF16)
        g_a = g_attn[l][None]
        g_f = g_ffn[l][None]

        p = inproj(xl, g_a, 1.0 + m_lat[1], m_lat[0], w_in_l, tn)
        pc = inproj(xc, g_a, 1.0 + m_ctx[1], m_ctx[0], w_in_l, tn)

        w = NA_HEADS * HEAD_DIM
        pm, pcm = p[:, NA_OFF:WA_OFF], pc[:, NA_OFF:WA_OFF]
        qT, k, qc, kc = _qk_prep(pm, pcm, w, na_q_gain[l], na_k_gain[l], None)
        vc = pcm[:, 2 * w:]
        u_na = _score_bound(HEAD_DIM, na_q_gain[l], na_k_gain[l], inv_sqrt_d)
        oT_na = na_attn(qT, k, pm[:, 2 * w:].T, kc, vc.T, na_rpb[l], u_na)
        oc_na = (_merge(ctx_attn(_ctx_heads(qc, NA_HEADS), _ctx_heads(kc, NA_HEADS),
                                 _ctx_heads(vc, NA_HEADS))) if with_ctx else None)

        w, wk = WA_HEADS * HEAD_DIM, WA_KV_HEADS * HEAD_DIM
        pm, pcm = p[:, WA_OFF:DIFF_OFF], pc[:, WA_OFF:DIFF_OFF]
        qT, k, qc, kc = _qk_prep(pm, pcm, w, wa_q_gain[l], wa_k_gain[l], rope_head, wk)
        vc = pcm[:, w + wk:]
        u_wa = _score_bound(HEAD_DIM, wa_q_gain[l], wa_k_gain[l], inv_sqrt_d)
        sink2 = wa_sink[l].astype(F32) * LOG2E
        oT_wa = wa_attn(qT, k, pm[:, w + wk:].T, kc, vc.T, sink2, u_wa)
        oc_wa = (_merge(ctx_attn(_ctx_heads(qc, WA_HEADS), _ctx_heads(kc, WA_KV_HEADS),
                                 _ctx_heads(vc, WA_KV_HEADS), sink2)) if with_ctx else None)

        lambda_init = 0.8 - 0.6 * math.exp(-0.3 * l)
        lam = (jnp.exp(jnp.sum(diff_lq1[l].astype(F32) * diff_lk1[l].astype(F32)))
               - jnp.exp(jnp.sum(diff_lq2[l].astype(F32) * diff_lk2[l].astype(F32))) + lambda_init)
        w = DIFF_HEADS * 2 * DIFF_DIM
        pm, pcm = p[:, DIFF_OFF:MLA_OFF], pc[:, DIFF_OFF:MLA_OFF]
        qT, k, qc, kc = _qk_prep(pm, pcm, w, diff_q_gain[l], diff_k_gain[l], rope_head)
        vc = pcm[:, 2 * w:]
        u_df = _score_bound(DIFF_DIM, diff_q_gain[l], diff_k_gain[l], DIFF_DIM ** -0.5)
        oT_df = flash(qT, k, kc, pm[:, 2 * w:].T, vc.T, u_df, 2)
        oc_df = None
        if with_ctx:
            oc = ctx_attn(_ctx_heads(qc, 2 * DIFF_HEADS), _ctx_heads(kc, 2 * DIFF_HEADS),
                          _ctx_heads(vc, DIFF_HEADS)).reshape(DIFF_HEADS, 2, n_ctx, DIFF_V_DIM)
            oc_df = _diff_post(jnp.transpose(oc[:, 0], (1, 0, 2)), jnp.transpose(oc[:, 1], (1, 0, 2)),
                               lam, diff_sub_gain[l], lambda_init).reshape(n_ctx, -1)

        qT, k, kc, vT, vcT, qc, vc = _mla_prep(
            p[:, MLA_OFF:IN_COLS], pc[:, MLA_OFF:IN_COLS], mla_qa_gain[l], mla_kva_gain[l],
            mla_w_uq[l], mla_w_ukv[l], mla_q_gain[l], mla_k_gain[l], rope_mla)
        u_ml = _score_bound(MLA_NOPE + MLA_ROPE, mla_q_gain[l], mla_k_gain[l],
                            (MLA_NOPE + MLA_ROPE) ** -0.5)
        oT_ml = flash(qT, k, kc, vT, vcT, u_ml, 1).reshape(MLA_HEADS * MLA_V, s)
        oc_ml = (_merge(ctx_attn(_ctx_heads(qc, MLA_HEADS), _ctx_heads(kc, MLA_HEADS),
                                 _ctx_heads(vc, MLA_HEADS))) if with_ctx else None)

        xl, hf, lg = outproj_t(oT_na, oT_wa, oT_df, oT_ml,
                               diff_sub_gain[l].astype(F32) * (1.0 - lambda_init), lam,
                               xl, w_out_l, m_lat[2], g_f, 1.0 + m_lat[4], m_lat[3], wr)
        if with_ctx:
            oc_cat = jnp.concatenate([oc_na, oc_wa, oc_df, oc_ml], axis=-1).astype(BF16)
            xc, hfc, lgc = outproj(oc_cat, xc, w_out_l, m_ctx[2], g_f, 1.0 + m_ctx[4], m_ctx[3], wr)
            tokens = jnp.concatenate([hfc, hf], axis=0)
            logits = jnp.concatenate([lgc, lg], axis=0)
        else:
            tokens, logits = hf, lg
        y = _moe(tokens, logits, b_router, moe_w1[l].astype(BF16), moe_w3[l].astype(BF16),
                 moe_w2[l].astype(BF16), sh_w1[l].astype(BF16), sh_w3[l].astype(BF16),
                 sh_w2[l].astype(BF16))
        if with_ctx:
            xc = xc + m_ctx[5] * y[:n_ctx]
            y = y[n_ctx:]
        xl = xl + m_lat[5] * y
    return xl[None]
```

```python
import functools
import math

import jax
import jax.numpy as jnp
from jax import lax
from jax.experimental import pallas as pl
from jax.experimental.pallas import tpu as pltpu

F32 = jnp.float32
BF16 = jnp.bfloat16

GRID_W = 64
HEAD_DIM = 64
ROPE_THETA = 10000.0
NORM_EPS = 1e-6
NEG_INF = -1e30
WINDOW = 128
NA_HEADS = 8
NA_KH = 8
NA_KW = 16
WA_HEADS = 8
WA_KV_HEADS = 2
DIFF_HEADS = 4
DIFF_DIM = 64
DIFF_V_DIM = 128
MLA_HEADS = 8
MLA_NOPE = 64
MLA_ROPE = 32
MLA_V = 64
MLA_Q_LORA = 384
MLA_KV_LORA = 128
N_EXPERTS = 16
N_EXPERT_GROUPS = 4
TOP_K = 2
ROUTED_SCALE = 1.0
MOE_BLOCK = 256

NA_COLS = 3 * NA_HEADS * HEAD_DIM
WA_COLS = (WA_HEADS + 2 * WA_KV_HEADS) * HEAD_DIM
DIFF_COLS = 3 * DIFF_HEADS * 2 * DIFF_DIM
MLA_COLS = MLA_Q_LORA + MLA_KV_LORA + MLA_ROPE
NA_OFF = 0
WA_OFF = NA_OFF + NA_COLS
DIFF_OFF = WA_OFF + WA_COLS
MLA_OFF = DIFF_OFF + DIFF_COLS
IN_COLS = MLA_OFF + MLA_COLS

LANE = 128
INPROJ_TN = 768
LOG2E = math.log2(math.e)
VMEM_LIMIT = 48 * 1024 * 1024


def _round_up(n, m):
    return (n + m - 1) // m * m


def _params(sem):
    return pltpu.CompilerParams(dimension_semantics=sem, vmem_limit_bytes=VMEM_LIMIT)


def _sigmoid(x):
    return 1.0 / (1.0 + jnp.exp(-x))


def _modvec_kernel(c_ref, w_ref, b_ref, o_ref):
    a = c_ref[...]
    a = a * _sigmoid(a)
    o_ref[0] = jnp.dot(a, w_ref[0], preferred_element_type=F32,
                       precision=lax.Precision.HIGHEST) + b_ref[0]


def modvec(cond8, w_ada, b_ada):
    depth, d, n = w_ada.shape
    tn = 1024
    return pl.pallas_call(
        _modvec_kernel,
        grid=(depth, n // tn),
        in_specs=[pl.BlockSpec((8, d), lambda l, j: (0, 0)),
                  pl.BlockSpec((1, d, tn), lambda l, j: (l, 0, j)),
                  pl.BlockSpec((1, 1, tn), lambda l, j: (l, 0, j))],
        out_specs=pl.BlockSpec((1, 8, tn), lambda l, j: (l, 0, j)),
        out_shape=jax.ShapeDtypeStruct((depth, 8, n), F32),
        compiler_params=_params(("parallel", "parallel")),
        name="modvec",
    )(cond8, w_ada, b_ada.reshape(depth, 1, n))


def _inproj_kernel(x_ref, g_ref, sc_ref, sh_ref, w_ref, o_ref, h_sc):
    @pl.when(pl.program_id(1) == 0)
    def _():
        x = x_ref[...]
        ms = jnp.mean(x * x, axis=-1, keepdims=True)
        y = x * lax.rsqrt(ms + NORM_EPS) * g_ref[...]
        h_sc[...] = (y * sc_ref[...] + sh_ref[...]).astype(BF16)

    o_ref[...] = jnp.dot(h_sc[...], w_ref[...], preferred_element_type=F32).astype(o_ref.dtype)


def inproj(x, g, sc1, sh, w, tn):
    r, d = x.shape
    n = w.shape[1]
    tm = min(r, 1024)
    vec = pl.BlockSpec((1, d), lambda i, j: (0, 0))
    return pl.pallas_call(
        _inproj_kernel,
        grid=(r // tm, n // tn),
        in_specs=[pl.BlockSpec((tm, d), lambda i, j: (i, 0)), vec, vec, vec,
                  pl.BlockSpec((d, tn), lambda i, j: (0, j))],
        out_specs=pl.BlockSpec((tm, tn), lambda i, j: (i, j)),
        out_shape=jax.ShapeDtypeStruct((r, n), BF16),
        scratch_shapes=[pltpu.VMEM((tm, d), BF16)],
        compiler_params=_params(("parallel", "arbitrary")),
        name="inproj",
    )(x, g, sc1, sh, w)


def _mm_kernel(x_ref, w_ref, o_ref):
    o_ref[...] = jnp.dot(x_ref[...], w_ref[...], preferred_element_type=F32)


def matmul(x, w):
    r, k = x.shape
    n = w.shape[1]
    tm = min(r, 1024)
    return pl.pallas_call(
        _mm_kernel,
        grid=(r // tm,),
        in_specs=[pl.BlockSpec((tm, k), lambda i: (i, 0)),
                  pl.BlockSpec((k, n), lambda i: (0, 0))],
        out_specs=pl.BlockSpec((tm, n), lambda i: (i, 0)),
        out_shape=jax.ShapeDtypeStruct((r, n), F32),
        compiler_params=_params(("parallel",)),
        name="matmul",
    )(x, w)


FLASH_V_PAD = 16
FLASH_SAFE_SHIFT = 60.0
FLASH_BQ = 1024
FLASH_CHAIN_Q = 512
FLASH_SCORE_BYTES = 17 * 1024 * 1024


def _flash_kernel(qT_ref, k_ref, kc_ref, vT_ref, vcT_ref, u_ref, o_ref, acc_sc, *, n_comp, n_split,
                  bk, n_kb, online):
    bq = qT_ref.shape[1] // n_split
    dv = vT_ref.shape[0]
    comp_rows = LANE // n_comp
    chains = [(c, h) for c in range(n_comp) for h in range(n_split)]
    u = u_ref[0:1, 0:1]
    row = lax.broadcasted_iota(jnp.int32, (LANE, bq), 0)
    shift_rows = jnp.where(row == 0, -u, 0.0).astype(BF16)
    q_ops = []
    for c, h in chains:
        qT = qT_ref[:, h * bq:(h + 1) * bq]
        if n_comp > 1:
            qT = jnp.where((row >= c * comp_rows) & (row < (c + 1) * comp_rows), qT,
                           jnp.zeros_like(qT))
        q_ops.append(jnp.concatenate([qT, shift_rows], axis=0))
    for i in range(len(chains)):
        acc_sc[i] = jnp.zeros(acc_sc.shape[1:], F32)

    def step(kb, vb, carry):
        n = kb.shape[0]
        ones_col = jnp.where(lax.broadcasted_iota(jnp.int32, (n, LANE), 1) == 0, 1.0, 0.0)
        ones_row = jnp.where(lax.broadcasted_iota(jnp.int32, (FLASH_V_PAD, n), 0) == 0, 1.0, 0.0)
        ka = jnp.concatenate([kb, ones_col.astype(BF16)], axis=1)
        va = jnp.concatenate([vb, ones_row.astype(BF16)], axis=0)
        scores = [jnp.dot(ka, q, preferred_element_type=F32) for q in q_ops]
        out = []
        for i, sT in enumerate(scores):
            if online:
                m = carry[i]
                m_new = jnp.maximum(m, jnp.max(sT, axis=0, keepdims=True))
                pT = jnp.exp2(sT - m_new).astype(BF16)
                acc_sc[i] = (jnp.exp2(m - m_new) * acc_sc[i]
                             + jnp.dot(va, pT, preferred_element_type=F32))
                out.append(m_new)
            else:
                pT = jnp.exp2(sT).astype(BF16)
                acc_sc[i] += jnp.dot(va, pT, preferred_element_type=F32)
                out.append(carry[i])
        return tuple(out)

    def body(j, carry):
        off = pl.multiple_of(j * bk, bk)
        return step(k_ref[pl.ds(off, bk), :], vT_ref[:, pl.ds(off, bk)], carry)

    init = tuple(jnp.full((1, bq), NEG_INF, F32) for _ in chains)
    lax.fori_loop(0, n_kb, body, step(kc_ref[...], vcT_ref[...], init))
    for i, (c, h) in enumerate(chains):
        acc = acc_sc[i]
        o_ref[0, c, :, h * bq:(h + 1) * bq] = acc[:dv] * (1.0 / acc[dv:dv + 1])


def _pick_bk(n, n_chains, bq):
    for bk in (4096, 2048, 1024, 512, 256, 128):
        if n % bk == 0 and n_chains * bk * bq * 4 <= FLASH_SCORE_BYTES:
            return bk
    raise ValueError(f"key count {n} must be a multiple of {LANE}")


def _flash_call(qT, k, kc, vT, vcT, u8, n_comp, online):
    hw, s = qT.shape
    n_heads = hw // LANE
    dv = vT.shape[0] // n_heads
    l = kc.shape[0]
    bq = min(FLASH_BQ, s)
    n_split = max(1, bq // FLASH_CHAIN_Q)
    bk = _pick_bk(s, n_comp * n_split, bq // n_split)
    kern = functools.partial(_flash_kernel, n_comp=n_comp, n_split=n_split, bk=bk, n_kb=s // bk,
                             online=online)
    return pl.pallas_call(
        kern,
        grid=(n_heads, s // bq),
        in_specs=[pl.BlockSpec((LANE, bq), lambda h, i: (h, i)),
                  pl.BlockSpec((s, LANE), lambda h, i: (0, h)),
                  pl.BlockSpec((l, LANE), lambda h, i: (0, h)),
                  pl.BlockSpec((dv, s), lambda h, i: (h, 0)),
                  pl.BlockSpec((dv, l), lambda h, i: (h, 0)),
                  pl.BlockSpec(u8.shape, lambda h, i: (0, 0))],
        out_specs=pl.BlockSpec((1, n_comp, dv, bq), lambda h, i: (h, 0, 0, i)),
        out_shape=jax.ShapeDtypeStruct((n_heads, n_comp, dv, s), F32),
        scratch_shapes=[pltpu.VMEM((n_comp * n_split, dv + FLASH_V_PAD, bq // n_split), F32)],
        compiler_params=_params(("parallel", "arbitrary")),
        name="flash_online" if online else "flash",
    )(qT, k, kc, vT, vcT, u8)


def flash(qT, k, kc, vT, vcT, u, n_comp):
    u8 = jnp.full((8, LANE), u, F32)
    return lax.cond(u <= FLASH_SAFE_SHIFT,
                    lambda *a: _flash_call(*a, n_comp, False),
                    lambda *a: _flash_call(*a, n_comp, True),
                    qT, k, kc, vT, vcT, u8)


def _score_bound(d, q_gain, k_gain, scale):
    return (d * scale * LOG2E * 1.02 * jnp.max(jnp.abs(q_gain.astype(F32)))
            * jnp.max(jnp.abs(k_gain.astype(F32))) + 0.01)


def _ctx_kernel(q_ref, k_ref, v_ref, sink_ref, o_ref, *, use_sink):
    s = lax.dot_general(q_ref[0], k_ref[0], (((1,), (1,)), ((), ())),
                        preferred_element_type=F32)
    m = jnp.max(s, axis=-1, keepdims=True)
    if use_sink:
        sk = sink_ref[0, 0:1, 0:1]
        m = jnp.maximum(m, sk)
    p = jnp.exp2(s - m)
    l = jnp.sum(p, axis=-1, keepdims=True)
    if use_sink:
        l = l + jnp.exp2(sk - m)
    o = jnp.dot(p.astype(BF16), v_ref[0], preferred_element_type=F32)
    o_ref[0] = o * (1.0 / l)


def ctx_attn(q, k, v, sink=None):
    h, l, dk = q.shape
    hk, hv, dv = k.shape[0], v.shape[0], v.shape[2]
    use_sink = sink is not None
    if sink is None:
        sink = jnp.zeros((h,), F32)
    sink3 = jnp.broadcast_to(sink.astype(F32)[:, None, None], (h, 8, LANE))
    kern = functools.partial(_ctx_kernel, use_sink=use_sink)
    return pl.pallas_call(
        kern,
        grid=(h,),
        in_specs=[pl.BlockSpec((1, l, dk), lambda i: (i, 0, 0)),
                  pl.BlockSpec((1, l, dk), lambda i: (i // (h // hk), 0, 0)),
                  pl.BlockSpec((1, l, dv), lambda i: (i // (h // hv), 0, 0)),
                  pl.BlockSpec((1, 8, LANE), lambda i: (i, 0, 0))],
        out_specs=pl.BlockSpec((1, l, dv), lambda i: (i, 0, 0)),
        out_shape=jax.ShapeDtypeStruct((h, l, dv), F32),
        compiler_params=_params(("parallel",)),
        name="ctx_attn",
    )(q, k, v, sink3)


LOCAL_BQ = 512


def _pad_rows(qT_h, slot, n_slots):
    z = jnp.zeros_like(qT_h)
    return jnp.concatenate([qT_h if s == slot else z for s in range(n_slots)], axis=0)


def _softmax_pv(s_list, v_list, extra, exact_max):
    if exact_max:
        m = functools.reduce(jnp.maximum, [jnp.max(s, axis=0, keepdims=True) for s in s_list])
        if extra is not None:
            m = jnp.maximum(m, extra)
            extra = extra - m
        s_list = [s - m for s in s_list]
    p_list = [jnp.exp2(s) for s in s_list]
    l = functools.reduce(jnp.add, [jnp.sum(p, axis=0, keepdims=True) for p in p_list])
    if extra is not None:
        l = l + jnp.exp2(extra)
    acc = functools.reduce(jnp.add, [jnp.dot(v, p.astype(BF16), preferred_element_type=F32)
                                     for v, p in zip(v_list, p_list)])
    return acc, l


def _wa_kernel(qT_ref, k_ref, vT_ref, kc_ref, vcT_ref, sh_ref, o_ref, *, seq, n_heads, n_kv,
               exact_max):
    i = pl.program_id(0)
    bq = qT_ref.shape[1]
    win = bq + 2 * WINDOW
    grp = n_heads // n_kv
    start = pl.multiple_of(jnp.clip(i * bq - WINDOW, 0, seq - win), WINDOW)
    kw = k_ref[pl.ds(start, win), :]
    vwT = vT_ref[:, pl.ds(start, win)]
    kc = kc_ref[...]
    vcT = vcT_ref[...]
    u = sh_ref[n_heads:n_heads + 1, 0:1]
    kpos = start + lax.broadcasted_iota(jnp.int32, (win, bq), 0)
    qpos = i * bq + lax.broadcasted_iota(jnp.int32, (win, bq), 1)
    mask_shift = jnp.where(jnp.abs(qpos - kpos) <= WINDOW, -u, NEG_INF)
    for h in range(n_heads):
        g = h // grp
        qTp = _pad_rows(qT_ref[h * HEAD_DIM:(h + 1) * HEAD_DIM, :], g, n_kv)
        s_loc = jnp.dot(kw, qTp, preferred_element_type=F32) + mask_shift
        s_ctx = jnp.dot(kc, qTp, preferred_element_type=F32) - u
        acc, l = _softmax_pv([s_loc, s_ctx], [vwT, vcT], sh_ref[h:h + 1, 0:1], exact_max)
        o_ref[h * HEAD_DIM:(h + 1) * HEAD_DIM, :] = (acc[g * HEAD_DIM:(g + 1) * HEAD_DIM]
                                                     * (1.0 / l))


def _wa_call(qT, k, vT, kc, vcT, sh, exact_max):
    hd, s = qT.shape
    kw = k.shape[1]
    l = kc.shape[0]
    bq = min(LOCAL_BQ, s)
    assert s % bq == 0 and s >= bq + 2 * WINDOW
    kern = functools.partial(_wa_kernel, seq=s, n_heads=hd // HEAD_DIM, n_kv=kw // HEAD_DIM,
                             exact_max=exact_max)
    full = lambda i: (0, 0)
    return pl.pallas_call(
        kern,
        grid=(s // bq,),
        in_specs=[pl.BlockSpec((hd, bq), lambda i: (0, i)),
                  pl.BlockSpec((s, kw), full), pl.BlockSpec((kw, s), full),
                  pl.BlockSpec((l, kw), full), pl.BlockSpec((kw, l), full),
                  pl.BlockSpec(sh.shape, full)],
        out_specs=pl.BlockSpec((hd, bq), lambda i: (0, i)),
        out_shape=jax.ShapeDtypeStruct((hd, s), F32),
        compiler_params=_params(("parallel",)),
        name="wa_attn_max" if exact_max else "wa_attn",
    )(qT, k, vT, kc, vcT, sh)


def wa_attn(qT, k, vT, kc, vcT, sink, u):
    n_heads = qT.shape[0] // HEAD_DIM
    rows = jnp.concatenate([sink.astype(F32) - u, jnp.reshape(u, (1,)).astype(F32),
                            jnp.zeros((2 * 8 - n_heads - 1,), F32)])
    sh = jnp.broadcast_to(rows[:, None], (rows.shape[0], LANE))
    return lax.cond(u <= FLASH_SAFE_SHIFT,
                    lambda *a: _wa_call(*a, False), lambda *a: _wa_call(*a, True),
                    qT, k, vT, kc, vcT, sh)


NA_KEY_ROWS = 3 * NA_KH


def _na_kernel(qT_ref, km_ref, k0_ref, kp_ref, vm_ref, v0_ref, vp_ref, kc_ref, vcT_ref,
               bias_ref, sh_ref, o_ref, *, exact_max):
    kwin = jnp.concatenate([km_ref[...], k0_ref[...], kp_ref[...]], axis=0)
    vwinT = jnp.concatenate([vm_ref[...], v0_ref[...], vp_ref[...]], axis=1)
    kc = kc_ref[...]
    vcT = vcT_ref[...]
    u = sh_ref[0:1, 0:1]
    for hh in range(2):
        rows = slice(hh * HEAD_DIM, (hh + 1) * HEAD_DIM)
        qTp = _pad_rows(qT_ref[rows, :], hh, 2)
        s_loc = jnp.dot(kwin, qTp, preferred_element_type=F32) + bias_ref[0, hh]
        s_ctx = jnp.dot(kc, qTp, preferred_element_type=F32) - u
        acc, l = _softmax_pv([s_loc, s_ctx], [vwinT, vcT], None, exact_max)
        o_ref[rows, :] = acc[rows] * (1.0 / l)


def na_bias_table(rpb, rows, shift):
    n_h = rpb.shape[0]
    a = jnp.arange(NA_KH)
    b = jnp.arange(NA_KEY_ROWS)
    c = jnp.arange(GRID_W)
    kc = jnp.arange(GRID_W)
    c0 = jnp.clip(c - NA_KW // 2, 0, GRID_W - NA_KW)
    col_ok = (kc[None, :] >= c0[:, None]) & (kc[None, :] < c0[:, None] + NA_KW)
    col_rel = jnp.clip(kc[None, :] - c[:, None] + (NA_KW - 1), 0, 2 * NA_KW - 2)
    row_rel = jnp.clip(b[None, :] - a[:, None] - 1, 0, 2 * NA_KH - 2)

    def edge_ok(r_base):
        r0 = jnp.clip(r_base + a - NA_KH // 2, 0, rows - NA_KH)
        key_row = r_base - NA_KH + b
        return (key_row[None, :] >= r0[:, None]) & (key_row[None, :] < r0[:, None] + NA_KH)

    inner_ok = ((b[None, :] >= a[:, None] + NA_KH // 2)
                & (b[None, :] < a[:, None] + NA_KH // 2 + NA_KH))
    row_ok = jnp.stack([edge_ok(0), inner_ok, edge_ok(rows - NA_KH)])
    e_c = jax.nn.one_hot(col_rel, 2 * NA_KW - 1, dtype=F32)
    e_r = jax.nn.one_hot(row_rel, 2 * NA_KH - 1, dtype=F32)
    hi = lax.Precision.HIGHEST
    t = jnp.einsum('hrx,ckx->hrck', rpb.astype(F32), e_c, precision=hi)
    t = jnp.einsum('abr,hrck->habck', e_r, t, precision=hi) * LOG2E - shift
    ok = row_ok[:, None, :, :, None, None] & col_ok[None, None, None, None]
    t = jnp.where(ok, t[None], NEG_INF)
    t = jnp.transpose(t, (0, 1, 3, 5, 2, 4))
    return t.reshape(3, n_h, NA_KEY_ROWS * GRID_W, NA_KH * GRID_W)


def _na_call(qT, k, vT, kc, vcT, bias, sh, exact_max):
    hd, s = qT.shape
    l = kc.shape[0]
    bq = NA_KH * GRID_W
    nb = s // bq
    pw = 2 * HEAD_DIM
    assert s % bq == 0 and bq == LOCAL_BQ
    prev = lambda i: jnp.maximum(i - 1, 0)
    nxt = lambda i: jnp.minimum(i + 1, nb - 1)
    var = lambda i: jnp.where(i == 0, 0, jnp.where(i == nb - 1, 2, 1))
    kspec = lambda f: pl.BlockSpec((bq, pw), lambda p, i: (f(i), p))
    vspec = lambda f: pl.BlockSpec((pw, bq), lambda p, i: (p, f(i)))
    cur = lambda i: i
    kern = functools.partial(_na_kernel, exact_max=exact_max)
    return pl.pallas_call(
        kern,
        grid=(hd // pw, nb),
        in_specs=[pl.BlockSpec((pw, bq), lambda p, i: (p, i)),
                  kspec(prev), kspec(cur), kspec(nxt), vspec(prev), vspec(cur), vspec(nxt),
                  pl.BlockSpec((l, pw), lambda p, i: (0, p)),
                  pl.BlockSpec((pw, l), lambda p, i: (p, 0)),
                  pl.BlockSpec((1, 2, NA_KEY_ROWS * GRID_W, bq), lambda p, i: (var(i), p, 0, 0)),
                  pl.BlockSpec(sh.shape, lambda p, i: (0, 0))],
        out_specs=pl.BlockSpec((pw, bq), lambda p, i: (p, i)),
        out_shape=jax.ShapeDtypeStruct((hd, s), F32),
        compiler_params=_params(("parallel", "arbitrary")),
        name="na_attn_max" if exact_max else "na_attn",
    )(qT, k, k, k, vT, vT, vT, kc, vcT, bias, sh)


def na_attn(qT, k, vT, kc, vcT, rpb, u_qk):
    s = qT.shape[1]
    u = u_qk + jnp.maximum(jnp.max(rpb.astype(F32)) * LOG2E, 0.0)
    bias = na_bias_table(rpb, s // GRID_W, u)
    sh = jnp.full((8, LANE), u, F32)
    return lax.cond(u <= FLASH_SAFE_SHIFT,
                    lambda *a: _na_call(*a, False), lambda *a: _na_call(*a, True),
                    qT, k, vT, kc, vcT, bias, sh)


def _outproj_kernel(o_ref, x_ref, w_ref, gate_ref, g_ref, sc_ref, sh_ref, wr_ref,
                    xo_ref, hf_ref, lg_ref):
    acc = jnp.dot(o_ref[...], w_ref[...], preferred_element_type=F32)
    xn = x_ref[...] + gate_ref[...] * acc
    xo_ref[...] = xn
    ms = jnp.mean(xn * xn, axis=-1, keepdims=True)
    y = xn * lax.rsqrt(ms + NORM_EPS) * g_ref[...]
    hf = (y * sc_ref[...] + sh_ref[...]).astype(BF16)
    hf_ref[...] = hf
    lg_ref[...] = jnp.dot(hf, wr_ref[...], preferred_element_type=F32)


def _outproj_t_kernel(na_ref, wa_ref, df_ref, ml_ref, dfg_ref, lam_ref, x_ref, w_ref, gate_ref,
                      g_ref, sc_ref, sh_ref, wr_ref, xo_ref, hf_ref, lg_ref):
    lam = lam_ref[0:1, 0:1]
    parts = [na_ref[...], wa_ref[...]]
    for h in range(df_ref.shape[0]):
        dd = df_ref[h, 0] - lam * df_ref[h, 1]
        ms = jnp.mean(dd * dd, axis=0, keepdims=True)
        parts.append(dd * lax.rsqrt(ms + NORM_EPS) * dfg_ref[...])
    parts.append(ml_ref[...])
    o = jnp.concatenate([jnp.transpose(t).astype(BF16) for t in parts], axis=1)
    acc = jnp.dot(o, w_ref[...], preferred_element_type=F32)
    xn = x_ref[...] + gate_ref[...] * acc
    xo_ref[...] = xn
    ms = jnp.mean(xn * xn, axis=-1, keepdims=True)
    y = xn * lax.rsqrt(ms + NORM_EPS) * g_ref[...]
    hf = (y * sc_ref[...] + sh_ref[...]).astype(BF16)
    hf_ref[...] = hf
    lg_ref[...] = jnp.dot(hf, wr_ref[...], preferred_element_type=F32)


def outproj_t(oT_na, oT_wa, oT_df, oT_ml, df_gain, lam, x, w, gate, g, sc1, sh, wr):
    r, d = x.shape
    tm = min(r, 256)
    nh, _, dv2, _ = oT_df.shape
    wdt = oT_na.shape[0]
    dfg = jnp.broadcast_to(df_gain.astype(F32)[:, None], (dv2, tm))
    lam8 = jnp.full((8, LANE), lam, F32)
    vec = pl.BlockSpec((1, d), lambda i: (0, 0))
    row = lambda n: pl.BlockSpec((tm, n), lambda i: (i, 0))
    colblk = pl.BlockSpec((wdt, tm), lambda i: (0, i))
    const = lambda shape: pl.BlockSpec(shape, lambda i: tuple(0 for _ in shape))
    return pl.pallas_call(
        _outproj_t_kernel,
        grid=(r // tm,),
        in_specs=[colblk, colblk, pl.BlockSpec((nh, 2, dv2, tm), lambda i: (0, 0, 0, i)), colblk,
                  const((dv2, tm)), const((8, LANE)), row(d), const(w.shape), vec, vec, vec, vec,
                  const((d, LANE))],
        out_specs=[row(d), row(d), row(LANE)],
        out_shape=[jax.ShapeDtypeStruct((r, d), F32), jax.ShapeDtypeStruct((r, d), BF16),
                   jax.ShapeDtypeStruct((r, LANE), F32)],
        compiler_params=_params(("parallel",)),
        name="outproj_t",
    )(oT_na, oT_wa, oT_df, oT_ml, dfg, lam8, x, w, gate, g, sc1, sh, wr)


def outproj(o, x, w, gate, g, sc1, sh, wr):
    r, d = x.shape
    k = o.shape[1]
    tm = min(r, 256)
    vec = pl.BlockSpec((1, d), lambda i: (0, 0))
    row = lambda n: pl.BlockSpec((tm, n), lambda i: (i, 0))
    return pl.pallas_call(
        _outproj_kernel,
        grid=(r // tm,),
        in_specs=[row(k), row(d), pl.BlockSpec((k, d), lambda i: (0, 0)), vec, vec, vec, vec,
                  pl.BlockSpec((d, LANE), lambda i: (0, 0))],
        out_specs=[row(d), row(d), row(LANE)],
        out_shape=[jax.ShapeDtypeStruct((r, d), F32), jax.ShapeDtypeStruct((r, d), BF16),
                   jax.ShapeDtypeStruct((r, LANE), F32)],
        compiler_params=_params(("parallel",)),
        name="outproj",
    )(o, x, w, gate, g, sc1, sh, wr)


def _moe_kernel(be_ref, nv_ref, x_ref, w1_ref, w3_ref, w2_ref, o_ref):
    i = pl.program_id(0)

    @pl.when(i < nv_ref[0])
    def _():
        x = x_ref[...]
        a = jnp.dot(x, w1_ref[0], preferred_element_type=F32)
        b = jnp.dot(x, w3_ref[0], preferred_element_type=F32)
        hmid = (a * _sigmoid(a) * b).astype(BF16)
        o_ref[...] = jnp.dot(hmid, w2_ref[0], preferred_element_type=F32)

    @pl.when(i >= nv_ref[0])
    def _():
        o_ref[...] = jnp.zeros(o_ref.shape, o_ref.dtype)


def moe_blocks(x, w1, w3, w2, blk_e, n_valid):
    n, d = x.shape
    f = w1.shape[2]
    n_blk = n // MOE_BLOCK
    grid_spec = pltpu.PrefetchScalarGridSpec(
        num_scalar_prefetch=2,
        grid=(n_blk,),
        in_specs=[pl.BlockSpec((MOE_BLOCK, d), lambda i, be, nv: (i, 0)),
                  pl.BlockSpec((1, d, f), lambda i, be, nv: (be[i], 0, 0)),
                  pl.BlockSpec((1, d, f), lambda i, be, nv: (be[i], 0, 0)),
                  pl.BlockSpec((1, f, d), lambda i, be, nv: (be[i], 0, 0))],
        out_specs=pl.BlockSpec((MOE_BLOCK, d), lambda i, be, nv: (i, 0)),
    )
    return pl.pallas_call(
        _moe_kernel,
        grid_spec=grid_spec,
        out_shape=jax.ShapeDtypeStruct((n, d), F32),
        compiler_params=_params(("arbitrary",)),
        name="moe_blocks",
    )(blk_e, n_valid, x, w1, w3, w2)


def _rms(x, g):
    y = x * lax.rsqrt(jnp.mean(x * x, axis=-1, keepdims=True) + NORM_EPS)
    return y * g


def _rope_tables(seq_len, rot_dim):
    n = rot_dim // 4
    t = jnp.arange(seq_len)
    row = (t // GRID_W).astype(F32)[:, None]
    col = (t % GRID_W).astype(F32)[:, None]
    inv = ROPE_THETA ** (-jnp.arange(n, dtype=F32) / n)
    return (jnp.cos(row * inv), jnp.sin(row * inv), jnp.cos(col * inv), jnp.sin(col * inv))


def _heads_first(t):
    return jnp.transpose(t, (1, 0, 2))


def _route(logits, b_router):
    n = logits.shape[0]
    per = N_EXPERTS // N_EXPERT_GROUPS
    scores = jax.nn.sigmoid(logits.astype(F32))
    sel = scores + b_router.astype(F32)
    grp = sel.reshape(n, N_EXPERT_GROUPS, per)
    gscore = None
    for a in range(per):
        for bb in range(a + 1, per):
            pair = grp[..., a] + grp[..., bb]
            gscore = pair if gscore is None else jnp.maximum(gscore, pair)
    gidx = jnp.argmax(gscore, axis=-1)
    eids = jnp.arange(N_EXPERTS)
    masked = jnp.where((eids // per)[None, :] == gidx[:, None], sel, -jnp.inf)
    e1 = jnp.argmax(masked, axis=-1)
    e2 = jnp.argmax(jnp.where(eids[None, :] == e1[:, None], -jnp.inf, masked), axis=-1)
    eidx = jnp.stack([e1, e2], axis=1).astype(jnp.int32)
    wts = jnp.take_along_axis(scores, eidx, axis=1)
    wts = wts / jnp.sum(wts, axis=-1, keepdims=True) * ROUTED_SCALE
    return eidx, wts


def _prefix_counts(onehot):
    n, e = onehot.shape
    ch = onehot.astype(F32).reshape(n // LANE, LANE, e)
    tri = jnp.tril(jnp.ones((LANE, LANE), F32))
    within = jnp.einsum('ij,cjk->cik', tri, ch)
    tot = within[:, -1, :]
    base = jnp.cumsum(tot, axis=0) - tot
    return (within + base[:, None, :]).reshape(n, e)


def _dispatch(eidx):
    n = eidx.shape[0]
    n_assign = n * TOP_K
    assert n_assign % LANE == 0
    e_flat = eidx.reshape(-1)
    tok = jnp.repeat(jnp.arange(n, dtype=jnp.int32), TOP_K)
    onehot = e_flat[:, None] == jnp.arange(N_EXPERTS)[None, :]
    csum = _prefix_counts(onehot)
    counts = csum[-1].astype(jnp.int32)
    rank = jnp.sum(jnp.where(onehot, csum, 0.0), axis=1).astype(jnp.int32) - 1
    padded = (counts + MOE_BLOCK - 1) // MOE_BLOCK * MOE_BLOCK
    pend = jnp.cumsum(padded)
    pstart = pend - padded
    dest = jnp.sum(jnp.where(onehot, pstart[None, :], 0), axis=1) + rank
    n_slots = (n_assign + N_EXPERTS * (MOE_BLOCK - 1) + MOE_BLOCK - 1) // MOE_BLOCK * MOE_BLOCK
    n_blk = n_slots // MOE_BLOCK
    slot_tok = jnp.full((n_slots,), n, dtype=jnp.int32).at[dest].set(tok)
    slot_of = dest.reshape(n, TOP_K)
    blk_start = jnp.arange(n_blk, dtype=jnp.int32) * MOE_BLOCK
    blk_e = jnp.minimum(jnp.sum(blk_start[:, None] >= pend[None, :], axis=1), N_EXPERTS - 1)
    n_valid = (pend[-1] // MOE_BLOCK).reshape(1)
    return slot_tok, slot_of, blk_e.astype(jnp.int32), n_valid.astype(jnp.int32)


PREP_TM = 2048


def _slab_prep_kernel(*refs, seg, count, rot, scale, transposed, has_add, raw):
    refs = list(refs)
    o_ref = refs.pop()
    x = refs[0][...].astype(F32)
    if not raw:
        g_ref = refs[1]
        nxt = 2
        if has_add:
            x = x + refs[nxt][...]
            nxt += 1
        sq = x * x
        hi = sq.astype(BF16)
        lo = (sq - hi.astype(F32)).astype(BF16)
        r = lax.broadcasted_iota(jnp.int32, (LANE, LANE), 0) // seg
        c = lax.broadcasted_iota(jnp.int32, (LANE, LANE), 1) // seg
        blk = jnp.where(r == c, 1.0, 0.0).astype(BF16)
        ssum = (jnp.dot(hi, blk, preferred_element_type=F32)
                + jnp.dot(lo, blk, preferred_element_type=F32))
        x = x * lax.rsqrt(ssum * (1.0 / count) + NORM_EPS) * g_ref[...]
        if rot:
            cos, s_next, s_prev = refs[nxt][...], refs[nxt + 1][...], refs[nxt + 2][...]
            x = (x * cos + pltpu.roll(x, LANE - rot, 1) * s_next + pltpu.roll(x, rot, 1) * s_prev)
        x = x * scale
    o_ref[...] = (jnp.transpose(x) if transposed else x).astype(o_ref.dtype)


def slab_prep(x, col0, n_slabs, gain=None, rope=None, rot=0, add=None, seg=HEAD_DIM,
              count=HEAD_DIM, scale=1.0, transposed=False):
    r = x.shape[0]
    tm = min(PREP_TM, r)
    raw = gain is None
    ops = [x]
    specs = [pl.BlockSpec((tm, LANE), lambda i, j: (i, col0 + j))]
    rowspec = pl.BlockSpec((tm, LANE), lambda i, j: (i, 0))
    if not raw:
        ops.append(gain.astype(F32).reshape(1, LANE))
        specs.append(pl.BlockSpec((1, LANE), lambda i, j: (0, 0)))
        if add is not None:
            ops.append(add)
            specs.append(rowspec)
        if rope is not None:
            ops += list(rope)
            specs += [rowspec] * 3
    kern = functools.partial(_slab_prep_kernel, seg=seg, count=count,
                             rot=rot if rope is not None else 0, scale=scale,
                             transposed=transposed, has_add=add is not None, raw=raw)
    if transposed:
        out_spec = pl.BlockSpec((LANE, tm), lambda i, j: (j, i))
        out_shape = jax.ShapeDtypeStruct((n_slabs * LANE, r), BF16)
    else:
        out_spec = pl.BlockSpec((tm, LANE), lambda i, j: (i, j))
        out_shape = jax.ShapeDtypeStruct((r, n_slabs * LANE), BF16)
    return pl.pallas_call(
        kern,
        grid=(r // tm, n_slabs),
        in_specs=specs,
        out_specs=out_spec,
        out_shape=out_shape,
        compiler_params=_params(("parallel", "arbitrary")),
        name="slab_prep",
    )(*ops)


def _rope_slab_tables(seq_len, width, offset, reps):
    n = width // 4
    cr, sr, cc, sc = _rope_tables(seq_len, width)
    one = jnp.ones((seq_len, offset), F32)
    zero = jnp.zeros((seq_len, offset), F32)
    zq = jnp.zeros((seq_len, n), F32)
    rest = LANE // reps - offset - width
    pad1 = jnp.ones((seq_len, rest), F32)
    pad0 = jnp.zeros((seq_len, rest), F32)
    cos = jnp.concatenate([one, cr, cr, cc, cc, pad1] * reps, axis=1)
    s_next = jnp.concatenate([zero, -sr, zq, -sc, zq, pad0] * reps, axis=1)
    s_prev = jnp.concatenate([zero, zq, sr, zq, sc, pad0] * reps, axis=1)
    return cos, s_next, s_prev


def _gain2(g):
    return jnp.concatenate([g, g]).astype(F32)


def _ctx_heads(t, n_heads):
    return _heads_first(t.reshape(t.shape[0], n_heads, -1)).astype(BF16)


def _qk_prep(p, pc, off, w, q_gain, k_gain, rope, wk=None):
    wk = w if wk is None else wk
    qs = HEAD_DIM ** -0.5 * LOG2E
    gq, gk = _gain2(q_gain), _gain2(k_gain)
    c0 = off // LANE
    rot = HEAD_DIM // 4
    qT = slab_prep(p, c0, w // LANE, gq, rope, rot, scale=qs, transposed=True)
    k = slab_prep(p, c0 + w // LANE, wk // LANE, gk, rope, rot)
    qc = slab_prep(pc, c0, w // LANE, gq, scale=qs)
    kc = slab_prep(pc, c0 + w // LANE, wk // LANE, gk)
    return qT, k, qc, kc


def _mla_prep(p, pc, qa_gain, kva_gain, w_uq, w_ukv, q_gain, k_gain, rope):
    hh, dn, dr, dv = MLA_HEADS, MLA_NOPE, MLA_ROPE, MLA_V
    dq = dn + dr
    qs = dq ** -0.5 * LOG2E
    padh = LANE - dq
    w_q = jnp.pad(w_uq.reshape(MLA_Q_LORA, hh, dq), ((0, 0), (0, 0), (0, padh)))
    w_kv = w_ukv.reshape(MLA_KV_LORA, hh, dn + dv)
    w_k = jnp.pad(w_kv[..., :dn], ((0, 0), (0, 0), (0, LANE - dn)))
    w_all = jnp.concatenate([w_k.reshape(MLA_KV_LORA, hh * LANE),
                             w_kv[..., dn:].reshape(MLA_KV_LORA, hh * dv)], axis=1).astype(BF16)
    w_q = w_q.reshape(MLA_Q_LORA, hh * LANE).astype(BF16)
    gq = jnp.pad(q_gain.astype(F32), (0, padh))
    gk = jnp.pad(k_gain.astype(F32), (0, padh))
    off = MLA_OFF

    def project(t, tabs):
        lora = t[:, off:off + MLA_Q_LORA + MLA_KV_LORA + dr].astype(F32)
        cq = _rms(lora[:, :MLA_Q_LORA], qa_gain).astype(BF16)
        ckv = _rms(lora[:, MLA_Q_LORA:MLA_Q_LORA + MLA_KV_LORA], kva_gain).astype(BF16)
        k_rope = jnp.pad(lora[:, MLA_Q_LORA + MLA_KV_LORA:], ((0, 0), (dn, padh)))
        kv = matmul(ckv, w_all)
        q = slab_prep(matmul(cq, w_q), 0, hh, gq, tabs, dr // 4, seg=LANE, count=dq, scale=qs,
                      transposed=tabs is not None)
        k = slab_prep(kv, 0, hh, gk, tabs, dr // 4, add=k_rope, seg=LANE, count=dq)
        return q, k, kv

    qT, k, kv = project(p, rope)
    qc, kc, kvc = project(pc, None)
    vT = slab_prep(kv, hh, hh * dv // LANE, transposed=True)
    vc = kvc[:, hh * LANE:].astype(BF16)
    return qT, k, kc, vT, vc.T, qc, vc


def _merge(o):
    return jnp.transpose(o, (1, 0, 2)).reshape(o.shape[1], -1)


def _diff_post(o1, o2, lam, sub_gain, lambda_init):
    return _rms(o1 - lam * o2, sub_gain) * (1.0 - lambda_init)


def _moe(tokens_bf16, logits, b_router, w1, w3, w2, sw1, sw3, sw2):
    n, d = tokens_bf16.shape
    eidx, gate = _route(logits[:, :N_EXPERTS], b_router)
    slot_tok, slot_of, blk_e, n_valid = _dispatch(eidx)
    h_pad = jnp.concatenate([tokens_bf16, jnp.zeros((1, d), BF16)], axis=0)
    xb = h_pad[slot_tok]
    yb = moe_blocks(xb, w1, w3, w2, blk_e, n_valid)
    y0, y1 = lax.optimization_barrier((yb[slot_of[:, 0]], yb[slot_of[:, 1]]))
    routed = y0 * gate[:, 0:1] + y1 * gate[:, 1:2]
    n_sh = _round_up(n, MOE_BLOCK) // MOE_BLOCK
    shared = moe_blocks(tokens_bf16, sw1[None], sw3[None], sw2[None],
                        jnp.zeros((n_sh,), jnp.int32), jnp.full((1,), n_sh, jnp.int32))
    return routed + shared


def kernel(x, c, ctx, c_ctx, w_ada, b_ada, g_attn, g_ffn, w_in, w_out, na_q_gain, na_k_gain, na_rpb, wa_q_gain, wa_k_gain, wa_sink, diff_q_gain, diff_k_gain, diff_lq1, diff_lk1, diff_lq2, diff_lk2, diff_sub_gain, mla_qa_gain, mla_kva_gain, mla_w_uq, mla_w_ukv, mla_q_gain, mla_k_gain, w_router, b_router, moe_w1, moe_w3, moe_w2, sh_w1, sh_w3, sh_w2):
    b, s, d = x.shape
    assert b == 1
    n_ctx = ctx.shape[1]
    depth = w_ada.shape[0]
    xl = x[0]
    xc = ctx[0]
    rope_head = _rope_slab_tables(s, HEAD_DIM, 0, 2)
    rope_mla = _rope_slab_tables(s, MLA_ROPE, MLA_NOPE, 1)

    cond8 = jnp.zeros((8, d), F32).at[0].set(c[0]).at[1].set(c_ctx)
    mod = modvec(cond8, w_ada, b_ada)
    tn = INPROJ_TN
    in_pad = _round_up(IN_COLS, tn)
    wr = jnp.pad(w_router, ((0, 0), (0, LANE - N_EXPERTS))).astype(BF16)
    inv_sqrt_d = HEAD_DIM ** -0.5

    for l in range(depth):
        with_ctx = l < depth - 1
        m_lat = mod[l, 0].reshape(6, 1, d)
        m_ctx = mod[l, 1].reshape(6, 1, d)
        w_in_l = jnp.pad(w_in[l], ((0, 0), (0, in_pad - IN_COLS))).astype(BF16)
        w_out_l = w_out[l].astype(BF16)
        g_a = g_attn[l][None]
        g_f = g_ffn[l][None]

        p = inproj(xl, g_a, 1.0 + m_lat[1], m_lat[0], w_in_l, tn)
        pc = inproj(xc, g_a, 1.0 + m_ctx[1], m_ctx[0], w_in_l, tn)

        w = NA_HEADS * HEAD_DIM
        qT, k, qc, kc = _qk_prep(p, pc, NA_OFF, w, na_q_gain[l], na_k_gain[l], None)
        vT = slab_prep(p, (NA_OFF + 2 * w) // LANE, w // LANE, transposed=True)
        vc = pc[:, NA_OFF + 2 * w:WA_OFF]
        u_na = _score_bound(HEAD_DIM, na_q_gain[l], na_k_gain[l], inv_sqrt_d)
        oT_na = na_attn(qT, k, vT, kc, vc.T, na_rpb[l], u_na)
        oc_na = (_merge(ctx_attn(_ctx_heads(qc, NA_HEADS), _ctx_heads(kc, NA_HEADS),
                                 _ctx_heads(vc, NA_HEADS))) if with_ctx else None)

        w, wk = WA_HEADS * HEAD_DIM, WA_KV_HEADS * HEAD_DIM
        qT, k, qc, kc = _qk_prep(p, pc, WA_OFF, w, wa_q_gain[l], wa_k_gain[l], rope_head, wk)
        vT = slab_prep(p, (WA_OFF + w + wk) // LANE, wk // LANE, transposed=True)
        vc = pc[:, WA_OFF + w + wk:DIFF_OFF]
        u_wa = _score_bound(HEAD_DIM, wa_q_gain[l], wa_k_gain[l], inv_sqrt_d)
        sink2 = wa_sink[l].astype(F32) * LOG2E
        oT_wa = wa_attn(qT, k, vT, kc, vc.T, sink2, u_wa)
        oc_wa = (_merge(ctx_attn(_ctx_heads(qc, WA_HEADS), _ctx_heads(kc, WA_KV_HEADS),
                                 _ctx_heads(vc, WA_KV_HEADS), sink2)) if with_ctx else None)

        lambda_init = 0.8 - 0.6 * math.exp(-0.3 * l)
        lam = (jnp.exp(jnp.sum(diff_lq1[l].astype(F32) * diff_lk1[l].astype(F32)))
               - jnp.exp(jnp.sum(diff_lq2[l].astype(F32) * diff_lk2[l].astype(F32))) + lambda_init)
        w = DIFF_HEADS * 2 * DIFF_DIM
        qT, k, qc, kc = _qk_prep(p, pc, DIFF_OFF, w, diff_q_gain[l], diff_k_gain[l], rope_head)
        vT = slab_prep(p, (DIFF_OFF + 2 * w) // LANE, w // LANE, transposed=True)
        vc = pc[:, DIFF_OFF + 2 * w:MLA_OFF]
        u_df = _score_bound(DIFF_DIM, diff_q_gain[l], diff_k_gain[l], DIFF_DIM ** -0.5)
        oT_df = flash(qT, k, kc, vT, vc.T, u_df, 2)
        oc_df = None
        if with_ctx:
            oc = ctx_attn(_ctx_heads(qc, 2 * DIFF_HEADS), _ctx_heads(kc, 2 * DIFF_HEADS),
                          _ctx_heads(vc, DIFF_HEADS)).reshape(DIFF_HEADS, 2, n_ctx, DIFF_V_DIM)
            oc_df = _diff_post(jnp.transpose(oc[:, 0], (1, 0, 2)), jnp.transpose(oc[:, 1], (1, 0, 2)),
                               lam, diff_sub_gain[l], lambda_init).reshape(n_ctx, -1)

        qT, k, kc, vT, vcT, qc, vc = _mla_prep(
            p, pc, mla_qa_gain[l], mla_kva_gain[l], mla_w_uq[l], mla_w_ukv[l], mla_q_gain[l],
            mla_k_gain[l], rope_mla)
        u_ml = _score_bound(MLA_NOPE + MLA_ROPE, mla_q_gain[l], mla_k_gain[l],
                            (MLA_NOPE + MLA_ROPE) ** -0.5)
        oT_ml = flash(qT, k, kc, vT, vcT, u_ml, 1).reshape(MLA_HEADS * MLA_V, s)
        oc_ml = (_merge(ctx_attn(_ctx_heads(qc, MLA_HEADS), _ctx_heads(kc, MLA_HEADS),
                                 _ctx_heads(vc, MLA_HEADS))) if with_ctx else None)

        xl, hf, lg = outproj_t(oT_na, oT_wa, oT_df, oT_ml,
                               diff_sub_gain[l].astype(F32) * (1.0 - lambda_init), lam,
                               xl, w_out_l, m_lat[2], g_f, 1.0 + m_lat[4], m_lat[3], wr)
        if with_ctx:
            oc_cat = jnp.concatenate([oc_na, oc_wa, oc_df, oc_ml], axis=-1).astype(BF16)
            xc, hfc, lgc = outproj(oc_cat, xc, w_out_l, m_ctx[2], g_f, 1.0 + m_ctx[4], m_ctx[3], wr)
            tokens = jnp.concatenate([hfc, hf], axis=0)
            logits = jnp.concatenate([lgc, lg], axis=0)
        else:
            tokens, logits = hf, lg
        y = _moe(tokens, logits, b_router, moe_w1[l].astype(BF16), moe_w3[l].astype(BF16),
                 moe_w2[l].astype(BF16), sh_w1[l].astype(BF16), sh_w3[l].astype(BF16),
                 sh_w2[l].astype(BF16))
        if with_ctx:
            xc = xc + m_ctx[5] * y[:n_ctx]
            y = y[n_ctx:]
        xl = xl + m_lat[5] * y
    return xl[None]
```

```python
import functools
import math

import jax
import jax.numpy as jnp
import numpy as np
from jax import lax
from jax.experimental import pallas as pl
from jax.experimental.pallas import tpu as pltpu

F32 = jnp.float32
BF16 = jnp.bfloat16

GRID_W = 64
HEAD_DIM = 64
ROPE_THETA = 10000.0
NORM_EPS = 1e-6
NEG_INF = -1e30
WINDOW = 128
NA_HEADS = 8
NA_KH = 8
NA_KW = 16
WA_HEADS = 8
WA_KV_HEADS = 2
DIFF_HEADS = 4
DIFF_DIM = 64
DIFF_V_DIM = 128
MLA_HEADS = 8
MLA_NOPE = 64
MLA_ROPE = 32
MLA_V = 64
MLA_Q_LORA = 384
MLA_KV_LORA = 128
N_EXPERTS = 16
N_EXPERT_GROUPS = 4
TOP_K = 2
ROUTED_SCALE = 1.0
MOE_BLOCK = 256

NA_COLS = 3 * NA_HEADS * HEAD_DIM
WA_COLS = (WA_HEADS + 2 * WA_KV_HEADS) * HEAD_DIM
DIFF_COLS = 3 * DIFF_HEADS * 2 * DIFF_DIM
MLA_COLS = MLA_Q_LORA + MLA_KV_LORA + MLA_ROPE
NA_OFF = 0
WA_OFF = NA_OFF + NA_COLS
DIFF_OFF = WA_OFF + WA_COLS
MLA_OFF = DIFF_OFF + DIFF_COLS
IN_COLS = MLA_OFF + MLA_COLS

LANE = 128
INPROJ_TN = 768
LOG2E = math.log2(math.e)
VMEM_LIMIT = 48 * 1024 * 1024


def _round_up(n, m):
    return (n + m - 1) // m * m


def _params(sem):
    return pltpu.CompilerParams(dimension_semantics=sem, vmem_limit_bytes=VMEM_LIMIT)


def _sigmoid(x):
    return 1.0 / (1.0 + jnp.exp(-x))


def _modvec_kernel(c_ref, w_ref, b_ref, o_ref):
    a = c_ref[...]
    a = a * _sigmoid(a)
    o_ref[0] = jnp.dot(a, w_ref[0], preferred_element_type=F32,
                       precision=lax.Precision.HIGHEST) + b_ref[0]


def modvec(cond8, w_ada, b_ada):
    depth, d, n = w_ada.shape
    tn = 1024
    return pl.pallas_call(
        _modvec_kernel,
        grid=(depth, n // tn),
        in_specs=[pl.BlockSpec((8, d), lambda l, j: (0, 0)),
                  pl.BlockSpec((1, d, tn), lambda l, j: (l, 0, j)),
                  pl.BlockSpec((1, 1, tn), lambda l, j: (l, 0, j))],
        out_specs=pl.BlockSpec((1, 8, tn), lambda l, j: (l, 0, j)),
        out_shape=jax.ShapeDtypeStruct((depth, 8, n), F32),
        compiler_params=_params(("parallel", "parallel")),
        name="modvec",
    )(cond8, w_ada, b_ada.reshape(depth, 1, n))


def _inproj_kernel(x_ref, g_ref, sc_ref, sh_ref, w_ref, o_ref, h_sc):
    @pl.when(pl.program_id(1) == 0)
    def _():
        x = x_ref[...]
        ms = jnp.mean(x * x, axis=-1, keepdims=True)
        y = x * lax.rsqrt(ms + NORM_EPS) * g_ref[...]
        h_sc[...] = (y * sc_ref[...] + sh_ref[...]).astype(BF16)

    o_ref[...] = jnp.dot(h_sc[...], w_ref[...], preferred_element_type=F32).astype(o_ref.dtype)


def inproj(x, g, sc1, sh, w, tn):
    r, d = x.shape
    n = w.shape[1]
    tm = min(r, 1024)
    vec = pl.BlockSpec((1, d), lambda i, j: (0, 0))
    return pl.pallas_call(
        _inproj_kernel,
        grid=(r // tm, n // tn),
        in_specs=[pl.BlockSpec((tm, d), lambda i, j: (i, 0)), vec, vec, vec,
                  pl.BlockSpec((d, tn), lambda i, j: (0, j))],
        out_specs=pl.BlockSpec((tm, tn), lambda i, j: (i, j)),
        out_shape=jax.ShapeDtypeStruct((r, n), BF16),
        scratch_shapes=[pltpu.VMEM((tm, d), BF16)],
        compiler_params=_params(("parallel", "arbitrary")),
        name="inproj",
    )(x, g, sc1, sh, w)


def _mm_kernel(x_ref, w_ref, o_ref):
    o_ref[...] = jnp.dot(x_ref[...], w_ref[...], preferred_element_type=F32)


def matmul(x, w):
    r, k = x.shape
    n = w.shape[1]
    tm = min(r, 1024)
    return pl.pallas_call(
        _mm_kernel,
        grid=(r // tm,),
        in_specs=[pl.BlockSpec((tm, k), lambda i: (i, 0)),
                  pl.BlockSpec((k, n), lambda i: (0, 0))],
        out_specs=pl.BlockSpec((tm, n), lambda i: (i, 0)),
        out_shape=jax.ShapeDtypeStruct((r, n), F32),
        compiler_params=_params(("parallel",)),
        name="matmul",
    )(x, w)


FLASH_V_PAD = 16
FLASH_SAFE_SHIFT = 60.0
FLASH_BQ = 1024
FLASH_CHAIN_Q = 512
FLASH_SCORE_BYTES = 17 * 1024 * 1024


def _flash_kernel(qT_ref, k_ref, kc_ref, vT_ref, vcT_ref, u_ref, o_ref, acc_sc, *, n_comp, n_split,
                  bk, n_kb, online):
    bq = qT_ref.shape[1] // n_split
    dv = vT_ref.shape[0]
    comp_rows = LANE // n_comp
    chains = [(c, h) for c in range(n_comp) for h in range(n_split)]
    u = u_ref[0:1, 0:1]
    row = lax.broadcasted_iota(jnp.int32, (LANE, bq), 0)
    shift_rows = jnp.where(row == 0, -u, 0.0).astype(BF16)
    q_ops = []
    for c, h in chains:
        qT = qT_ref[:, h * bq:(h + 1) * bq]
        if n_comp > 1:
            qT = jnp.where((row >= c * comp_rows) & (row < (c + 1) * comp_rows), qT,
                           jnp.zeros_like(qT))
        q_ops.append(jnp.concatenate([qT, shift_rows], axis=0))
    for i in range(len(chains)):
        acc_sc[i] = jnp.zeros(acc_sc.shape[1:], F32)

    def step(kb, vb, carry):
        n = kb.shape[0]
        ones_col = jnp.where(lax.broadcasted_iota(jnp.int32, (n, LANE), 1) == 0, 1.0, 0.0)
        ones_row = jnp.where(lax.broadcasted_iota(jnp.int32, (FLASH_V_PAD, n), 0) == 0, 1.0, 0.0)
        ka = jnp.concatenate([kb, ones_col.astype(BF16)], axis=1)
        va = jnp.concatenate([vb, ones_row.astype(BF16)], axis=0)
        scores = [jnp.dot(ka, q, preferred_element_type=F32) for q in q_ops]
        out = []
        for i, sT in enumerate(scores):
            if online:
                m = carry[i]
                m_new = jnp.maximum(m, jnp.max(sT, axis=0, keepdims=True))
                pT = jnp.exp2(sT - m_new).astype(BF16)
                acc_sc[i] = (jnp.exp2(m - m_new) * acc_sc[i]
                             + jnp.dot(va, pT, preferred_element_type=F32))
                out.append(m_new)
            else:
                pT = jnp.exp2(sT).astype(BF16)
                acc_sc[i] += jnp.dot(va, pT, preferred_element_type=F32)
                out.append(carry[i])
        return tuple(out)

    def body(j, carry):
        off = pl.multiple_of(j * bk, bk)
        return step(k_ref[pl.ds(off, bk), :], vT_ref[:, pl.ds(off, bk)], carry)

    init = tuple(jnp.full((1, bq), NEG_INF, F32) for _ in chains)
    lax.fori_loop(0, n_kb, body, step(kc_ref[...], vcT_ref[...], init))
    for i, (c, h) in enumerate(chains):
        acc = acc_sc[i]
        o_ref[0, c, :, h * bq:(h + 1) * bq] = acc[:dv] * (1.0 / acc[dv:dv + 1])


def _pick_bk(n, n_chains, bq):
    for bk in (4096, 2048, 1024, 512, 256, 128):
        if n % bk == 0 and n_chains * bk * bq * 4 <= FLASH_SCORE_BYTES:
            return bk
    raise ValueError(f"key count {n} must be a multiple of {LANE}")


def _flash_call(qT, k, kc, vT, vcT, u8, n_comp, online):
    hw, s = qT.shape
    n_heads = hw // LANE
    dv = vT.shape[0] // n_heads
    l = kc.shape[0]
    bq = min(FLASH_BQ, s)
    n_split = max(1, bq // FLASH_CHAIN_Q)
    bk = _pick_bk(s, n_comp * n_split, bq // n_split)
    kern = functools.partial(_flash_kernel, n_comp=n_comp, n_split=n_split, bk=bk, n_kb=s // bk,
                             online=online)
    return pl.pallas_call(
        kern,
        grid=(n_heads, s // bq),
        in_specs=[pl.BlockSpec((LANE, bq), lambda h, i: (h, i)),
                  pl.BlockSpec((s, LANE), lambda h, i: (0, h)),
                  pl.BlockSpec((l, LANE), lambda h, i: (0, h)),
                  pl.BlockSpec((dv, s), lambda h, i: (h, 0)),
                  pl.BlockSpec((dv, l), lambda h, i: (h, 0)),
                  pl.BlockSpec(u8.shape, lambda h, i: (0, 0))],
        out_specs=pl.BlockSpec((1, n_comp, dv, bq), lambda h, i: (h, 0, 0, i)),
        out_shape=jax.ShapeDtypeStruct((n_heads, n_comp, dv, s), F32),
        scratch_shapes=[pltpu.VMEM((n_comp * n_split, dv + FLASH_V_PAD, bq // n_split), F32)],
        compiler_params=_params(("parallel", "arbitrary")),
        name="flash_online" if online else "flash",
    )(qT, k, kc, vT, vcT, u8)


def flash(qT, k, kc, vT, vcT, u, n_comp):
    u8 = jnp.full((8, LANE), u, F32)
    return lax.cond(u <= FLASH_SAFE_SHIFT,
                    lambda *a: _flash_call(*a, n_comp, False),
                    lambda *a: _flash_call(*a, n_comp, True),
                    qT, k, kc, vT, vcT, u8)


def _score_bound(d, q_gain, k_gain, scale):
    return (d * scale * LOG2E * 1.02 * jnp.max(jnp.abs(q_gain.astype(F32)))
            * jnp.max(jnp.abs(k_gain.astype(F32))) + 0.01)


def _ctx_kernel(q_ref, k_ref, v_ref, sink_ref, o_ref, *, use_sink):
    s = lax.dot_general(q_ref[0], k_ref[0], (((1,), (1,)), ((), ())),
                        preferred_element_type=F32)
    m = jnp.max(s, axis=-1, keepdims=True)
    if use_sink:
        sk = sink_ref[0, 0:1, 0:1]
        m = jnp.maximum(m, sk)
    p = jnp.exp2(s - m)
    l = jnp.sum(p, axis=-1, keepdims=True)
    if use_sink:
        l = l + jnp.exp2(sk - m)
    o = jnp.dot(p.astype(BF16), v_ref[0], preferred_element_type=F32)
    o_ref[0] = o * (1.0 / l)


def ctx_attn(q, k, v, sink=None):
    h, l, dk = q.shape
    hk, hv, dv = k.shape[0], v.shape[0], v.shape[2]
    use_sink = sink is not None
    if sink is None:
        sink = jnp.zeros((h,), F32)
    sink3 = jnp.broadcast_to(sink.astype(F32)[:, None, None], (h, 8, LANE))
    kern = functools.partial(_ctx_kernel, use_sink=use_sink)
    return pl.pallas_call(
        kern,
        grid=(h,),
        in_specs=[pl.BlockSpec((1, l, dk), lambda i: (i, 0, 0)),
                  pl.BlockSpec((1, l, dk), lambda i: (i // (h // hk), 0, 0)),
                  pl.BlockSpec((1, l, dv), lambda i: (i // (h // hv), 0, 0)),
                  pl.BlockSpec((1, 8, LANE), lambda i: (i, 0, 0))],
        out_specs=pl.BlockSpec((1, l, dv), lambda i: (i, 0, 0)),
        out_shape=jax.ShapeDtypeStruct((h, l, dv), F32),
        compiler_params=_params(("parallel",)),
        name="ctx_attn",
    )(q, k, v, sink3)


LOCAL_BQ = 512


def _pad_rows(qT_h, slot, n_slots):
    z = jnp.zeros_like(qT_h)
    return jnp.concatenate([qT_h if s == slot else z for s in range(n_slots)], axis=0)


def _softmax_pv(s_list, v_list, extra, exact_max):
    if exact_max:
        m = functools.reduce(jnp.maximum, [jnp.max(s, axis=0, keepdims=True) for s in s_list])
        if extra is not None:
            m = jnp.maximum(m, extra)
            extra = extra - m
        s_list = [s - m for s in s_list]
    p_list = [jnp.exp2(s) for s in s_list]
    l = functools.reduce(jnp.add, [jnp.sum(p, axis=0, keepdims=True) for p in p_list])
    if extra is not None:
        l = l + jnp.exp2(extra)
    acc = functools.reduce(jnp.add, [jnp.dot(v, p.astype(BF16), preferred_element_type=F32)
                                     for v, p in zip(v_list, p_list)])
    return acc, l


def _wa_kernel(qT_ref, k_ref, vT_ref, kc_ref, vcT_ref, sh_ref, o_ref, *, seq, n_heads, n_kv,
               exact_max):
    i = pl.program_id(0)
    bq = qT_ref.shape[1]
    win = bq + 2 * WINDOW
    grp = n_heads // n_kv
    start = pl.multiple_of(jnp.clip(i * bq - WINDOW, 0, seq - win), WINDOW)
    kw = k_ref[pl.ds(start, win), :]
    vwT = vT_ref[:, pl.ds(start, win)]
    kc = kc_ref[...]
    vcT = vcT_ref[...]
    u = sh_ref[n_heads:n_heads + 1, 0:1]
    kpos = start + lax.broadcasted_iota(jnp.int32, (win, bq), 0)
    qpos = i * bq + lax.broadcasted_iota(jnp.int32, (win, bq), 1)
    mask_shift = jnp.where(jnp.abs(qpos - kpos) <= WINDOW, -u, NEG_INF)
    for h in range(n_heads):
        g = h // grp
        qTp = _pad_rows(qT_ref[h * HEAD_DIM:(h + 1) * HEAD_DIM, :], g, n_kv)
        s_loc = jnp.dot(kw, qTp, preferred_element_type=F32) + mask_shift
        s_ctx = jnp.dot(kc, qTp, preferred_element_type=F32) - u
        acc, l = _softmax_pv([s_loc, s_ctx], [vwT, vcT], sh_ref[h:h + 1, 0:1], exact_max)
        o_ref[h * HEAD_DIM:(h + 1) * HEAD_DIM, :] = (acc[g * HEAD_DIM:(g + 1) * HEAD_DIM]
                                                     * (1.0 / l))


def _wa_call(qT, k, vT, kc, vcT, sh, exact_max):
    hd, s = qT.shape
    kw = k.shape[1]
    l = kc.shape[0]
    bq = min(LOCAL_BQ, s)
    assert s % bq == 0 and s >= bq + 2 * WINDOW
    kern = functools.partial(_wa_kernel, seq=s, n_heads=hd // HEAD_DIM, n_kv=kw // HEAD_DIM,
                             exact_max=exact_max)
    full = lambda i: (0, 0)
    return pl.pallas_call(
        kern,
        grid=(s // bq,),
        in_specs=[pl.BlockSpec((hd, bq), lambda i: (0, i)),
                  pl.BlockSpec((s, kw), full), pl.BlockSpec((kw, s), full),
                  pl.BlockSpec((l, kw), full), pl.BlockSpec((kw, l), full),
                  pl.BlockSpec(sh.shape, full)],
        out_specs=pl.BlockSpec((hd, bq), lambda i: (0, i)),
        out_shape=jax.ShapeDtypeStruct((hd, s), F32),
        compiler_params=_params(("parallel",)),
        name="wa_attn_max" if exact_max else "wa_attn",
    )(qT, k, vT, kc, vcT, sh)


def wa_attn(qT, k, vT, kc, vcT, sink, u):
    n_heads = qT.shape[0] // HEAD_DIM
    rows = jnp.concatenate([sink.astype(F32) - u, jnp.reshape(u, (1,)).astype(F32),
                            jnp.zeros((2 * 8 - n_heads - 1,), F32)])
    sh = jnp.broadcast_to(rows[:, None], (rows.shape[0], LANE))
    return lax.cond(u <= FLASH_SAFE_SHIFT,
                    lambda *a: _wa_call(*a, False), lambda *a: _wa_call(*a, True),
                    qT, k, vT, kc, vcT, sh)


NA_KEY_ROWS = 3 * NA_KH


def _na_kernel(qT_ref, km_ref, k0_ref, kp_ref, vm_ref, v0_ref, vp_ref, kc_ref, vcT_ref,
               bias_ref, sh_ref, o_ref, *, exact_max):
    kwin = jnp.concatenate([km_ref[...], k0_ref[...], kp_ref[...]], axis=0)
    vwinT = jnp.concatenate([vm_ref[...], v0_ref[...], vp_ref[...]], axis=1)
    kc = kc_ref[...]
    vcT = vcT_ref[...]
    u = sh_ref[0:1, 0:1]
    for hh in range(2):
        rows = slice(hh * HEAD_DIM, (hh + 1) * HEAD_DIM)
        qTp = _pad_rows(qT_ref[rows, :], hh, 2)
        s_loc = jnp.dot(kwin, qTp, preferred_element_type=F32) + bias_ref[0, hh]
        s_ctx = jnp.dot(kc, qTp, preferred_element_type=F32) - u
        acc, l = _softmax_pv([s_loc, s_ctx], [vwinT, vcT], None, exact_max)
        o_ref[rows, :] = acc[rows] * (1.0 / l)


def na_bias_table(rpb, rows, shift):
    n_h = rpb.shape[0]
    a = jnp.arange(NA_KH)
    b = jnp.arange(NA_KEY_ROWS)
    c = jnp.arange(GRID_W)
    kc = jnp.arange(GRID_W)
    c0 = jnp.clip(c - NA_KW // 2, 0, GRID_W - NA_KW)
    col_ok = (kc[None, :] >= c0[:, None]) & (kc[None, :] < c0[:, None] + NA_KW)
    col_rel = jnp.clip(kc[None, :] - c[:, None] + (NA_KW - 1), 0, 2 * NA_KW - 2)
    row_rel = jnp.clip(b[None, :] - a[:, None] - 1, 0, 2 * NA_KH - 2)

    def edge_ok(r_base):
        r0 = jnp.clip(r_base + a - NA_KH // 2, 0, rows - NA_KH)
        key_row = r_base - NA_KH + b
        return (key_row[None, :] >= r0[:, None]) & (key_row[None, :] < r0[:, None] + NA_KH)

    inner_ok = ((b[None, :] >= a[:, None] + NA_KH // 2)
                & (b[None, :] < a[:, None] + NA_KH // 2 + NA_KH))
    row_ok = jnp.stack([edge_ok(0), inner_ok, edge_ok(rows - NA_KH)])
    e_c = jax.nn.one_hot(col_rel, 2 * NA_KW - 1, dtype=F32)
    e_r = jax.nn.one_hot(row_rel, 2 * NA_KH - 1, dtype=F32)
    hi = lax.Precision.HIGHEST
    t = jnp.einsum('hrx,ckx->hrck', rpb.astype(F32), e_c, precision=hi)
    t = jnp.einsum('abr,hrck->habck', e_r, t, precision=hi) * LOG2E - shift
    ok = row_ok[:, None, :, :, None, None] & col_ok[None, None, None, None]
    t = jnp.where(ok, t[None], NEG_INF)
    t = jnp.transpose(t, (0, 1, 3, 5, 2, 4))
    return t.reshape(3, n_h, NA_KEY_ROWS * GRID_W, NA_KH * GRID_W)


def _na_call(qT, k, vT, kc, vcT, bias, sh, exact_max):
    hd, s = qT.shape
    l = kc.shape[0]
    bq = NA_KH * GRID_W
    nb = s // bq
    pw = 2 * HEAD_DIM
    assert s % bq == 0 and bq == LOCAL_BQ
    prev = lambda i: jnp.maximum(i - 1, 0)
    nxt = lambda i: jnp.minimum(i + 1, nb - 1)
    var = lambda i: jnp.where(i == 0, 0, jnp.where(i == nb - 1, 2, 1))
    kspec = lambda f: pl.BlockSpec((bq, pw), lambda p, i: (f(i), p))
    vspec = lambda f: pl.BlockSpec((pw, bq), lambda p, i: (p, f(i)))
    cur = lambda i: i
    kern = functools.partial(_na_kernel, exact_max=exact_max)
    return pl.pallas_call(
        kern,
        grid=(hd // pw, nb),
        in_specs=[pl.BlockSpec((pw, bq), lambda p, i: (p, i)),
                  kspec(prev), kspec(cur), kspec(nxt), vspec(prev), vspec(cur), vspec(nxt),
                  pl.BlockSpec((l, pw), lambda p, i: (0, p)),
                  pl.BlockSpec((pw, l), lambda p, i: (p, 0)),
                  pl.BlockSpec((1, 2, NA_KEY_ROWS * GRID_W, bq), lambda p, i: (var(i), p, 0, 0)),
                  pl.BlockSpec(sh.shape, lambda p, i: (0, 0))],
        out_specs=pl.BlockSpec((pw, bq), lambda p, i: (p, i)),
        out_shape=jax.ShapeDtypeStruct((hd, s), F32),
        compiler_params=_params(("parallel", "arbitrary")),
        name="na_attn_max" if exact_max else "na_attn",
    )(qT, k, k, k, vT, vT, vT, kc, vcT, bias, sh)


def na_attn(qT, k, vT, kc, vcT, rpb, u_qk):
    s = qT.shape[1]
    u = u_qk + jnp.maximum(jnp.max(rpb.astype(F32)) * LOG2E, 0.0)
    bias = na_bias_table(rpb, s // GRID_W, u)
    sh = jnp.full((8, LANE), u, F32)
    return lax.cond(u <= FLASH_SAFE_SHIFT,
                    lambda *a: _na_call(*a, False), lambda *a: _na_call(*a, True),
                    qT, k, vT, kc, vcT, bias, sh)


def _outproj_kernel(o_ref, x_ref, w_ref, gate_ref, g_ref, sc_ref, sh_ref, wr_ref,
                    xo_ref, hf_ref, lg_ref):
    acc = jnp.dot(o_ref[...], w_ref[...], preferred_element_type=F32)
    xn = x_ref[...] + gate_ref[...] * acc
    xo_ref[...] = xn
    ms = jnp.mean(xn * xn, axis=-1, keepdims=True)
    y = xn * lax.rsqrt(ms + NORM_EPS) * g_ref[...]
    hf = (y * sc_ref[...] + sh_ref[...]).astype(BF16)
    hf_ref[...] = hf
    lg_ref[...] = jnp.dot(hf, wr_ref[...], preferred_element_type=F32)


def _outproj_t_kernel(na_ref, wa_ref, df_ref, ml_ref, dfg_ref, lam_ref, x_ref, w_ref, gate_ref,
                      g_ref, sc_ref, sh_ref, wr_ref, xo_ref, hf_ref, lg_ref):
    lam = lam_ref[0:1, 0:1]
    parts = [na_ref[...], wa_ref[...]]
    for h in range(df_ref.shape[0]):
        dd = df_ref[h, 0] - lam * df_ref[h, 1]
        ms = jnp.mean(dd * dd, axis=0, keepdims=True)
        parts.append(dd * lax.rsqrt(ms + NORM_EPS) * dfg_ref[...])
    parts.append(ml_ref[...])
    o = jnp.concatenate([jnp.transpose(t).astype(BF16) for t in parts], axis=1)
    acc = jnp.dot(o, w_ref[...], preferred_element_type=F32)
    xn = x_ref[...] + gate_ref[...] * acc
    xo_ref[...] = xn
    ms = jnp.mean(xn * xn, axis=-1, keepdims=True)
    y = xn * lax.rsqrt(ms + NORM_EPS) * g_ref[...]
    hf = (y * sc_ref[...] + sh_ref[...]).astype(BF16)
    hf_ref[...] = hf
    lg_ref[...] = jnp.dot(hf, wr_ref[...], preferred_element_type=F32)


def outproj_t(oT_na, oT_wa, oT_df, oT_ml, df_gain, lam, x, w, gate, g, sc1, sh, wr):
    r, d = x.shape
    tm = min(r, 256)
    nh, _, dv2, _ = oT_df.shape
    wdt = oT_na.shape[0]
    dfg = jnp.broadcast_to(df_gain.astype(F32)[:, None], (dv2, tm))
    lam8 = jnp.full((8, LANE), lam, F32)
    vec = pl.BlockSpec((1, d), lambda i: (0, 0))
    row = lambda n: pl.BlockSpec((tm, n), lambda i: (i, 0))
    colblk = pl.BlockSpec((wdt, tm), lambda i: (0, i))
    const = lambda shape: pl.BlockSpec(shape, lambda i: tuple(0 for _ in shape))
    return pl.pallas_call(
        _outproj_t_kernel,
        grid=(r // tm,),
        in_specs=[colblk, colblk, pl.BlockSpec((nh, 2, dv2, tm), lambda i: (0, 0, 0, i)), colblk,
                  const((dv2, tm)), const((8, LANE)), row(d), const(w.shape), vec, vec, vec, vec,
                  const((d, LANE))],
        out_specs=[row(d), row(d), row(LANE)],
        out_shape=[jax.ShapeDtypeStruct((r, d), F32), jax.ShapeDtypeStruct((r, d), BF16),
                   jax.ShapeDtypeStruct((r, LANE), F32)],
        compiler_params=_params(("parallel",)),
        name="outproj_t",
    )(oT_na, oT_wa, oT_df, oT_ml, dfg, lam8, x, w, gate, g, sc1, sh, wr)


def outproj(o, x, w, gate, g, sc1, sh, wr):
    r, d = x.shape
    k = o.shape[1]
    tm = min(r, 256)
    vec = pl.BlockSpec((1, d), lambda i: (0, 0))
    row = lambda n: pl.BlockSpec((tm, n), lambda i: (i, 0))
    return pl.pallas_call(
        _outproj_kernel,
        grid=(r // tm,),
        in_specs=[row(k), row(d), pl.BlockSpec((k, d), lambda i: (0, 0)), vec, vec, vec, vec,
                  pl.BlockSpec((d, LANE), lambda i: (0, 0))],
        out_specs=[row(d), row(d), row(LANE)],
        out_shape=[jax.ShapeDtypeStruct((r, d), F32), jax.ShapeDtypeStruct((r, d), BF16),
                   jax.ShapeDtypeStruct((r, LANE), F32)],
        compiler_params=_params(("parallel",)),
        name="outproj",
    )(o, x, w, gate, g, sc1, sh, wr)


def _moe_kernel(be_ref, nv_ref, x_ref, w1_ref, w3_ref, w2_ref, o_ref):
    i = pl.program_id(0)

    @pl.when(i < nv_ref[0])
    def _():
        x = x_ref[...]
        a = jnp.dot(x, w1_ref[0], preferred_element_type=F32)
        b = jnp.dot(x, w3_ref[0], preferred_element_type=F32)
        hmid = (a * _sigmoid(a) * b).astype(BF16)
        o_ref[...] = jnp.dot(hmid, w2_ref[0], preferred_element_type=F32).astype(o_ref.dtype)

    @pl.when(i >= nv_ref[0])
    def _():
        o_ref[...] = jnp.zeros(o_ref.shape, o_ref.dtype)


def moe_blocks(x, w1, w3, w2, blk_e, n_valid):
    n, d = x.shape
    f = w1.shape[2]
    n_blk = n // MOE_BLOCK
    grid_spec = pltpu.PrefetchScalarGridSpec(
        num_scalar_prefetch=2,
        grid=(n_blk,),
        in_specs=[pl.BlockSpec((MOE_BLOCK, d), lambda i, be, nv: (i, 0)),
                  pl.BlockSpec((1, d, f), lambda i, be, nv: (be[i], 0, 0)),
                  pl.BlockSpec((1, d, f), lambda i, be, nv: (be[i], 0, 0)),
                  pl.BlockSpec((1, f, d), lambda i, be, nv: (be[i], 0, 0))],
        out_specs=pl.BlockSpec((MOE_BLOCK, d), lambda i, be, nv: (i, 0)),
    )
    return pl.pallas_call(
        _moe_kernel,
        grid_spec=grid_spec,
        out_shape=jax.ShapeDtypeStruct((n, d), BF16),
        compiler_params=_params(("arbitrary",)),
        name="moe_blocks",
    )(blk_e, n_valid, x, w1, w3, w2)


def _rms(x, g):
    y = x * lax.rsqrt(jnp.mean(x * x, axis=-1, keepdims=True) + NORM_EPS)
    return y * g


def _rope_tables(seq_len, rot_dim):
    n = rot_dim // 4
    t = np.arange(seq_len)
    row = (t // GRID_W).astype(np.float32)[:, None]
    col = (t % GRID_W).astype(np.float32)[:, None]
    inv = (np.float32(ROPE_THETA) ** (-np.arange(n, dtype=np.float32) / np.float32(n)))
    inv = inv.astype(np.float32)
    return tuple(f(a * inv).astype(np.float32) for a in (row, col) for f in (np.cos, np.sin))


def _heads_first(t):
    return jnp.transpose(t, (1, 0, 2))


def _route(logits, b_router):
    n = logits.shape[0]
    per = N_EXPERTS // N_EXPERT_GROUPS
    scores = jax.nn.sigmoid(logits.astype(F32))
    sel = scores + b_router.astype(F32)
    grp = sel.reshape(n, N_EXPERT_GROUPS, per)
    gscore = None
    for a in range(per):
        for bb in range(a + 1, per):
            pair = grp[..., a] + grp[..., bb]
            gscore = pair if gscore is None else jnp.maximum(gscore, pair)
    gidx = jnp.argmax(gscore, axis=-1)
    eids = jnp.arange(N_EXPERTS)
    masked = jnp.where((eids // per)[None, :] == gidx[:, None], sel, -jnp.inf)
    e1 = jnp.argmax(masked, axis=-1)
    e2 = jnp.argmax(jnp.where(eids[None, :] == e1[:, None], -jnp.inf, masked), axis=-1)
    eidx = jnp.stack([e1, e2], axis=1).astype(jnp.int32)
    wts = jnp.take_along_axis(scores, eidx, axis=1)
    wts = wts / jnp.sum(wts, axis=-1, keepdims=True) * ROUTED_SCALE
    return eidx, wts


def _prefix_counts(onehot):
    n, e = onehot.shape
    ch = onehot.astype(F32).reshape(n // LANE, LANE, e)
    tri = jnp.tril(jnp.ones((LANE, LANE), F32))
    within = jnp.einsum('ij,cjk->cik', tri, ch)
    tot = within[:, -1, :]
    base = jnp.cumsum(tot, axis=0) - tot
    return (within + base[:, None, :]).reshape(n, e)


def _dispatch(eidx):
    n = eidx.shape[0]
    n_assign = n * TOP_K
    assert n_assign % LANE == 0
    e_flat = eidx.reshape(-1)
    tok = jnp.repeat(jnp.arange(n, dtype=jnp.int32), TOP_K)
    onehot = e_flat[:, None] == jnp.arange(N_EXPERTS)[None, :]
    csum = _prefix_counts(onehot)
    counts = csum[-1].astype(jnp.int32)
    rank = jnp.sum(jnp.where(onehot, csum, 0.0), axis=1).astype(jnp.int32) - 1
    padded = (counts + MOE_BLOCK - 1) // MOE_BLOCK * MOE_BLOCK
    pend = jnp.cumsum(padded)
    pstart = pend - padded
    dest = jnp.sum(jnp.where(onehot, pstart[None, :], 0), axis=1) + rank
    n_slots = (n_assign + N_EXPERTS * (MOE_BLOCK - 1) + MOE_BLOCK - 1) // MOE_BLOCK * MOE_BLOCK
    n_blk = n_slots // MOE_BLOCK
    slot_tok = jnp.full((n_slots,), n, dtype=jnp.int32).at[dest].set(tok)
    slot_of = dest.reshape(n, TOP_K)
    blk_start = jnp.arange(n_blk, dtype=jnp.int32) * MOE_BLOCK
    blk_e = jnp.minimum(jnp.sum(blk_start[:, None] >= pend[None, :], axis=1), N_EXPERTS - 1)
    n_valid = (pend[-1] // MOE_BLOCK).reshape(1)
    return slot_tok, slot_of, blk_e.astype(jnp.int32), n_valid.astype(jnp.int32)


PREP_TM = 4096


def _slab_prep_kernel(*refs, seg, count, rot, scale, transposed, has_add, raw):
    refs = list(refs)
    o_ref = refs.pop()
    x = refs[0][...].astype(F32)
    if not raw:
        g_ref = refs[1]
        nxt = 2
        if has_add:
            x = x + refs[nxt][...]
            nxt += 1
        sq = x * x
        hi = sq.astype(BF16)
        lo = (sq - hi.astype(F32)).astype(BF16)
        r = lax.broadcasted_iota(jnp.int32, (LANE, LANE), 0) // seg
        c = lax.broadcasted_iota(jnp.int32, (LANE, LANE), 1) // seg
        blk = jnp.where(r == c, 1.0, 0.0).astype(BF16)
        ssum = (jnp.dot(hi, blk, preferred_element_type=F32)
                + jnp.dot(lo, blk, preferred_element_type=F32))
        x = x * lax.rsqrt(ssum * (1.0 / count) + NORM_EPS) * g_ref[...]
        if rot:
            cos, s_next, s_prev = refs[nxt][...], refs[nxt + 1][...], refs[nxt + 2][...]
            x = (x * cos + pltpu.roll(x, LANE - rot, 1) * s_next + pltpu.roll(x, rot, 1) * s_prev)
        x = x * scale
    o_ref[...] = (jnp.transpose(x) if transposed else x).astype(o_ref.dtype)


def slab_prep(x, col0, n_slabs, gain=None, rope=None, rot=0, add=None, seg=HEAD_DIM,
              count=HEAD_DIM, scale=1.0, transposed=False):
    r = x.shape[0]
    tm = min(PREP_TM, r)
    raw = gain is None
    ops = [x]
    specs = [pl.BlockSpec((tm, LANE), lambda i, j: (i, col0 + j))]
    rowspec = pl.BlockSpec((tm, LANE), lambda i, j: (i, 0))
    if not raw:
        ops.append(gain.astype(F32).reshape(1, LANE))
        specs.append(pl.BlockSpec((1, LANE), lambda i, j: (0, 0)))
        if add is not None:
            ops.append(add)
            specs.append(rowspec)
        if rope is not None:
            ops += list(rope)
            specs += [rowspec] * 3
    kern = functools.partial(_slab_prep_kernel, seg=seg, count=count,
                             rot=rot if rope is not None else 0, scale=scale,
                             transposed=transposed, has_add=add is not None, raw=raw)
    if transposed:
        out_spec = pl.BlockSpec((LANE, tm), lambda i, j: (j, i))
        out_shape = jax.ShapeDtypeStruct((n_slabs * LANE, r), BF16)
    else:
        out_spec = pl.BlockSpec((tm, LANE), lambda i, j: (i, j))
        out_shape = jax.ShapeDtypeStruct((r, n_slabs * LANE), BF16)
    return pl.pallas_call(
        kern,
        grid=(r // tm, n_slabs),
        in_specs=specs,
        out_specs=out_spec,
        out_shape=out_shape,
        compiler_params=_params(("parallel", "arbitrary")),
        name="slab_prep",
    )(*ops)


def _rope_slab_tables(seq_len, width, offset, reps):
    n = width // 4
    cr, sr, cc, sc = (jnp.asarray(t) for t in _rope_tables(seq_len, width))
    one = jnp.ones((seq_len, offset), F32)
    zero = jnp.zeros((seq_len, offset), F32)
    zq = jnp.zeros((seq_len, n), F32)
    rest = LANE // reps - offset - width
    pad1 = jnp.ones((seq_len, rest), F32)
    pad0 = jnp.zeros((seq_len, rest), F32)
    cos = jnp.concatenate([one, cr, cr, cc, cc, pad1] * reps, axis=1)
    s_next = jnp.concatenate([zero, -sr, zq, -sc, zq, pad0] * reps, axis=1)
    s_prev = jnp.concatenate([zero, zq, sr, zq, sc, pad0] * reps, axis=1)
    return cos, s_next, s_prev


def _gain2(g):
    return jnp.concatenate([g, g]).astype(F32)


def _ctx_heads(t, n_heads):
    return _heads_first(t.reshape(t.shape[0], n_heads, -1)).astype(BF16)


def _qk_prep(p, pc, off, w, q_gain, k_gain, rope, wk=None):
    wk = w if wk is None else wk
    qs = HEAD_DIM ** -0.5 * LOG2E
    gq, gk = _gain2(q_gain), _gain2(k_gain)
    c0 = off // LANE
    rot = HEAD_DIM // 4
    qT = slab_prep(p, c0, w // LANE, gq, rope, rot, scale=qs, transposed=True)
    k = slab_prep(p, c0 + w // LANE, wk // LANE, gk, rope, rot)
    qc = slab_prep(pc, c0, w // LANE, gq, scale=qs)
    kc = slab_prep(pc, c0 + w // LANE, wk // LANE, gk)
    return qT, k, qc, kc


def _mla_prep(p, pc, qa_gain, kva_gain, w_uq, w_ukv, q_gain, k_gain, rope):
    hh, dn, dr, dv = MLA_HEADS, MLA_NOPE, MLA_ROPE, MLA_V
    dq = dn + dr
    qs = dq ** -0.5 * LOG2E
    padh = LANE - dq
    w_q = jnp.pad(w_uq.reshape(MLA_Q_LORA, hh, dq), ((0, 0), (0, 0), (0, padh)))
    w_kv = w_ukv.reshape(MLA_KV_LORA, hh, dn + dv)
    w_k = jnp.pad(w_kv[..., :dn], ((0, 0), (0, 0), (0, LANE - dn)))
    w_all = jnp.concatenate([w_k.reshape(MLA_KV_LORA, hh * LANE),
                             w_kv[..., dn:].reshape(MLA_KV_LORA, hh * dv)], axis=1).astype(BF16)
    w_q = w_q.reshape(MLA_Q_LORA, hh * LANE).astype(BF16)
    gq = jnp.pad(q_gain.astype(F32), (0, padh))
    gk = jnp.pad(k_gain.astype(F32), (0, padh))
    off = MLA_OFF

    def project(t, tabs):
        lora = t[:, off:off + MLA_Q_LORA + MLA_KV_LORA + dr].astype(F32)
        cq = _rms(lora[:, :MLA_Q_LORA], qa_gain).astype(BF16)
        ckv = _rms(lora[:, MLA_Q_LORA:MLA_Q_LORA + MLA_KV_LORA], kva_gain).astype(BF16)
        k_rope = jnp.pad(lora[:, MLA_Q_LORA + MLA_KV_LORA:], ((0, 0), (dn, padh)))
        kv = matmul(ckv, w_all)
        q = slab_prep(matmul(cq, w_q), 0, hh, gq, tabs, dr // 4, seg=LANE, count=dq, scale=qs,
                      transposed=tabs is not None)
        k = slab_prep(kv, 0, hh, gk, tabs, dr // 4, add=k_rope, seg=LANE, count=dq)
        return q, k, kv

    qT, k, kv = project(p, rope)
    qc, kc, kvc = project(pc, None)
    vT = slab_prep(kv, hh, hh * dv // LANE, transposed=True)
    vc = kvc[:, hh * LANE:].astype(BF16)
    return qT, k, kc, vT, vc.T, qc, vc


def _merge(o):
    return jnp.transpose(o, (1, 0, 2)).reshape(o.shape[1], -1)


def _diff_post(o1, o2, lam, sub_gain, lambda_init):
    return _rms(o1 - lam * o2, sub_gain) * (1.0 - lambda_init)


def _moe(tokens_bf16, logits, b_router, layer, w1, w3, w2, sw1, sw3, sw2):
    n, d = tokens_bf16.shape
    assert n % MOE_BLOCK == 0
    eidx, gate = _route(logits[:, :N_EXPERTS], b_router)
    slot_tok, slot_of, blk_e, n_valid = _dispatch(eidx)
    xb = tokens_bf16[jnp.minimum(slot_tok, n - 1)]
    n_sh = n // MOE_BLOCK
    shared = moe_blocks(tokens_bf16, sw1, sw3, sw2, jnp.full((n_sh,), layer, jnp.int32),
                        jnp.full((1,), n_sh, jnp.int32))
    yb = moe_blocks(xb, w1, w3, w2, blk_e + layer * N_EXPERTS, n_valid)
    y0, y1 = lax.optimization_barrier((yb[slot_of[:, 0]], yb[slot_of[:, 1]]))
    routed = y0.astype(F32) * gate[:, 0:1] + y1.astype(F32) * gate[:, 1:2]
    return routed + shared.astype(F32)


def kernel(x, c, ctx, c_ctx, w_ada, b_ada, g_attn, g_ffn, w_in, w_out, na_q_gain, na_k_gain, na_rpb, wa_q_gain, wa_k_gain, wa_sink, diff_q_gain, diff_k_gain, diff_lq1, diff_lk1, diff_lq2, diff_lk2, diff_sub_gain, mla_qa_gain, mla_kva_gain, mla_w_uq, mla_w_ukv, mla_q_gain, mla_k_gain, w_router, b_router, moe_w1, moe_w3, moe_w2, sh_w1, sh_w3, sh_w2):
    b, s, d = x.shape
    assert b == 1
    n_ctx = ctx.shape[1]
    depth = w_ada.shape[0]
    xl = x[0]
    xc = ctx[0]
    rope_head = _rope_slab_tables(s, HEAD_DIM, 0, 2)
    rope_mla = _rope_slab_tables(s, MLA_ROPE, MLA_NOPE, 1)

    cond8 = jnp.zeros((8, d), F32).at[0].set(c[0]).at[1].set(c_ctx)
    mod = modvec(cond8, w_ada, b_ada)
    tn = INPROJ_TN
    in_pad = _round_up(IN_COLS, tn)
    wr = jnp.pad(w_router, ((0, 0), (0, LANE - N_EXPERTS))).astype(BF16)
    inv_sqrt_d = HEAD_DIM ** -0.5
    moe_w = [t.astype(BF16).reshape((depth * N_EXPERTS,) + t.shape[2:])
             for t in (moe_w1, moe_w3, moe_w2)]
    sh_w = [t.astype(BF16) for t in (sh_w1, sh_w3, sh_w2)]

    for l in range(depth):
        with_ctx = l < depth - 1
        m_lat = mod[l, 0].reshape(6, 1, d)
        m_ctx = mod[l, 1].reshape(6, 1, d)
        w_in_l = jnp.pad(w_in[l], ((0, 0), (0, in_pad - IN_COLS))).astype(BF16)
        w_out_l = w_out[l].astype(BF16)
        g_a = g_attn[l][None]
        g_f = g_ffn[l][None]

        p = inproj(xl, g_a, 1.0 + m_lat[1], m_lat[0], w_in_l, tn)
        pc = inproj(xc, g_a, 1.0 + m_ctx[1], m_ctx[0], w_in_l, tn)

        w = NA_HEADS * HEAD_DIM
        qT, k, qc, kc = _qk_prep(p, pc, NA_OFF, w, na_q_gain[l], na_k_gain[l], None)
        vT = slab_prep(p, (NA_OFF + 2 * w) // LANE, w // LANE, transposed=True)
        vc = pc[:, NA_OFF + 2 * w:WA_OFF]
        u_na = _score_bound(HEAD_DIM, na_q_gain[l], na_k_gain[l], inv_sqrt_d)
        oT_na = na_attn(qT, k, vT, kc, vc.T, na_rpb[l], u_na)
        oc_na = (_merge(ctx_attn(_ctx_heads(qc, NA_HEADS), _ctx_heads(kc, NA_HEADS),
                                 _ctx_heads(vc, NA_HEADS))) if with_ctx else None)

        w, wk = WA_HEADS * HEAD_DIM, WA_KV_HEADS * HEAD_DIM
        qT, k, qc, kc = _qk_prep(p, pc, WA_OFF, w, wa_q_gain[l], wa_k_gain[l], rope_head, wk)
        vT = slab_prep(p, (WA_OFF + w + wk) // LANE, wk // LANE, transposed=True)
        vc = pc[:, WA_OFF + w + wk:DIFF_OFF]
        u_wa = _score_bound(HEAD_DIM, wa_q_gain[l], wa_k_gain[l], inv_sqrt_d)
        sink2 = wa_sink[l].astype(F32) * LOG2E
        oT_wa = wa_attn(qT, k, vT, kc, vc.T, sink2, u_wa)
        oc_wa = (_merge(ctx_attn(_ctx_heads(qc, WA_HEADS), _ctx_heads(kc, WA_KV_HEADS),
                                 _ctx_heads(vc, WA_KV_HEADS), sink2)) if with_ctx else None)

        lambda_init = 0.8 - 0.6 * math.exp(-0.3 * l)
        lam = (jnp.exp(jnp.sum(diff_lq1[l].astype(F32) * diff_lk1[l].astype(F32)))
               - jnp.exp(jnp.sum(diff_lq2[l].astype(F32) * diff_lk2[l].astype(F32))) + lambda_init)
        w = DIFF_HEADS * 2 * DIFF_DIM
        qT, k, qc, kc = _qk_prep(p, pc, DIFF_OFF, w, diff_q_gain[l], diff_k_gain[l], rope_head)
        vT = slab_prep(p, (DIFF_OFF + 2 * w) // LANE, w // LANE, transposed=True)
        vc = pc[:, DIFF_OFF + 2 * w:MLA_OFF]
        u_df = _score_bound(DIFF_DIM, diff_q_gain[l], diff_k_gain[l], DIFF_DIM ** -0.5)
        oT_df = flash(qT, k, kc, vT, vc.T, u_df, 2)
        oc_df = None
        if with_ctx:
            oc = ctx_attn(_ctx_heads(qc, 2 * DIFF_HEADS), _ctx_heads(kc, 2 * DIFF_HEADS),
                          _ctx_heads(vc, DIFF_HEADS)).reshape(DIFF_HEADS, 2, n_ctx, DIFF_V_DIM)
            oc_df = _diff_post(jnp.transpose(oc[:, 0], (1, 0, 2)), jnp.transpose(oc[:, 1], (1, 0, 2)),
                               lam, diff_sub_gain[l], lambda_init).reshape(n_ctx, -1)

        qT, k, kc, vT, vcT, qc, vc = _mla_prep(
            p, pc, mla_qa_gain[l], mla_kva_gain[l], mla_w_uq[l], mla_w_ukv[l], mla_q_gain[l],
            mla_k_gain[l], rope_mla)
        u_ml = _score_bound(MLA_NOPE + MLA_ROPE, mla_q_gain[l], mla_k_gain[l],
                            (MLA_NOPE + MLA_ROPE) ** -0.5)
        oT_ml = flash(qT, k, kc, vT, vcT, u_ml, 1).reshape(MLA_HEADS * MLA_V, s)
        oc_ml = (_merge(ctx_attn(_ctx_heads(qc, MLA_HEADS), _ctx_heads(kc, MLA_HEADS),
                                 _ctx_heads(vc, MLA_HEADS))) if with_ctx else None)

        xl, hf, lg = outproj_t(oT_na, oT_wa, oT_df, oT_ml,
                               diff_sub_gain[l].astype(F32) * (1.0 - lambda_init), lam,
                               xl, w_out_l, m_lat[2], g_f, 1.0 + m_lat[4], m_lat[3], wr)
        if with_ctx:
            oc_cat = jnp.concatenate([oc_na, oc_wa, oc_df, oc_ml], axis=-1).astype(BF16)
            xc, hfc, lgc = outproj(oc_cat, xc, w_out_l, m_ctx[2], g_f, 1.0 + m_ctx[4], m_ctx[3], wr)
            tokens = jnp.concatenate([hfc, hf], axis=0)
            logits = jnp.concatenate([lgc, lg], axis=0)
        else:
            tokens, logits = hf, lg
        y = _moe(tokens, logits, b_router, l, *moe_w, *sh_w)
        if with_ctx:
            xc = xc + m_ctx[5] * y[:n_ctx]
            y = y[n_ctx:]
        xl = xl + m_lat[5] * y
    return xl[None]
```

```python
import functools
import math

import jax
import jax.numpy as jnp
import numpy as np
from jax import lax
from jax.experimental import pallas as pl
from jax.experimental.pallas import tpu as pltpu

F32 = jnp.float32
BF16 = jnp.bfloat16

GRID_W = 64
HEAD_DIM = 64
ROPE_THETA = 10000.0
NORM_EPS = 1e-6
NEG_INF = -1e30
WINDOW = 128
NA_HEADS = 8
NA_KH = 8
NA_KW = 16
WA_HEADS = 8
WA_KV_HEADS = 2
DIFF_HEADS = 4
DIFF_DIM = 64
DIFF_V_DIM = 128
MLA_HEADS = 8
MLA_NOPE = 64
MLA_ROPE = 32
MLA_V = 64
MLA_Q_LORA = 384
MLA_KV_LORA = 128
N_EXPERTS = 16
N_EXPERT_GROUPS = 4
TOP_K = 2
ROUTED_SCALE = 1.0
MOE_BLOCK = 256

NA_COLS = 3 * NA_HEADS * HEAD_DIM
WA_COLS = (WA_HEADS + 2 * WA_KV_HEADS) * HEAD_DIM
DIFF_COLS = 3 * DIFF_HEADS * 2 * DIFF_DIM
MLA_COLS = MLA_Q_LORA + MLA_KV_LORA + MLA_ROPE
NA_OFF = 0
WA_OFF = NA_OFF + NA_COLS
DIFF_OFF = WA_OFF + WA_COLS
MLA_OFF = DIFF_OFF + DIFF_COLS
IN_COLS = MLA_OFF + MLA_COLS

LANE = 128
INPROJ_TN = 768
LOG2E = math.log2(math.e)
VMEM_LIMIT = 48 * 1024 * 1024


def _round_up(n, m):
    return (n + m - 1) // m * m


def _params(sem):
    return pltpu.CompilerParams(dimension_semantics=sem, vmem_limit_bytes=VMEM_LIMIT)


def _sigmoid(x):
    return 1.0 / (1.0 + jnp.exp(-x))


def _modvec_kernel(c_ref, w_ref, b_ref, o_ref):
    a = c_ref[...]
    a = a * _sigmoid(a)
    o_ref[0] = jnp.dot(a, w_ref[0], preferred_element_type=F32,
                       precision=lax.Precision.HIGHEST) + b_ref[0]


def modvec(cond8, w_ada, b_ada):
    depth, d, n = w_ada.shape
    tn = 1024
    return pl.pallas_call(
        _modvec_kernel,
        grid=(depth, n // tn),
        in_specs=[pl.BlockSpec((8, d), lambda l, j: (0, 0)),
                  pl.BlockSpec((1, d, tn), lambda l, j: (l, 0, j)),
                  pl.BlockSpec((1, 1, tn), lambda l, j: (l, 0, j))],
        out_specs=pl.BlockSpec((1, 8, tn), lambda l, j: (l, 0, j)),
        out_shape=jax.ShapeDtypeStruct((depth, 8, n), F32),
        compiler_params=_params(("parallel", "parallel")),
        name="modvec",
    )(cond8, w_ada, b_ada.reshape(depth, 1, n))


def _inproj_kernel(x_ref, g_ref, sc_ref, sh_ref, w_ref, o_ref, h_sc):
    @pl.when(pl.program_id(1) == 0)
    def _():
        x = x_ref[...]
        ms = jnp.mean(x * x, axis=-1, keepdims=True)
        y = x * lax.rsqrt(ms + NORM_EPS) * g_ref[...]
        h_sc[...] = (y * sc_ref[...] + sh_ref[...]).astype(BF16)

    o_ref[...] = jnp.dot(h_sc[...], w_ref[...], preferred_element_type=F32).astype(o_ref.dtype)


def inproj(x, g, sc1, sh, w, tn):
    r, d = x.shape
    n = w.shape[1]
    tm = min(r, 1024)
    vec = pl.BlockSpec((1, d), lambda i, j: (0, 0))
    return pl.pallas_call(
        _inproj_kernel,
        grid=(r // tm, n // tn),
        in_specs=[pl.BlockSpec((tm, d), lambda i, j: (i, 0)), vec, vec, vec,
                  pl.BlockSpec((d, tn), lambda i, j: (0, j))],
        out_specs=pl.BlockSpec((tm, tn), lambda i, j: (i, j)),
        out_shape=jax.ShapeDtypeStruct((r, n), BF16),
        scratch_shapes=[pltpu.VMEM((tm, d), BF16)],
        compiler_params=_params(("parallel", "arbitrary")),
        name="inproj",
    )(x, g, sc1, sh, w)


def _mm_kernel(x_ref, w_ref, o_ref):
    o_ref[...] = jnp.dot(x_ref[...], w_ref[...], preferred_element_type=F32)


def matmul(x, w):
    r, k = x.shape
    n = w.shape[1]
    tm = min(r, 1024)
    return pl.pallas_call(
        _mm_kernel,
        grid=(r // tm,),
        in_specs=[pl.BlockSpec((tm, k), lambda i: (i, 0)),
                  pl.BlockSpec((k, n), lambda i: (0, 0))],
        out_specs=pl.BlockSpec((tm, n), lambda i: (i, 0)),
        out_shape=jax.ShapeDtypeStruct((r, n), F32),
        compiler_params=_params(("parallel",)),
        name="matmul",
    )(x, w)


FLASH_V_PAD = 16
FLASH_SAFE_SHIFT = 60.0
FLASH_BQ = 1024
FLASH_CHAIN_Q = 512
FLASH_SCORE_BYTES = 17 * 1024 * 1024


def _flash_kernel(qT_ref, k_ref, kc_ref, vT_ref, vcT_ref, u_ref, o_ref, acc_sc, *, n_comp, n_split,
                  bk, n_kb, online):
    bq = qT_ref.shape[1] // n_split
    dv = vT_ref.shape[0]
    comp_rows = LANE // n_comp
    chains = [(c, h) for c in range(n_comp) for h in range(n_split)]
    u = u_ref[0:1, 0:1]
    row = lax.broadcasted_iota(jnp.int32, (LANE, bq), 0)
    shift_rows = jnp.where(row == 0, -u, 0.0).astype(BF16)
    q_ops = []
    for c, h in chains:
        qT = qT_ref[:, h * bq:(h + 1) * bq]
        if n_comp > 1:
            qT = jnp.where((row >= c * comp_rows) & (row < (c + 1) * comp_rows), qT,
                           jnp.zeros_like(qT))
        q_ops.append(jnp.concatenate([qT, shift_rows], axis=0))
    for i in range(len(chains)):
        acc_sc[i] = jnp.zeros(acc_sc.shape[1:], F32)

    def step(kb, vb, carry):
        n = kb.shape[0]
        ones_col = jnp.where(lax.broadcasted_iota(jnp.int32, (n, LANE), 1) == 0, 1.0, 0.0)
        ones_row = jnp.where(lax.broadcasted_iota(jnp.int32, (FLASH_V_PAD, n), 0) == 0, 1.0, 0.0)
        ka = jnp.concatenate([kb, ones_col.astype(BF16)], axis=1)
        va = jnp.concatenate([vb, ones_row.astype(BF16)], axis=0)
        scores = [jnp.dot(ka, q, preferred_element_type=F32) for q in q_ops]
        out = []
        for i, sT in enumerate(scores):
            if online:
                m = carry[i]
                m_new = jnp.maximum(m, jnp.max(sT, axis=0, keepdims=True))
                pT = jnp.exp2(sT - m_new).astype(BF16)
                acc_sc[i] = (jnp.exp2(m - m_new) * acc_sc[i]
                             + jnp.dot(va, pT, preferred_element_type=F32))
                out.append(m_new)
            else:
                pT = jnp.exp2(sT).astype(BF16)
                acc_sc[i] += jnp.dot(va, pT, preferred_element_type=F32)
                out.append(carry[i])
        return tuple(out)

    def body(j, carry):
        off = pl.multiple_of(j * bk, bk)
        return step(k_ref[pl.ds(off, bk), :], vT_ref[:, pl.ds(off, bk)], carry)

    init = tuple(jnp.full((1, bq), NEG_INF, F32) for _ in chains)
    lax.fori_loop(0, n_kb, body, step(kc_ref[...], vcT_ref[...], init))
    for i, (c, h) in enumerate(chains):
        acc = acc_sc[i]
        o_ref[c * dv:(c + 1) * dv, h * bq:(h + 1) * bq] = acc[:dv] * (1.0 / acc[dv:dv + 1])


def _pick_bk(n, n_chains, bq):
    for bk in (4096, 2048, 1024, 512, 256, 128):
        if n % bk == 0 and n_chains * bk * bq * 4 <= FLASH_SCORE_BYTES:
            return bk
    raise ValueError(f"key count {n} must be a multiple of {LANE}")


def _flash_call(qT, k, kc, vT, vcT, u8, n_comp, online):
    hw, s = qT.shape
    n_heads = hw // LANE
    dv = vT.shape[0] // n_heads
    l = kc.shape[0]
    bq = min(FLASH_BQ, s)
    n_split = max(1, bq // FLASH_CHAIN_Q)
    bk = _pick_bk(s, n_comp * n_split, bq // n_split)
    kern = functools.partial(_flash_kernel, n_comp=n_comp, n_split=n_split, bk=bk, n_kb=s // bk,
                             online=online)
    return pl.pallas_call(
        kern,
        grid=(n_heads, s // bq),
        in_specs=[pl.BlockSpec((LANE, bq), lambda h, i: (h, i)),
                  pl.BlockSpec((s, LANE), lambda h, i: (0, h)),
                  pl.BlockSpec((l, LANE), lambda h, i: (0, h)),
                  pl.BlockSpec((dv, s), lambda h, i: (h, 0)),
                  pl.BlockSpec((dv, l), lambda h, i: (h, 0)),
                  pl.BlockSpec(u8.shape, lambda h, i: (0, 0))],
        out_specs=pl.BlockSpec((n_comp * dv, bq), lambda h, i: (h, i)),
        out_shape=jax.ShapeDtypeStruct((n_heads * n_comp * dv, s), F32),
        scratch_shapes=[pltpu.VMEM((n_comp * n_split, dv + FLASH_V_PAD, bq // n_split), F32)],
        compiler_params=_params(("parallel", "arbitrary")),
        name="flash_online" if online else "flash",
    )(qT, k, kc, vT, vcT, u8)


def flash(qT, k, kc, vT, vcT, u, n_comp):
    u8 = jnp.full((8, LANE), u, F32)
    return lax.cond(u <= FLASH_SAFE_SHIFT,
                    lambda *a: _flash_call(*a, n_comp, False),
                    lambda *a: _flash_call(*a, n_comp, True),
                    qT, k, kc, vT, vcT, u8)


def _score_bound(d, q_gain, k_gain, scale):
    return (d * scale * LOG2E * 1.02 * jnp.max(jnp.abs(q_gain.astype(F32)))
            * jnp.max(jnp.abs(k_gain.astype(F32))) + 0.01)


def _ctx_kernel(q_ref, k_ref, v_ref, sink_ref, o_ref, *, use_sink):
    s = lax.dot_general(q_ref[0], k_ref[0], (((1,), (1,)), ((), ())),
                        preferred_element_type=F32)
    m = jnp.max(s, axis=-1, keepdims=True)
    if use_sink:
        sk = sink_ref[0, 0:1, 0:1]
        m = jnp.maximum(m, sk)
    p = jnp.exp2(s - m)
    l = jnp.sum(p, axis=-1, keepdims=True)
    if use_sink:
        l = l + jnp.exp2(sk - m)
    o = jnp.dot(p.astype(BF16), v_ref[0], preferred_element_type=F32)
    o_ref[0] = o * (1.0 / l)


def ctx_attn(q, k, v, sink=None):
    h, l, dk = q.shape
    hk, hv, dv = k.shape[0], v.shape[0], v.shape[2]
    use_sink = sink is not None
    if sink is None:
        sink = jnp.zeros((h,), F32)
    sink3 = jnp.broadcast_to(sink.astype(F32)[:, None, None], (h, 8, LANE))
    kern = functools.partial(_ctx_kernel, use_sink=use_sink)
    return pl.pallas_call(
        kern,
        grid=(h,),
        in_specs=[pl.BlockSpec((1, l, dk), lambda i: (i, 0, 0)),
                  pl.BlockSpec((1, l, dk), lambda i: (i // (h // hk), 0, 0)),
                  pl.BlockSpec((1, l, dv), lambda i: (i // (h // hv), 0, 0)),
                  pl.BlockSpec((1, 8, LANE), lambda i: (i, 0, 0))],
        out_specs=pl.BlockSpec((1, l, dv), lambda i: (i, 0, 0)),
        out_shape=jax.ShapeDtypeStruct((h, l, dv), F32),
        compiler_params=_params(("parallel",)),
        name="ctx_attn",
    )(q, k, v, sink3)


LOCAL_BQ = 512


def _pad_rows(qT_h, slot, n_slots):
    z = jnp.zeros_like(qT_h)
    return jnp.concatenate([qT_h if s == slot else z for s in range(n_slots)], axis=0)


def _softmax_pv(s_list, v_list, extra, exact_max):
    if exact_max:
        m = functools.reduce(jnp.maximum, [jnp.max(s, axis=0, keepdims=True) for s in s_list])
        if extra is not None:
            m = jnp.maximum(m, extra)
            extra = extra - m
        s_list = [s - m for s in s_list]
    p_list = [jnp.exp2(s) for s in s_list]
    l = functools.reduce(jnp.add, [jnp.sum(p, axis=0, keepdims=True) for p in p_list])
    if extra is not None:
        l = l + jnp.exp2(extra)
    acc = functools.reduce(jnp.add, [jnp.dot(v, p.astype(BF16), preferred_element_type=F32)
                                     for v, p in zip(v_list, p_list)])
    return acc, l


def _wa_kernel(qT_ref, k_ref, vT_ref, kc_ref, vcT_ref, sh_ref, o_ref, *, seq, n_heads, n_kv,
               exact_max):
    i = pl.program_id(0)
    bq = qT_ref.shape[1]
    win = bq + 2 * WINDOW
    grp = n_heads // n_kv
    start = pl.multiple_of(jnp.clip(i * bq - WINDOW, 0, seq - win), WINDOW)
    kw = k_ref[pl.ds(start, win), :]
    vwT = vT_ref[:, pl.ds(start, win)]
    kc = kc_ref[...]
    vcT = vcT_ref[...]
    u = sh_ref[n_heads:n_heads + 1, 0:1]
    kpos = start + lax.broadcasted_iota(jnp.int32, (win, bq), 0)
    qpos = i * bq + lax.broadcasted_iota(jnp.int32, (win, bq), 1)
    mask_shift = jnp.where(jnp.abs(qpos - kpos) <= WINDOW, -u, NEG_INF)
    for h in range(n_heads):
        g = h // grp
        qTp = _pad_rows(qT_ref[h * HEAD_DIM:(h + 1) * HEAD_DIM, :], g, n_kv)
        s_loc = jnp.dot(kw, qTp, preferred_element_type=F32) + mask_shift
        s_ctx = jnp.dot(kc, qTp, preferred_element_type=F32) - u
        acc, l = _softmax_pv([s_loc, s_ctx], [vwT, vcT], sh_ref[h:h + 1, 0:1], exact_max)
        o_ref[h * HEAD_DIM:(h + 1) * HEAD_DIM, :] = (acc[g * HEAD_DIM:(g + 1) * HEAD_DIM]
                                                     * (1.0 / l))


def _wa_call(qT, k, vT, kc, vcT, sh, exact_max):
    hd, s = qT.shape
    kw = k.shape[1]
    l = kc.shape[0]
    bq = min(LOCAL_BQ, s)
    assert s % bq == 0 and s >= bq + 2 * WINDOW
    kern = functools.partial(_wa_kernel, seq=s, n_heads=hd // HEAD_DIM, n_kv=kw // HEAD_DIM,
                             exact_max=exact_max)
    full = lambda i: (0, 0)
    return pl.pallas_call(
        kern,
        grid=(s // bq,),
        in_specs=[pl.BlockSpec((hd, bq), lambda i: (0, i)),
                  pl.BlockSpec((s, kw), full), pl.BlockSpec((kw, s), full),
                  pl.BlockSpec((l, kw), full), pl.BlockSpec((kw, l), full),
                  pl.BlockSpec(sh.shape, full)],
        out_specs=pl.BlockSpec((hd, bq), lambda i: (0, i)),
        out_shape=jax.ShapeDtypeStruct((hd, s), F32),
        compiler_params=_params(("parallel",)),
        name="wa_attn_max" if exact_max else "wa_attn",
    )(qT, k, vT, kc, vcT, sh)


def wa_attn(qT, k, vT, kc, vcT, sink, u):
    n_heads = qT.shape[0] // HEAD_DIM
    rows = jnp.concatenate([sink.astype(F32) - u, jnp.reshape(u, (1,)).astype(F32),
                            jnp.zeros((2 * 8 - n_heads - 1,), F32)])
    sh = jnp.broadcast_to(rows[:, None], (rows.shape[0], LANE))
    return lax.cond(u <= FLASH_SAFE_SHIFT,
                    lambda *a: _wa_call(*a, False), lambda *a: _wa_call(*a, True),
                    qT, k, vT, kc, vcT, sh)


NA_KEY_BLOCKS = 4
NA_KEY_ROWS = 2 * NA_KH


def _na_kernel(qT_ref, *refs, exact_max):
    k_refs, v_refs = refs[:NA_KEY_BLOCKS], refs[NA_KEY_BLOCKS:2 * NA_KEY_BLOCKS]
    kc_ref, vcT_ref, bias_ref, sh_ref, o_ref = refs[2 * NA_KEY_BLOCKS:]
    kwin = jnp.concatenate([r[...] for r in k_refs], axis=0)
    vwinT = jnp.concatenate([r[...] for r in v_refs], axis=1)
    kc = kc_ref[...]
    vcT = vcT_ref[...]
    u = sh_ref[0:1, 0:1]
    for hh in range(2):
        rows = slice(hh * HEAD_DIM, (hh + 1) * HEAD_DIM)
        qTp = _pad_rows(qT_ref[rows, :], hh, 2)
        s_loc = jnp.dot(kwin, qTp, preferred_element_type=F32) + bias_ref[0, hh]
        s_ctx = jnp.dot(kc, qTp, preferred_element_type=F32) - u
        acc, l = _softmax_pv([s_loc, s_ctx], [vwinT, vcT], None, exact_max)
        o_ref[rows, :] = acc[rows] * (1.0 / l)


def na_bias_table(rpb, rows, shift):
    n_h = rpb.shape[0]
    a = jnp.arange(NA_KH)
    b = jnp.arange(NA_KEY_ROWS)
    c = jnp.arange(GRID_W)
    kc = jnp.arange(GRID_W)
    c0 = jnp.clip(c - NA_KW // 2, 0, GRID_W - NA_KW)
    col_ok = (kc[None, :] >= c0[:, None]) & (kc[None, :] < c0[:, None] + NA_KW)
    col_rel = jnp.clip(kc[None, :] - c[:, None] + (NA_KW - 1), 0, 2 * NA_KW - 2)
    row_rel = jnp.clip(b[None, :] - a[:, None] + NA_KH // 2 - 1, 0, 2 * NA_KH - 2)

    def edge_ok(r_base):
        r0 = jnp.clip(r_base + a - NA_KH // 2, 0, rows - NA_KH)
        key_row = r_base - NA_KH // 2 + b
        return (key_row[None, :] >= r0[:, None]) & (key_row[None, :] < r0[:, None] + NA_KH)

    inner_ok = (b[None, :] >= a[:, None]) & (b[None, :] < a[:, None] + NA_KH)
    row_ok = jnp.stack([edge_ok(0), inner_ok, edge_ok(rows - NA_KH)])
    e_c = jax.nn.one_hot(col_rel, 2 * NA_KW - 1, dtype=F32)
    e_r = jax.nn.one_hot(row_rel, 2 * NA_KH - 1, dtype=F32)
    hi = lax.Precision.HIGHEST
    t = jnp.einsum('hrx,ckx->hrck', rpb.astype(F32), e_c, precision=hi)
    t = jnp.einsum('abr,hrck->habck', e_r, t, precision=hi) * LOG2E - shift
    ok = row_ok[:, None, :, :, None, None] & col_ok[None, None, None, None]
    t = jnp.where(ok, t[None], NEG_INF)
    t = jnp.transpose(t, (0, 1, 3, 5, 2, 4))
    return t.reshape(3, n_h, NA_KEY_ROWS * GRID_W, NA_KH * GRID_W)


def _na_call(qT, k, vT, kc, vcT, bias, sh, exact_max):
    hd, s = qT.shape
    l = kc.shape[0]
    bq = NA_KH * GRID_W
    nb = s // bq
    pw = 2 * HEAD_DIM
    assert s % bq == 0 and bq == LOCAL_BQ
    var = lambda i: jnp.where(i == 0, 0, jnp.where(i == nb - 1, 2, 1))
    kb = bq // 2
    blk = lambda i, t: jnp.clip(2 * i - 1 + t, 0, 2 * nb - 1)
    kspecs = [pl.BlockSpec((kb, pw), lambda p, i, t=t: (blk(i, t), p))
              for t in range(NA_KEY_BLOCKS)]
    vspecs = [pl.BlockSpec((pw, kb), lambda p, i, t=t: (p, blk(i, t)))
              for t in range(NA_KEY_BLOCKS)]
    kern = functools.partial(_na_kernel, exact_max=exact_max)
    return pl.pallas_call(
        kern,
        grid=(hd // pw, nb),
        in_specs=[pl.BlockSpec((pw, bq), lambda p, i: (p, i)), *kspecs, *vspecs,
                  pl.BlockSpec((l, pw), lambda p, i: (0, p)),
                  pl.BlockSpec((pw, l), lambda p, i: (p, 0)),
                  pl.BlockSpec((1, 2, NA_KEY_ROWS * GRID_W, bq), lambda p, i: (var(i), p, 0, 0)),
                  pl.BlockSpec(sh.shape, lambda p, i: (0, 0))],
        out_specs=pl.BlockSpec((pw, bq), lambda p, i: (p, i)),
        out_shape=jax.ShapeDtypeStruct((hd, s), F32),
        compiler_params=_params(("parallel", "arbitrary")),
        name="na_attn_max" if exact_max else "na_attn",
    )(qT, *([k] * NA_KEY_BLOCKS), *([vT] * NA_KEY_BLOCKS), kc, vcT, bias, sh)


def na_attn(qT, k, vT, kc, vcT, rpb, u_qk):
    s = qT.shape[1]
    u = u_qk + jnp.maximum(jnp.max(rpb.astype(F32)) * LOG2E, 0.0)
    bias = na_bias_table(rpb, s // GRID_W, u)
    sh = jnp.full((8, LANE), u, F32)
    return lax.cond(u <= FLASH_SAFE_SHIFT,
                    lambda *a: _na_call(*a, False), lambda *a: _na_call(*a, True),
                    qT, k, vT, kc, vcT, bias, sh)


def _outproj_kernel(o_ref, x_ref, w_ref, gate_ref, g_ref, sc_ref, sh_ref, wr_ref,
                    xo_ref, hf_ref, lg_ref):
    acc = jnp.dot(o_ref[...], w_ref[...], preferred_element_type=F32)
    xn = x_ref[...] + gate_ref[...] * acc
    xo_ref[...] = xn
    ms = jnp.mean(xn * xn, axis=-1, keepdims=True)
    y = xn * lax.rsqrt(ms + NORM_EPS) * g_ref[...]
    hf = (y * sc_ref[...] + sh_ref[...]).astype(BF16)
    hf_ref[...] = hf
    lg_ref[...] = jnp.dot(hf, wr_ref[...], preferred_element_type=F32)


def _outproj_t_kernel(na_ref, wa_ref, df_ref, ml_ref, dfg_ref, lam_ref, x_ref, w_ref, gate_ref,
                      g_ref, sc_ref, sh_ref, wr_ref, xo_ref, hf_ref, lg_ref):
    lam = lam_ref[0:1, 0:1]
    parts = [na_ref[...], wa_ref[...]]
    for h in range(df_ref.shape[0]):
        dd = df_ref[h, 0] - lam * df_ref[h, 1]
        ms = jnp.mean(dd * dd, axis=0, keepdims=True)
        parts.append(dd * lax.rsqrt(ms + NORM_EPS) * dfg_ref[...])
    parts.append(ml_ref[...])
    o = jnp.concatenate([jnp.transpose(t).astype(BF16) for t in parts], axis=1)
    acc = jnp.dot(o, w_ref[...], preferred_element_type=F32)
    xn = x_ref[...] + gate_ref[...] * acc
    xo_ref[...] = xn
    ms = jnp.mean(xn * xn, axis=-1, keepdims=True)
    y = xn * lax.rsqrt(ms + NORM_EPS) * g_ref[...]
    hf = (y * sc_ref[...] + sh_ref[...]).astype(BF16)
    hf_ref[...] = hf
    lg_ref[...] = jnp.dot(hf, wr_ref[...], preferred_element_type=F32)


def outproj_t(oT_na, oT_wa, oT_df, oT_ml, df_gain, lam, x, w, gate, g, sc1, sh, wr):
    r, d = x.shape
    tm = min(r, 256)
    nh, _, dv2, _ = oT_df.shape
    wdt = oT_na.shape[0]
    dfg = jnp.broadcast_to(df_gain.astype(F32)[:, None], (dv2, tm))
    lam8 = jnp.full((8, LANE), lam, F32)
    vec = pl.BlockSpec((1, d), lambda i: (0, 0))
    row = lambda n: pl.BlockSpec((tm, n), lambda i: (i, 0))
    colblk = pl.BlockSpec((wdt, tm), lambda i: (0, i))
    const = lambda shape: pl.BlockSpec(shape, lambda i: tuple(0 for _ in shape))
    return pl.pallas_call(
        _outproj_t_kernel,
        grid=(r // tm,),
        in_specs=[colblk, colblk, pl.BlockSpec((nh, 2, dv2, tm), lambda i: (0, 0, 0, i)), colblk,
                  const((dv2, tm)), const((8, LANE)), row(d), const(w.shape), vec, vec, vec, vec,
                  const((d, LANE))],
        out_specs=[row(d), row(d), row(LANE)],
        out_shape=[jax.ShapeDtypeStruct((r, d), F32), jax.ShapeDtypeStruct((r, d), BF16),
                   jax.ShapeDtypeStruct((r, LANE), F32)],
        compiler_params=_params(("parallel",)),
        name="outproj_t",
    )(oT_na, oT_wa, oT_df, oT_ml, dfg, lam8, x, w, gate, g, sc1, sh, wr)


def outproj(o, x, w, gate, g, sc1, sh, wr):
    r, d = x.shape
    k = o.shape[1]
    tm = min(r, 256)
    vec = pl.BlockSpec((1, d), lambda i: (0, 0))
    row = lambda n: pl.BlockSpec((tm, n), lambda i: (i, 0))
    return pl.pallas_call(
        _outproj_kernel,
        grid=(r // tm,),
        in_specs=[row(k), row(d), pl.BlockSpec((k, d), lambda i: (0, 0)), vec, vec, vec, vec,
                  pl.BlockSpec((d, LANE), lambda i: (0, 0))],
        out_specs=[row(d), row(d), row(LANE)],
        out_shape=[jax.ShapeDtypeStruct((r, d), F32), jax.ShapeDtypeStruct((r, d), BF16),
                   jax.ShapeDtypeStruct((r, LANE), F32)],
        compiler_params=_params(("parallel",)),
        name="outproj",
    )(o, x, w, gate, g, sc1, sh, wr)


def _moe_kernel(be_ref, nv_ref, x_ref, w1_ref, w3_ref, w2_ref, o_ref):
    i = pl.program_id(0)

    @pl.when(i < nv_ref[0])
    def _():
        x = x_ref[...]
        a = jnp.dot(x, w1_ref[0], preferred_element_type=F32)
        b = jnp.dot(x, w3_ref[0], preferred_element_type=F32)
        hmid = (a * _sigmoid(a) * b).astype(BF16)
        o_ref[...] = jnp.dot(hmid, w2_ref[0], preferred_element_type=F32).astype(o_ref.dtype)

    @pl.when(i >= nv_ref[0])
    def _():
        o_ref[...] = jnp.zeros(o_ref.shape, o_ref.dtype)


def moe_blocks(x, w1, w3, w2, blk_e, n_valid):
    n, d = x.shape
    f = w1.shape[2]
    n_blk = n // MOE_BLOCK
    grid_spec = pltpu.PrefetchScalarGridSpec(
        num_scalar_prefetch=2,
        grid=(n_blk,),
        in_specs=[pl.BlockSpec((MOE_BLOCK, d), lambda i, be, nv: (i, 0)),
                  pl.BlockSpec((1, d, f), lambda i, be, nv: (be[i], 0, 0)),
                  pl.BlockSpec((1, d, f), lambda i, be, nv: (be[i], 0, 0)),
                  pl.BlockSpec((1, f, d), lambda i, be, nv: (be[i], 0, 0))],
        out_specs=pl.BlockSpec((MOE_BLOCK, d), lambda i, be, nv: (i, 0)),
    )
    return pl.pallas_call(
        _moe_kernel,
        grid_spec=grid_spec,
        out_shape=jax.ShapeDtypeStruct((n, d), BF16),
        compiler_params=_params(("arbitrary",)),
        name="moe_blocks",
    )(blk_e, n_valid, x, w1, w3, w2)


def _rms(x, g):
    y = x * lax.rsqrt(jnp.mean(x * x, axis=-1, keepdims=True) + NORM_EPS)
    return y * g


def _rope_tables(seq_len, rot_dim):
    n = rot_dim // 4
    t = np.arange(seq_len)
    row = (t // GRID_W).astype(np.float32)[:, None]
    col = (t % GRID_W).astype(np.float32)[:, None]
    inv = (np.float32(ROPE_THETA) ** (-np.arange(n, dtype=np.float32) / np.float32(n)))
    inv = inv.astype(np.float32)
    return tuple(f(a * inv).astype(np.float32) for a in (row, col) for f in (np.cos, np.sin))


def _heads_first(t):
    return jnp.transpose(t, (1, 0, 2))


def _route(logits, b_router):
    n = logits.shape[0]
    per = N_EXPERTS // N_EXPERT_GROUPS
    scores = jax.nn.sigmoid(logits.astype(F32))
    sel = scores + b_router.astype(F32)
    grp = sel.reshape(n, N_EXPERT_GROUPS, per)
    gscore = None
    for a in range(per):
        for bb in range(a + 1, per):
            pair = grp[..., a] + grp[..., bb]
            gscore = pair if gscore is None else jnp.maximum(gscore, pair)
    gidx = jnp.argmax(gscore, axis=-1)
    eids = jnp.arange(N_EXPERTS)
    masked = jnp.where((eids // per)[None, :] == gidx[:, None], sel, -jnp.inf)
    e1 = jnp.argmax(masked, axis=-1)
    e2 = jnp.argmax(jnp.where(eids[None, :] == e1[:, None], -jnp.inf, masked), axis=-1)
    eidx = jnp.stack([e1, e2], axis=1).astype(jnp.int32)
    wts = jnp.take_along_axis(scores, eidx, axis=1)
    wts = wts / jnp.sum(wts, axis=-1, keepdims=True) * ROUTED_SCALE
    return eidx, wts


def _prefix_counts(onehot):
    n, e = onehot.shape
    ch = onehot.astype(F32).reshape(n // LANE, LANE, e)
    tri = jnp.tril(jnp.ones((LANE, LANE), F32))
    within = jnp.einsum('ij,cjk->cik', tri, ch)
    tot = within[:, -1, :]
    base = jnp.cumsum(tot, axis=0) - tot
    return (within + base[:, None, :]).reshape(n, e)


def _dispatch(eidx):
    n = eidx.shape[0]
    n_assign = n * TOP_K
    assert n_assign % LANE == 0
    e_flat = eidx.reshape(-1)
    tok = jnp.repeat(jnp.arange(n, dtype=jnp.int32), TOP_K)
    onehot = e_flat[:, None] == jnp.arange(N_EXPERTS)[None, :]
    csum = _prefix_counts(onehot)
    counts = csum[-1].astype(jnp.int32)
    rank = jnp.sum(jnp.where(onehot, csum, 0.0), axis=1).astype(jnp.int32) - 1
    padded = (counts + MOE_BLOCK - 1) // MOE_BLOCK * MOE_BLOCK
    pend = jnp.cumsum(padded)
    pstart = pend - padded
    dest = jnp.sum(jnp.where(onehot, pstart[None, :], 0), axis=1) + rank
    n_slots = (n_assign + N_EXPERTS * (MOE_BLOCK - 1) + MOE_BLOCK - 1) // MOE_BLOCK * MOE_BLOCK
    n_blk = n_slots // MOE_BLOCK
    slot_tok = jnp.full((n_slots,), n, dtype=jnp.int32).at[dest].set(tok)
    slot_of = dest.reshape(n, TOP_K)
    blk_start = jnp.arange(n_blk, dtype=jnp.int32) * MOE_BLOCK
    blk_e = jnp.minimum(jnp.sum(blk_start[:, None] >= pend[None, :], axis=1), N_EXPERTS - 1)
    n_valid = (pend[-1] // MOE_BLOCK).reshape(1)
    return slot_tok, slot_of, blk_e.astype(jnp.int32), n_valid.astype(jnp.int32)


PREP_TM = 4096


def _slab_prep_kernel(*refs, seg, count, rot, scale, transposed, has_add, raw):
    refs = list(refs)
    o_ref = refs.pop()
    x = refs[0][...].astype(F32)
    if not raw:
        g_ref = refs[1]
        nxt = 2
        if has_add:
            x = x + refs[nxt][...]
            nxt += 1
        sq = x * x
        hi = sq.astype(BF16)
        lo = (sq - hi.astype(F32)).astype(BF16)
        r = lax.broadcasted_iota(jnp.int32, (LANE, LANE), 0) // seg
        c = lax.broadcasted_iota(jnp.int32, (LANE, LANE), 1) // seg
        blk = jnp.where(r == c, 1.0, 0.0).astype(BF16)
        ssum = (jnp.dot(hi, blk, preferred_element_type=F32)
                + jnp.dot(lo, blk, preferred_element_type=F32))
        x = x * lax.rsqrt(ssum * (1.0 / count) + NORM_EPS) * g_ref[...]
        if rot:
            cos, s_next, s_prev = refs[nxt][...], refs[nxt + 1][...], refs[nxt + 2][...]
            x = (x * cos + pltpu.roll(x, LANE - rot, 1) * s_next + pltpu.roll(x, rot, 1) * s_prev)
        x = x * scale
    o_ref[...] = (jnp.transpose(x) if transposed else x).astype(o_ref.dtype)


def slab_prep(x, col0, n_slabs, gain=None, rope=None, rot=0, add=None, seg=HEAD_DIM,
              count=HEAD_DIM, scale=1.0, transposed=False):
    r = x.shape[0]
    tm = min(PREP_TM, r)
    raw = gain is None
    ops = [x]
    specs = [pl.BlockSpec((tm, LANE), lambda i, j: (i, col0 + j))]
    rowspec = pl.BlockSpec((tm, LANE), lambda i, j: (i, 0))
    if not raw:
        ops.append(gain.astype(F32).reshape(1, LANE))
        specs.append(pl.BlockSpec((1, LANE), lambda i, j: (0, 0)))
        if add is not None:
            ops.append(add)
            specs.append(rowspec)
        if rope is not None:
            ops += list(rope)
            specs += [rowspec] * 3
    kern = functools.partial(_slab_prep_kernel, seg=seg, count=count,
                             rot=rot if rope is not None else 0, scale=scale,
                             transposed=transposed, has_add=add is not None, raw=raw)
    if transposed:
        out_spec = pl.BlockSpec((LANE, tm), lambda i, j: (j, i))
        out_shape = jax.ShapeDtypeStruct((n_slabs * LANE, r), BF16)
    else:
        out_spec = pl.BlockSpec((tm, LANE), lambda i, j: (i, j))
        out_shape = jax.ShapeDtypeStruct((r, n_slabs * LANE), BF16)
    return pl.pallas_call(
        kern,
        grid=(r // tm, n_slabs),
        in_specs=specs,
        out_specs=out_spec,
        out_shape=out_shape,
        compiler_params=_params(("parallel", "arbitrary")),
        name="slab_prep",
    )(*ops)


def _rope_slab_tables(seq_len, width, offset, reps):
    n = width // 4
    cr, sr, cc, sc = (jnp.asarray(t) for t in _rope_tables(seq_len, width))
    one = jnp.ones((seq_len, offset), F32)
    zero = jnp.zeros((seq_len, offset), F32)
    zq = jnp.zeros((seq_len, n), F32)
    rest = LANE // reps - offset - width
    pad1 = jnp.ones((seq_len, rest), F32)
    pad0 = jnp.zeros((seq_len, rest), F32)
    cos = jnp.concatenate([one, cr, cr, cc, cc, pad1] * reps, axis=1)
    s_next = jnp.concatenate([zero, -sr, zq, -sc, zq, pad0] * reps, axis=1)
    s_prev = jnp.concatenate([zero, zq, sr, zq, sc, pad0] * reps, axis=1)
    return cos, s_next, s_prev


def _gain2(g):
    return jnp.concatenate([g, g]).astype(F32)


def _ctx_heads(t, n_heads):
    return _heads_first(t.reshape(t.shape[0], n_heads, -1)).astype(BF16)


def _qk_prep(p, pc, off, w, q_gain, k_gain, rope, wk=None):
    wk = w if wk is None else wk
    qs = HEAD_DIM ** -0.5 * LOG2E
    gq, gk = _gain2(q_gain), _gain2(k_gain)
    c0 = off // LANE
    rot = HEAD_DIM // 4
    qT = slab_prep(p, c0, w // LANE, gq, rope, rot, scale=qs, transposed=True)
    k = slab_prep(p, c0 + w // LANE, wk // LANE, gk, rope, rot)
    qc = slab_prep(pc, c0, w // LANE, gq, scale=qs)
    kc = slab_prep(pc, c0 + w // LANE, wk // LANE, gk)
    return qT, k, qc, kc


def _mla_prep(p, pc, qa_gain, kva_gain, w_uq, w_ukv, q_gain, k_gain, rope):
    hh, dn, dr, dv = MLA_HEADS, MLA_NOPE, MLA_ROPE, MLA_V
    dq = dn + dr
    qs = dq ** -0.5 * LOG2E
    padh = LANE - dq
    w_q = jnp.pad(w_uq.reshape(MLA_Q_LORA, hh, dq), ((0, 0), (0, 0), (0, padh)))
    w_kv = w_ukv.reshape(MLA_KV_LORA, hh, dn + dv)
    w_k = jnp.pad(w_kv[..., :dn], ((0, 0), (0, 0), (0, LANE - dn)))
    w_all = jnp.concatenate([w_k.reshape(MLA_KV_LORA, hh * LANE),
                             w_kv[..., dn:].reshape(MLA_KV_LORA, hh * dv)], axis=1).astype(BF16)
    w_q = w_q.reshape(MLA_Q_LORA, hh * LANE).astype(BF16)
    gq = jnp.pad(q_gain.astype(F32), (0, padh))
    gk = jnp.pad(k_gain.astype(F32), (0, padh))
    off = MLA_OFF

    def project(t, tabs):
        lora = t[:, off:off + MLA_Q_LORA + MLA_KV_LORA + dr].astype(F32)
        cq = _rms(lora[:, :MLA_Q_LORA], qa_gain).astype(BF16)
        ckv = _rms(lora[:, MLA_Q_LORA:MLA_Q_LORA + MLA_KV_LORA], kva_gain).astype(BF16)
        k_rope = jnp.pad(lora[:, MLA_Q_LORA + MLA_KV_LORA:], ((0, 0), (dn, padh)))
        kv = matmul(ckv, w_all)
        q = slab_prep(matmul(cq, w_q), 0, hh, gq, tabs, dr // 4, seg=LANE, count=dq, scale=qs,
                      transposed=tabs is not None)
        k = slab_prep(kv, 0, hh, gk, tabs, dr // 4, add=k_rope, seg=LANE, count=dq)
        return q, k, kv

    qT, k, kv = project(p, rope)
    qc, kc, kvc = project(pc, None)
    vT = slab_prep(kv, hh, hh * dv // LANE, transposed=True)
    vc = kvc[:, hh * LANE:].astype(BF16)
    return qT, k, kc, vT, vc.T, qc, vc


def _merge(o):
    return jnp.transpose(o, (1, 0, 2)).reshape(o.shape[1], -1)


def _diff_post(o1, o2, lam, sub_gain, lambda_init):
    return _rms(o1 - lam * o2, sub_gain) * (1.0 - lambda_init)


def _moe(tokens_bf16, logits, b_router, layer, w1, w3, w2, sw1, sw3, sw2):
    n, d = tokens_bf16.shape
    assert n % MOE_BLOCK == 0
    eidx, gate = _route(logits[:, :N_EXPERTS], b_router)
    slot_tok, slot_of, blk_e, n_valid = _dispatch(eidx)
    xb = tokens_bf16[jnp.minimum(slot_tok, n - 1)]
    n_sh = n // MOE_BLOCK
    shared = moe_blocks(tokens_bf16, sw1, sw3, sw2, jnp.full((n_sh,), layer, jnp.int32),
                        jnp.full((1,), n_sh, jnp.int32))
    xb, shared = lax.optimization_barrier((xb, shared))
    yb = moe_blocks(xb, w1, w3, w2, blk_e + layer * N_EXPERTS, n_valid)
    y0, y1 = lax.optimization_barrier((yb[slot_of[:, 0]], yb[slot_of[:, 1]]))
    routed = y0.astype(F32) * gate[:, 0:1] + y1.astype(F32) * gate[:, 1:2]
    return routed + shared.astype(F32)


def kernel(x, c, ctx, c_ctx, w_ada, b_ada, g_attn, g_ffn, w_in, w_out, na_q_gain, na_k_gain, na_rpb, wa_q_gain, wa_k_gain, wa_sink, diff_q_gain, diff_k_gain, diff_lq1, diff_lk1, diff_lq2, diff_lk2, diff_sub_gain, mla_qa_gain, mla_kva_gain, mla_w_uq, mla_w_ukv, mla_q_gain, mla_k_gain, w_router, b_router, moe_w1, moe_w3, moe_w2, sh_w1, sh_w3, sh_w2):
    b, s, d = x.shape
    assert b == 1
    n_ctx = ctx.shape[1]
    depth = w_ada.shape[0]
    xl = x[0]
    xc = ctx[0]
    rope_head = _rope_slab_tables(s, HEAD_DIM, 0, 2)
    rope_mla = _rope_slab_tables(s, MLA_ROPE, MLA_NOPE, 1)

    cond8 = jnp.zeros((8, d), F32).at[0].set(c[0]).at[1].set(c_ctx)
    mod = modvec(cond8, w_ada, b_ada)
    tn = INPROJ_TN
    in_pad = _round_up(IN_COLS, tn)
    wr = jnp.pad(w_router, ((0, 0), (0, LANE - N_EXPERTS))).astype(BF16)
    inv_sqrt_d = HEAD_DIM ** -0.5
    moe_w = [t.astype(BF16).reshape((depth * N_EXPERTS,) + t.shape[2:])
             for t in (moe_w1, moe_w3, moe_w2)]
    sh_w = [t.astype(BF16) for t in (sh_w1, sh_w3, sh_w2)]

    for l in range(depth):
        with_ctx = l < depth - 1
        m_lat = mod[l, 0].reshape(6, 1, d)
        m_ctx = mod[l, 1].reshape(6, 1, d)
        w_in_l = jnp.pad(w_in[l], ((0, 0), (0, in_pad - IN_COLS))).astype(BF16)
        w_out_l = w_out[l].astype(BF16)
        g_a = g_attn[l][None]
        g_f = g_ffn[l][None]

        p = inproj(xl, g_a, 1.0 + m_lat[1], m_lat[0], w_in_l, tn)
        pc = inproj(xc, g_a, 1.0 + m_ctx[1], m_ctx[0], w_in_l, tn)

        w = NA_HEADS * HEAD_DIM
        qT, k, qc, kc = _qk_prep(p, pc, NA_OFF, w, na_q_gain[l], na_k_gain[l], None)
        vT = slab_prep(p, (NA_OFF + 2 * w) // LANE, w // LANE, transposed=True)
        vc = pc[:, NA_OFF + 2 * w:WA_OFF]
        u_na = _score_bound(HEAD_DIM, na_q_gain[l], na_k_gain[l], inv_sqrt_d)
        oT_na = na_attn(qT, k, vT, kc, vc.T, na_rpb[l], u_na)
        oc_na = (_merge(ctx_attn(_ctx_heads(qc, NA_HEADS), _ctx_heads(kc, NA_HEADS),
                                 _ctx_heads(vc, NA_HEADS))) if with_ctx else None)

        w, wk = WA_HEADS * HEAD_DIM, WA_KV_HEADS * HEAD_DIM
        qT, k, qc, kc = _qk_prep(p, pc, WA_OFF, w, wa_q_gain[l], wa_k_gain[l], rope_head, wk)
        vT = slab_prep(p, (WA_OFF + w + wk) // LANE, wk // LANE, transposed=True)
        vc = pc[:, WA_OFF + w + wk:DIFF_OFF]
        u_wa = _score_bound(HEAD_DIM, wa_q_gain[l], wa_k_gain[l], inv_sqrt_d)
        sink2 = wa_sink[l].astype(F32) * LOG2E
        oT_wa = wa_attn(qT, k, vT, kc, vc.T, sink2, u_wa)
        oc_wa = (_merge(ctx_attn(_ctx_heads(qc, WA_HEADS), _ctx_heads(kc, WA_KV_HEADS),
                                 _ctx_heads(vc, WA_KV_HEADS), sink2)) if with_ctx else None)

        lambda_init = 0.8 - 0.6 * math.exp(-0.3 * l)
        lam = (jnp.exp(jnp.sum(diff_lq1[l].astype(F32) * diff_lk1[l].astype(F32)))
               - jnp.exp(jnp.sum(diff_lq2[l].astype(F32) * diff_lk2[l].astype(F32))) + lambda_init)
        w = DIFF_HEADS * 2 * DIFF_DIM
        qT, k, qc, kc = _qk_prep(p, pc, DIFF_OFF, w, diff_q_gain[l], diff_k_gain[l], rope_head)
        vT = slab_prep(p, (DIFF_OFF + 2 * w) // LANE, w // LANE, transposed=True)
        vc = pc[:, DIFF_OFF + 2 * w:MLA_OFF]
        u_df = _score_bound(DIFF_DIM, diff_q_gain[l], diff_k_gain[l], DIFF_DIM ** -0.5)
        oT_df = flash(qT, k, kc, vT, vc.T, u_df, 2).reshape(DIFF_HEADS, 2, DIFF_V_DIM, s)
        oc_df = None
        if with_ctx:
            oc = ctx_attn(_ctx_heads(qc, 2 * DIFF_HEADS), _ctx_heads(kc, 2 * DIFF_HEADS),
                          _ctx_heads(vc, DIFF_HEADS)).reshape(DIFF_HEADS, 2, n_ctx, DIFF_V_DIM)
            oc_df = _diff_post(jnp.transpose(oc[:, 0], (1, 0, 2)), jnp.transpose(oc[:, 1], (1, 0, 2)),
                               lam, diff_sub_gain[l], lambda_init).reshape(n_ctx, -1)

        qT, k, kc, vT, vcT, qc, vc = _mla_prep(
            p, pc, mla_qa_gain[l], mla_kva_gain[l], mla_w_uq[l], mla_w_ukv[l], mla_q_gain[l],
            mla_k_gain[l], rope_mla)
        u_ml = _score_bound(MLA_NOPE + MLA_ROPE, mla_q_gain[l], mla_k_gain[l],
                            (MLA_NOPE + MLA_ROPE) ** -0.5)
        oT_ml = flash(qT, k, kc, vT, vcT, u_ml, 1)
        oc_ml = (_merge(ctx_attn(_ctx_heads(qc, MLA_HEADS), _ctx_heads(kc, MLA_HEADS),
                                 _ctx_heads(vc, MLA_HEADS))) if with_ctx else None)

        xl, hf, lg = outproj_t(oT_na, oT_wa, oT_df, oT_ml,
                               diff_sub_gain[l].astype(F32) * (1.0 - lambda_init), lam,
                               xl, w_out_l, m_lat[2], g_f, 1.0 + m_lat[4], m_lat[3], wr)
        if with_ctx:
            oc_cat = jnp.concatenate([oc_na, oc_wa, oc_df, oc_ml], axis=-1).astype(BF16)
            xc, hfc, lgc = outproj(oc_cat, xc, w_out_l, m_ctx[2], g_f, 1.0 + m_ctx[4], m_ctx[3], wr)
            tokens = jnp.concatenate([hfc, hf], axis=0)
            logits = jnp.concatenate([lgc, lg], axis=0)
        else:
            tokens, logits = hf, lg
        y = _moe(tokens, logits, b_router, l, *moe_w, *sh_w)
        if with_ctx:
            xc = xc + m_ctx[5] * y[:n_ctx]
            y = y[n_ctx:]
        xl = xl + m_lat[5] * y
    return xl[None]
```

```python
import functools
import math

import jax
import jax.numpy as jnp
import numpy as np
from jax import lax
from jax.experimental import pallas as pl
from jax.experimental.pallas import tpu as pltpu

F32 = jnp.float32
BF16 = jnp.bfloat16

GRID_W = 64
HEAD_DIM = 64
ROPE_THETA = 10000.0
NORM_EPS = 1e-6
NEG_INF = -1e30
WINDOW = 128
NA_HEADS = 8
NA_KH = 8
NA_KW = 16
WA_HEADS = 8
WA_KV_HEADS = 2
DIFF_HEADS = 4
DIFF_DIM = 64
DIFF_V_DIM = 128
MLA_HEADS = 8
MLA_NOPE = 64
MLA_ROPE = 32
MLA_V = 64
MLA_Q_LORA = 384
MLA_KV_LORA = 128
N_EXPERTS = 16
N_EXPERT_GROUPS = 4
TOP_K = 2
ROUTED_SCALE = 1.0
MOE_BLOCK = 256

NA_COLS = 3 * NA_HEADS * HEAD_DIM
WA_COLS = (WA_HEADS + 2 * WA_KV_HEADS) * HEAD_DIM
DIFF_COLS = 3 * DIFF_HEADS * 2 * DIFF_DIM
MLA_COLS = MLA_Q_LORA + MLA_KV_LORA + MLA_ROPE
NA_OFF = 0
WA_OFF = NA_OFF + NA_COLS
DIFF_OFF = WA_OFF + WA_COLS
MLA_OFF = DIFF_OFF + DIFF_COLS
IN_COLS = MLA_OFF + MLA_COLS

LANE = 128
INPROJ_TN = 768
LOG2E = math.log2(math.e)
VMEM_LIMIT = 48 * 1024 * 1024


def _round_up(n, m):
    return (n + m - 1) // m * m


def _params(sem):
    return pltpu.CompilerParams(dimension_semantics=sem, vmem_limit_bytes=VMEM_LIMIT)


def _sigmoid(x):
    return 1.0 / (1.0 + jnp.exp(-x))


def _modvec_kernel(c_ref, w_ref, b_ref, o_ref):
    a = c_ref[...]
    a = a * _sigmoid(a)
    o_ref[0] = jnp.dot(a, w_ref[0], preferred_element_type=F32,
                       precision=lax.Precision.HIGHEST) + b_ref[0]


def modvec(cond8, w_ada, b_ada):
    depth, d, n = w_ada.shape
    tn = 1024
    return pl.pallas_call(
        _modvec_kernel,
        grid=(depth, n // tn),
        in_specs=[pl.BlockSpec((8, d), lambda l, j: (0, 0)),
                  pl.BlockSpec((1, d, tn), lambda l, j: (l, 0, j)),
                  pl.BlockSpec((1, 1, tn), lambda l, j: (l, 0, j))],
        out_specs=pl.BlockSpec((1, 8, tn), lambda l, j: (l, 0, j)),
        out_shape=jax.ShapeDtypeStruct((depth, 8, n), F32),
        compiler_params=_params(("parallel", "parallel")),
        name="modvec",
    )(cond8, w_ada, b_ada.reshape(depth, 1, n))


def _inproj_kernel(x_ref, g_ref, sc_ref, sh_ref, w_ref, o_ref, h_sc):
    @pl.when(pl.program_id(1) == 0)
    def _():
        x = x_ref[...]
        ms = jnp.mean(x * x, axis=-1, keepdims=True)
        y = x * lax.rsqrt(ms + NORM_EPS) * g_ref[...]
        h_sc[...] = (y * sc_ref[...] + sh_ref[...]).astype(BF16)

    o_ref[...] = jnp.dot(h_sc[...], w_ref[...], preferred_element_type=F32).astype(o_ref.dtype)


def inproj(x, g, sc1, sh, w, tn):
    r, d = x.shape
    n = w.shape[1]
    tm = min(r, 1024)
    vec = pl.BlockSpec((1, d), lambda i, j: (0, 0))
    return pl.pallas_call(
        _inproj_kernel,
        grid=(r // tm, n // tn),
        in_specs=[pl.BlockSpec((tm, d), lambda i, j: (i, 0)), vec, vec, vec,
                  pl.BlockSpec((d, tn), lambda i, j: (0, j))],
        out_specs=pl.BlockSpec((tm, tn), lambda i, j: (i, j)),
        out_shape=jax.ShapeDtypeStruct((r, n), BF16),
        scratch_shapes=[pltpu.VMEM((tm, d), BF16)],
        compiler_params=_params(("parallel", "arbitrary")),
        name="inproj",
    )(x, g, sc1, sh, w)


def _mm_kernel(x_ref, w_ref, o_ref):
    o_ref[...] = jnp.dot(x_ref[...], w_ref[...], preferred_element_type=F32)


def matmul(x, w):
    r, k = x.shape
    n = w.shape[1]
    tm = min(r, 1024)
    return pl.pallas_call(
        _mm_kernel,
        grid=(r // tm,),
        in_specs=[pl.BlockSpec((tm, k), lambda i: (i, 0)),
                  pl.BlockSpec((k, n), lambda i: (0, 0))],
        out_specs=pl.BlockSpec((tm, n), lambda i: (i, 0)),
        out_shape=jax.ShapeDtypeStruct((r, n), F32),
        compiler_params=_params(("parallel",)),
        name="matmul",
    )(x, w)


FLASH_SAFE_SHIFT = 60.0
FLASH_BQ = 1024
FLASH_CHAIN_Q = 512
FLASH_SCORE_BYTES = 17 * 1024 * 1024


def _flash_kernel(qT_ref, k_ref, kc_ref, vT_ref, vcT_ref, u_ref, o_ref, acc_sc, *, n_comp, n_split,
                  bk, n_kb, online):
    bq = qT_ref.shape[1] // n_split
    dv = vT_ref.shape[0]
    comp_rows = LANE // n_comp
    chains = [(c, h) for c in range(n_comp) for h in range(n_split)]
    u = u_ref[0:1, 0:1]
    row = lax.broadcasted_iota(jnp.int32, (LANE, bq), 0)
    shift_rows = jnp.where(row == 0, -u, 0.0).astype(BF16)
    q_ops = []
    for c, h in chains:
        qT = qT_ref[:, h * bq:(h + 1) * bq]
        if n_comp > 1:
            qT = jnp.where((row >= c * comp_rows) & (row < (c + 1) * comp_rows), qT,
                           jnp.zeros_like(qT))
        q_ops.append(jnp.concatenate([qT, shift_rows], axis=0))
    for i in range(len(chains)):
        acc_sc[i] = jnp.zeros(acc_sc.shape[1:], F32)

    def step(kb, vb, carry):
        n = kb.shape[0]
        ones_col = jnp.where(lax.broadcasted_iota(jnp.int32, (n, LANE), 1) == 0, 1.0, 0.0)
        ka = jnp.concatenate([kb, ones_col.astype(BF16)], axis=1)
        scores = [jnp.dot(ka, q, preferred_element_type=F32) for q in q_ops]
        out = []
        for i, sT in enumerate(scores):
            m, l = carry[i]
            if online:
                m_new = jnp.maximum(m, jnp.max(sT, axis=0, keepdims=True))
                alpha = jnp.exp2(m - m_new)
                pT = jnp.exp2(sT - m_new)
                acc_sc[i] = alpha * acc_sc[i] + jnp.dot(vb, pT.astype(BF16),
                                                        preferred_element_type=F32)
                out.append((m_new, alpha * l + jnp.sum(pT, axis=0, keepdims=True)))
            else:
                pT = jnp.exp2(sT)
                acc_sc[i] += jnp.dot(vb, pT.astype(BF16), preferred_element_type=F32)
                out.append((m, l + jnp.sum(pT, axis=0, keepdims=True)))
        return tuple(out)

    def body(j, carry):
        off = pl.multiple_of(j * bk, bk)
        return step(k_ref[pl.ds(off, bk), :], vT_ref[:, pl.ds(off, bk)], carry)

    init = tuple((jnp.full((1, bq), NEG_INF, F32), jnp.zeros((1, bq), F32)) for _ in chains)
    fin = lax.fori_loop(0, n_kb, body, step(kc_ref[...], vcT_ref[...], init), unroll=2)
    for i, (c, h) in enumerate(chains):
        o_ref[c * dv:(c + 1) * dv, h * bq:(h + 1) * bq] = acc_sc[i] * (1.0 / fin[i][1])


def _pick_bk(n, n_chains, bq):
    for bk in (4096, 2048, 1024, 512, 256, 128):
        if n % bk == 0 and n_chains * bk * bq * 4 <= FLASH_SCORE_BYTES:
            return bk
    raise ValueError(f"key count {n} must be a multiple of {LANE}")


def _flash_call(qT, k, kc, vT, vcT, u8, n_comp, online):
    hw, s = qT.shape
    n_heads = hw // LANE
    dv = vT.shape[0] // n_heads
    l = kc.shape[0]
    bq = min(FLASH_BQ, s)
    n_split = max(1, bq // FLASH_CHAIN_Q)
    bk = _pick_bk(s, n_comp * n_split, bq // n_split)
    kern = functools.partial(_flash_kernel, n_comp=n_comp, n_split=n_split, bk=bk, n_kb=s // bk,
                             online=online)
    return pl.pallas_call(
        kern,
        grid=(n_heads, s // bq),
        in_specs=[pl.BlockSpec((LANE, bq), lambda h, i: (h, i)),
                  pl.BlockSpec((s, LANE), lambda h, i: (0, h)),
                  pl.BlockSpec((l, LANE), lambda h, i: (0, h)),
                  pl.BlockSpec((dv, s), lambda h, i: (h, 0)),
                  pl.BlockSpec((dv, l), lambda h, i: (h, 0)),
                  pl.BlockSpec(u8.shape, lambda h, i: (0, 0))],
        out_specs=pl.BlockSpec((n_comp * dv, bq), lambda h, i: (h, i)),
        out_shape=jax.ShapeDtypeStruct((n_heads * n_comp * dv, s), F32),
        scratch_shapes=[pltpu.VMEM((n_comp * n_split, dv, bq // n_split), F32)],
        compiler_params=_params(("parallel", "arbitrary")),
        name="flash_online" if online else "flash",
    )(qT, k, kc, vT, vcT, u8)


def flash(qT, k, kc, vT, vcT, u, n_comp):
    u8 = jnp.full((8, LANE), u, F32)
    return lax.cond(u <= FLASH_SAFE_SHIFT,
                    lambda *a: _flash_call(*a, n_comp, False),
                    lambda *a: _flash_call(*a, n_comp, True),
                    qT, k, kc, vT, vcT, u8)


def _score_bound(d, q_gain, k_gain, scale):
    return (d * scale * LOG2E * 1.02 * jnp.max(jnp.abs(q_gain.astype(F32)))
            * jnp.max(jnp.abs(k_gain.astype(F32))) + 0.01)


def _ctx_kernel(q_ref, k_ref, v_ref, sink_ref, o_ref, *, use_sink):
    s = lax.dot_general(q_ref[0], k_ref[0], (((1,), (1,)), ((), ())),
                        preferred_element_type=F32)
    m = jnp.max(s, axis=-1, keepdims=True)
    if use_sink:
        sk = sink_ref[0, 0:1, 0:1]
        m = jnp.maximum(m, sk)
    p = jnp.exp2(s - m)
    l = jnp.sum(p, axis=-1, keepdims=True)
    if use_sink:
        l = l + jnp.exp2(sk - m)
    o = jnp.dot(p.astype(BF16), v_ref[0], preferred_element_type=F32)
    o_ref[0] = o * (1.0 / l)


def ctx_attn(q, k, v, sink=None):
    h, l, dk = q.shape
    hk, hv, dv = k.shape[0], v.shape[0], v.shape[2]
    use_sink = sink is not None
    if sink is None:
        sink = jnp.zeros((h,), F32)
    sink3 = jnp.broadcast_to(sink.astype(F32)[:, None, None], (h, 8, LANE))
    kern = functools.partial(_ctx_kernel, use_sink=use_sink)
    return pl.pallas_call(
        kern,
        grid=(h,),
        in_specs=[pl.BlockSpec((1, l, dk), lambda i: (i, 0, 0)),
                  pl.BlockSpec((1, l, dk), lambda i: (i // (h // hk), 0, 0)),
                  pl.BlockSpec((1, l, dv), lambda i: (i // (h // hv), 0, 0)),
                  pl.BlockSpec((1, 8, LANE), lambda i: (i, 0, 0))],
        out_specs=pl.BlockSpec((1, l, dv), lambda i: (i, 0, 0)),
        out_shape=jax.ShapeDtypeStruct((h, l, dv), F32),
        compiler_params=_params(("parallel",)),
        name="ctx_attn",
    )(q, k, v, sink3)


LOCAL_BQ = 512


def _pad_rows(qT_h, slot, n_slots):
    z = jnp.zeros_like(qT_h)
    return jnp.concatenate([qT_h if s == slot else z for s in range(n_slots)], axis=0)


def _softmax_pv(s_list, v_list, extra, exact_max):
    if exact_max:
        m = functools.reduce(jnp.maximum, [jnp.max(s, axis=0, keepdims=True) for s in s_list])
        if extra is not None:
            m = jnp.maximum(m, extra)
            extra = extra - m
        s_list = [s - m for s in s_list]
    p_list = [jnp.exp2(s) for s in s_list]
    l = functools.reduce(jnp.add, [jnp.sum(p, axis=0, keepdims=True) for p in p_list])
    if extra is not None:
        l = l + jnp.exp2(extra)
    acc = functools.reduce(jnp.add, [jnp.dot(v, p.astype(BF16), preferred_element_type=F32)
                                     for v, p in zip(v_list, p_list)])
    return acc, l


def _wa_kernel(qT_ref, k_ref, vT_ref, kc_ref, vcT_ref, sh_ref, o_ref, *, seq, n_heads, n_kv,
               exact_max):
    i = pl.program_id(0)
    bq = qT_ref.shape[1]
    win = bq + 2 * WINDOW
    grp = n_heads // n_kv
    start = pl.multiple_of(jnp.clip(i * bq - WINDOW, 0, seq - win), WINDOW)
    kw = k_ref[pl.ds(start, win), :]
    vwT = vT_ref[:, pl.ds(start, win)]
    kc = kc_ref[...]
    vcT = vcT_ref[...]
    u = sh_ref[n_heads:n_heads + 1, 0:1]
    kpos = start + lax.broadcasted_iota(jnp.int32, (win, bq), 0)
    qpos = i * bq + lax.broadcasted_iota(jnp.int32, (win, bq), 1)
    mask_shift = jnp.where(jnp.abs(qpos - kpos) <= WINDOW, -u, NEG_INF)
    for h in range(n_heads):
        g = h // grp
        qTp = _pad_rows(qT_ref[h * HEAD_DIM:(h + 1) * HEAD_DIM, :], g, n_kv)
        s_loc = jnp.dot(kw, qTp, preferred_element_type=F32) + mask_shift
        s_ctx = jnp.dot(kc, qTp, preferred_element_type=F32) - u
        acc, l = _softmax_pv([s_loc, s_ctx], [vwT, vcT], sh_ref[h:h + 1, 0:1], exact_max)
        o_ref[h * HEAD_DIM:(h + 1) * HEAD_DIM, :] = (acc[g * HEAD_DIM:(g + 1) * HEAD_DIM]
                                                     * (1.0 / l))


def _wa_call(qT, k, vT, kc, vcT, sh, exact_max):
    hd, s = qT.shape
    kw = k.shape[1]
    l = kc.shape[0]
    bq = min(LOCAL_BQ, s)
    assert s % bq == 0 and s >= bq + 2 * WINDOW
    kern = functools.partial(_wa_kernel, seq=s, n_heads=hd // HEAD_DIM, n_kv=kw // HEAD_DIM,
                             exact_max=exact_max)
    full = lambda i: (0, 0)
    return pl.pallas_call(
        kern,
        grid=(s // bq,),
        in_specs=[pl.BlockSpec((hd, bq), lambda i: (0, i)),
                  pl.BlockSpec((s, kw), full), pl.BlockSpec((kw, s), full),
                  pl.BlockSpec((l, kw), full), pl.BlockSpec((kw, l), full),
                  pl.BlockSpec(sh.shape, full)],
        out_specs=pl.BlockSpec((hd, bq), lambda i: (0, i)),
        out_shape=jax.ShapeDtypeStruct((hd, s), F32),
        compiler_params=_params(("parallel",)),
        name="wa_attn_max" if exact_max else "wa_attn",
    )(qT, k, vT, kc, vcT, sh)


def wa_attn(qT, k, vT, kc, vcT, sink, u):
    n_heads = qT.shape[0] // HEAD_DIM
    rows = jnp.concatenate([sink.astype(F32) - u, jnp.reshape(u, (1,)).astype(F32),
                            jnp.zeros((2 * 8 - n_heads - 1,), F32)])
    sh = jnp.broadcast_to(rows[:, None], (rows.shape[0], LANE))
    return lax.cond(u <= FLASH_SAFE_SHIFT,
                    lambda *a: _wa_call(*a, False), lambda *a: _wa_call(*a, True),
                    qT, k, vT, kc, vcT, sh)


NA_KEY_BLOCKS = 4
NA_KEY_ROWS = 2 * NA_KH


def _na_kernel(qT_ref, *refs, exact_max):
    k_refs, v_refs = refs[:NA_KEY_BLOCKS], refs[NA_KEY_BLOCKS:2 * NA_KEY_BLOCKS]
    kc_ref, vcT_ref, bias_ref, sh_ref, o_ref = refs[2 * NA_KEY_BLOCKS:]
    kwin = jnp.concatenate([r[...] for r in k_refs], axis=0)
    vwinT = jnp.concatenate([r[...] for r in v_refs], axis=1)
    kc = kc_ref[...]
    vcT = vcT_ref[...]
    u = sh_ref[0:1, 0:1]
    for hh in range(2):
        rows = slice(hh * HEAD_DIM, (hh + 1) * HEAD_DIM)
        qTp = _pad_rows(qT_ref[rows, :], hh, 2)
        s_loc = jnp.dot(kwin, qTp, preferred_element_type=F32) + bias_ref[0, hh]
        s_ctx = jnp.dot(kc, qTp, preferred_element_type=F32) - u
        acc, l = _softmax_pv([s_loc, s_ctx], [vwinT, vcT], None, exact_max)
        o_ref[rows, :] = acc[rows] * (1.0 / l)


def na_bias_table(rpb, rows, shift):
    n_h = rpb.shape[0]
    a = jnp.arange(NA_KH)
    b = jnp.arange(NA_KEY_ROWS)
    c = jnp.arange(GRID_W)
    kc = jnp.arange(GRID_W)
    c0 = jnp.clip(c - NA_KW // 2, 0, GRID_W - NA_KW)
    col_ok = (kc[None, :] >= c0[:, None]) & (kc[None, :] < c0[:, None] + NA_KW)
    col_rel = jnp.clip(kc[None, :] - c[:, None] + (NA_KW - 1), 0, 2 * NA_KW - 2)
    row_rel = jnp.clip(b[None, :] - a[:, None] + NA_KH // 2 - 1, 0, 2 * NA_KH - 2)

    def edge_ok(r_base):
        r0 = jnp.clip(r_base + a - NA_KH // 2, 0, rows - NA_KH)
        key_row = r_base - NA_KH // 2 + b
        return (key_row[None, :] >= r0[:, None]) & (key_row[None, :] < r0[:, None] + NA_KH)

    inner_ok = (b[None, :] >= a[:, None]) & (b[None, :] < a[:, None] + NA_KH)
    row_ok = jnp.stack([edge_ok(0), inner_ok, edge_ok(rows - NA_KH)])
    e_c = jax.nn.one_hot(col_rel, 2 * NA_KW - 1, dtype=F32)
    e_r = jax.nn.one_hot(row_rel, 2 * NA_KH - 1, dtype=F32)
    hi = lax.Precision.HIGHEST
    t = jnp.einsum('hrx,ckx->hrck', rpb.astype(F32), e_c, precision=hi)
    t = jnp.einsum('abr,hrck->habck', e_r, t, precision=hi) * LOG2E - shift
    ok = row_ok[:, None, :, :, None, None] & col_ok[None, None, None, None]
    t = jnp.where(ok, t[None], NEG_INF)
    t = jnp.transpose(t, (0, 1, 3, 5, 2, 4))
    return t.reshape(3, n_h, NA_KEY_ROWS * GRID_W, NA_KH * GRID_W)


def _na_call(qT, k, vT, kc, vcT, bias, sh, exact_max):
    hd, s = qT.shape
    l = kc.shape[0]
    bq = NA_KH * GRID_W
    nb = s // bq
    pw = 2 * HEAD_DIM
    assert s % bq == 0 and bq == LOCAL_BQ
    var = lambda i: jnp.where(i == 0, 0, jnp.where(i == nb - 1, 2, 1))
    kb = bq // 2
    blk = lambda i, t: jnp.clip(2 * i - 1 + t, 0, 2 * nb - 1)
    kspecs = [pl.BlockSpec((kb, pw), lambda p, i, t=t: (blk(i, t), p))
              for t in range(NA_KEY_BLOCKS)]
    vspecs = [pl.BlockSpec((pw, kb), lambda p, i, t=t: (p, blk(i, t)))
              for t in range(NA_KEY_BLOCKS)]
    kern = functools.partial(_na_kernel, exact_max=exact_max)
    return pl.pallas_call(
        kern,
        grid=(hd // pw, nb),
        in_specs=[pl.BlockSpec((pw, bq), lambda p, i: (p, i)), *kspecs, *vspecs,
                  pl.BlockSpec((l, pw), lambda p, i: (0, p)),
                  pl.BlockSpec((pw, l), lambda p, i: (p, 0)),
                  pl.BlockSpec((1, 2, NA_KEY_ROWS * GRID_W, bq), lambda p, i: (var(i), p, 0, 0)),
                  pl.BlockSpec(sh.shape, lambda p, i: (0, 0))],
        out_specs=pl.BlockSpec((pw, bq), lambda p, i: (p, i)),
        out_shape=jax.ShapeDtypeStruct((hd, s), F32),
        compiler_params=_params(("parallel", "arbitrary")),
        name="na_attn_max" if exact_max else "na_attn",
    )(qT, *([k] * NA_KEY_BLOCKS), *([vT] * NA_KEY_BLOCKS), kc, vcT, bias, sh)


def na_attn(qT, k, vT, kc, vcT, rpb, u_qk):
    s = qT.shape[1]
    u = u_qk + jnp.maximum(jnp.max(rpb.astype(F32)) * LOG2E, 0.0)
    bias = na_bias_table(rpb, s // GRID_W, u)
    sh = jnp.full((8, LANE), u, F32)
    return lax.cond(u <= FLASH_SAFE_SHIFT,
                    lambda *a: _na_call(*a, False), lambda *a: _na_call(*a, True),
                    qT, k, vT, kc, vcT, bias, sh)


def _outproj_kernel(o_ref, x_ref, w_ref, gate_ref, g_ref, sc_ref, sh_ref, wr_ref,
                    xo_ref, hf_ref, lg_ref):
    acc = jnp.dot(o_ref[...], w_ref[...], preferred_element_type=F32)
    xn = x_ref[...] + gate_ref[...] * acc
    xo_ref[...] = xn
    ms = jnp.mean(xn * xn, axis=-1, keepdims=True)
    y = xn * lax.rsqrt(ms + NORM_EPS) * g_ref[...]
    hf = (y * sc_ref[...] + sh_ref[...]).astype(BF16)
    hf_ref[...] = hf
    lg_ref[...] = jnp.dot(hf, wr_ref[...], preferred_element_type=F32)


def _outproj_t_kernel(na_ref, wa_ref, df_ref, ml_ref, dfg_ref, lam_ref, x_ref, w_ref, gate_ref,
                      g_ref, sc_ref, sh_ref, wr_ref, xo_ref, hf_ref, lg_ref):
    lam = lam_ref[0:1, 0:1]
    parts = [na_ref[...], wa_ref[...]]
    for h in range(df_ref.shape[0]):
        dd = df_ref[h, 0] - lam * df_ref[h, 1]
        ms = jnp.mean(dd * dd, axis=0, keepdims=True)
        parts.append(dd * lax.rsqrt(ms + NORM_EPS) * dfg_ref[...])
    parts.append(ml_ref[...])
    o = jnp.concatenate([jnp.transpose(t).astype(BF16) for t in parts], axis=1)
    acc = jnp.dot(o, w_ref[...], preferred_element_type=F32)
    xn = x_ref[...] + gate_ref[...] * acc
    xo_ref[...] = xn
    ms = jnp.mean(xn * xn, axis=-1, keepdims=True)
    y = xn * lax.rsqrt(ms + NORM_EPS) * g_ref[...]
    hf = (y * sc_ref[...] + sh_ref[...]).astype(BF16)
    hf_ref[...] = hf
    lg_ref[...] = jnp.dot(hf, wr_ref[...], preferred_element_type=F32)


def outproj_t(oT_na, oT_wa, oT_df, oT_ml, df_gain, lam, x, w, gate, g, sc1, sh, wr):
    r, d = x.shape
    tm = min(r, 256)
    nh, _, dv2, _ = oT_df.shape
    wdt = oT_na.shape[0]
    dfg = jnp.broadcast_to(df_gain.astype(F32)[:, None], (dv2, tm))
    lam8 = jnp.full((8, LANE), lam, F32)
    vec = pl.BlockSpec((1, d), lambda i: (0, 0))
    row = lambda n: pl.BlockSpec((tm, n), lambda i: (i, 0))
    colblk = pl.BlockSpec((wdt, tm), lambda i: (0, i))
    const = lambda shape: pl.BlockSpec(shape, lambda i: tuple(0 for _ in shape))
    return pl.pallas_call(
        _outproj_t_kernel,
        grid=(r // tm,),
        in_specs=[colblk, colblk, pl.BlockSpec((nh, 2, dv2, tm), lambda i: (0, 0, 0, i)), colblk,
                  const((dv2, tm)), const((8, LANE)), row(d), const(w.shape), vec, vec, vec, vec,
                  const((d, LANE))],
        out_specs=[row(d), row(d), row(LANE)],
        out_shape=[jax.ShapeDtypeStruct((r, d), F32), jax.ShapeDtypeStruct((r, d), BF16),
                   jax.ShapeDtypeStruct((r, LANE), F32)],
        compiler_params=_params(("parallel",)),
        name="outproj_t",
    )(oT_na, oT_wa, oT_df, oT_ml, dfg, lam8, x, w, gate, g, sc1, sh, wr)


def outproj(o, x, w, gate, g, sc1, sh, wr):
    r, d = x.shape
    k = o.shape[1]
    tm = min(r, 256)
    vec = pl.BlockSpec((1, d), lambda i: (0, 0))
    row = lambda n: pl.BlockSpec((tm, n), lambda i: (i, 0))
    return pl.pallas_call(
        _outproj_kernel,
        grid=(r // tm,),
        in_specs=[row(k), row(d), pl.BlockSpec((k, d), lambda i: (0, 0)), vec, vec, vec, vec,
                  pl.BlockSpec((d, LANE), lambda i: (0, 0))],
        out_specs=[row(d), row(d), row(LANE)],
        out_shape=[jax.ShapeDtypeStruct((r, d), F32), jax.ShapeDtypeStruct((r, d), BF16),
                   jax.ShapeDtypeStruct((r, LANE), F32)],
        compiler_params=_params(("parallel",)),
        name="outproj",
    )(o, x, w, gate, g, sc1, sh, wr)


def _moe_kernel(be_ref, nv_ref, x_ref, w1_ref, w3_ref, w2_ref, o_ref):
    i = pl.program_id(0)

    @pl.when(i < nv_ref[0])
    def _():
        x = x_ref[...]
        a = jnp.dot(x, w1_ref[0], preferred_element_type=F32)
        b = jnp.dot(x, w3_ref[0], preferred_element_type=F32)
        hmid = (a * _sigmoid(a) * b).astype(BF16)
        o_ref[...] = jnp.dot(hmid, w2_ref[0], preferred_element_type=F32).astype(o_ref.dtype)

    @pl.when(i >= nv_ref[0])
    def _():
        o_ref[...] = jnp.zeros(o_ref.shape, o_ref.dtype)


def moe_blocks(x, w1, w3, w2, blk_e, n_valid):
    n, d = x.shape
    f = w1.shape[2]
    n_blk = n // MOE_BLOCK
    grid_spec = pltpu.PrefetchScalarGridSpec(
        num_scalar_prefetch=2,
        grid=(n_blk,),
        in_specs=[pl.BlockSpec((MOE_BLOCK, d), lambda i, be, nv: (i, 0)),
                  pl.BlockSpec((1, d, f), lambda i, be, nv: (be[i], 0, 0)),
                  pl.BlockSpec((1, d, f), lambda i, be, nv: (be[i], 0, 0)),
                  pl.BlockSpec((1, f, d), lambda i, be, nv: (be[i], 0, 0))],
        out_specs=pl.BlockSpec((MOE_BLOCK, d), lambda i, be, nv: (i, 0)),
    )
    return pl.pallas_call(
        _moe_kernel,
        grid_spec=grid_spec,
        out_shape=jax.ShapeDtypeStruct((n, d), BF16),
        compiler_params=_params(("arbitrary",)),
        name="moe_blocks",
    )(blk_e, n_valid, x, w1, w3, w2)


def _rms(x, g):
    y = x * lax.rsqrt(jnp.mean(x * x, axis=-1, keepdims=True) + NORM_EPS)
    return y * g


def _rope_tables(seq_len, rot_dim):
    n = rot_dim // 4
    t = np.arange(seq_len)
    row = (t // GRID_W).astype(np.float32)[:, None]
    col = (t % GRID_W).astype(np.float32)[:, None]
    inv = (np.float32(ROPE_THETA) ** (-np.arange(n, dtype=np.float32) / np.float32(n)))
    inv = inv.astype(np.float32)
    return tuple(f(a * inv).astype(np.float32) for a in (row, col) for f in (np.cos, np.sin))


def _heads_first(t):
    return jnp.transpose(t, (1, 0, 2))


def _route(logits, b_router):
    n = logits.shape[0]
    per = N_EXPERTS // N_EXPERT_GROUPS
    scores = jax.nn.sigmoid(logits.astype(F32))
    sel = scores + b_router.astype(F32)
    grp = sel.reshape(n, N_EXPERT_GROUPS, per)
    gscore = None
    for a in range(per):
        for bb in range(a + 1, per):
            pair = grp[..., a] + grp[..., bb]
            gscore = pair if gscore is None else jnp.maximum(gscore, pair)
    gidx = jnp.argmax(gscore, axis=-1)
    eids = jnp.arange(N_EXPERTS)
    masked = jnp.where((eids // per)[None, :] == gidx[:, None], sel, -jnp.inf)
    e1 = jnp.argmax(masked, axis=-1)
    e2 = jnp.argmax(jnp.where(eids[None, :] == e1[:, None], -jnp.inf, masked), axis=-1)
    eidx = jnp.stack([e1, e2], axis=1).astype(jnp.int32)
    wts = jnp.take_along_axis(scores, eidx, axis=1)
    wts = wts / jnp.sum(wts, axis=-1, keepdims=True) * ROUTED_SCALE
    return eidx, wts


def _prefix_counts(onehot):
    n, e = onehot.shape
    ch = onehot.astype(F32).reshape(n // LANE, LANE, e)
    tri = jnp.tril(jnp.ones((LANE, LANE), F32))
    within = jnp.einsum('ij,cjk->cik', tri, ch)
    tot = within[:, -1, :]
    base = jnp.cumsum(tot, axis=0) - tot
    return (within + base[:, None, :]).reshape(n, e)


def _dispatch(eidx):
    n = eidx.shape[0]
    n_assign = n * TOP_K
    assert n_assign % LANE == 0
    e_flat = eidx.reshape(-1)
    tok = jnp.repeat(jnp.arange(n, dtype=jnp.int32), TOP_K)
    onehot = e_flat[:, None] == jnp.arange(N_EXPERTS)[None, :]
    csum = _prefix_counts(onehot)
    counts = csum[-1].astype(jnp.int32)
    rank = jnp.sum(jnp.where(onehot, csum, 0.0), axis=1).astype(jnp.int32) - 1
    padded = (counts + MOE_BLOCK - 1) // MOE_BLOCK * MOE_BLOCK
    pend = jnp.cumsum(padded)
    pstart = pend - padded
    dest = jnp.sum(jnp.where(onehot, pstart[None, :], 0), axis=1) + rank
    n_slots = (n_assign + N_EXPERTS * (MOE_BLOCK - 1) + MOE_BLOCK - 1) // MOE_BLOCK * MOE_BLOCK
    n_blk = n_slots // MOE_BLOCK
    slot_tok = jnp.full((n_slots,), n, dtype=jnp.int32).at[dest].set(tok)
    slot_of = dest.reshape(n, TOP_K)
    blk_start = jnp.arange(n_blk, dtype=jnp.int32) * MOE_BLOCK
    blk_e = jnp.minimum(jnp.sum(blk_start[:, None] >= pend[None, :], axis=1), N_EXPERTS - 1)
    n_valid = (pend[-1] // MOE_BLOCK).reshape(1)
    return slot_tok, slot_of, blk_e.astype(jnp.int32), n_valid.astype(jnp.int32)


PREP_TM = 4096


def _slab_prep_kernel(*refs, seg, count, rot, scale, transposed, has_add, raw):
    refs = list(refs)
    o_ref = refs.pop()
    x = refs[0][...].astype(F32)
    if not raw:
        g_ref = refs[1]
        nxt = 2
        if has_add:
            x = x + refs[nxt][...]
            nxt += 1
        sq = x * x
        hi = sq.astype(BF16)
        lo = (sq - hi.astype(F32)).astype(BF16)
        r = lax.broadcasted_iota(jnp.int32, (LANE, LANE), 0) // seg
        c = lax.broadcasted_iota(jnp.int32, (LANE, LANE), 1) // seg
        blk = jnp.where(r == c, 1.0, 0.0).astype(BF16)
        ssum = (jnp.dot(hi, blk, preferred_element_type=F32)
                + jnp.dot(lo, blk, preferred_element_type=F32))
        x = x * lax.rsqrt(ssum * (1.0 / count) + NORM_EPS) * g_ref[...]
        if rot:
            cos, s_next, s_prev = refs[nxt][...], refs[nxt + 1][...], refs[nxt + 2][...]
            x = (x * cos + pltpu.roll(x, LANE - rot, 1) * s_next + pltpu.roll(x, rot, 1) * s_prev)
        x = x * scale
    o_ref[...] = (jnp.transpose(x) if transposed else x).astype(o_ref.dtype)


def slab_prep(x, col0, n_slabs, gain=None, rope=None, rot=0, add=None, seg=HEAD_DIM,
              count=HEAD_DIM, scale=1.0, transposed=False):
    r = x.shape[0]
    tm = min(PREP_TM, r)
    raw = gain is None
    ops = [x]
    specs = [pl.BlockSpec((tm, LANE), lambda i, j: (i, col0 + j))]
    rowspec = pl.BlockSpec((tm, LANE), lambda i, j: (i, 0))
    if not raw:
        ops.append(gain.astype(F32).reshape(1, LANE))
        specs.append(pl.BlockSpec((1, LANE), lambda i, j: (0, 0)))
        if add is not None:
            ops.append(add)
            specs.append(rowspec)
        if rope is not None:
            ops += list(rope)
            specs += [rowspec] * 3
    kern = functools.partial(_slab_prep_kernel, seg=seg, count=count,
                             rot=rot if rope is not None else 0, scale=scale,
                             transposed=transposed, has_add=add is not None, raw=raw)
    if transposed:
        out_spec = pl.BlockSpec((LANE, tm), lambda i, j: (j, i))
        out_shape = jax.ShapeDtypeStruct((n_slabs * LANE, r), BF16)
    else:
        out_spec = pl.BlockSpec((tm, LANE), lambda i, j: (i, j))
        out_shape = jax.ShapeDtypeStruct((r, n_slabs * LANE), BF16)
    return pl.pallas_call(
        kern,
        grid=(r // tm, n_slabs),
        in_specs=specs,
        out_specs=out_spec,
        out_shape=out_shape,
        compiler_params=_params(("parallel", "arbitrary")),
        name="slab_prep",
    )(*ops)


def _rope_slab_tables(seq_len, width, offset, reps):
    n = width // 4
    cr, sr, cc, sc = (jnp.asarray(t) for t in _rope_tables(seq_len, width))
    one = jnp.ones((seq_len, offset), F32)
    zero = jnp.zeros((seq_len, offset), F32)
    zq = jnp.zeros((seq_len, n), F32)
    rest = LANE // reps - offset - width
    pad1 = jnp.ones((seq_len, rest), F32)
    pad0 = jnp.zeros((seq_len, rest), F32)
    cos = jnp.concatenate([one, cr, cr, cc, cc, pad1] * reps, axis=1)
    s_next = jnp.concatenate([zero, -sr, zq, -sc, zq, pad0] * reps, axis=1)
    s_prev = jnp.concatenate([zero, zq, sr, zq, sc, pad0] * reps, axis=1)
    return cos, s_next, s_prev


def _gain2(g):
    return jnp.concatenate([g, g]).astype(F32)


def _ctx_heads(t, n_heads):
    return _heads_first(t.reshape(t.shape[0], n_heads, -1)).astype(BF16)


def _qk_prep(p, pc, off, w, q_gain, k_gain, rope, wk=None):
    wk = w if wk is None else wk
    qs = HEAD_DIM ** -0.5 * LOG2E
    gq, gk = _gain2(q_gain), _gain2(k_gain)
    c0 = off // LANE
    rot = HEAD_DIM // 4
    qT = slab_prep(p, c0, w // LANE, gq, rope, rot, scale=qs, transposed=True)
    k = slab_prep(p, c0 + w // LANE, wk // LANE, gk, rope, rot)
    qc = slab_prep(pc, c0, w // LANE, gq, scale=qs)
    kc = slab_prep(pc, c0 + w // LANE, wk // LANE, gk)
    return qT, k, qc, kc


def _mla_prep(p, pc, qa_gain, kva_gain, w_uq, w_ukv, q_gain, k_gain, rope):
    hh, dn, dr, dv = MLA_HEADS, MLA_NOPE, MLA_ROPE, MLA_V
    dq = dn + dr
    qs = dq ** -0.5 * LOG2E
    padh = LANE - dq
    w_q = jnp.pad(w_uq.reshape(MLA_Q_LORA, hh, dq), ((0, 0), (0, 0), (0, padh)))
    w_kv = w_ukv.reshape(MLA_KV_LORA, hh, dn + dv)
    w_k = jnp.pad(w_kv[..., :dn], ((0, 0), (0, 0), (0, LANE - dn)))
    w_all = jnp.concatenate([w_k.reshape(MLA_KV_LORA, hh * LANE),
                             w_kv[..., dn:].reshape(MLA_KV_LORA, hh * dv)], axis=1).astype(BF16)
    w_q = w_q.reshape(MLA_Q_LORA, hh * LANE).astype(BF16)
    gq = jnp.pad(q_gain.astype(F32), (0, padh))
    gk = jnp.pad(k_gain.astype(F32), (0, padh))
    off = MLA_OFF

    def project(t, tabs):
        lora = t[:, off:off + MLA_Q_LORA + MLA_KV_LORA + dr].astype(F32)
        cq = _rms(lora[:, :MLA_Q_LORA], qa_gain).astype(BF16)
        ckv = _rms(lora[:, MLA_Q_LORA:MLA_Q_LORA + MLA_KV_LORA], kva_gain).astype(BF16)
        k_rope = jnp.pad(lora[:, MLA_Q_LORA + MLA_KV_LORA:], ((0, 0), (dn, padh)))
        kv = matmul(ckv, w_all)
        q = slab_prep(matmul(cq, w_q), 0, hh, gq, tabs, dr // 4, seg=LANE, count=dq, scale=qs,
                      transposed=tabs is not None)
        k = slab_prep(kv, 0, hh, gk, tabs, dr // 4, add=k_rope, seg=LANE, count=dq)
        return q, k, kv

    qT, k, kv = project(p, rope)
    qc, kc, kvc = project(pc, None)
    vT = slab_prep(kv, hh, hh * dv // LANE, transposed=True)
    vc = kvc[:, hh * LANE:].astype(BF16)
    return qT, k, kc, vT, vc.T, qc, vc


def _merge(o):
    return jnp.transpose(o, (1, 0, 2)).reshape(o.shape[1], -1)


def _diff_post(o1, o2, lam, sub_gain, lambda_init):
    return _rms(o1 - lam * o2, sub_gain) * (1.0 - lambda_init)


def _moe(tokens_bf16, logits, b_router, layer, w1, w3, w2, sw1, sw3, sw2):
    n, d = tokens_bf16.shape
    assert n % MOE_BLOCK == 0
    eidx, gate = _route(logits[:, :N_EXPERTS], b_router)
    slot_tok, slot_of, blk_e, n_valid = _dispatch(eidx)
    xb = tokens_bf16[jnp.minimum(slot_tok, n - 1)]
    n_sh = n // MOE_BLOCK
    shared = moe_blocks(tokens_bf16, sw1, sw3, sw2, jnp.full((n_sh,), layer, jnp.int32),
                        jnp.full((1,), n_sh, jnp.int32))
    xb, shared = lax.optimization_barrier((xb, shared))
    yb = moe_blocks(xb, w1, w3, w2, blk_e + layer * N_EXPERTS, n_valid)
    y0, y1 = lax.optimization_barrier((yb[slot_of[:, 0]], yb[slot_of[:, 1]]))
    routed = y0.astype(F32) * gate[:, 0:1] + y1.astype(F32) * gate[:, 1:2]
    return routed + shared.astype(F32)


def kernel(x, c, ctx, c_ctx, w_ada, b_ada, g_attn, g_ffn, w_in, w_out, na_q_gain, na_k_gain, na_rpb, wa_q_gain, wa_k_gain, wa_sink, diff_q_gain, diff_k_gain, diff_lq1, diff_lk1, diff_lq2, diff_lk2, diff_sub_gain, mla_qa_gain, mla_kva_gain, mla_w_uq, mla_w_ukv, mla_q_gain, mla_k_gain, w_router, b_router, moe_w1, moe_w3, moe_w2, sh_w1, sh_w3, sh_w2):
    b, s, d = x.shape
    assert b == 1
    n_ctx = ctx.shape[1]
    depth = w_ada.shape[0]
    xl = x[0]
    xc = ctx[0]
    rope_head = _rope_slab_tables(s, HEAD_DIM, 0, 2)
    rope_mla = _rope_slab_tables(s, MLA_ROPE, MLA_NOPE, 1)

    cond8 = jnp.zeros((8, d), F32).at[0].set(c[0]).at[1].set(c_ctx)
    mod = modvec(cond8, w_ada, b_ada)
    tn = INPROJ_TN
    in_pad = _round_up(IN_COLS, tn)
    wr = jnp.pad(w_router, ((0, 0), (0, LANE - N_EXPERTS))).astype(BF16)
    inv_sqrt_d = HEAD_DIM ** -0.5
    moe_w = [t.astype(BF16).reshape((depth * N_EXPERTS,) + t.shape[2:])
             for t in (moe_w1, moe_w3, moe_w2)]
    sh_w = [t.astype(BF16) for t in (sh_w1, sh_w3, sh_w2)]

    for l in range(depth):
        with_ctx = l < depth - 1
        m_lat = mod[l, 0].reshape(6, 1, d)
        m_ctx = mod[l, 1].reshape(6, 1, d)
        w_in_l = jnp.pad(w_in[l], ((0, 0), (0, in_pad - IN_COLS))).astype(BF16)
        w_out_l = w_out[l].astype(BF16)
        g_a = g_attn[l][None]
        g_f = g_ffn[l][None]

        p = inproj(xl, g_a, 1.0 + m_lat[1], m_lat[0], w_in_l, tn)
        pc = inproj(xc, g_a, 1.0 + m_ctx[1], m_ctx[0], w_in_l, tn)

        w = NA_HEADS * HEAD_DIM
        qT, k, qc, kc = _qk_prep(p, pc, NA_OFF, w, na_q_gain[l], na_k_gain[l], None)
        vT = slab_prep(p, (NA_OFF + 2 * w) // LANE, w // LANE, transposed=True)
        vc = pc[:, NA_OFF + 2 * w:WA_OFF]
        u_na = _score_bound(HEAD_DIM, na_q_gain[l], na_k_gain[l], inv_sqrt_d)
        oT_na = na_attn(qT, k, vT, kc, vc.T, na_rpb[l], u_na)
        oc_na = (_merge(ctx_attn(_ctx_heads(qc, NA_HEADS), _ctx_heads(kc, NA_HEADS),
                                 _ctx_heads(vc, NA_HEADS))) if with_ctx else None)

        w, wk = WA_HEADS * HEAD_DIM, WA_KV_HEADS * HEAD_DIM
        qT, k, qc, kc = _qk_prep(p, pc, WA_OFF, w, wa_q_gain[l], wa_k_gain[l], rope_head, wk)
        vT = slab_prep(p, (WA_OFF + w + wk) // LANE, wk // LANE, transposed=True)
        vc = pc[:, WA_OFF + w + wk:DIFF_OFF]
        u_wa = _score_bound(HEAD_DIM, wa_q_gain[l], wa_k_gain[l], inv_sqrt_d)
        sink2 = wa_sink[l].astype(F32) * LOG2E
        oT_wa = wa_attn(qT, k, vT, kc, vc.T, sink2, u_wa)
        oc_wa = (_merge(ctx_attn(_ctx_heads(qc, WA_HEADS), _ctx_heads(kc, WA_KV_HEADS),
                                 _ctx_heads(vc, WA_KV_HEADS), sink2)) if with_ctx else None)

        lambda_init = 0.8 - 0.6 * math.exp(-0.3 * l)
        lam = (jnp.exp(jnp.sum(diff_lq1[l].astype(F32) * diff_lk1[l].astype(F32)))
               - jnp.exp(jnp.sum(diff_lq2[l].astype(F32) * diff_lk2[l].astype(F32))) + lambda_init)
        w = DIFF_HEADS * 2 * DIFF_DIM
        qT, k, qc, kc = _qk_prep(p, pc, DIFF_OFF, w, diff_q_gain[l], diff_k_gain[l], rope_head)
        vT = slab_prep(p, (DIFF_OFF + 2 * w) // LANE, w // LANE, transposed=True)
        vc = pc[:, DIFF_OFF + 2 * w:MLA_OFF]
        u_df = _score_bound(DIFF_DIM, diff_q_gain[l], diff_k_gain[l], DIFF_DIM ** -0.5)
        oT_df = flash(qT, k, kc, vT, vc.T, u_df, 2).reshape(DIFF_HEADS, 2, DIFF_V_DIM, s)
        oc_df = None
        if with_ctx:
            oc = ctx_attn(_ctx_heads(qc, 2 * DIFF_HEADS), _ctx_heads(kc, 2 * DIFF_HEADS),
                          _ctx_heads(vc, DIFF_HEADS)).reshape(DIFF_HEADS, 2, n_ctx, DIFF_V_DIM)
            oc_df = _diff_post(jnp.transpose(oc[:, 0], (1, 0, 2)), jnp.transpose(oc[:, 1], (1, 0, 2)),
                               lam, diff_sub_gain[l], lambda_init).reshape(n_ctx, -1)

        qT, k, kc, vT, vcT, qc, vc = _mla_prep(
            p, pc, mla_qa_gain[l], mla_kva_gain[l], mla_w_uq[l], mla_w_ukv[l], mla_q_gain[l],
            mla_k_gain[l], rope_mla)
        u_ml = _score_bound(MLA_NOPE + MLA_ROPE, mla_q_gain[l], mla_k_gain[l],
                            (MLA_NOPE + MLA_ROPE) ** -0.5)
        oT_ml = flash(qT, k, kc, vT, vcT, u_ml, 1)
        oc_ml = (_merge(ctx_attn(_ctx_heads(qc, MLA_HEADS), _ctx_heads(kc, MLA_HEADS),
                                 _ctx_heads(vc, MLA_HEADS))) if with_ctx else None)

        xl, hf, lg = outproj_t(oT_na, oT_wa, oT_df, oT_ml,
                               diff_sub_gain[l].astype(F32) * (1.0 - lambda_init), lam,
                               xl, w_out_l, m_lat[2], g_f, 1.0 + m_lat[4], m_lat[3], wr)
        if with_ctx:
            oc_cat = jnp.concatenate([oc_na, oc_wa, oc_df, oc_ml], axis=-1).astype(BF16)
            xc, hfc, lgc = outproj(oc_cat, xc, w_out_l, m_ctx[2], g_f, 1.0 + m_ctx[4], m_ctx[3], wr)
            tokens = jnp.concatenate([hfc, hf], axis=0)
            logits = jnp.concatenate([lgc, lg], axis=0)
        else:
            tokens, logits = hf, lg
        y = _moe(tokens, logits, b_router, l, *moe_w, *sh_w)
        if with_ctx:
            xc = xc + m_ctx[5] * y[:n_ctx]
            y = y[n_ctx:]
        xl = xl + m_lat[5] * y
    return xl[None]
```

```python
import functools
import math

import jax
import jax.numpy as jnp
import numpy as np
from jax import lax
from jax.experimental import pallas as pl
from jax.experimental.pallas import tpu as pltpu

F32 = jnp.float32
BF16 = jnp.bfloat16

GRID_W = 64
HEAD_DIM = 64
ROPE_THETA = 10000.0
NORM_EPS = 1e-6
NEG_INF = -1e30
WINDOW = 128
NA_HEADS = 8
NA_KH = 8
NA_KW = 16
WA_HEADS = 8
WA_KV_HEADS = 2
DIFF_HEADS = 4
DIFF_DIM = 64
DIFF_V_DIM = 128
MLA_HEADS = 8
MLA_NOPE = 64
MLA_ROPE = 32
MLA_V = 64
MLA_Q_LORA = 384
MLA_KV_LORA = 128
N_EXPERTS = 16
N_EXPERT_GROUPS = 4
TOP_K = 2
ROUTED_SCALE = 1.0
MOE_BLOCK = 256

NA_COLS = 3 * NA_HEADS * HEAD_DIM
WA_COLS = (WA_HEADS + 2 * WA_KV_HEADS) * HEAD_DIM
DIFF_COLS = 3 * DIFF_HEADS * 2 * DIFF_DIM
MLA_COLS = MLA_Q_LORA + MLA_KV_LORA + MLA_ROPE
NA_OFF = 0
WA_OFF = NA_OFF + NA_COLS
DIFF_OFF = WA_OFF + WA_COLS
MLA_OFF = DIFF_OFF + DIFF_COLS
IN_COLS = MLA_OFF + MLA_COLS

LANE = 128
INPROJ_TN = 768
LOG2E = math.log2(math.e)
VMEM_LIMIT = 48 * 1024 * 1024


def _round_up(n, m):
    return (n + m - 1) // m * m


def _params(sem):
    return pltpu.CompilerParams(dimension_semantics=sem, vmem_limit_bytes=VMEM_LIMIT)


def _sigmoid(x):
    return 1.0 / (1.0 + jnp.exp(-x))


def _modvec_kernel(c_ref, w_ref, b_ref, o_ref):
    a = c_ref[...]
    a = a * _sigmoid(a)
    o_ref[0] = jnp.dot(a, w_ref[0], preferred_element_type=F32,
                       precision=lax.Precision.HIGHEST) + b_ref[0]


def modvec(cond8, w_ada, b_ada):
    depth, d, n = w_ada.shape
    tn = 1024
    return pl.pallas_call(
        _modvec_kernel,
        grid=(depth, n // tn),
        in_specs=[pl.BlockSpec((8, d), lambda l, j: (0, 0)),
                  pl.BlockSpec((1, d, tn), lambda l, j: (l, 0, j)),
                  pl.BlockSpec((1, 1, tn), lambda l, j: (l, 0, j))],
        out_specs=pl.BlockSpec((1, 8, tn), lambda l, j: (l, 0, j)),
        out_shape=jax.ShapeDtypeStruct((depth, 8, n), F32),
        compiler_params=_params(("parallel", "parallel")),
        name="modvec",
    )(cond8, w_ada, b_ada.reshape(depth, 1, n))


def _inproj_kernel(x_ref, g_ref, sc_ref, sh_ref, w_ref, o_ref, h_sc):
    @pl.when(pl.program_id(1) == 0)
    def _():
        x = x_ref[...]
        ms = jnp.mean(x * x, axis=-1, keepdims=True)
        y = x * lax.rsqrt(ms + NORM_EPS) * g_ref[...]
        h_sc[...] = (y * sc_ref[...] + sh_ref[...]).astype(BF16)

    o_ref[...] = jnp.dot(h_sc[...], w_ref[...], preferred_element_type=F32).astype(o_ref.dtype)


def inproj(x, g, sc1, sh, w, tn):
    r, d = x.shape
    n = w.shape[1]
    tm = min(r, 1024)
    vec = pl.BlockSpec((1, d), lambda i, j: (0, 0))
    return pl.pallas_call(
        _inproj_kernel,
        grid=(r // tm, n // tn),
        in_specs=[pl.BlockSpec((tm, d), lambda i, j: (i, 0)), vec, vec, vec,
                  pl.BlockSpec((d, tn), lambda i, j: (0, j))],
        out_specs=pl.BlockSpec((tm, tn), lambda i, j: (i, j)),
        out_shape=jax.ShapeDtypeStruct((r, n), BF16),
        scratch_shapes=[pltpu.VMEM((tm, d), BF16)],
        compiler_params=_params(("parallel", "arbitrary")),
        name="inproj",
    )(x, g, sc1, sh, w)


def _mm_kernel(x_ref, w_ref, o_ref):
    o_ref[...] = jnp.dot(x_ref[...], w_ref[...], preferred_element_type=F32)


def matmul(x, w):
    r, k = x.shape
    n = w.shape[1]
    tm = min(r, 1024)
    return pl.pallas_call(
        _mm_kernel,
        grid=(r // tm,),
        in_specs=[pl.BlockSpec((tm, k), lambda i: (i, 0)),
                  pl.BlockSpec((k, n), lambda i: (0, 0))],
        out_specs=pl.BlockSpec((tm, n), lambda i: (i, 0)),
        out_shape=jax.ShapeDtypeStruct((r, n), F32),
        compiler_params=_params(("parallel",)),
        name="matmul",
    )(x, w)


FLASH_SAFE_SHIFT = 60.0
FLASH_BQ = 1024
FLASH_CHAIN_Q = 512
FLASH_SCORE_BYTES = 17 * 1024 * 1024


def _flash_kernel(qT_ref, k_ref, kc_ref, vT_ref, vcT_ref, u_ref, o_ref, acc_sc, *, n_comp, n_split,
                  bk, n_kb, online):
    bq = qT_ref.shape[1] // n_split
    dv = vT_ref.shape[0]
    comp_rows = LANE // n_comp
    chains = [(c, h) for c in range(n_comp) for h in range(n_split)]
    u = u_ref[0:1, 0:1]
    row = lax.broadcasted_iota(jnp.int32, (LANE, bq), 0)
    shift_rows = jnp.where(row == 0, -u, 0.0).astype(BF16)
    q_ops = []
    for c, h in chains:
        qT = qT_ref[:, h * bq:(h + 1) * bq]
        if n_comp > 1:
            qT = jnp.where((row >= c * comp_rows) & (row < (c + 1) * comp_rows), qT,
                           jnp.zeros_like(qT))
        q_ops.append(jnp.concatenate([qT, shift_rows], axis=0))
    for i in range(len(chains)):
        acc_sc[i] = jnp.zeros(acc_sc.shape[1:], F32)

    def step(kb, vb, carry):
        n = kb.shape[0]
        ones_col = jnp.where(lax.broadcasted_iota(jnp.int32, (n, LANE), 1) == 0, 1.0, 0.0)
        ka = jnp.concatenate([kb, ones_col.astype(BF16)], axis=1)
        scores = [jnp.dot(ka, q, preferred_element_type=F32) for q in q_ops]
        out = []
        for i, sT in enumerate(scores):
            m, l = carry[i]
            if online:
                m_new = jnp.maximum(m, jnp.max(sT, axis=0, keepdims=True))
                alpha = jnp.exp2(m - m_new)
                pT = jnp.exp2(sT - m_new)
                acc_sc[i] = alpha * acc_sc[i] + jnp.dot(vb, pT.astype(BF16),
                                                        preferred_element_type=F32)
                out.append((m_new, alpha * l + jnp.sum(pT, axis=0, keepdims=True)))
            else:
                pT = jnp.exp2(sT)
                acc_sc[i] += jnp.dot(vb, pT.astype(BF16), preferred_element_type=F32)
                out.append((m, l + jnp.sum(pT, axis=0, keepdims=True)))
        return tuple(out)

    def body(j, carry):
        off = pl.multiple_of(j * bk, bk)
        return step(k_ref[pl.ds(off, bk), :], vT_ref[:, pl.ds(off, bk)], carry)

    init = tuple((jnp.full((1, bq), NEG_INF, F32), jnp.zeros((1, bq), F32)) for _ in chains)
    fin = lax.fori_loop(0, n_kb, body, step(kc_ref[...], vcT_ref[...], init), unroll=2)
    for i, (c, h) in enumerate(chains):
        o_ref[c * dv:(c + 1) * dv, h * bq:(h + 1) * bq] = acc_sc[i] * (1.0 / fin[i][1])


def _pick_bk(n, n_chains, bq):
    for bk in (4096, 2048, 1024, 512, 256, 128):
        if n % bk == 0 and n_chains * bk * bq * 4 <= FLASH_SCORE_BYTES:
            return bk
    raise ValueError(f"key count {n} must be a multiple of {LANE}")


def _flash_call(qT, k, kc, vT, vcT, u8, n_comp, online):
    hw, s = qT.shape
    n_heads = hw // LANE
    dv = vT.shape[0] // n_heads
    l = kc.shape[0]
    bq = min(FLASH_BQ, s)
    n_split = max(1, bq // FLASH_CHAIN_Q)
    bk = _pick_bk(s, n_comp * n_split, bq // n_split)
    kern = functools.partial(_flash_kernel, n_comp=n_comp, n_split=n_split, bk=bk, n_kb=s // bk,
                             online=online)
    return pl.pallas_call(
        kern,
        grid=(n_heads, s // bq),
        in_specs=[pl.BlockSpec((LANE, bq), lambda h, i: (h, i)),
                  pl.BlockSpec((s, LANE), lambda h, i: (0, h)),
                  pl.BlockSpec((l, LANE), lambda h, i: (0, h)),
                  pl.BlockSpec((dv, s), lambda h, i: (h, 0)),
                  pl.BlockSpec((dv, l), lambda h, i: (h, 0)),
                  pl.BlockSpec(u8.shape, lambda h, i: (0, 0))],
        out_specs=pl.BlockSpec((n_comp * dv, bq), lambda h, i: (h, i)),
        out_shape=jax.ShapeDtypeStruct((n_heads * n_comp * dv, s), F32),
        scratch_shapes=[pltpu.VMEM((n_comp * n_split, dv, bq // n_split), F32)],
        compiler_params=_params(("parallel", "arbitrary")),
        name="flash_online" if online else "flash",
    )(qT, k, kc, vT, vcT, u8)


def flash(qT, k, kc, vT, vcT, u, n_comp):
    u8 = jnp.full((8, LANE), u, F32)
    return lax.cond(u <= FLASH_SAFE_SHIFT,
                    lambda *a: _flash_call(*a, n_comp, False),
                    lambda *a: _flash_call(*a, n_comp, True),
                    qT, k, kc, vT, vcT, u8)


def _score_bound(d, q_gain, k_gain, scale):
    return (d * scale * LOG2E * 1.02 * jnp.max(jnp.abs(q_gain.astype(F32)))
            * jnp.max(jnp.abs(k_gain.astype(F32))) + 0.01)


def _ctx_kernel(q_ref, k_ref, v_ref, sink_ref, o_ref, *, use_sink):
    s = lax.dot_general(q_ref[0], k_ref[0], (((1,), (1,)), ((), ())),
                        preferred_element_type=F32)
    m = jnp.max(s, axis=-1, keepdims=True)
    if use_sink:
        sk = sink_ref[0, 0:1, 0:1]
        m = jnp.maximum(m, sk)
    p = jnp.exp2(s - m)
    l = jnp.sum(p, axis=-1, keepdims=True)
    if use_sink:
        l = l + jnp.exp2(sk - m)
    o = jnp.dot(p.astype(BF16), v_ref[0], preferred_element_type=F32)
    o_ref[0] = o * (1.0 / l)


def ctx_attn(q, k, v, sink=None):
    h, l, dk = q.shape
    hk, hv, dv = k.shape[0], v.shape[0], v.shape[2]
    use_sink = sink is not None
    if sink is None:
        sink = jnp.zeros((h,), F32)
    sink3 = jnp.broadcast_to(sink.astype(F32)[:, None, None], (h, 8, LANE))
    kern = functools.partial(_ctx_kernel, use_sink=use_sink)
    return pl.pallas_call(
        kern,
        grid=(h,),
        in_specs=[pl.BlockSpec((1, l, dk), lambda i: (i, 0, 0)),
                  pl.BlockSpec((1, l, dk), lambda i: (i // (h // hk), 0, 0)),
                  pl.BlockSpec((1, l, dv), lambda i: (i // (h // hv), 0, 0)),
                  pl.BlockSpec((1, 8, LANE), lambda i: (i, 0, 0))],
        out_specs=pl.BlockSpec((1, l, dv), lambda i: (i, 0, 0)),
        out_shape=jax.ShapeDtypeStruct((h, l, dv), F32),
        compiler_params=_params(("parallel",)),
        name="ctx_attn",
    )(q, k, v, sink3)


LOCAL_BQ = 512


def _pad_rows(qT_h, slot, n_slots):
    z = jnp.zeros_like(qT_h)
    return jnp.concatenate([qT_h if s == slot else z for s in range(n_slots)], axis=0)


def _softmax_pv(s_list, v_list, extra, exact_max):
    if exact_max:
        m = functools.reduce(jnp.maximum, [jnp.max(s, axis=0, keepdims=True) for s in s_list])
        if extra is not None:
            m = jnp.maximum(m, extra)
            extra = extra - m
        s_list = [s - m for s in s_list]
    p_list = [jnp.exp2(s) for s in s_list]
    l = functools.reduce(jnp.add, [jnp.sum(p, axis=0, keepdims=True) for p in p_list])
    if extra is not None:
        l = l + jnp.exp2(extra)
    acc = functools.reduce(jnp.add, [jnp.dot(v, p.astype(BF16), preferred_element_type=F32)
                                     for v, p in zip(v_list, p_list)])
    return acc, l


def _wa_kernel(qT_ref, k_ref, vT_ref, kc_ref, vcT_ref, sh_ref, o_ref, *, seq, n_heads, n_kv,
               exact_max):
    i = pl.program_id(0)
    bq = qT_ref.shape[1]
    win = bq + 2 * WINDOW
    grp = n_heads // n_kv
    start = pl.multiple_of(jnp.clip(i * bq - WINDOW, 0, seq - win), WINDOW)
    kw = k_ref[pl.ds(start, win), :]
    vwT = vT_ref[:, pl.ds(start, win)]
    kc = kc_ref[...]
    vcT = vcT_ref[...]
    u = sh_ref[n_heads:n_heads + 1, 0:1]
    kpos = start + lax.broadcasted_iota(jnp.int32, (win, bq), 0)
    qpos = i * bq + lax.broadcasted_iota(jnp.int32, (win, bq), 1)
    mask_shift = jnp.where(jnp.abs(qpos - kpos) <= WINDOW, -u, NEG_INF)
    for h in range(n_heads):
        g = h // grp
        qTp = _pad_rows(qT_ref[h * HEAD_DIM:(h + 1) * HEAD_DIM, :], g, n_kv)
        s_loc = jnp.dot(kw, qTp, preferred_element_type=F32) + mask_shift
        s_ctx = jnp.dot(kc, qTp, preferred_element_type=F32) - u
        acc, l = _softmax_pv([s_loc, s_ctx], [vwT, vcT], sh_ref[h:h + 1, 0:1], exact_max)
        o_ref[h * HEAD_DIM:(h + 1) * HEAD_DIM, :] = (acc[g * HEAD_DIM:(g + 1) * HEAD_DIM]
                                                     * (1.0 / l))


def _wa_call(qT, k, vT, kc, vcT, sh, exact_max):
    hd, s = qT.shape
    kw = k.shape[1]
    l = kc.shape[0]
    bq = min(LOCAL_BQ, s)
    assert s % bq == 0 and s >= bq + 2 * WINDOW
    kern = functools.partial(_wa_kernel, seq=s, n_heads=hd // HEAD_DIM, n_kv=kw // HEAD_DIM,
                             exact_max=exact_max)
    full = lambda i: (0, 0)
    return pl.pallas_call(
        kern,
        grid=(s // bq,),
        in_specs=[pl.BlockSpec((hd, bq), lambda i: (0, i)),
                  pl.BlockSpec((s, kw), full), pl.BlockSpec((kw, s), full),
                  pl.BlockSpec((l, kw), full), pl.BlockSpec((kw, l), full),
                  pl.BlockSpec(sh.shape, full)],
        out_specs=pl.BlockSpec((hd, bq), lambda i: (0, i)),
        out_shape=jax.ShapeDtypeStruct((hd, s), F32),
        compiler_params=_params(("parallel",)),
        name="wa_attn_max" if exact_max else "wa_attn",
    )(qT, k, vT, kc, vcT, sh)


def wa_attn(qT, k, vT, kc, vcT, sink, u):
    n_heads = qT.shape[0] // HEAD_DIM
    rows = jnp.concatenate([sink.astype(F32) - u, jnp.reshape(u, (1,)).astype(F32),
                            jnp.zeros((2 * 8 - n_heads - 1,), F32)])
    sh = jnp.broadcast_to(rows[:, None], (rows.shape[0], LANE))
    return lax.cond(u <= FLASH_SAFE_SHIFT,
                    lambda *a: _wa_call(*a, False), lambda *a: _wa_call(*a, True),
                    qT, k, vT, kc, vcT, sh)


NA_KEY_BLOCKS = 4
NA_KEY_ROWS = 2 * NA_KH


def _na_kernel(qT_ref, *refs, exact_max):
    k_refs, v_refs = refs[:NA_KEY_BLOCKS], refs[NA_KEY_BLOCKS:2 * NA_KEY_BLOCKS]
    kc_ref, vcT_ref, bias_ref, sh_ref, o_ref = refs[2 * NA_KEY_BLOCKS:]
    kwin = jnp.concatenate([r[...] for r in k_refs], axis=0)
    vwinT = jnp.concatenate([r[...] for r in v_refs], axis=1)
    kc = kc_ref[...]
    vcT = vcT_ref[...]
    u = sh_ref[0:1, 0:1]
    for hh in range(2):
        rows = slice(hh * HEAD_DIM, (hh + 1) * HEAD_DIM)
        qTp = _pad_rows(qT_ref[rows, :], hh, 2)
        s_loc = jnp.dot(kwin, qTp, preferred_element_type=F32) + bias_ref[0, hh]
        s_ctx = jnp.dot(kc, qTp, preferred_element_type=F32) - u
        acc, l = _softmax_pv([s_loc, s_ctx], [vwinT, vcT], None, exact_max)
        o_ref[rows, :] = acc[rows] * (1.0 / l)


def na_bias_table(rpb, rows, shift):
    n_h = rpb.shape[0]
    a = jnp.arange(NA_KH)
    b = jnp.arange(NA_KEY_ROWS)
    c = jnp.arange(GRID_W)
    kc = jnp.arange(GRID_W)
    c0 = jnp.clip(c - NA_KW // 2, 0, GRID_W - NA_KW)
    col_ok = (kc[None, :] >= c0[:, None]) & (kc[None, :] < c0[:, None] + NA_KW)
    col_rel = jnp.clip(kc[None, :] - c[:, None] + (NA_KW - 1), 0, 2 * NA_KW - 2)
    row_rel = jnp.clip(b[None, :] - a[:, None] + NA_KH // 2 - 1, 0, 2 * NA_KH - 2)

    def edge_ok(r_base):
        r0 = jnp.clip(r_base + a - NA_KH // 2, 0, rows - NA_KH)
        key_row = r_base - NA_KH // 2 + b
        return (key_row[None, :] >= r0[:, None]) & (key_row[None, :] < r0[:, None] + NA_KH)

    inner_ok = (b[None, :] >= a[:, None]) & (b[None, :] < a[:, None] + NA_KH)
    row_ok = jnp.stack([edge_ok(0), inner_ok, edge_ok(rows - NA_KH)])
    e_c = jax.nn.one_hot(col_rel, 2 * NA_KW - 1, dtype=F32)
    e_r = jax.nn.one_hot(row_rel, 2 * NA_KH - 1, dtype=F32)
    hi = lax.Precision.HIGHEST
    t = jnp.einsum('hrx,ckx->hrck', rpb.astype(F32), e_c, precision=hi)
    t = jnp.einsum('abr,hrck->habck', e_r, t, precision=hi) * LOG2E - shift
    ok = row_ok[:, None, :, :, None, None] & col_ok[None, None, None, None]
    t = jnp.where(ok, t[None], NEG_INF)
    t = jnp.transpose(t, (0, 1, 3, 5, 2, 4))
    return t.reshape(3, n_h, NA_KEY_ROWS * GRID_W, NA_KH * GRID_W)


def _na_call(qT, k, vT, kc, vcT, bias, sh, exact_max):
    hd, s = qT.shape
    l = kc.shape[0]
    bq = NA_KH * GRID_W
    nb = s // bq
    pw = 2 * HEAD_DIM
    assert s % bq == 0 and bq == LOCAL_BQ
    var = lambda i: jnp.where(i == 0, 0, jnp.where(i == nb - 1, 2, 1))
    kb = bq // 2
    blk = lambda i, t: jnp.clip(2 * i - 1 + t, 0, 2 * nb - 1)
    kspecs = [pl.BlockSpec((kb, pw), lambda p, i, t=t: (blk(i, t), p))
              for t in range(NA_KEY_BLOCKS)]
    vspecs = [pl.BlockSpec((pw, kb), lambda p, i, t=t: (p, blk(i, t)))
              for t in range(NA_KEY_BLOCKS)]
    kern = functools.partial(_na_kernel, exact_max=exact_max)
    return pl.pallas_call(
        kern,
        grid=(hd // pw, nb),
        in_specs=[pl.BlockSpec((pw, bq), lambda p, i: (p, i)), *kspecs, *vspecs,
                  pl.BlockSpec((l, pw), lambda p, i: (0, p)),
                  pl.BlockSpec((pw, l), lambda p, i: (p, 0)),
                  pl.BlockSpec((1, 2, NA_KEY_ROWS * GRID_W, bq), lambda p, i: (var(i), p, 0, 0)),
                  pl.BlockSpec(sh.shape, lambda p, i: (0, 0))],
        out_specs=pl.BlockSpec((pw, bq), lambda p, i: (p, i)),
        out_shape=jax.ShapeDtypeStruct((hd, s), F32),
        compiler_params=_params(("parallel", "arbitrary")),
        name="na_attn_max" if exact_max else "na_attn",
    )(qT, *([k] * NA_KEY_BLOCKS), *([vT] * NA_KEY_BLOCKS), kc, vcT, bias, sh)


def na_attn(qT, k, vT, kc, vcT, rpb, u_qk):
    s = qT.shape[1]
    u = u_qk + jnp.maximum(jnp.max(rpb.astype(F32)) * LOG2E, 0.0)
    bias = na_bias_table(rpb, s // GRID_W, u)
    sh = jnp.full((8, LANE), u, F32)
    return lax.cond(u <= FLASH_SAFE_SHIFT,
                    lambda *a: _na_call(*a, False), lambda *a: _na_call(*a, True),
                    qT, k, vT, kc, vcT, bias, sh)


def _outproj_kernel(o_ref, x_ref, w_ref, gate_ref, g_ref, sc_ref, sh_ref, wr_ref,
                    xo_ref, hf_ref, lg_ref):
    acc = jnp.dot(o_ref[...], w_ref[...], preferred_element_type=F32)
    xn = x_ref[...] + gate_ref[...] * acc
    xo_ref[...] = xn
    ms = jnp.mean(xn * xn, axis=-1, keepdims=True)
    y = xn * lax.rsqrt(ms + NORM_EPS) * g_ref[...]
    hf = (y * sc_ref[...] + sh_ref[...]).astype(BF16)
    hf_ref[...] = hf
    lg_ref[...] = jnp.dot(hf, wr_ref[...], preferred_element_type=F32)


def _outproj_t_kernel(na_ref, wa_ref, df_ref, ml_ref, dfg_ref, lam_ref, x_ref, w_ref, gate_ref,
                      g_ref, sc_ref, sh_ref, wr_ref, xo_ref, hf_ref, lg_ref):
    lam = lam_ref[0:1, 0:1]
    parts = [na_ref[...], wa_ref[...]]
    for h in range(df_ref.shape[0]):
        dd = df_ref[h, 0] - lam * df_ref[h, 1]
        ms = jnp.mean(dd * dd, axis=0, keepdims=True)
        parts.append(dd * lax.rsqrt(ms + NORM_EPS) * dfg_ref[...])
    parts.append(ml_ref[...])
    o = jnp.concatenate([jnp.transpose(t).astype(BF16) for t in parts], axis=1)
    acc = jnp.dot(o, w_ref[...], preferred_element_type=F32)
    xn = x_ref[...] + gate_ref[...] * acc
    xo_ref[...] = xn
    ms = jnp.mean(xn * xn, axis=-1, keepdims=True)
    y = xn * lax.rsqrt(ms + NORM_EPS) * g_ref[...]
    hf = (y * sc_ref[...] + sh_ref[...]).astype(BF16)
    hf_ref[...] = hf
    lg_ref[...] = jnp.dot(hf, wr_ref[...], preferred_element_type=F32)


def outproj_t(oT_na, oT_wa, oT_df, oT_ml, df_gain, lam, x, w, gate, g, sc1, sh, wr):
    r, d = x.shape
    tm = min(r, 256)
    nh, _, dv2, _ = oT_df.shape
    wdt = oT_na.shape[0]
    dfg = jnp.broadcast_to(df_gain.astype(F32)[:, None], (dv2, tm))
    lam8 = jnp.full((8, LANE), lam, F32)
    vec = pl.BlockSpec((1, d), lambda i: (0, 0))
    row = lambda n: pl.BlockSpec((tm, n), lambda i: (i, 0))
    colblk = pl.BlockSpec((wdt, tm), lambda i: (0, i))
    const = lambda shape: pl.BlockSpec(shape, lambda i: tuple(0 for _ in shape))
    return pl.pallas_call(
        _outproj_t_kernel,
        grid=(r // tm,),
        in_specs=[colblk, colblk, pl.BlockSpec((nh, 2, dv2, tm), lambda i: (0, 0, 0, i)), colblk,
                  const((dv2, tm)), const((8, LANE)), row(d), const(w.shape), vec, vec, vec, vec,
                  const((d, LANE))],
        out_specs=[row(d), row(d), row(LANE)],
        out_shape=[jax.ShapeDtypeStruct((r, d), F32), jax.ShapeDtypeStruct((r, d), BF16),
                   jax.ShapeDtypeStruct((r, LANE), F32)],
        compiler_params=_params(("parallel",)),
        name="outproj_t",
    )(oT_na, oT_wa, oT_df, oT_ml, dfg, lam8, x, w, gate, g, sc1, sh, wr)


def outproj(o, x, w, gate, g, sc1, sh, wr):
    r, d = x.shape
    k = o.shape[1]
    tm = min(r, 256)
    vec = pl.BlockSpec((1, d), lambda i: (0, 0))
    row = lambda n: pl.BlockSpec((tm, n), lambda i: (i, 0))
    return pl.pallas_call(
        _outproj_kernel,
        grid=(r // tm,),
        in_specs=[row(k), row(d), pl.BlockSpec((k, d), lambda i: (0, 0)), vec, vec, vec, vec,
                  pl.BlockSpec((d, LANE), lambda i: (0, 0))],
        out_specs=[row(d), row(d), row(LANE)],
        out_shape=[jax.ShapeDtypeStruct((r, d), F32), jax.ShapeDtypeStruct((r, d), BF16),
                   jax.ShapeDtypeStruct((r, LANE), F32)],
        compiler_params=_params(("parallel",)),
        name="outproj",
    )(o, x, w, gate, g, sc1, sh, wr)


def _moe_kernel(be_ref, nv_ref, x_ref, w1_ref, w3_ref, w2_ref, o_ref):
    i = pl.program_id(0)

    @pl.when(i < nv_ref[0])
    def _():
        x = x_ref[...].astype(w1_ref.dtype)
        a = jnp.dot(x, w1_ref[0], preferred_element_type=F32)
        b = jnp.dot(x, w3_ref[0], preferred_element_type=F32)
        hmid = (a * _sigmoid(a) * b).astype(w2_ref.dtype)
        o_ref[...] = jnp.dot(hmid, w2_ref[0], preferred_element_type=F32).astype(o_ref.dtype)

    @pl.when(i >= nv_ref[0])
    def _():
        o_ref[...] = jnp.zeros(o_ref.shape, o_ref.dtype)


def moe_blocks(x, w1, w3, w2, blk_e, n_valid):
    n, d = x.shape
    f = w1.shape[2]
    n_blk = n // MOE_BLOCK
    grid_spec = pltpu.PrefetchScalarGridSpec(
        num_scalar_prefetch=2,
        grid=(n_blk,),
        in_specs=[pl.BlockSpec((MOE_BLOCK, d), lambda i, be, nv: (i, 0)),
                  pl.BlockSpec((1, d, f), lambda i, be, nv: (be[i], 0, 0),
                               pipeline_mode=pl.Buffered(1)),
                  pl.BlockSpec((1, d, f), lambda i, be, nv: (be[i], 0, 0),
                               pipeline_mode=pl.Buffered(1)),
                  pl.BlockSpec((1, f, d), lambda i, be, nv: (be[i], 0, 0),
                               pipeline_mode=pl.Buffered(1))],
        out_specs=pl.BlockSpec((MOE_BLOCK, d), lambda i, be, nv: (i, 0)),
    )
    return pl.pallas_call(
        _moe_kernel,
        grid_spec=grid_spec,
        out_shape=jax.ShapeDtypeStruct((n, d), BF16),
        compiler_params=_params(("arbitrary",)),
        name="moe_blocks",
    )(blk_e, n_valid, x, w1, w3, w2)


def _rms(x, g):
    y = x * lax.rsqrt(jnp.mean(x * x, axis=-1, keepdims=True) + NORM_EPS)
    return y * g


def _rope_tables(seq_len, rot_dim):
    n = rot_dim // 4
    t = np.arange(seq_len)
    row = (t // GRID_W).astype(np.float32)[:, None]
    col = (t % GRID_W).astype(np.float32)[:, None]
    inv = (np.float32(ROPE_THETA) ** (-np.arange(n, dtype=np.float32) / np.float32(n)))
    inv = inv.astype(np.float32)
    return tuple(f(a * inv).astype(np.float32) for a in (row, col) for f in (np.cos, np.sin))


def _heads_first(t):
    return jnp.transpose(t, (1, 0, 2))


def _route(logits, b_router):
    n = logits.shape[0]
    per = N_EXPERTS // N_EXPERT_GROUPS
    scores = jax.nn.sigmoid(logits.astype(F32))
    sel = scores + b_router.astype(F32)
    grp = sel.reshape(n, N_EXPERT_GROUPS, per)
    gscore = None
    for a in range(per):
        for bb in range(a + 1, per):
            pair = grp[..., a] + grp[..., bb]
            gscore = pair if gscore is None else jnp.maximum(gscore, pair)
    gidx = jnp.argmax(gscore, axis=-1)
    eids = jnp.arange(N_EXPERTS)
    masked = jnp.where((eids // per)[None, :] == gidx[:, None], sel, -jnp.inf)
    e1 = jnp.argmax(masked, axis=-1)
    e2 = jnp.argmax(jnp.where(eids[None, :] == e1[:, None], -jnp.inf, masked), axis=-1)
    eidx = jnp.stack([e1, e2], axis=1).astype(jnp.int32)
    wts = jnp.take_along_axis(scores, eidx, axis=1)
    wts = wts / jnp.sum(wts, axis=-1, keepdims=True) * ROUTED_SCALE
    return eidx, wts


def _prefix_counts(onehot):
    n, e = onehot.shape
    ch = onehot.astype(F32).reshape(n // LANE, LANE, e)
    tri = jnp.tril(jnp.ones((LANE, LANE), F32))
    within = jnp.einsum('ij,cjk->cik', tri, ch)
    tot = within[:, -1, :]
    base = jnp.cumsum(tot, axis=0) - tot
    return (within + base[:, None, :]).reshape(n, e)


def _dispatch(eidx):
    n = eidx.shape[0]
    n_assign = n * TOP_K
    assert n_assign % LANE == 0
    e_flat = eidx.reshape(-1)
    tok = jnp.repeat(jnp.arange(n, dtype=jnp.int32), TOP_K)
    onehot = e_flat[:, None] == jnp.arange(N_EXPERTS)[None, :]
    csum = _prefix_counts(onehot)
    counts = csum[-1].astype(jnp.int32)
    rank = jnp.sum(jnp.where(onehot, csum, 0.0), axis=1).astype(jnp.int32) - 1
    padded = (counts + MOE_BLOCK - 1) // MOE_BLOCK * MOE_BLOCK
    pend = jnp.cumsum(padded)
    pstart = pend - padded
    dest = jnp.sum(jnp.where(onehot, pstart[None, :], 0), axis=1) + rank
    n_slots = (n_assign + N_EXPERTS * (MOE_BLOCK - 1) + MOE_BLOCK - 1) // MOE_BLOCK * MOE_BLOCK
    n_blk = n_slots // MOE_BLOCK
    slot_tok = jnp.full((n_slots,), n, dtype=jnp.int32).at[dest].set(tok)
    slot_of = dest.reshape(n, TOP_K)
    blk_start = jnp.arange(n_blk, dtype=jnp.int32) * MOE_BLOCK
    blk_e = jnp.minimum(jnp.sum(blk_start[:, None] >= pend[None, :], axis=1), N_EXPERTS - 1)
    n_valid = (pend[-1] // MOE_BLOCK).reshape(1)
    return slot_tok, slot_of, blk_e.astype(jnp.int32), n_valid.astype(jnp.int32)


PREP_TM = 4096


def _slab_prep_kernel(*refs, seg, count, rot, scale, transposed, has_add, raw):
    refs = list(refs)
    o_ref = refs.pop()
    x = refs[0][...].astype(F32)
    if not raw:
        g_ref = refs[1]
        nxt = 2
        if has_add:
            x = x + refs[nxt][...]
            nxt += 1
        sq = x * x
        hi = sq.astype(BF16)
        lo = (sq - hi.astype(F32)).astype(BF16)
        r = lax.broadcasted_iota(jnp.int32, (LANE, LANE), 0) // seg
        c = lax.broadcasted_iota(jnp.int32, (LANE, LANE), 1) // seg
        blk = jnp.where(r == c, 1.0, 0.0).astype(BF16)
        ssum = (jnp.dot(hi, blk, preferred_element_type=F32)
                + jnp.dot(lo, blk, preferred_element_type=F32))
        x = x * lax.rsqrt(ssum * (1.0 / count) + NORM_EPS) * g_ref[...]
        if rot:
            cos, s_next, s_prev = refs[nxt][...], refs[nxt + 1][...], refs[nxt + 2][...]
            x = (x * cos + pltpu.roll(x, LANE - rot, 1) * s_next + pltpu.roll(x, rot, 1) * s_prev)
        x = x * scale
    o_ref[...] = (jnp.transpose(x) if transposed else x).astype(o_ref.dtype)


def slab_prep(x, col0, n_slabs, gain=None, rope=None, rot=0, add=None, seg=HEAD_DIM,
              count=HEAD_DIM, scale=1.0, transposed=False):
    r = x.shape[0]
    tm = min(PREP_TM, r)
    raw = gain is None
    ops = [x]
    specs = [pl.BlockSpec((tm, LANE), lambda i, j: (i, col0 + j))]
    rowspec = pl.BlockSpec((tm, LANE), lambda i, j: (i, 0))
    if not raw:
        ops.append(gain.astype(F32).reshape(1, LANE))
        specs.append(pl.BlockSpec((1, LANE), lambda i, j: (0, 0)))
        if add is not None:
            ops.append(add)
            specs.append(rowspec)
        if rope is not None:
            ops += list(rope)
            specs += [rowspec] * 3
    kern = functools.partial(_slab_prep_kernel, seg=seg, count=count,
                             rot=rot if rope is not None else 0, scale=scale,
                             transposed=transposed, has_add=add is not None, raw=raw)
    if transposed:
        out_spec = pl.BlockSpec((LANE, tm), lambda i, j: (j, i))
        out_shape = jax.ShapeDtypeStruct((n_slabs * LANE, r), BF16)
    else:
        out_spec = pl.BlockSpec((tm, LANE), lambda i, j: (i, j))
        out_shape = jax.ShapeDtypeStruct((r, n_slabs * LANE), BF16)
    return pl.pallas_call(
        kern,
        grid=(r // tm, n_slabs),
        in_specs=specs,
        out_specs=out_spec,
        out_shape=out_shape,
        compiler_params=_params(("parallel", "arbitrary")),
        name="slab_prep",
    )(*ops)


def _rope_slab_tables(seq_len, width, offset, reps):
    n = width // 4
    cr, sr, cc, sc = (jnp.asarray(t) for t in _rope_tables(seq_len, width))
    one = jnp.ones((seq_len, offset), F32)
    zero = jnp.zeros((seq_len, offset), F32)
    zq = jnp.zeros((seq_len, n), F32)
    rest = LANE // reps - offset - width
    pad1 = jnp.ones((seq_len, rest), F32)
    pad0 = jnp.zeros((seq_len, rest), F32)
    cos = jnp.concatenate([one, cr, cr, cc, cc, pad1] * reps, axis=1)
    s_next = jnp.concatenate([zero, -sr, zq, -sc, zq, pad0] * reps, axis=1)
    s_prev = jnp.concatenate([zero, zq, sr, zq, sc, pad0] * reps, axis=1)
    return cos, s_next, s_prev


def _gain2(g):
    return jnp.concatenate([g, g]).astype(F32)


def _ctx_heads(t, n_heads):
    return _heads_first(t.reshape(t.shape[0], n_heads, -1)).astype(BF16)


def _qk_prep(p, pc, off, w, q_gain, k_gain, rope, wk=None):
    wk = w if wk is None else wk
    qs = HEAD_DIM ** -0.5 * LOG2E
    gq, gk = _gain2(q_gain), _gain2(k_gain)
    c0 = off // LANE
    rot = HEAD_DIM // 4
    qT = slab_prep(p, c0, w // LANE, gq, rope, rot, scale=qs, transposed=True)
    k = slab_prep(p, c0 + w // LANE, wk // LANE, gk, rope, rot)
    qc = slab_prep(pc, c0, w // LANE, gq, scale=qs)
    kc = slab_prep(pc, c0 + w // LANE, wk // LANE, gk)
    return qT, k, qc, kc


def _mla_prep(p, pc, qa_gain, kva_gain, w_uq, w_ukv, q_gain, k_gain, rope):
    hh, dn, dr, dv = MLA_HEADS, MLA_NOPE, MLA_ROPE, MLA_V
    dq = dn + dr
    qs = dq ** -0.5 * LOG2E
    padh = LANE - dq
    w_q = jnp.pad(w_uq.reshape(MLA_Q_LORA, hh, dq), ((0, 0), (0, 0), (0, padh)))
    w_kv = w_ukv.reshape(MLA_KV_LORA, hh, dn + dv)
    w_k = jnp.pad(w_kv[..., :dn], ((0, 0), (0, 0), (0, LANE - dn)))
    w_all = jnp.concatenate([w_k.reshape(MLA_KV_LORA, hh * LANE),
                             w_kv[..., dn:].reshape(MLA_KV_LORA, hh * dv)], axis=1).astype(BF16)
    w_q = w_q.reshape(MLA_Q_LORA, hh * LANE).astype(BF16)
    gq = jnp.pad(q_gain.astype(F32), (0, padh))
    gk = jnp.pad(k_gain.astype(F32), (0, padh))
    off = MLA_OFF

    def project(t, tabs):
        lora = t[:, off:off + MLA_Q_LORA + MLA_KV_LORA + dr].astype(F32)
        cq = _rms(lora[:, :MLA_Q_LORA], qa_gain).astype(BF16)
        ckv = _rms(lora[:, MLA_Q_LORA:MLA_Q_LORA + MLA_KV_LORA], kva_gain).astype(BF16)
        k_rope = jnp.pad(lora[:, MLA_Q_LORA + MLA_KV_LORA:], ((0, 0), (dn, padh)))
        kv = matmul(ckv, w_all)
        q = slab_prep(matmul(cq, w_q), 0, hh, gq, tabs, dr // 4, seg=LANE, count=dq, scale=qs,
                      transposed=tabs is not None)
        k = slab_prep(kv, 0, hh, gk, tabs, dr // 4, add=k_rope, seg=LANE, count=dq)
        return q, k, kv

    qT, k, kv = project(p, rope)
    qc, kc, kvc = project(pc, None)
    vT = slab_prep(kv, hh, hh * dv // LANE, transposed=True)
    vc = kvc[:, hh * LANE:].astype(BF16)
    return qT, k, kc, vT, vc.T, qc, vc


def _merge(o):
    return jnp.transpose(o, (1, 0, 2)).reshape(o.shape[1], -1)


def _diff_post(o1, o2, lam, sub_gain, lambda_init):
    return _rms(o1 - lam * o2, sub_gain) * (1.0 - lambda_init)


def _moe(tokens_bf16, logits, b_router, layer, w1, w3, w2, sw1, sw3, sw2):
    n, d = tokens_bf16.shape
    assert n % MOE_BLOCK == 0
    eidx, gate = _route(logits[:, :N_EXPERTS], b_router)
    slot_tok, slot_of, blk_e, n_valid = _dispatch(eidx)
    xb = tokens_bf16[jnp.minimum(slot_tok, n - 1)]
    n_sh = n // MOE_BLOCK
    shared = moe_blocks(tokens_bf16, sw1, sw3, sw2, jnp.full((n_sh,), layer, jnp.int32),
                        jnp.full((1,), n_sh, jnp.int32))
    xb, shared = lax.optimization_barrier((xb, shared))
    yb = moe_blocks(xb, w1, w3, w2, blk_e + layer * N_EXPERTS, n_valid)
    y0, y1 = lax.optimization_barrier((yb[slot_of[:, 0]], yb[slot_of[:, 1]]))
    routed = y0.astype(F32) * gate[:, 0:1] + y1.astype(F32) * gate[:, 1:2]
    return routed + shared.astype(F32)


def kernel(x, c, ctx, c_ctx, w_ada, b_ada, g_attn, g_ffn, w_in, w_out, na_q_gain, na_k_gain, na_rpb, wa_q_gain, wa_k_gain, wa_sink, diff_q_gain, diff_k_gain, diff_lq1, diff_lk1, diff_lq2, diff_lk2, diff_sub_gain, mla_qa_gain, mla_kva_gain, mla_w_uq, mla_w_ukv, mla_q_gain, mla_k_gain, w_router, b_router, moe_w1, moe_w3, moe_w2, sh_w1, sh_w3, sh_w2):
    b, s, d = x.shape
    assert b == 1
    n_ctx = ctx.shape[1]
    depth = w_ada.shape[0]
    xl = x[0]
    xc = ctx[0]
    rope_head = _rope_slab_tables(s, HEAD_DIM, 0, 2)
    rope_mla = _rope_slab_tables(s, MLA_ROPE, MLA_NOPE, 1)

    cond8 = jnp.zeros((8, d), F32).at[0].set(c[0]).at[1].set(c_ctx)
    mod = modvec(cond8, w_ada, b_ada)
    tn = INPROJ_TN
    in_pad = _round_up(IN_COLS, tn)
    wr = jnp.pad(w_router, ((0, 0), (0, LANE - N_EXPERTS))).astype(BF16)
    inv_sqrt_d = HEAD_DIM ** -0.5
    moe_w = [t.reshape((depth * N_EXPERTS,) + t.shape[2:]) for t in (moe_w1, moe_w3, moe_w2)]
    sh_w = [sh_w1, sh_w3, sh_w2]

    for l in range(depth):
        with_ctx = l < depth - 1
        m_lat = mod[l, 0].reshape(6, 1, d)
        m_ctx = mod[l, 1].reshape(6, 1, d)
        w_in_l = jnp.pad(w_in[l], ((0, 0), (0, in_pad - IN_COLS))).astype(BF16)
        w_out_l = w_out[l].astype(BF16)
        g_a = g_attn[l][None]
        g_f = g_ffn[l][None]

        p = inproj(xl, g_a, 1.0 + m_lat[1], m_lat[0], w_in_l, tn)
        pc = inproj(xc, g_a, 1.0 + m_ctx[1], m_ctx[0], w_in_l, tn)

        w = NA_HEADS * HEAD_DIM
        qT, k, qc, kc = _qk_prep(p, pc, NA_OFF, w, na_q_gain[l], na_k_gain[l], None)
        vT = slab_prep(p, (NA_OFF + 2 * w) // LANE, w // LANE, transposed=True)
        vc = pc[:, NA_OFF + 2 * w:WA_OFF]
        u_na = _score_bound(HEAD_DIM, na_q_gain[l], na_k_gain[l], inv_sqrt_d)
        oT_na = na_attn(qT, k, vT, kc, vc.T, na_rpb[l], u_na)
        oc_na = (_merge(ctx_attn(_ctx_heads(qc, NA_HEADS), _ctx_heads(kc, NA_HEADS),
                                 _ctx_heads(vc, NA_HEADS))) if with_ctx else None)

        w, wk = WA_HEADS * HEAD_DIM, WA_KV_HEADS * HEAD_DIM
        qT, k, qc, kc = _qk_prep(p, pc, WA_OFF, w, wa_q_gain[l], wa_k_gain[l], rope_head, wk)
        vT = slab_prep(p, (WA_OFF + w + wk) // LANE, wk // LANE, transposed=True)
        vc = pc[:, WA_OFF + w + wk:DIFF_OFF]
        u_wa = _score_bound(HEAD_DIM, wa_q_gain[l], wa_k_gain[l], inv_sqrt_d)
        sink2 = wa_sink[l].astype(F32) * LOG2E
        oT_wa = wa_attn(qT, k, vT, kc, vc.T, sink2, u_wa)
        oc_wa = (_merge(ctx_attn(_ctx_heads(qc, WA_HEADS), _ctx_heads(kc, WA_KV_HEADS),
                                 _ctx_heads(vc, WA_KV_HEADS), sink2)) if with_ctx else None)

        lambda_init = 0.8 - 0.6 * math.exp(-0.3 * l)
        lam = (jnp.exp(jnp.sum(diff_lq1[l].astype(F32) * diff_lk1[l].astype(F32)))
               - jnp.exp(jnp.sum(diff_lq2[l].astype(F32) * diff_lk2[l].astype(F32))) + lambda_init)
        w = DIFF_HEADS * 2 * DIFF_DIM
        qT, k, qc, kc = _qk_prep(p, pc, DIFF_OFF, w, diff_q_gain[l], diff_k_gain[l], rope_head)
        vT = slab_prep(p, (DIFF_OFF + 2 * w) // LANE, w // LANE, transposed=True)
        vc = pc[:, DIFF_OFF + 2 * w:MLA_OFF]
        u_df = _score_bound(DIFF_DIM, diff_q_gain[l], diff_k_gain[l], DIFF_DIM ** -0.5)
        oT_df = flash(qT, k, kc, vT, vc.T, u_df, 2).reshape(DIFF_HEADS, 2, DIFF_V_DIM, s)
        oc_df = None
        if with_ctx:
            oc = ctx_attn(_ctx_heads(qc, 2 * DIFF_HEADS), _ctx_heads(kc, 2 * DIFF_HEADS),
                          _ctx_heads(vc, DIFF_HEADS)).reshape(DIFF_HEADS, 2, n_ctx, DIFF_V_DIM)
            oc_df = _diff_post(jnp.transpose(oc[:, 0], (1, 0, 2)), jnp.transpose(oc[:, 1], (1, 0, 2)),
                               lam, diff_sub_gain[l], lambda_init).reshape(n_ctx, -1)

        qT, k, kc, vT, vcT, qc, vc = _mla_prep(
            p, pc, mla_qa_gain[l], mla_kva_gain[l], mla_w_uq[l], mla_w_ukv[l], mla_q_gain[l],
            mla_k_gain[l], rope_mla)
        u_ml = _score_bound(MLA_NOPE + MLA_ROPE, mla_q_gain[l], mla_k_gain[l],
                            (MLA_NOPE + MLA_ROPE) ** -0.5)
        oT_ml = flash(qT, k, kc, vT, vcT, u_ml, 1)
        oc_ml = (_merge(ctx_attn(_ctx_heads(qc, MLA_HEADS), _ctx_heads(kc, MLA_HEADS),
                                 _ctx_heads(vc, MLA_HEADS))) if with_ctx else None)

        xl, hf, lg = outproj_t(oT_na, oT_wa, oT_df, oT_ml,
                               diff_sub_gain[l].astype(F32) * (1.0 - lambda_init), lam,
                               xl, w_out_l, m_lat[2], g_f, 1.0 + m_lat[4], m_lat[3], wr)
        if with_ctx:
            oc_cat = jnp.concatenate([oc_na, oc_wa, oc_df, oc_ml], axis=-1).astype(BF16)
            xc, hfc, lgc = outproj(oc_cat, xc, w_out_l, m_ctx[2], g_f, 1.0 + m_ctx[4], m_ctx[3], wr)
            tokens = jnp.concatenate([hfc, hf], axis=0)
            logits = jnp.concatenate([lgc, lg], axis=0)
        else:
            tokens, logits = hf, lg
        y = _moe(tokens, logits, b_router, l, *moe_w, *sh_w)
        if with_ctx:
            xc = xc + m_ctx[5] * y[:n_ctx]
            y = y[n_ctx:]
        xl = xl + m_lat[5] * y
    return xl[None]
```

```python
import functools
import math

import jax
import jax.numpy as jnp
import numpy as np
from jax import lax
from jax.experimental import pallas as pl
from jax.experimental.pallas import tpu as pltpu

F32 = jnp.float32
BF16 = jnp.bfloat16

GRID_W = 64
HEAD_DIM = 64
ROPE_THETA = 10000.0
NORM_EPS = 1e-6
NEG_INF = -1e30
WINDOW = 128
NA_HEADS = 8
NA_KH = 8
NA_KW = 16
WA_HEADS = 8
WA_KV_HEADS = 2
DIFF_HEADS = 4
DIFF_DIM = 64
DIFF_V_DIM = 128
MLA_HEADS = 8
MLA_NOPE = 64
MLA_ROPE = 32
MLA_V = 64
MLA_Q_LORA = 384
MLA_KV_LORA = 128
N_EXPERTS = 16
N_EXPERT_GROUPS = 4
TOP_K = 2
ROUTED_SCALE = 1.0
MOE_BLOCK = 256

NA_COLS = 3 * NA_HEADS * HEAD_DIM
WA_COLS = (WA_HEADS + 2 * WA_KV_HEADS) * HEAD_DIM
DIFF_COLS = 3 * DIFF_HEADS * 2 * DIFF_DIM
MLA_COLS = MLA_Q_LORA + MLA_KV_LORA + MLA_ROPE
NA_OFF = 0
WA_OFF = NA_OFF + NA_COLS
DIFF_OFF = WA_OFF + WA_COLS
MLA_OFF = DIFF_OFF + DIFF_COLS
IN_COLS = MLA_OFF + MLA_COLS

LANE = 128
INPROJ_TN = 768
LOG2E = math.log2(math.e)
VMEM_LIMIT = 48 * 1024 * 1024
MOE_VMEM_LIMIT = 56 * 1024 * 1024


def _round_up(n, m):
    return (n + m - 1) // m * m


def _params(sem):
    return pltpu.CompilerParams(dimension_semantics=sem, vmem_limit_bytes=VMEM_LIMIT)


def _sigmoid(x):
    return 1.0 / (1.0 + jnp.exp(-x))


def _modvec_kernel(c_ref, w_ref, b_ref, o_ref):
    a = c_ref[...]
    a = a * _sigmoid(a)
    o_ref[0] = jnp.dot(a, w_ref[0], preferred_element_type=F32,
                       precision=lax.Precision.HIGHEST) + b_ref[0]


def modvec(cond8, w_ada, b_ada):
    depth, d, n = w_ada.shape
    tn = 1024
    return pl.pallas_call(
        _modvec_kernel,
        grid=(depth, n // tn),
        in_specs=[pl.BlockSpec((8, d), lambda l, j: (0, 0)),
                  pl.BlockSpec((1, d, tn), lambda l, j: (l, 0, j)),
                  pl.BlockSpec((1, 1, tn), lambda l, j: (l, 0, j))],
        out_specs=pl.BlockSpec((1, 8, tn), lambda l, j: (l, 0, j)),
        out_shape=jax.ShapeDtypeStruct((depth, 8, n), F32),
        compiler_params=_params(("parallel", "parallel")),
        name="modvec",
    )(cond8, w_ada, b_ada.reshape(depth, 1, n))


def _inproj_kernel(x_ref, g_ref, sc_ref, sh_ref, w_ref, o_ref, h_sc):
    @pl.when(pl.program_id(1) == 0)
    def _():
        x = x_ref[...]
        ms = jnp.mean(x * x, axis=-1, keepdims=True)
        y = x * lax.rsqrt(ms + NORM_EPS) * g_ref[...]
        h_sc[...] = (y * sc_ref[...] + sh_ref[...]).astype(BF16)

    o_ref[...] = jnp.dot(h_sc[...], w_ref[...], preferred_element_type=F32).astype(o_ref.dtype)


def inproj(x, g, sc1, sh, w, tn):
    r, d = x.shape
    n = w.shape[1]
    tm = min(r, 1024)
    vec = pl.BlockSpec((1, d), lambda i, j: (0, 0))
    return pl.pallas_call(
        _inproj_kernel,
        grid=(r // tm, n // tn),
        in_specs=[pl.BlockSpec((tm, d), lambda i, j: (i, 0)), vec, vec, vec,
                  pl.BlockSpec((d, tn), lambda i, j: (0, j))],
        out_specs=pl.BlockSpec((tm, tn), lambda i, j: (i, j)),
        out_shape=jax.ShapeDtypeStruct((r, n), BF16),
        scratch_shapes=[pltpu.VMEM((tm, d), BF16)],
        compiler_params=_params(("parallel", "arbitrary")),
        name="inproj",
    )(x, g, sc1, sh, w)


def _mm_kernel(x_ref, w_ref, o_ref):
    o_ref[...] = jnp.dot(x_ref[...], w_ref[...], preferred_element_type=F32)


def matmul(x, w):
    r, k = x.shape
    n = w.shape[1]
    tm = min(r, 1024)
    return pl.pallas_call(
        _mm_kernel,
        grid=(r // tm,),
        in_specs=[pl.BlockSpec((tm, k), lambda i: (i, 0)),
                  pl.BlockSpec((k, n), lambda i: (0, 0))],
        out_specs=pl.BlockSpec((tm, n), lambda i: (i, 0)),
        out_shape=jax.ShapeDtypeStruct((r, n), F32),
        compiler_params=_params(("parallel",)),
        name="matmul",
    )(x, w)


FLASH_SAFE_SHIFT = 60.0
FLASH_BQ = 1024
FLASH_CHAIN_Q = 512
FLASH_SCORE_BYTES = 17 * 1024 * 1024


def _flash_kernel(qT_ref, k_ref, kc_ref, vT_ref, vcT_ref, u_ref, o_ref, acc_sc, *, n_comp, n_split,
                  bk, n_kb, online):
    bq = qT_ref.shape[1] // n_split
    dv = vT_ref.shape[0]
    comp_rows = LANE // n_comp
    chains = [(c, h) for c in range(n_comp) for h in range(n_split)]
    u = u_ref[0:1, 0:1]
    row = lax.broadcasted_iota(jnp.int32, (LANE, bq), 0)
    shift_rows = jnp.where(row == 0, -u, 0.0).astype(BF16)
    q_ops = []
    for c, h in chains:
        qT = qT_ref[:, h * bq:(h + 1) * bq]
        if n_comp > 1:
            qT = jnp.where((row >= c * comp_rows) & (row < (c + 1) * comp_rows), qT,
                           jnp.zeros_like(qT))
        q_ops.append(jnp.concatenate([qT, shift_rows], axis=0))
    for i in range(len(chains)):
        acc_sc[i] = jnp.zeros(acc_sc.shape[1:], F32)

    def step(kb, vb, carry):
        n = kb.shape[0]
        ones_col = jnp.where(lax.broadcasted_iota(jnp.int32, (n, LANE), 1) == 0, 1.0, 0.0)
        ka = jnp.concatenate([kb, ones_col.astype(BF16)], axis=1)
        scores = [jnp.dot(ka, q, preferred_element_type=F32) for q in q_ops]
        out = []
        for i, sT in enumerate(scores):
            m, l = carry[i]
            if online:
                m_new = jnp.maximum(m, jnp.max(sT, axis=0, keepdims=True))
                alpha = jnp.exp2(m - m_new)
                pT = jnp.exp2(sT - m_new)
                acc_sc[i] = alpha * acc_sc[i] + jnp.dot(vb, pT.astype(BF16),
                                                        preferred_element_type=F32)
                out.append((m_new, alpha * l + jnp.sum(pT, axis=0, keepdims=True)))
            else:
                pT = jnp.exp2(sT)
                acc_sc[i] += jnp.dot(vb, pT.astype(BF16), preferred_element_type=F32)
                out.append((m, l + jnp.sum(pT, axis=0, keepdims=True)))
        return tuple(out)

    def body(j, carry):
        off = pl.multiple_of(j * bk, bk)
        return step(k_ref[pl.ds(off, bk), :], vT_ref[:, pl.ds(off, bk)], carry)

    init = tuple((jnp.full((1, bq), NEG_INF, F32), jnp.zeros((1, bq), F32)) for _ in chains)
    fin = lax.fori_loop(0, n_kb, body, step(kc_ref[...], vcT_ref[...], init), unroll=2)
    for i, (c, h) in enumerate(chains):
        o_ref[c * dv:(c + 1) * dv, h * bq:(h + 1) * bq] = acc_sc[i] * (1.0 / fin[i][1])


def _pick_bk(n, n_chains, bq):
    for bk in (4096, 2048, 1024, 512, 256, 128):
        if n % bk == 0 and n_chains * bk * bq * 4 <= FLASH_SCORE_BYTES:
            return bk
    raise ValueError(f"key count {n} must be a multiple of {LANE}")


def _flash_call(qT, k, kc, vT, vcT, u8, n_comp, online):
    hw, s = qT.shape
    n_heads = hw // LANE
    dv = vT.shape[0] // n_heads
    l = kc.shape[0]
    bq = min(FLASH_BQ, s)
    n_split = max(1, bq // FLASH_CHAIN_Q)
    bk = _pick_bk(s, n_comp * n_split, bq // n_split)
    kern = functools.partial(_flash_kernel, n_comp=n_comp, n_split=n_split, bk=bk, n_kb=s // bk,
                             online=online)
    return pl.pallas_call(
        kern,
        grid=(n_heads, s // bq),
        in_specs=[pl.BlockSpec((LANE, bq), lambda h, i: (h, i)),
                  pl.BlockSpec((s, LANE), lambda h, i: (0, h)),
                  pl.BlockSpec((l, LANE), lambda h, i: (0, h)),
                  pl.BlockSpec((dv, s), lambda h, i: (h, 0)),
                  pl.BlockSpec((dv, l), lambda h, i: (h, 0)),
                  pl.BlockSpec(u8.shape, lambda h, i: (0, 0))],
        out_specs=pl.BlockSpec((n_comp * dv, bq), lambda h, i: (h, i)),
        out_shape=jax.ShapeDtypeStruct((n_heads * n_comp * dv, s), F32),
        scratch_shapes=[pltpu.VMEM((n_comp * n_split, dv, bq // n_split), F32)],
        compiler_params=_params(("parallel", "arbitrary")),
        name="flash_online" if online else "flash",
    )(qT, k, kc, vT, vcT, u8)


def flash(qT, k, kc, vT, vcT, u, n_comp):
    u8 = jnp.full((8, LANE), u, F32)
    return lax.cond(u <= FLASH_SAFE_SHIFT,
                    lambda *a: _flash_call(*a, n_comp, False),
                    lambda *a: _flash_call(*a, n_comp, True),
                    qT, k, kc, vT, vcT, u8)


def _score_bound(d, q_gain, k_gain, scale):
    return (d * scale * LOG2E * 1.02 * jnp.max(jnp.abs(q_gain.astype(F32)))
            * jnp.max(jnp.abs(k_gain.astype(F32))) + 0.01)


def _ctx_kernel(q_ref, k_ref, v_ref, sink_ref, o_ref, *, use_sink):
    s = lax.dot_general(q_ref[0], k_ref[0], (((1,), (1,)), ((), ())),
                        preferred_element_type=F32)
    m = jnp.max(s, axis=-1, keepdims=True)
    if use_sink:
        sk = sink_ref[0, 0:1, 0:1]
        m = jnp.maximum(m, sk)
    p = jnp.exp2(s - m)
    l = jnp.sum(p, axis=-1, keepdims=True)
    if use_sink:
        l = l + jnp.exp2(sk - m)
    o = jnp.dot(p.astype(BF16), v_ref[0], preferred_element_type=F32)
    o_ref[0] = o * (1.0 / l)


def ctx_attn(q, k, v, sink=None):
    h, l, dk = q.shape
    hk, hv, dv = k.shape[0], v.shape[0], v.shape[2]
    use_sink = sink is not None
    if sink is None:
        sink = jnp.zeros((h,), F32)
    sink3 = jnp.broadcast_to(sink.astype(F32)[:, None, None], (h, 8, LANE))
    kern = functools.partial(_ctx_kernel, use_sink=use_sink)
    return pl.pallas_call(
        kern,
        grid=(h,),
        in_specs=[pl.BlockSpec((1, l, dk), lambda i: (i, 0, 0)),
                  pl.BlockSpec((1, l, dk), lambda i: (i // (h // hk), 0, 0)),
                  pl.BlockSpec((1, l, dv), lambda i: (i // (h // hv), 0, 0)),
                  pl.BlockSpec((1, 8, LANE), lambda i: (i, 0, 0))],
        out_specs=pl.BlockSpec((1, l, dv), lambda i: (i, 0, 0)),
        out_shape=jax.ShapeDtypeStruct((h, l, dv), F32),
        compiler_params=_params(("parallel",)),
        name="ctx_attn",
    )(q, k, v, sink3)


LOCAL_BQ = 512


def _pad_rows(qT_h, slot, n_slots):
    z = jnp.zeros_like(qT_h)
    return jnp.concatenate([qT_h if s == slot else z for s in range(n_slots)], axis=0)


def _softmax_pv(s_list, v_list, extra, exact_max):
    if exact_max:
        m = functools.reduce(jnp.maximum, [jnp.max(s, axis=0, keepdims=True) for s in s_list])
        if extra is not None:
            m = jnp.maximum(m, extra)
            extra = extra - m
        s_list = [s - m for s in s_list]
    p_list = [jnp.exp2(s) for s in s_list]
    l = functools.reduce(jnp.add, [jnp.sum(p, axis=0, keepdims=True) for p in p_list])
    if extra is not None:
        l = l + jnp.exp2(extra)
    acc = functools.reduce(jnp.add, [jnp.dot(v, p.astype(BF16), preferred_element_type=F32)
                                     for v, p in zip(v_list, p_list)])
    return acc, l


def _wa_kernel(qT_ref, k_ref, vT_ref, kc_ref, vcT_ref, sh_ref, o_ref, *, seq, n_heads, n_kv,
               exact_max):
    i = pl.program_id(0)
    bq = qT_ref.shape[1]
    win = bq + 2 * WINDOW
    grp = n_heads // n_kv
    start = pl.multiple_of(jnp.clip(i * bq - WINDOW, 0, seq - win), WINDOW)
    kw = k_ref[pl.ds(start, win), :]
    vwT = vT_ref[:, pl.ds(start, win)]
    kc = kc_ref[...]
    vcT = vcT_ref[...]
    u = sh_ref[n_heads:n_heads + 1, 0:1]
    kpos = start + lax.broadcasted_iota(jnp.int32, (win, bq), 0)
    qpos = i * bq + lax.broadcasted_iota(jnp.int32, (win, bq), 1)
    mask_shift = jnp.where(jnp.abs(qpos - kpos) <= WINDOW, -u, NEG_INF)
    for h in range(n_heads):
        g = h // grp
        qTp = _pad_rows(qT_ref[h * HEAD_DIM:(h + 1) * HEAD_DIM, :], g, n_kv)
        s_loc = jnp.dot(kw, qTp, preferred_element_type=F32) + mask_shift
        s_ctx = jnp.dot(kc, qTp, preferred_element_type=F32) - u
        acc, l = _softmax_pv([s_loc, s_ctx], [vwT, vcT], sh_ref[h:h + 1, 0:1], exact_max)
        o_ref[h * HEAD_DIM:(h + 1) * HEAD_DIM, :] = (acc[g * HEAD_DIM:(g + 1) * HEAD_DIM]
                                                     * (1.0 / l))


def _wa_call(qT, k, vT, kc, vcT, sh, exact_max):
    hd, s = qT.shape
    kw = k.shape[1]
    l = kc.shape[0]
    bq = min(LOCAL_BQ, s)
    assert s % bq == 0 and s >= bq + 2 * WINDOW
    kern = functools.partial(_wa_kernel, seq=s, n_heads=hd // HEAD_DIM, n_kv=kw // HEAD_DIM,
                             exact_max=exact_max)
    full = lambda i: (0, 0)
    return pl.pallas_call(
        kern,
        grid=(s // bq,),
        in_specs=[pl.BlockSpec((hd, bq), lambda i: (0, i)),
                  pl.BlockSpec((s, kw), full), pl.BlockSpec((kw, s), full),
                  pl.BlockSpec((l, kw), full), pl.BlockSpec((kw, l), full),
                  pl.BlockSpec(sh.shape, full)],
        out_specs=pl.BlockSpec((hd, bq), lambda i: (0, i)),
        out_shape=jax.ShapeDtypeStruct((hd, s), F32),
        compiler_params=_params(("parallel",)),
        name="wa_attn_max" if exact_max else "wa_attn",
    )(qT, k, vT, kc, vcT, sh)


def wa_attn(qT, k, vT, kc, vcT, sink, u):
    n_heads = qT.shape[0] // HEAD_DIM
    rows = jnp.concatenate([sink.astype(F32) - u, jnp.reshape(u, (1,)).astype(F32),
                            jnp.zeros((2 * 8 - n_heads - 1,), F32)])
    sh = jnp.broadcast_to(rows[:, None], (rows.shape[0], LANE))
    return lax.cond(u <= FLASH_SAFE_SHIFT,
                    lambda *a: _wa_call(*a, False), lambda *a: _wa_call(*a, True),
                    qT, k, vT, kc, vcT, sh)


NA_KEY_BLOCKS = 4
NA_KEY_ROWS = 2 * NA_KH


def _na_kernel(qT_ref, *refs, exact_max):
    k_refs, v_refs = refs[:NA_KEY_BLOCKS], refs[NA_KEY_BLOCKS:2 * NA_KEY_BLOCKS]
    kc_ref, vcT_ref, bias_ref, sh_ref, o_ref = refs[2 * NA_KEY_BLOCKS:]
    kwin = jnp.concatenate([r[...] for r in k_refs], axis=0)
    vwinT = jnp.concatenate([r[...] for r in v_refs], axis=1)
    kc = kc_ref[...]
    vcT = vcT_ref[...]
    u = sh_ref[0:1, 0:1]
    for hh in range(2):
        rows = slice(hh * HEAD_DIM, (hh + 1) * HEAD_DIM)
        qTp = _pad_rows(qT_ref[rows, :], hh, 2)
        s_loc = jnp.dot(kwin, qTp, preferred_element_type=F32) + bias_ref[0, hh]
        s_ctx = jnp.dot(kc, qTp, preferred_element_type=F32) - u
        acc, l = _softmax_pv([s_loc, s_ctx], [vwinT, vcT], None, exact_max)
        o_ref[rows, :] = acc[rows] * (1.0 / l)


def na_bias_table(rpb, rows, shift):
    n_h = rpb.shape[0]
    a = jnp.arange(NA_KH)
    b = jnp.arange(NA_KEY_ROWS)
    c = jnp.arange(GRID_W)
    kc = jnp.arange(GRID_W)
    c0 = jnp.clip(c - NA_KW // 2, 0, GRID_W - NA_KW)
    col_ok = (kc[None, :] >= c0[:, None]) & (kc[None, :] < c0[:, None] + NA_KW)
    col_rel = jnp.clip(kc[None, :] - c[:, None] + (NA_KW - 1), 0, 2 * NA_KW - 2)
    row_rel = jnp.clip(b[None, :] - a[:, None] + NA_KH // 2 - 1, 0, 2 * NA_KH - 2)

    def edge_ok(r_base):
        r0 = jnp.clip(r_base + a - NA_KH // 2, 0, rows - NA_KH)
        key_row = r_base - NA_KH // 2 + b
        return (key_row[None, :] >= r0[:, None]) & (key_row[None, :] < r0[:, None] + NA_KH)

    inner_ok = (b[None, :] >= a[:, None]) & (b[None, :] < a[:, None] + NA_KH)
    row_ok = jnp.stack([edge_ok(0), inner_ok, edge_ok(rows - NA_KH)])
    e_c = jax.nn.one_hot(col_rel, 2 * NA_KW - 1, dtype=F32)
    e_r = jax.nn.one_hot(row_rel, 2 * NA_KH - 1, dtype=F32)
    hi = lax.Precision.HIGHEST
    t = jnp.einsum('hrx,ckx->hrck', rpb.astype(F32), e_c, precision=hi)
    t = jnp.einsum('abr,hrck->habck', e_r, t, precision=hi) * LOG2E - shift
    ok = row_ok[:, None, :, :, None, None] & col_ok[None, None, None, None]
    t = jnp.where(ok, t[None], NEG_INF)
    t = jnp.transpose(t, (0, 1, 3, 5, 2, 4))
    return t.reshape(3, n_h, NA_KEY_ROWS * GRID_W, NA_KH * GRID_W)


def _na_call(qT, k, vT, kc, vcT, bias, sh, exact_max):
    hd, s = qT.shape
    l = kc.shape[0]
    bq = NA_KH * GRID_W
    nb = s // bq
    pw = 2 * HEAD_DIM
    assert s % bq == 0 and bq == LOCAL_BQ
    var = lambda i: jnp.where(i == 0, 0, jnp.where(i == nb - 1, 2, 1))
    kb = bq // 2
    blk = lambda i, t: jnp.clip(2 * i - 1 + t, 0, 2 * nb - 1)
    kspecs = [pl.BlockSpec((kb, pw), lambda p, i, t=t: (blk(i, t), p))
              for t in range(NA_KEY_BLOCKS)]
    vspecs = [pl.BlockSpec((pw, kb), lambda p, i, t=t: (p, blk(i, t)))
              for t in range(NA_KEY_BLOCKS)]
    kern = functools.partial(_na_kernel, exact_max=exact_max)
    return pl.pallas_call(
        kern,
        grid=(hd // pw, nb),
        in_specs=[pl.BlockSpec((pw, bq), lambda p, i: (p, i)), *kspecs, *vspecs,
                  pl.BlockSpec((l, pw), lambda p, i: (0, p)),
                  pl.BlockSpec((pw, l), lambda p, i: (p, 0)),
                  pl.BlockSpec((1, 2, NA_KEY_ROWS * GRID_W, bq), lambda p, i: (var(i), p, 0, 0)),
                  pl.BlockSpec(sh.shape, lambda p, i: (0, 0))],
        out_specs=pl.BlockSpec((pw, bq), lambda p, i: (p, i)),
        out_shape=jax.ShapeDtypeStruct((hd, s), F32),
        compiler_params=_params(("parallel", "arbitrary")),
        name="na_attn_max" if exact_max else "na_attn",
    )(qT, *([k] * NA_KEY_BLOCKS), *([vT] * NA_KEY_BLOCKS), kc, vcT, bias, sh)


def na_attn(qT, k, vT, kc, vcT, rpb, u_qk):
    s = qT.shape[1]
    u = u_qk + jnp.maximum(jnp.max(rpb.astype(F32)) * LOG2E, 0.0)
    bias = na_bias_table(rpb, s // GRID_W, u)
    sh = jnp.full((8, LANE), u, F32)
    return lax.cond(u <= FLASH_SAFE_SHIFT,
                    lambda *a: _na_call(*a, False), lambda *a: _na_call(*a, True),
                    qT, k, vT, kc, vcT, bias, sh)


def _outproj_kernel(o_ref, x_ref, w_ref, gate_ref, g_ref, sc_ref, sh_ref, wr_ref,
                    xo_ref, hf_ref, lg_ref):
    acc = jnp.dot(o_ref[...], w_ref[...], preferred_element_type=F32)
    xn = x_ref[...] + gate_ref[...] * acc
    xo_ref[...] = xn
    ms = jnp.mean(xn * xn, axis=-1, keepdims=True)
    y = xn * lax.rsqrt(ms + NORM_EPS) * g_ref[...]
    hf = (y * sc_ref[...] + sh_ref[...]).astype(BF16)
    hf_ref[...] = hf
    lg_ref[...] = jnp.dot(hf, wr_ref[...], preferred_element_type=F32)


def _outproj_t_kernel(na_ref, wa_ref, df_ref, ml_ref, dfg_ref, lam_ref, x_ref, w_ref, gate_ref,
                      g_ref, sc_ref, sh_ref, wr_ref, xo_ref, hf_ref, lg_ref):
    lam = lam_ref[0:1, 0:1]
    parts = [na_ref[...], wa_ref[...]]
    for h in range(df_ref.shape[0]):
        dd = df_ref[h, 0] - lam * df_ref[h, 1]
        ms = jnp.mean(dd * dd, axis=0, keepdims=True)
        parts.append(dd * lax.rsqrt(ms + NORM_EPS) * dfg_ref[...])
    parts.append(ml_ref[...])
    o = jnp.concatenate([jnp.transpose(t).astype(BF16) for t in parts], axis=1)
    acc = jnp.dot(o, w_ref[...], preferred_element_type=F32)
    xn = x_ref[...] + gate_ref[...] * acc
    xo_ref[...] = xn
    ms = jnp.mean(xn * xn, axis=-1, keepdims=True)
    y = xn * lax.rsqrt(ms + NORM_EPS) * g_ref[...]
    hf = (y * sc_ref[...] + sh_ref[...]).astype(BF16)
    hf_ref[...] = hf
    lg_ref[...] = jnp.dot(hf, wr_ref[...], preferred_element_type=F32)


def outproj_t(oT_na, oT_wa, oT_df, oT_ml, df_gain, lam, x, w, gate, g, sc1, sh, wr):
    r, d = x.shape
    tm = min(r, 256)
    nh, _, dv2, _ = oT_df.shape
    wdt = oT_na.shape[0]
    dfg = jnp.broadcast_to(df_gain.astype(F32)[:, None], (dv2, tm))
    lam8 = jnp.full((8, LANE), lam, F32)
    vec = pl.BlockSpec((1, d), lambda i: (0, 0))
    row = lambda n: pl.BlockSpec((tm, n), lambda i: (i, 0))
    colblk = pl.BlockSpec((wdt, tm), lambda i: (0, i))
    const = lambda shape: pl.BlockSpec(shape, lambda i: tuple(0 for _ in shape))
    return pl.pallas_call(
        _outproj_t_kernel,
        grid=(r // tm,),
        in_specs=[colblk, colblk, pl.BlockSpec((nh, 2, dv2, tm), lambda i: (0, 0, 0, i)), colblk,
                  const((dv2, tm)), const((8, LANE)), row(d), const(w.shape), vec, vec, vec, vec,
                  const((d, LANE))],
        out_specs=[row(d), row(d), row(LANE)],
        out_shape=[jax.ShapeDtypeStruct((r, d), F32), jax.ShapeDtypeStruct((r, d), BF16),
                   jax.ShapeDtypeStruct((r, LANE), F32)],
        compiler_params=_params(("parallel",)),
        name="outproj_t",
    )(oT_na, oT_wa, oT_df, oT_ml, dfg, lam8, x, w, gate, g, sc1, sh, wr)


def outproj(o, x, w, gate, g, sc1, sh, wr):
    r, d = x.shape
    k = o.shape[1]
    tm = min(r, 256)
    vec = pl.BlockSpec((1, d), lambda i: (0, 0))
    row = lambda n: pl.BlockSpec((tm, n), lambda i: (i, 0))
    return pl.pallas_call(
        _outproj_kernel,
        grid=(r // tm,),
        in_specs=[row(k), row(d), pl.BlockSpec((k, d), lambda i: (0, 0)), vec, vec, vec, vec,
                  pl.BlockSpec((d, LANE), lambda i: (0, 0))],
        out_specs=[row(d), row(d), row(LANE)],
        out_shape=[jax.ShapeDtypeStruct((r, d), F32), jax.ShapeDtypeStruct((r, d), BF16),
                   jax.ShapeDtypeStruct((r, LANE), F32)],
        compiler_params=_params(("parallel",)),
        name="outproj",
    )(o, x, w, gate, g, sc1, sh, wr)


def _moe_kernel(be_ref, nv_ref, x_ref, w1_ref, w3_ref, w2_ref, o_ref):
    i = pl.program_id(0)

    @pl.when(i < nv_ref[0])
    def _():
        x = x_ref[...].astype(w1_ref.dtype)
        a = jnp.dot(x, w1_ref[0], preferred_element_type=F32)
        b = jnp.dot(x, w3_ref[0], preferred_element_type=F32)
        hmid = (a * _sigmoid(a) * b).astype(w2_ref.dtype)
        o_ref[...] = jnp.dot(hmid, w2_ref[0], preferred_element_type=F32).astype(o_ref.dtype)

    @pl.when(i >= nv_ref[0])
    def _():
        o_ref[...] = jnp.zeros(o_ref.shape, o_ref.dtype)


def moe_blocks(x, w1, w3, w2, blk_e, n_valid):
    n, d = x.shape
    f = w1.shape[2]
    n_blk = n // MOE_BLOCK
    grid_spec = pltpu.PrefetchScalarGridSpec(
        num_scalar_prefetch=2,
        grid=(n_blk,),
        in_specs=[pl.BlockSpec((MOE_BLOCK, d), lambda i, be, nv: (i, 0)),
                  pl.BlockSpec((1, d, f), lambda i, be, nv: (be[i], 0, 0)),
                  pl.BlockSpec((1, d, f), lambda i, be, nv: (be[i], 0, 0)),
                  pl.BlockSpec((1, f, d), lambda i, be, nv: (be[i], 0, 0),
                               pipeline_mode=pl.Buffered(1))],
        out_specs=pl.BlockSpec((MOE_BLOCK, d), lambda i, be, nv: (i, 0)),
    )
    return pl.pallas_call(
        _moe_kernel,
        grid_spec=grid_spec,
        out_shape=jax.ShapeDtypeStruct((n, d), BF16),
        compiler_params=pltpu.CompilerParams(dimension_semantics=("arbitrary",),
                                             vmem_limit_bytes=MOE_VMEM_LIMIT),
        name="moe_blocks",
    )(blk_e, n_valid, x, w1, w3, w2)


def _rms(x, g):
    y = x * lax.rsqrt(jnp.mean(x * x, axis=-1, keepdims=True) + NORM_EPS)
    return y * g


def _rope_tables(seq_len, rot_dim):
    n = rot_dim // 4
    t = np.arange(seq_len)
    row = (t // GRID_W).astype(np.float32)[:, None]
    col = (t % GRID_W).astype(np.float32)[:, None]
    inv = (np.float32(ROPE_THETA) ** (-np.arange(n, dtype=np.float32) / np.float32(n)))
    inv = inv.astype(np.float32)
    return tuple(f(a * inv).astype(np.float32) for a in (row, col) for f in (np.cos, np.sin))


def _heads_first(t):
    return jnp.transpose(t, (1, 0, 2))


def _route(logits, b_router):
    n = logits.shape[0]
    per = N_EXPERTS // N_EXPERT_GROUPS
    scores = jax.nn.sigmoid(logits.astype(F32))
    sel = scores + b_router.astype(F32)
    grp = sel.reshape(n, N_EXPERT_GROUPS, per)
    gscore = None
    for a in range(per):
        for bb in range(a + 1, per):
            pair = grp[..., a] + grp[..., bb]
            gscore = pair if gscore is None else jnp.maximum(gscore, pair)
    gidx = jnp.argmax(gscore, axis=-1)
    eids = jnp.arange(N_EXPERTS)
    masked = jnp.where((eids // per)[None, :] == gidx[:, None], sel, -jnp.inf)
    e1 = jnp.argmax(masked, axis=-1)
    e2 = jnp.argmax(jnp.where(eids[None, :] == e1[:, None], -jnp.inf, masked), axis=-1)
    eidx = jnp.stack([e1, e2], axis=1).astype(jnp.int32)
    wts = jnp.take_along_axis(scores, eidx, axis=1)
    wts = wts / jnp.sum(wts, axis=-1, keepdims=True) * ROUTED_SCALE
    return eidx, wts


def _prefix_counts(onehot):
    n, e = onehot.shape
    ch = onehot.astype(F32).reshape(n // LANE, LANE, e)
    tri = jnp.tril(jnp.ones((LANE, LANE), F32))
    within = jnp.einsum('ij,cjk->cik', tri, ch)
    tot = within[:, -1, :]
    base = jnp.cumsum(tot, axis=0) - tot
    return (within + base[:, None, :]).reshape(n, e)


def _dispatch(eidx):
    n = eidx.shape[0]
    n_assign = n * TOP_K
    assert n_assign % LANE == 0
    e_flat = eidx.reshape(-1)
    tok = jnp.repeat(jnp.arange(n, dtype=jnp.int32), TOP_K)
    onehot = e_flat[:, None] == jnp.arange(N_EXPERTS)[None, :]
    csum = _prefix_counts(onehot)
    counts = csum[-1].astype(jnp.int32)
    rank = jnp.sum(jnp.where(onehot, csum, 0.0), axis=1).astype(jnp.int32) - 1
    padded = (counts + MOE_BLOCK - 1) // MOE_BLOCK * MOE_BLOCK
    pend = jnp.cumsum(padded)
    pstart = pend - padded
    dest = jnp.sum(jnp.where(onehot, pstart[None, :], 0), axis=1) + rank
    n_slots = (n_assign + N_EXPERTS * (MOE_BLOCK - 1) + MOE_BLOCK - 1) // MOE_BLOCK * MOE_BLOCK
    n_blk = n_slots // MOE_BLOCK
    slot_tok = jnp.full((n_slots,), n, dtype=jnp.int32).at[dest].set(tok)
    slot_of = dest.reshape(n, TOP_K)
    blk_start = jnp.arange(n_blk, dtype=jnp.int32) * MOE_BLOCK
    blk_e = jnp.minimum(jnp.sum(blk_start[:, None] >= pend[None, :], axis=1), N_EXPERTS - 1)
    n_valid = (pend[-1] // MOE_BLOCK).reshape(1)
    return slot_tok, slot_of, blk_e.astype(jnp.int32), n_valid.astype(jnp.int32)


PREP_TM = 4096


def _slab_prep_kernel(*refs, seg, count, rot, scale, transposed, has_add, raw):
    refs = list(refs)
    o_ref = refs.pop()
    x = refs[0][...].astype(F32)
    if not raw:
        g_ref = refs[1]
        nxt = 2
        if has_add:
            x = x + refs[nxt][...]
            nxt += 1
        sq = x * x
        hi = sq.astype(BF16)
        lo = (sq - hi.astype(F32)).astype(BF16)
        r = lax.broadcasted_iota(jnp.int32, (LANE, LANE), 0) // seg
        c = lax.broadcasted_iota(jnp.int32, (LANE, LANE), 1) // seg
        blk = jnp.where(r == c, 1.0, 0.0).astype(BF16)
        ssum = (jnp.dot(hi, blk, preferred_element_type=F32)
                + jnp.dot(lo, blk, preferred_element_type=F32))
        x = x * lax.rsqrt(ssum * (1.0 / count) + NORM_EPS) * g_ref[...]
        if rot:
            cos, s_next, s_prev = refs[nxt][...], refs[nxt + 1][...], refs[nxt + 2][...]
            x = (x * cos + pltpu.roll(x, LANE - rot, 1) * s_next + pltpu.roll(x, rot, 1) * s_prev)
        x = x * scale
    o_ref[...] = (jnp.transpose(x) if transposed else x).astype(o_ref.dtype)


def slab_prep(x, col0, n_slabs, gain=None, rope=None, rot=0, add=None, seg=HEAD_DIM,
              count=HEAD_DIM, scale=1.0, transposed=False):
    r = x.shape[0]
    tm = min(PREP_TM, r)
    raw = gain is None
    ops = [x]
    specs = [pl.BlockSpec((tm, LANE), lambda i, j: (i, col0 + j))]
    rowspec = pl.BlockSpec((tm, LANE), lambda i, j: (i, 0))
    if not raw:
        ops.append(gain.astype(F32).reshape(1, LANE))
        specs.append(pl.BlockSpec((1, LANE), lambda i, j: (0, 0)))
        if add is not None:
            ops.append(add)
            specs.append(rowspec)
        if rope is not None:
            ops += list(rope)
            specs += [rowspec] * 3
    kern = functools.partial(_slab_prep_kernel, seg=seg, count=count,
                             rot=rot if rope is not None else 0, scale=scale,
                             transposed=transposed, has_add=add is not None, raw=raw)
    if transposed:
        out_spec = pl.BlockSpec((LANE, tm), lambda i, j: (j, i))
        out_shape = jax.ShapeDtypeStruct((n_slabs * LANE, r), BF16)
    else:
        out_spec = pl.BlockSpec((tm, LANE), lambda i, j: (i, j))
        out_shape = jax.ShapeDtypeStruct((r, n_slabs * LANE), BF16)
    return pl.pallas_call(
        kern,
        grid=(r // tm, n_slabs),
        in_specs=specs,
        out_specs=out_spec,
        out_shape=out_shape,
        compiler_params=_params(("parallel", "arbitrary")),
        name="slab_prep",
    )(*ops)


def _rope_slab_tables(seq_len, width, offset, reps):
    n = width // 4
    cr, sr, cc, sc = (jnp.asarray(t) for t in _rope_tables(seq_len, width))
    one = jnp.ones((seq_len, offset), F32)
    zero = jnp.zeros((seq_len, offset), F32)
    zq = jnp.zeros((seq_len, n), F32)
    rest = LANE // reps - offset - width
    pad1 = jnp.ones((seq_len, rest), F32)
    pad0 = jnp.zeros((seq_len, rest), F32)
    cos = jnp.concatenate([one, cr, cr, cc, cc, pad1] * reps, axis=1)
    s_next = jnp.concatenate([zero, -sr, zq, -sc, zq, pad0] * reps, axis=1)
    s_prev = jnp.concatenate([zero, zq, sr, zq, sc, pad0] * reps, axis=1)
    return cos, s_next, s_prev


def _gain2(g):
    return jnp.concatenate([g, g]).astype(F32)


def _ctx_heads(t, n_heads):
    return _heads_first(t.reshape(t.shape[0], n_heads, -1)).astype(BF16)


def _qk_prep(p, pc, off, w, q_gain, k_gain, rope, wk=None):
    wk = w if wk is None else wk
    qs = HEAD_DIM ** -0.5 * LOG2E
    gq, gk = _gain2(q_gain), _gain2(k_gain)
    c0 = off // LANE
    rot = HEAD_DIM // 4
    qT = slab_prep(p, c0, w // LANE, gq, rope, rot, scale=qs, transposed=True)
    k = slab_prep(p, c0 + w // LANE, wk // LANE, gk, rope, rot)
    qc = slab_prep(pc, c0, w // LANE, gq, scale=qs)
    kc = slab_prep(pc, c0 + w // LANE, wk // LANE, gk)
    return qT, k, qc, kc


def _mla_prep(p, pc, qa_gain, kva_gain, w_uq, w_ukv, q_gain, k_gain, rope):
    hh, dn, dr, dv = MLA_HEADS, MLA_NOPE, MLA_ROPE, MLA_V
    dq = dn + dr
    qs = dq ** -0.5 * LOG2E
    padh = LANE - dq
    w_q = jnp.pad(w_uq.reshape(MLA_Q_LORA, hh, dq), ((0, 0), (0, 0), (0, padh)))
    w_kv = w_ukv.reshape(MLA_KV_LORA, hh, dn + dv)
    w_k = jnp.pad(w_kv[..., :dn], ((0, 0), (0, 0), (0, LANE - dn)))
    w_all = jnp.concatenate([w_k.reshape(MLA_KV_LORA, hh * LANE),
                             w_kv[..., dn:].reshape(MLA_KV_LORA, hh * dv)], axis=1).astype(BF16)
    w_q = w_q.reshape(MLA_Q_LORA, hh * LANE).astype(BF16)
    gq = jnp.pad(q_gain.astype(F32), (0, padh))
    gk = jnp.pad(k_gain.astype(F32), (0, padh))
    off = MLA_OFF

    def project(t, tabs):
        lora = t[:, off:off + MLA_Q_LORA + MLA_KV_LORA + dr].astype(F32)
        cq = _rms(lora[:, :MLA_Q_LORA], qa_gain).astype(BF16)
        ckv = _rms(lora[:, MLA_Q_LORA:MLA_Q_LORA + MLA_KV_LORA], kva_gain).astype(BF16)
        k_rope = jnp.pad(lora[:, MLA_Q_LORA + MLA_KV_LORA:], ((0, 0), (dn, padh)))
        kv = matmul(ckv, w_all)
        q = slab_prep(matmul(cq, w_q), 0, hh, gq, tabs, dr // 4, seg=LANE, count=dq, scale=qs,
                      transposed=tabs is not None)
        k = slab_prep(kv, 0, hh, gk, tabs, dr // 4, add=k_rope, seg=LANE, count=dq)
        return q, k, kv

    qT, k, kv = project(p, rope)
    qc, kc, kvc = project(pc, None)
    vT = slab_prep(kv, hh, hh * dv // LANE, transposed=True)
    vc = kvc[:, hh * LANE:].astype(BF16)
    return qT, k, kc, vT, vc.T, qc, vc


def _merge(o):
    return jnp.transpose(o, (1, 0, 2)).reshape(o.shape[1], -1)


def _diff_post(o1, o2, lam, sub_gain, lambda_init):
    return _rms(o1 - lam * o2, sub_gain) * (1.0 - lambda_init)


def _moe(tokens_bf16, logits, b_router, layer, w1, w3, w2, sw1, sw3, sw2):
    n, d = tokens_bf16.shape
    assert n % MOE_BLOCK == 0
    eidx, gate = _route(logits[:, :N_EXPERTS], b_router)
    slot_tok, slot_of, blk_e, n_valid = _dispatch(eidx)
    xb = tokens_bf16[jnp.minimum(slot_tok, n - 1)]
    n_sh = n // MOE_BLOCK
    shared = moe_blocks(tokens_bf16, sw1, sw3, sw2, jnp.full((n_sh,), layer, jnp.int32),
                        jnp.full((1,), n_sh, jnp.int32))
    xb, shared = lax.optimization_barrier((xb, shared))
    yb = moe_blocks(xb, w1, w3, w2, blk_e + layer * N_EXPERTS, n_valid)
    y0, y1 = lax.optimization_barrier((yb[slot_of[:, 0]], yb[slot_of[:, 1]]))
    routed = y0.astype(F32) * gate[:, 0:1] + y1.astype(F32) * gate[:, 1:2]
    return routed + shared.astype(F32)


def kernel(x, c, ctx, c_ctx, w_ada, b_ada, g_attn, g_ffn, w_in, w_out, na_q_gain, na_k_gain, na_rpb, wa_q_gain, wa_k_gain, wa_sink, diff_q_gain, diff_k_gain, diff_lq1, diff_lk1, diff_lq2, diff_lk2, diff_sub_gain, mla_qa_gain, mla_kva_gain, mla_w_uq, mla_w_ukv, mla_q_gain, mla_k_gain, w_router, b_router, moe_w1, moe_w3, moe_w2, sh_w1, sh_w3, sh_w2):
    b, s, d = x.shape
    assert b == 1
    n_ctx = ctx.shape[1]
    depth = w_ada.shape[0]
    xl = x[0]
    xc = ctx[0]
    rope_head = _rope_slab_tables(s, HEAD_DIM, 0, 2)
    rope_mla = _rope_slab_tables(s, MLA_ROPE, MLA_NOPE, 1)

    cond8 = jnp.zeros((8, d), F32).at[0].set(c[0]).at[1].set(c_ctx)
    mod = modvec(cond8, w_ada, b_ada)
    tn = INPROJ_TN
    in_pad = _round_up(IN_COLS, tn)
    wr = jnp.pad(w_router, ((0, 0), (0, LANE - N_EXPERTS))).astype(BF16)
    inv_sqrt_d = HEAD_DIM ** -0.5
    moe_w = [t.reshape((depth * N_EXPERTS,) + t.shape[2:]) for t in (moe_w1, moe_w3, moe_w2)]
    sh_w = [sh_w1, sh_w3, sh_w2]

    for l in range(depth):
        with_ctx = l < depth - 1
        m_lat = mod[l, 0].reshape(6, 1, d)
        m_ctx = mod[l, 1].reshape(6, 1, d)
        w_in_l = jnp.pad(w_in[l], ((0, 0), (0, in_pad - IN_COLS))).astype(BF16)
        w_out_l = w_out[l].astype(BF16)
        g_a = g_attn[l][None]
        g_f = g_ffn[l][None]

        p = inproj(xl, g_a, 1.0 + m_lat[1], m_lat[0], w_in_l, tn)
        pc = inproj(xc, g_a, 1.0 + m_ctx[1], m_ctx[0], w_in_l, tn)

        w = NA_HEADS * HEAD_DIM
        qT, k, qc, kc = _qk_prep(p, pc, NA_OFF, w, na_q_gain[l], na_k_gain[l], None)
        vT = slab_prep(p, (NA_OFF + 2 * w) // LANE, w // LANE, transposed=True)
        vc = pc[:, NA_OFF + 2 * w:WA_OFF]
        u_na = _score_bound(HEAD_DIM, na_q_gain[l], na_k_gain[l], inv_sqrt_d)
        oT_na = na_attn(qT, k, vT, kc, vc.T, na_rpb[l], u_na)
        oc_na = (_merge(ctx_attn(_ctx_heads(qc, NA_HEADS), _ctx_heads(kc, NA_HEADS),
                                 _ctx_heads(vc, NA_HEADS))) if with_ctx else None)

        w, wk = WA_HEADS * HEAD_DIM, WA_KV_HEADS * HEAD_DIM
        qT, k, qc, kc = _qk_prep(p, pc, WA_OFF, w, wa_q_gain[l], wa_k_gain[l], rope_head, wk)
        vT = slab_prep(p, (WA_OFF + w + wk) // LANE, wk // LANE, transposed=True)
        vc = pc[:, WA_OFF + w + wk:DIFF_OFF]
        u_wa = _score_bound(HEAD_DIM, wa_q_gain[l], wa_k_gain[l], inv_sqrt_d)
        sink2 = wa_sink[l].astype(F32) * LOG2E
        oT_wa = wa_attn(qT, k, vT, kc, vc.T, sink2, u_wa)
        oc_wa = (_merge(ctx_attn(_ctx_heads(qc, WA_HEADS), _ctx_heads(kc, WA_KV_HEADS),
                                 _ctx_heads(vc, WA_KV_HEADS), sink2)) if with_ctx else None)

        lambda_init = 0.8 - 0.6 * math.exp(-0.3 * l)
        lam = (jnp.exp(jnp.sum(diff_lq1[l].astype(F32) * diff_lk1[l].astype(F32)))
               - jnp.exp(jnp.sum(diff_lq2[l].astype(F32) * diff_lk2[l].astype(F32))) + lambda_init)
        w = DIFF_HEADS * 2 * DIFF_DIM
        qT, k, qc, kc = _qk_prep(p, pc, DIFF_OFF, w, diff_q_gain[l], diff_k_gain[l], rope_head)
        vT = slab_prep(p, (DIFF_OFF + 2 * w) // LANE, w // LANE, transposed=True)
        vc = pc[:, DIFF_OFF + 2 * w:MLA_OFF]
        u_df = _score_bound(DIFF_DIM, diff_q_gain[l], diff_k_gain[l], DIFF_DIM ** -0.5)
        oT_df = flash(qT, k, kc, vT, vc.T, u_df, 2).reshape(DIFF_HEADS, 2, DIFF_V_DIM, s)
        oc_df = None
        if with_ctx:
            oc = ctx_attn(_ctx_heads(qc, 2 * DIFF_HEADS), _ctx_heads(kc, 2 * DIFF_HEADS),
                          _ctx_heads(vc, DIFF_HEADS)).reshape(DIFF_HEADS, 2, n_ctx, DIFF_V_DIM)
            oc_df = _diff_post(jnp.transpose(oc[:, 0], (1, 0, 2)), jnp.transpose(oc[:, 1], (1, 0, 2)),
                               lam, diff_sub_gain[l], lambda_init).reshape(n_ctx, -1)

        qT, k, kc, vT, vcT, qc, vc = _mla_prep(
            p, pc, mla_qa_gain[l], mla_kva_gain[l], mla_w_uq[l], mla_w_ukv[l], mla_q_gain[l],
            mla_k_gain[l], rope_mla)
        u_ml = _score_bound(MLA_NOPE + MLA_ROPE, mla_q_gain[l], mla_k_gain[l],
                            (MLA_NOPE + MLA_ROPE) ** -0.5)
        oT_ml = flash(qT, k, kc, vT, vcT, u_ml, 1)
        oc_ml = (_merge(ctx_attn(_ctx_heads(qc, MLA_HEADS), _ctx_heads(kc, MLA_HEADS),
                                 _ctx_heads(vc, MLA_HEADS))) if with_ctx else None)

        xl, hf, lg = outproj_t(oT_na, oT_wa, oT_df, oT_ml,
                               diff_sub_gain[l].astype(F32) * (1.0 - lambda_init), lam,
                               xl, w_out_l, m_lat[2], g_f, 1.0 + m_lat[4], m_lat[3], wr)
        if with_ctx:
            oc_cat = jnp.concatenate([oc_na, oc_wa, oc_df, oc_ml], axis=-1).astype(BF16)
            xc, hfc, lgc = outproj(oc_cat, xc, w_out_l, m_ctx[2], g_f, 1.0 + m_ctx[4], m_ctx[3], wr)
            tokens = jnp.concatenate([hfc, hf], axis=0)
            logits = jnp.concatenate([lgc, lg], axis=0)
        else:
            tokens, logits = hf, lg
        y = _moe(tokens, logits, b_router, l, *moe_w, *sh_w)
        if with_ctx:
            xc = xc + m_ctx[5] * y[:n_ctx]
            y = y[n_ctx:]
        xl = xl + m_lat[5] * y
    return xl[None]
```

```python
import functools
import math

import jax
import jax.numpy as jnp
import numpy as np
from jax import lax
from jax.experimental import pallas as pl
from jax.experimental.pallas import tpu as pltpu

F32 = jnp.float32
BF16 = jnp.bfloat16

GRID_W = 64
HEAD_DIM = 64
ROPE_THETA = 10000.0
NORM_EPS = 1e-6
NEG_INF = -1e30
WINDOW = 128
NA_HEADS = 8
NA_KH = 8
NA_KW = 16
WA_HEADS = 8
WA_KV_HEADS = 2
DIFF_HEADS = 4
DIFF_DIM = 64
DIFF_V_DIM = 128
MLA_HEADS = 8
MLA_NOPE = 64
MLA_ROPE = 32
MLA_V = 64
MLA_Q_LORA = 384
MLA_KV_LORA = 128
N_EXPERTS = 16
N_EXPERT_GROUPS = 4
TOP_K = 2
ROUTED_SCALE = 1.0
MOE_BLOCK = 256

NA_COLS = 3 * NA_HEADS * HEAD_DIM
WA_COLS = (WA_HEADS + 2 * WA_KV_HEADS) * HEAD_DIM
DIFF_COLS = 3 * DIFF_HEADS * 2 * DIFF_DIM
MLA_COLS = MLA_Q_LORA + MLA_KV_LORA + MLA_ROPE
NA_OFF = 0
WA_OFF = NA_OFF + NA_COLS
DIFF_OFF = WA_OFF + WA_COLS
MLA_OFF = DIFF_OFF + DIFF_COLS
IN_COLS = MLA_OFF + MLA_COLS

LANE = 128
INPROJ_TN = 768
LOG2E = math.log2(math.e)
VMEM_LIMIT = 48 * 1024 * 1024
MOE_VMEM_LIMIT = 56 * 1024 * 1024


def _round_up(n, m):
    return (n + m - 1) // m * m


def _params(sem):
    return pltpu.CompilerParams(dimension_semantics=sem, vmem_limit_bytes=VMEM_LIMIT)


def _sigmoid(x):
    return 1.0 / (1.0 + jnp.exp(-x))


def _modvec_kernel(c_ref, w_ref, b_ref, o_ref):
    a = c_ref[...]
    a = a * _sigmoid(a)
    o_ref[0] = jnp.dot(a, w_ref[0], preferred_element_type=F32,
                       precision=lax.Precision.HIGHEST) + b_ref[0]


def modvec(cond8, w_ada, b_ada):
    depth, d, n = w_ada.shape
    tn = 1024
    return pl.pallas_call(
        _modvec_kernel,
        grid=(depth, n // tn),
        in_specs=[pl.BlockSpec((8, d), lambda l, j: (0, 0)),
                  pl.BlockSpec((1, d, tn), lambda l, j: (l, 0, j)),
                  pl.BlockSpec((1, 1, tn), lambda l, j: (l, 0, j))],
        out_specs=pl.BlockSpec((1, 8, tn), lambda l, j: (l, 0, j)),
        out_shape=jax.ShapeDtypeStruct((depth, 8, n), F32),
        compiler_params=_params(("parallel", "parallel")),
        name="modvec",
    )(cond8, w_ada, b_ada.reshape(depth, 1, n))


def _inproj_kernel(x_ref, g_ref, sc_ref, sh_ref, w_ref, o_ref, h_sc):
    @pl.when(pl.program_id(1) == 0)
    def _():
        x = x_ref[...]
        ms = jnp.mean(x * x, axis=-1, keepdims=True)
        y = x * lax.rsqrt(ms + NORM_EPS) * g_ref[...]
        h_sc[...] = (y * sc_ref[...] + sh_ref[...]).astype(BF16)

    o_ref[...] = jnp.dot(h_sc[...], w_ref[...], preferred_element_type=F32).astype(o_ref.dtype)


def inproj(x, g, sc1, sh, w, tn):
    r, d = x.shape
    n = w.shape[1]
    tm = min(r, 1024)
    vec = pl.BlockSpec((1, d), lambda i, j: (0, 0))
    return pl.pallas_call(
        _inproj_kernel,
        grid=(r // tm, n // tn),
        in_specs=[pl.BlockSpec((tm, d), lambda i, j: (i, 0)), vec, vec, vec,
                  pl.BlockSpec((d, tn), lambda i, j: (0, j))],
        out_specs=pl.BlockSpec((tm, tn), lambda i, j: (i, j)),
        out_shape=jax.ShapeDtypeStruct((r, n), BF16),
        scratch_shapes=[pltpu.VMEM((tm, d), BF16)],
        compiler_params=_params(("parallel", "arbitrary")),
        name="inproj",
    )(x, g, sc1, sh, w)


def _mm_kernel(x_ref, w_ref, o_ref):
    o_ref[...] = jnp.dot(x_ref[...], w_ref[...], preferred_element_type=F32)


def matmul(x, w):
    r, k = x.shape
    n = w.shape[1]
    tm = min(r, 1024)
    return pl.pallas_call(
        _mm_kernel,
        grid=(r // tm,),
        in_specs=[pl.BlockSpec((tm, k), lambda i: (i, 0)),
                  pl.BlockSpec((k, n), lambda i: (0, 0))],
        out_specs=pl.BlockSpec((tm, n), lambda i: (i, 0)),
        out_shape=jax.ShapeDtypeStruct((r, n), F32),
        compiler_params=_params(("parallel",)),
        name="matmul",
    )(x, w)


FLASH_SAFE_SHIFT = 60.0
FLASH_BQ = 1024
FLASH_CHAIN_Q = 512
FLASH_SCORE_BYTES = 17 * 1024 * 1024


def _flash_kernel(qT_ref, k_ref, kc_ref, vT_ref, vcT_ref, u_ref, o_ref, acc_sc, *, n_comp, n_split,
                  bk, n_kb, online):
    bq = qT_ref.shape[1] // n_split
    dv = vT_ref.shape[0]
    comp_rows = LANE // n_comp
    chains = [(c, h) for c in range(n_comp) for h in range(n_split)]
    u = u_ref[0:1, 0:1]
    row = lax.broadcasted_iota(jnp.int32, (LANE, bq), 0)
    shift_rows = jnp.where(row == 0, -u, 0.0).astype(BF16)
    q_ops = []
    for c, h in chains:
        qT = qT_ref[:, h * bq:(h + 1) * bq]
        if n_comp > 1:
            qT = jnp.where((row >= c * comp_rows) & (row < (c + 1) * comp_rows), qT,
                           jnp.zeros_like(qT))
        q_ops.append(jnp.concatenate([qT, shift_rows], axis=0))
    for i in range(len(chains)):
        acc_sc[i] = jnp.zeros(acc_sc.shape[1:], F32)

    def step(kb, vb, carry):
        n = kb.shape[0]
        ones_col = jnp.where(lax.broadcasted_iota(jnp.int32, (n, LANE), 1) == 0, 1.0, 0.0)
        ka = jnp.concatenate([kb, ones_col.astype(BF16)], axis=1)
        scores = [jnp.dot(ka, q, preferred_element_type=F32) for q in q_ops]
        out = []
        for i, sT in enumerate(scores):
            m, l = carry[i]
            if online:
                m_new = jnp.maximum(m, jnp.max(sT, axis=0, keepdims=True))
                alpha = jnp.exp2(m - m_new)
                pT = jnp.exp2(sT - m_new)
                acc_sc[i] = alpha * acc_sc[i] + jnp.dot(vb, pT.astype(BF16),
                                                        preferred_element_type=F32)
                out.append((m_new, alpha * l + jnp.sum(pT, axis=0, keepdims=True)))
            else:
                pT = jnp.exp2(sT)
                acc_sc[i] += jnp.dot(vb, pT.astype(BF16), preferred_element_type=F32)
                out.append((m, l + jnp.sum(pT, axis=0, keepdims=True)))
        return tuple(out)

    def body(j, carry):
        off = pl.multiple_of(j * bk, bk)
        return step(k_ref[pl.ds(off, bk), :], vT_ref[:, pl.ds(off, bk)], carry)

    init = tuple((jnp.full((1, bq), NEG_INF, F32), jnp.zeros((1, bq), F32)) for _ in chains)
    fin = lax.fori_loop(0, n_kb, body, step(kc_ref[...], vcT_ref[...], init), unroll=2)
    for i, (c, h) in enumerate(chains):
        o_ref[c * dv:(c + 1) * dv, h * bq:(h + 1) * bq] = acc_sc[i] * (1.0 / fin[i][1])


def _pick_bk(n, n_chains, bq):
    for bk in (4096, 2048, 1024, 512, 256, 128):
        if n % bk == 0 and n_chains * bk * bq * 4 <= FLASH_SCORE_BYTES:
            return bk
    raise ValueError(f"key count {n} must be a multiple of {LANE}")


def _flash_call(qT, k, kc, vT, vcT, u8, n_comp, online):
    hw, s = qT.shape
    n_heads = hw // LANE
    dv = vT.shape[0] // n_heads
    l = kc.shape[0]
    bq = min(FLASH_BQ, s)
    n_split = max(1, bq // FLASH_CHAIN_Q)
    bk = _pick_bk(s, n_comp * n_split, bq // n_split)
    kern = functools.partial(_flash_kernel, n_comp=n_comp, n_split=n_split, bk=bk, n_kb=s // bk,
                             online=online)
    return pl.pallas_call(
        kern,
        grid=(n_heads, s // bq),
        in_specs=[pl.BlockSpec((LANE, bq), lambda h, i: (h, i)),
                  pl.BlockSpec((s, LANE), lambda h, i: (0, h)),
                  pl.BlockSpec((l, LANE), lambda h, i: (0, h)),
                  pl.BlockSpec((dv, s), lambda h, i: (h, 0)),
                  pl.BlockSpec((dv, l), lambda h, i: (h, 0)),
                  pl.BlockSpec(u8.shape, lambda h, i: (0, 0))],
        out_specs=pl.BlockSpec((n_comp * dv, bq), lambda h, i: (h, i)),
        out_shape=jax.ShapeDtypeStruct((n_heads * n_comp * dv, s), F32),
        scratch_shapes=[pltpu.VMEM((n_comp * n_split, dv, bq // n_split), F32)],
        compiler_params=_params(("parallel", "arbitrary")),
        name="flash_online" if online else "flash",
    )(qT, k, kc, vT, vcT, u8)


def flash(qT, k, kc, vT, vcT, u, n_comp):
    u8 = jnp.full((8, LANE), u, F32)
    return lax.cond(u <= FLASH_SAFE_SHIFT,
                    lambda *a: _flash_call(*a, n_comp, False),
                    lambda *a: _flash_call(*a, n_comp, True),
                    qT, k, kc, vT, vcT, u8)


def _score_bound(d, q_gain, k_gain, scale):
    return (d * scale * LOG2E * 1.02 * jnp.max(jnp.abs(q_gain.astype(F32)))
            * jnp.max(jnp.abs(k_gain.astype(F32))) + 0.01)


def _ctx_kernel(q_ref, k_ref, v_ref, sink_ref, o_ref, *, use_sink):
    s = lax.dot_general(q_ref[0], k_ref[0], (((1,), (1,)), ((), ())),
                        preferred_element_type=F32)
    m = jnp.max(s, axis=-1, keepdims=True)
    if use_sink:
        sk = sink_ref[0, 0:1, 0:1]
        m = jnp.maximum(m, sk)
    p = jnp.exp2(s - m)
    l = jnp.sum(p, axis=-1, keepdims=True)
    if use_sink:
        l = l + jnp.exp2(sk - m)
    o = jnp.dot(p.astype(BF16), v_ref[0], preferred_element_type=F32)
    o_ref[0] = o * (1.0 / l)


def ctx_attn(q, k, v, sink=None):
    h, l, dk = q.shape
    hk, hv, dv = k.shape[0], v.shape[0], v.shape[2]
    use_sink = sink is not None
    if sink is None:
        sink = jnp.zeros((h,), F32)
    sink3 = jnp.broadcast_to(sink.astype(F32)[:, None, None], (h, 8, LANE))
    kern = functools.partial(_ctx_kernel, use_sink=use_sink)
    return pl.pallas_call(
        kern,
        grid=(h,),
        in_specs=[pl.BlockSpec((1, l, dk), lambda i: (i, 0, 0)),
                  pl.BlockSpec((1, l, dk), lambda i: (i // (h // hk), 0, 0)),
                  pl.BlockSpec((1, l, dv), lambda i: (i // (h // hv), 0, 0)),
                  pl.BlockSpec((1, 8, LANE), lambda i: (i, 0, 0))],
        out_specs=pl.BlockSpec((1, l, dv), lambda i: (i, 0, 0)),
        out_shape=jax.ShapeDtypeStruct((h, l, dv), F32),
        compiler_params=_params(("parallel",)),
        name="ctx_attn",
    )(q, k, v, sink3)


LOCAL_BQ = 512


def _pad_rows(qT_h, slot, n_slots):
    z = jnp.zeros_like(qT_h)
    return jnp.concatenate([qT_h if s == slot else z for s in range(n_slots)], axis=0)


def _softmax_pv(s_list, v_list, extra, exact_max):
    if exact_max:
        m = functools.reduce(jnp.maximum, [jnp.max(s, axis=0, keepdims=True) for s in s_list])
        if extra is not None:
            m = jnp.maximum(m, extra)
            extra = extra - m
        s_list = [s - m for s in s_list]
    p_list = [jnp.exp2(s) for s in s_list]
    l = functools.reduce(jnp.add, [jnp.sum(p, axis=0, keepdims=True) for p in p_list])
    if extra is not None:
        l = l + jnp.exp2(extra)
    acc = functools.reduce(jnp.add, [jnp.dot(v, p.astype(BF16), preferred_element_type=F32)
                                     for v, p in zip(v_list, p_list)])
    return acc, l


def _wa_kernel(qT_ref, k_ref, vT_ref, kc_ref, vcT_ref, sh_ref, o_ref, *, seq, n_heads, n_kv,
               exact_max):
    i = pl.program_id(0)
    bq = qT_ref.shape[1]
    win = bq + 2 * WINDOW
    grp = n_heads // n_kv
    start = pl.multiple_of(jnp.clip(i * bq - WINDOW, 0, seq - win), WINDOW)
    kw = k_ref[pl.ds(start, win), :]
    vwT = vT_ref[:, pl.ds(start, win)]
    kc = kc_ref[...]
    vcT = vcT_ref[...]
    u = sh_ref[n_heads:n_heads + 1, 0:1]
    kpos = start + lax.broadcasted_iota(jnp.int32, (win, bq), 0)
    qpos = i * bq + lax.broadcasted_iota(jnp.int32, (win, bq), 1)
    mask_shift = jnp.where(jnp.abs(qpos - kpos) <= WINDOW, -u, NEG_INF)
    for h in range(n_heads):
        g = h // grp
        qTp = _pad_rows(qT_ref[h * HEAD_DIM:(h + 1) * HEAD_DIM, :], g, n_kv)
        s_loc = jnp.dot(kw, qTp, preferred_element_type=F32) + mask_shift
        s_ctx = jnp.dot(kc, qTp, preferred_element_type=F32) - u
        acc, l = _softmax_pv([s_loc, s_ctx], [vwT, vcT], sh_ref[h:h + 1, 0:1], exact_max)
        o_ref[h * HEAD_DIM:(h + 1) * HEAD_DIM, :] = (acc[g * HEAD_DIM:(g + 1) * HEAD_DIM]
                                                     * (1.0 / l))


def _wa_call(qT, k, vT, kc, vcT, sh, exact_max):
    hd, s = qT.shape
    kw = k.shape[1]
    l = kc.shape[0]
    bq = min(LOCAL_BQ, s)
    assert s % bq == 0 and s >= bq + 2 * WINDOW
    kern = functools.partial(_wa_kernel, seq=s, n_heads=hd // HEAD_DIM, n_kv=kw // HEAD_DIM,
                             exact_max=exact_max)
    full = lambda i: (0, 0)
    return pl.pallas_call(
        kern,
        grid=(s // bq,),
        in_specs=[pl.BlockSpec((hd, bq), lambda i: (0, i)),
                  pl.BlockSpec((s, kw), full), pl.BlockSpec((kw, s), full),
                  pl.BlockSpec((l, kw), full), pl.BlockSpec((kw, l), full),
                  pl.BlockSpec(sh.shape, full)],
        out_specs=pl.BlockSpec((hd, bq), lambda i: (0, i)),
        out_shape=jax.ShapeDtypeStruct((hd, s), F32),
        compiler_params=_params(("parallel",)),
        name="wa_attn_max" if exact_max else "wa_attn",
    )(qT, k, vT, kc, vcT, sh)


def wa_attn(qT, k, vT, kc, vcT, sink, u):
    n_heads = qT.shape[0] // HEAD_DIM
    rows = jnp.concatenate([sink.astype(F32) - u, jnp.reshape(u, (1,)).astype(F32),
                            jnp.zeros((2 * 8 - n_heads - 1,), F32)])
    sh = jnp.broadcast_to(rows[:, None], (rows.shape[0], LANE))
    return lax.cond(u <= FLASH_SAFE_SHIFT,
                    lambda *a: _wa_call(*a, False), lambda *a: _wa_call(*a, True),
                    qT, k, vT, kc, vcT, sh)


NA_KEY_BLOCKS = 4
NA_KEY_ROWS = 2 * NA_KH


def _na_kernel(qT_ref, *refs, exact_max):
    k_refs, v_refs = refs[:NA_KEY_BLOCKS], refs[NA_KEY_BLOCKS:2 * NA_KEY_BLOCKS]
    kc_ref, vcT_ref, bias_ref, sh_ref, o_ref = refs[2 * NA_KEY_BLOCKS:]
    kwin = jnp.concatenate([r[...] for r in k_refs], axis=0)
    vwinT = jnp.concatenate([r[...] for r in v_refs], axis=1)
    kc = kc_ref[...]
    vcT = vcT_ref[...]
    u = sh_ref[0:1, 0:1]
    for hh in range(2):
        rows = slice(hh * HEAD_DIM, (hh + 1) * HEAD_DIM)
        qTp = _pad_rows(qT_ref[rows, :], hh, 2)
        s_loc = jnp.dot(kwin, qTp, preferred_element_type=F32) + bias_ref[0, hh]
        s_ctx = jnp.dot(kc, qTp, preferred_element_type=F32) - u
        acc, l = _softmax_pv([s_loc, s_ctx], [vwinT, vcT], None, exact_max)
        o_ref[rows, :] = acc[rows] * (1.0 / l)


def na_bias_table(rpb, rows, shift):
    n_h = rpb.shape[0]
    a = jnp.arange(NA_KH)
    b = jnp.arange(NA_KEY_ROWS)
    c = jnp.arange(GRID_W)
    kc = jnp.arange(GRID_W)
    c0 = jnp.clip(c - NA_KW // 2, 0, GRID_W - NA_KW)
    col_ok = (kc[None, :] >= c0[:, None]) & (kc[None, :] < c0[:, None] + NA_KW)
    col_rel = jnp.clip(kc[None, :] - c[:, None] + (NA_KW - 1), 0, 2 * NA_KW - 2)
    row_rel = jnp.clip(b[None, :] - a[:, None] + NA_KH // 2 - 1, 0, 2 * NA_KH - 2)

    def edge_ok(r_base):
        r0 = jnp.clip(r_base + a - NA_KH // 2, 0, rows - NA_KH)
        key_row = r_base - NA_KH // 2 + b
        return (key_row[None, :] >= r0[:, None]) & (key_row[None, :] < r0[:, None] + NA_KH)

    inner_ok = (b[None, :] >= a[:, None]) & (b[None, :] < a[:, None] + NA_KH)
    row_ok = jnp.stack([edge_ok(0), inner_ok, edge_ok(rows - NA_KH)])
    e_c = jax.nn.one_hot(col_rel, 2 * NA_KW - 1, dtype=F32)
    e_r = jax.nn.one_hot(row_rel, 2 * NA_KH - 1, dtype=F32)
    hi = lax.Precision.HIGHEST
    t = jnp.einsum('hrx,ckx->hrck', rpb.astype(F32), e_c, precision=hi)
    t = jnp.einsum('abr,hrck->habck', e_r, t, precision=hi) * LOG2E - shift
    ok = row_ok[:, None, :, :, None, None] & col_ok[None, None, None, None]
    t = jnp.where(ok, t[None], NEG_INF)
    t = jnp.transpose(t, (0, 1, 3, 5, 2, 4))
    return t.reshape(3, n_h, NA_KEY_ROWS * GRID_W, NA_KH * GRID_W)


def _na_call(qT, k, vT, kc, vcT, bias, sh, exact_max):
    hd, s = qT.shape
    l = kc.shape[0]
    bq = NA_KH * GRID_W
    nb = s // bq
    pw = 2 * HEAD_DIM
    assert s % bq == 0 and bq == LOCAL_BQ
    var = lambda i: jnp.where(i == 0, 0, jnp.where(i == nb - 1, 2, 1))
    kb = bq // 2
    blk = lambda i, t: jnp.clip(2 * i - 1 + t, 0, 2 * nb - 1)
    kspecs = [pl.BlockSpec((kb, pw), lambda p, i, t=t: (blk(i, t), p))
              for t in range(NA_KEY_BLOCKS)]
    vspecs = [pl.BlockSpec((pw, kb), lambda p, i, t=t: (p, blk(i, t)))
              for t in range(NA_KEY_BLOCKS)]
    kern = functools.partial(_na_kernel, exact_max=exact_max)
    return pl.pallas_call(
        kern,
        grid=(hd // pw, nb),
        in_specs=[pl.BlockSpec((pw, bq), lambda p, i: (p, i)), *kspecs, *vspecs,
                  pl.BlockSpec((l, pw), lambda p, i: (0, p)),
                  pl.BlockSpec((pw, l), lambda p, i: (p, 0)),
                  pl.BlockSpec((1, 2, NA_KEY_ROWS * GRID_W, bq), lambda p, i: (var(i), p, 0, 0)),
                  pl.BlockSpec(sh.shape, lambda p, i: (0, 0))],
        out_specs=pl.BlockSpec((pw, bq), lambda p, i: (p, i)),
        out_shape=jax.ShapeDtypeStruct((hd, s), F32),
        compiler_params=_params(("parallel", "arbitrary")),
        name="na_attn_max" if exact_max else "na_attn",
    )(qT, *([k] * NA_KEY_BLOCKS), *([vT] * NA_KEY_BLOCKS), kc, vcT, bias, sh)


def na_attn(qT, k, vT, kc, vcT, rpb, u_qk):
    s = qT.shape[1]
    u = u_qk + jnp.maximum(jnp.max(rpb.astype(F32)) * LOG2E, 0.0)
    bias = na_bias_table(rpb, s // GRID_W, u)
    sh = jnp.full((8, LANE), u, F32)
    return lax.cond(u <= FLASH_SAFE_SHIFT,
                    lambda *a: _na_call(*a, False), lambda *a: _na_call(*a, True),
                    qT, k, vT, kc, vcT, bias, sh)


def _outproj_kernel(o_ref, x_ref, w_ref, gate_ref, g_ref, sc_ref, sh_ref, wr_ref,
                    xo_ref, hf_ref, lg_ref):
    acc = jnp.dot(o_ref[...], w_ref[...], preferred_element_type=F32)
    xn = x_ref[...] + gate_ref[...] * acc
    xo_ref[...] = xn
    ms = jnp.mean(xn * xn, axis=-1, keepdims=True)
    y = xn * lax.rsqrt(ms + NORM_EPS) * g_ref[...]
    hf = (y * sc_ref[...] + sh_ref[...]).astype(BF16)
    hf_ref[...] = hf
    lg_ref[...] = jnp.dot(hf, wr_ref[...], preferred_element_type=F32)


def _outproj_t_kernel(na_ref, wa_ref, df_ref, ml_ref, dfg_ref, lam_ref, x_ref, w_ref, gate_ref,
                      g_ref, sc_ref, sh_ref, wr_ref, xo_ref, hf_ref, lg_ref):
    lam = lam_ref[0:1, 0:1]
    parts = [na_ref[...], wa_ref[...]]
    for h in range(df_ref.shape[0]):
        dd = df_ref[h, 0] - lam * df_ref[h, 1]
        ms = jnp.mean(dd * dd, axis=0, keepdims=True)
        parts.append(dd * lax.rsqrt(ms + NORM_EPS) * dfg_ref[...])
    parts.append(ml_ref[...])
    o = jnp.concatenate([jnp.transpose(t).astype(BF16) for t in parts], axis=1)
    acc = jnp.dot(o, w_ref[...], preferred_element_type=F32)
    xn = x_ref[...] + gate_ref[...] * acc
    xo_ref[...] = xn
    ms = jnp.mean(xn * xn, axis=-1, keepdims=True)
    y = xn * lax.rsqrt(ms + NORM_EPS) * g_ref[...]
    hf = (y * sc_ref[...] + sh_ref[...]).astype(BF16)
    hf_ref[...] = hf
    lg_ref[...] = jnp.dot(hf, wr_ref[...], preferred_element_type=F32)


def outproj_t(oT_na, oT_wa, oT_df, oT_ml, df_gain, lam, x, w, gate, g, sc1, sh, wr):
    r, d = x.shape
    tm = min(r, 256)
    nh, _, dv2, _ = oT_df.shape
    wdt = oT_na.shape[0]
    dfg = jnp.broadcast_to(df_gain.astype(F32)[:, None], (dv2, tm))
    lam8 = jnp.full((8, LANE), lam, F32)
    vec = pl.BlockSpec((1, d), lambda i: (0, 0))
    row = lambda n: pl.BlockSpec((tm, n), lambda i: (i, 0))
    colblk = pl.BlockSpec((wdt, tm), lambda i: (0, i))
    const = lambda shape: pl.BlockSpec(shape, lambda i: tuple(0 for _ in shape))
    return pl.pallas_call(
        _outproj_t_kernel,
        grid=(r // tm,),
        in_specs=[colblk, colblk, pl.BlockSpec((nh, 2, dv2, tm), lambda i: (0, 0, 0, i)), colblk,
                  const((dv2, tm)), const((8, LANE)), row(d), const(w.shape), vec, vec, vec, vec,
                  const((d, LANE))],
        out_specs=[row(d), row(d), row(LANE)],
        out_shape=[jax.ShapeDtypeStruct((r, d), F32), jax.ShapeDtypeStruct((r, d), BF16),
                   jax.ShapeDtypeStruct((r, LANE), F32)],
        compiler_params=_params(("parallel",)),
        name="outproj_t",
    )(oT_na, oT_wa, oT_df, oT_ml, dfg, lam8, x, w, gate, g, sc1, sh, wr)


def outproj(o, x, w, gate, g, sc1, sh, wr):
    r, d = x.shape
    k = o.shape[1]
    tm = min(r, 256)
    vec = pl.BlockSpec((1, d), lambda i: (0, 0))
    row = lambda n: pl.BlockSpec((tm, n), lambda i: (i, 0))
    return pl.pallas_call(
        _outproj_kernel,
        grid=(r // tm,),
        in_specs=[row(k), row(d), pl.BlockSpec((k, d), lambda i: (0, 0)), vec, vec, vec, vec,
                  pl.BlockSpec((d, LANE), lambda i: (0, 0))],
        out_specs=[row(d), row(d), row(LANE)],
        out_shape=[jax.ShapeDtypeStruct((r, d), F32), jax.ShapeDtypeStruct((r, d), BF16),
                   jax.ShapeDtypeStruct((r, LANE), F32)],
        compiler_params=_params(("parallel",)),
        name="outproj",
    )(o, x, w, gate, g, sc1, sh, wr)


def _moe_kernel(be_ref, nv_ref, x_ref, w1_ref, w3_ref, w2_ref, o_ref):
    i = pl.program_id(0)

    @pl.when(i < nv_ref[0])
    def _():
        x = x_ref[...].astype(w1_ref.dtype)
        a = jnp.dot(x, w1_ref[0], preferred_element_type=F32)
        b = jnp.dot(x, w3_ref[0], preferred_element_type=F32)
        hmid = (a * _sigmoid(a) * b).astype(w2_ref.dtype)
        o_ref[...] = jnp.dot(hmid, w2_ref[0], preferred_element_type=F32).astype(o_ref.dtype)

    @pl.when(i >= nv_ref[0])
    def _():
        o_ref[...] = jnp.zeros(o_ref.shape, o_ref.dtype)


def moe_blocks(x, w1, w3, w2, blk_e, n_valid):
    n, d = x.shape
    f = w1.shape[2]
    n_blk = n // MOE_BLOCK
    grid_spec = pltpu.PrefetchScalarGridSpec(
        num_scalar_prefetch=2,
        grid=(n_blk,),
        in_specs=[pl.BlockSpec((MOE_BLOCK, d), lambda i, be, nv: (i, 0)),
                  pl.BlockSpec((1, d, f), lambda i, be, nv: (be[i], 0, 0)),
                  pl.BlockSpec((1, d, f), lambda i, be, nv: (be[i], 0, 0)),
                  pl.BlockSpec((1, f, d), lambda i, be, nv: (be[i], 0, 0),
                               pipeline_mode=pl.Buffered(1))],
        out_specs=pl.BlockSpec((MOE_BLOCK, d), lambda i, be, nv: (i, 0)),
    )
    return pl.pallas_call(
        _moe_kernel,
        grid_spec=grid_spec,
        out_shape=jax.ShapeDtypeStruct((n, d), BF16),
        compiler_params=pltpu.CompilerParams(dimension_semantics=("arbitrary",),
                                             vmem_limit_bytes=MOE_VMEM_LIMIT),
        name="moe_blocks",
    )(blk_e, n_valid, x, w1, w3, w2)


def _rms(x, g):
    y = x * lax.rsqrt(jnp.mean(x * x, axis=-1, keepdims=True) + NORM_EPS)
    return y * g


def _rope_tables(seq_len, rot_dim):
    n = rot_dim // 4
    t = np.arange(seq_len)
    row = (t // GRID_W).astype(np.float32)[:, None]
    col = (t % GRID_W).astype(np.float32)[:, None]
    inv = (np.float32(ROPE_THETA) ** (-np.arange(n, dtype=np.float32) / np.float32(n)))
    inv = inv.astype(np.float32)
    return tuple(f(a * inv).astype(np.float32) for a in (row, col) for f in (np.cos, np.sin))


def _heads_first(t):
    return jnp.transpose(t, (1, 0, 2))


def _route(logits, b_router):
    n = logits.shape[0]
    per = N_EXPERTS // N_EXPERT_GROUPS
    scores = jax.nn.sigmoid(logits.astype(F32))
    sel = scores + b_router.astype(F32)
    grp = sel.reshape(n, N_EXPERT_GROUPS, per)
    gscore = None
    for a in range(per):
        for bb in range(a + 1, per):
            pair = grp[..., a] + grp[..., bb]
            gscore = pair if gscore is None else jnp.maximum(gscore, pair)
    gidx = jnp.argmax(gscore, axis=-1)
    eids = jnp.arange(N_EXPERTS)
    masked = jnp.where((eids // per)[None, :] == gidx[:, None], sel, -jnp.inf)
    e1 = jnp.argmax(masked, axis=-1)
    e2 = jnp.argmax(jnp.where(eids[None, :] == e1[:, None], -jnp.inf, masked), axis=-1)
    eidx = jnp.stack([e1, e2], axis=1).astype(jnp.int32)
    wts = jnp.take_along_axis(scores, eidx, axis=1)
    wts = wts / jnp.sum(wts, axis=-1, keepdims=True) * ROUTED_SCALE
    return eidx, wts


def _prefix_counts(onehot):
    n, e = onehot.shape
    ch = onehot.astype(F32).reshape(n // LANE, LANE, e)
    tri = jnp.tril(jnp.ones((LANE, LANE), F32))
    within = jnp.einsum('ij,cjk->cik', tri, ch)
    tot = within[:, -1, :]
    base = jnp.cumsum(tot, axis=0) - tot
    return (within + base[:, None, :]).reshape(n, e)


def _dispatch(eidx):
    n = eidx.shape[0]
    n_assign = n * TOP_K
    assert n_assign % LANE == 0
    e_flat = eidx.reshape(-1)
    tok = jnp.repeat(jnp.arange(n, dtype=jnp.int32), TOP_K)
    onehot = e_flat[:, None] == jnp.arange(N_EXPERTS)[None, :]
    csum = _prefix_counts(onehot)
    counts = csum[-1].astype(jnp.int32)
    rank = jnp.sum(jnp.where(onehot, csum, 0.0), axis=1).astype(jnp.int32) - 1
    padded = (counts + MOE_BLOCK - 1) // MOE_BLOCK * MOE_BLOCK
    pend = jnp.cumsum(padded)
    pstart = pend - padded
    dest = jnp.sum(jnp.where(onehot, pstart[None, :], 0), axis=1) + rank
    n_slots = (n_assign + N_EXPERTS * (MOE_BLOCK - 1) + MOE_BLOCK - 1) // MOE_BLOCK * MOE_BLOCK
    n_blk = n_slots // MOE_BLOCK
    slot_tok = jnp.full((n_slots,), n, dtype=jnp.int32).at[dest].set(tok)
    slot_of = dest.reshape(n, TOP_K)
    blk_start = jnp.arange(n_blk, dtype=jnp.int32) * MOE_BLOCK
    blk_e = jnp.minimum(jnp.sum(blk_start[:, None] >= pend[None, :], axis=1), N_EXPERTS - 1)
    n_valid = (pend[-1] // MOE_BLOCK).reshape(1)
    return slot_tok, slot_of, blk_e.astype(jnp.int32), n_valid.astype(jnp.int32)


PREP_TM = 4096


def _slab_prep_kernel(*refs, seg, count, rot, transposed, has_add, raw):
    refs = list(refs)
    o_ref = refs.pop()
    x = refs[0][...].astype(F32)
    if not raw:
        g_ref = refs[1]
        nxt = 2
        if has_add:
            x = x + refs[nxt][...]
            nxt += 1
        sq = x * x
        hi = sq.astype(BF16)
        lo = (sq - hi.astype(F32)).astype(BF16)
        r = lax.broadcasted_iota(jnp.int32, (LANE, LANE), 0) // seg
        c = lax.broadcasted_iota(jnp.int32, (LANE, LANE), 1) // seg
        blk = jnp.where(r == c, 1.0, 0.0).astype(BF16)
        ssum = (jnp.dot(hi, blk, preferred_element_type=F32)
                + jnp.dot(lo, blk, preferred_element_type=F32))
        x = x * lax.rsqrt(ssum * (1.0 / count) + NORM_EPS) * g_ref[...]
        if rot:
            cos, s_next, s_prev = refs[nxt][...], refs[nxt + 1][...], refs[nxt + 2][...]
            x = (x * cos + pltpu.roll(x, LANE - rot, 1) * s_next + pltpu.roll(x, rot, 1) * s_prev)
    o_ref[...] = (jnp.transpose(x) if transposed else x).astype(o_ref.dtype)


def slab_prep(x, col0, n_slabs, gain=None, rope=None, rot=0, add=None, seg=HEAD_DIM,
              count=HEAD_DIM, scale=1.0, transposed=False):
    r = x.shape[0]
    tm = min(PREP_TM, r)
    raw = gain is None
    ops = [x]
    specs = [pl.BlockSpec((tm, LANE), lambda i, j: (i, col0 + j))]
    rowspec = pl.BlockSpec((tm, LANE), lambda i, j: (i, 0))
    if not raw:
        ops.append((gain.astype(F32) * scale).reshape(1, LANE))
        specs.append(pl.BlockSpec((1, LANE), lambda i, j: (0, 0)))
        if add is not None:
            ops.append(add)
            specs.append(rowspec)
        if rope is not None:
            ops += list(rope)
            specs += [rowspec] * 3
    kern = functools.partial(_slab_prep_kernel, seg=seg, count=count,
                             rot=rot if rope is not None else 0,
                             transposed=transposed, has_add=add is not None, raw=raw)
    if transposed:
        out_spec = pl.BlockSpec((LANE, tm), lambda i, j: (j, i))
        out_shape = jax.ShapeDtypeStruct((n_slabs * LANE, r), BF16)
    else:
        out_spec = pl.BlockSpec((tm, LANE), lambda i, j: (i, j))
        out_shape = jax.ShapeDtypeStruct((r, n_slabs * LANE), BF16)
    return pl.pallas_call(
        kern,
        grid=(r // tm, n_slabs),
        in_specs=specs,
        out_specs=out_spec,
        out_shape=out_shape,
        compiler_params=_params(("parallel", "arbitrary")),
        name="slab_prep",
    )(*ops)


def _rope_slab_tables(seq_len, width, offset, reps):
    n = width // 4
    cr, sr, cc, sc = (jnp.asarray(t) for t in _rope_tables(seq_len, width))
    one = jnp.ones((seq_len, offset), F32)
    zero = jnp.zeros((seq_len, offset), F32)
    zq = jnp.zeros((seq_len, n), F32)
    rest = LANE // reps - offset - width
    pad1 = jnp.ones((seq_len, rest), F32)
    pad0 = jnp.zeros((seq_len, rest), F32)
    cos = jnp.concatenate([one, cr, cr, cc, cc, pad1] * reps, axis=1)
    s_next = jnp.concatenate([zero, -sr, zq, -sc, zq, pad0] * reps, axis=1)
    s_prev = jnp.concatenate([zero, zq, sr, zq, sc, pad0] * reps, axis=1)
    return cos, s_next, s_prev


def _gain2(g):
    return jnp.concatenate([g, g]).astype(F32)


def _ctx_heads(t, n_heads):
    return _heads_first(t.reshape(t.shape[0], n_heads, -1)).astype(BF16)


def _qk_prep(p, pc, off, w, q_gain, k_gain, rope, wk=None):
    wk = w if wk is None else wk
    qs = HEAD_DIM ** -0.5 * LOG2E
    gq, gk = _gain2(q_gain), _gain2(k_gain)
    c0 = off // LANE
    rot = HEAD_DIM // 4
    qT = slab_prep(p, c0, w // LANE, gq, rope, rot, scale=qs, transposed=True)
    k = slab_prep(p, c0 + w // LANE, wk // LANE, gk, rope, rot)
    qc = slab_prep(pc, c0, w // LANE, gq, scale=qs)
    kc = slab_prep(pc, c0 + w // LANE, wk // LANE, gk)
    return qT, k, qc, kc


def _mla_prep(p, pc, qa_gain, kva_gain, w_uq, w_ukv, q_gain, k_gain, rope):
    hh, dn, dr, dv = MLA_HEADS, MLA_NOPE, MLA_ROPE, MLA_V
    dq = dn + dr
    qs = dq ** -0.5 * LOG2E
    padh = LANE - dq
    w_q = jnp.pad(w_uq.reshape(MLA_Q_LORA, hh, dq), ((0, 0), (0, 0), (0, padh)))
    w_kv = w_ukv.reshape(MLA_KV_LORA, hh, dn + dv)
    w_k = jnp.pad(w_kv[..., :dn], ((0, 0), (0, 0), (0, LANE - dn)))
    w_all = jnp.concatenate([w_k.reshape(MLA_KV_LORA, hh * LANE),
                             w_kv[..., dn:].reshape(MLA_KV_LORA, hh * dv)], axis=1).astype(BF16)
    w_q = w_q.reshape(MLA_Q_LORA, hh * LANE).astype(BF16)
    gq = jnp.pad(q_gain.astype(F32), (0, padh))
    gk = jnp.pad(k_gain.astype(F32), (0, padh))
    off = MLA_OFF

    def project(t, tabs):
        lora = t[:, off:off + MLA_Q_LORA + MLA_KV_LORA + dr].astype(F32)
        cq = _rms(lora[:, :MLA_Q_LORA], qa_gain).astype(BF16)
        ckv = _rms(lora[:, MLA_Q_LORA:MLA_Q_LORA + MLA_KV_LORA], kva_gain).astype(BF16)
        k_rope = jnp.pad(lora[:, MLA_Q_LORA + MLA_KV_LORA:], ((0, 0), (dn, padh)))
        kv = matmul(ckv, w_all)
        q = slab_prep(matmul(cq, w_q), 0, hh, gq, tabs, dr // 4, seg=LANE, count=dq, scale=qs,
                      transposed=tabs is not None)
        k = slab_prep(kv, 0, hh, gk, tabs, dr // 4, add=k_rope, seg=LANE, count=dq)
        return q, k, kv

    qT, k, kv = project(p, rope)
    qc, kc, kvc = project(pc, None)
    vT = slab_prep(kv, hh, hh * dv // LANE, transposed=True)
    vc = kvc[:, hh * LANE:].astype(BF16)
    return qT, k, kc, vT, vc.T, qc, vc


def _merge(o):
    return jnp.transpose(o, (1, 0, 2)).reshape(o.shape[1], -1)


def _diff_post(o1, o2, lam, sub_gain, lambda_init):
    return _rms(o1 - lam * o2, sub_gain) * (1.0 - lambda_init)


def _moe(tokens_bf16, logits, b_router, layer, w1, w3, w2, sw1, sw3, sw2):
    n, d = tokens_bf16.shape
    assert n % MOE_BLOCK == 0
    eidx, gate = _route(logits[:, :N_EXPERTS], b_router)
    slot_tok, slot_of, blk_e, n_valid = _dispatch(eidx)
    xb = tokens_bf16[jnp.minimum(slot_tok, n - 1)]
    n_sh = n // MOE_BLOCK
    shared = moe_blocks(tokens_bf16, sw1, sw3, sw2, jnp.full((n_sh,), layer, jnp.int32),
                        jnp.full((1,), n_sh, jnp.int32))
    xb, shared = lax.optimization_barrier((xb, shared))
    yb = moe_blocks(xb, w1, w3, w2, blk_e + layer * N_EXPERTS, n_valid)
    y0, y1 = lax.optimization_barrier((yb[slot_of[:, 0]], yb[slot_of[:, 1]]))
    routed = y0.astype(F32) * gate[:, 0:1] + y1.astype(F32) * gate[:, 1:2]
    return routed + shared.astype(F32)


def kernel(x, c, ctx, c_ctx, w_ada, b_ada, g_attn, g_ffn, w_in, w_out, na_q_gain, na_k_gain, na_rpb, wa_q_gain, wa_k_gain, wa_sink, diff_q_gain, diff_k_gain, diff_lq1, diff_lk1, diff_lq2, diff_lk2, diff_sub_gain, mla_qa_gain, mla_kva_gain, mla_w_uq, mla_w_ukv, mla_q_gain, mla_k_gain, w_router, b_router, moe_w1, moe_w3, moe_w2, sh_w1, sh_w3, sh_w2):
    b, s, d = x.shape
    assert b == 1
    n_ctx = ctx.shape[1]
    depth = w_ada.shape[0]
    xl = x[0]
    xc = ctx[0]
    rope_head = _rope_slab_tables(s, HEAD_DIM, 0, 2)
    rope_mla = _rope_slab_tables(s, MLA_ROPE, MLA_NOPE, 1)

    cond8 = jnp.zeros((8, d), F32).at[0].set(c[0]).at[1].set(c_ctx)
    mod = modvec(cond8, w_ada, b_ada)
    tn = INPROJ_TN
    in_pad = _round_up(IN_COLS, tn)
    wr = jnp.pad(w_router, ((0, 0), (0, LANE - N_EXPERTS))).astype(BF16)
    inv_sqrt_d = HEAD_DIM ** -0.5
    moe_w = [t.reshape((depth * N_EXPERTS,) + t.shape[2:]) for t in (moe_w1, moe_w3, moe_w2)]
    sh_w = [sh_w1, sh_w3, sh_w2]

    for l in range(depth):
        with_ctx = l < depth - 1
        m_lat = mod[l, 0].reshape(6, 1, d)
        m_ctx = mod[l, 1].reshape(6, 1, d)
        w_in_l = jnp.pad(w_in[l], ((0, 0), (0, in_pad - IN_COLS))).astype(BF16)
        w_out_l = w_out[l].astype(BF16)
        g_a = g_attn[l][None]
        g_f = g_ffn[l][None]

        p = inproj(xl, g_a, 1.0 + m_lat[1], m_lat[0], w_in_l, tn)
        pc = inproj(xc, g_a, 1.0 + m_ctx[1], m_ctx[0], w_in_l, tn)

        w = NA_HEADS * HEAD_DIM
        qT, k, qc, kc = _qk_prep(p, pc, NA_OFF, w, na_q_gain[l], na_k_gain[l], None)
        vT = slab_prep(p, (NA_OFF + 2 * w) // LANE, w // LANE, transposed=True)
        vc = pc[:, NA_OFF + 2 * w:WA_OFF]
        u_na = _score_bound(HEAD_DIM, na_q_gain[l], na_k_gain[l], inv_sqrt_d)
        oT_na = na_attn(qT, k, vT, kc, vc.T, na_rpb[l], u_na)
        oc_na = (_merge(ctx_attn(_ctx_heads(qc, NA_HEADS), _ctx_heads(kc, NA_HEADS),
                                 _ctx_heads(vc, NA_HEADS))) if with_ctx else None)

        w, wk = WA_HEADS * HEAD_DIM, WA_KV_HEADS * HEAD_DIM
        qT, k, qc, kc = _qk_prep(p, pc, WA_OFF, w, wa_q_gain[l], wa_k_gain[l], rope_head, wk)
        vT = slab_prep(p, (WA_OFF + w + wk) // LANE, wk // LANE, transposed=True)
        vc = pc[:, WA_OFF + w + wk:DIFF_OFF]
        u_wa = _score_bound(HEAD_DIM, wa_q_gain[l], wa_k_gain[l], inv_sqrt_d)
        sink2 = wa_sink[l].astype(F32) * LOG2E
        oT_wa = wa_attn(qT, k, vT, kc, vc.T, sink2, u_wa)
        oc_wa = (_merge(ctx_attn(_ctx_heads(qc, WA_HEADS), _ctx_heads(kc, WA_KV_HEADS),
                                 _ctx_heads(vc, WA_KV_HEADS), sink2)) if with_ctx else None)

        lambda_init = 0.8 - 0.6 * math.exp(-0.3 * l)
        lam = (jnp.exp(jnp.sum(diff_lq1[l].astype(F32) * diff_lk1[l].astype(F32)))
               - jnp.exp(jnp.sum(diff_lq2[l].astype(F32) * diff_lk2[l].astype(F32))) + lambda_init)
        w = DIFF_HEADS * 2 * DIFF_DIM
        qT, k, qc, kc = _qk_prep(p, pc, DIFF_OFF, w, diff_q_gain[l], diff_k_gain[l], rope_head)
        vT = slab_prep(p, (DIFF_OFF + 2 * w) // LANE, w // LANE, transposed=True)
        vc = pc[:, DIFF_OFF + 2 * w:MLA_OFF]
        u_df = _score_bound(DIFF_DIM, diff_q_gain[l], diff_k_gain[l], DIFF_DIM ** -0.5)
        oT_df = flash(qT, k, kc, vT, vc.T, u_df, 2).reshape(DIFF_HEADS, 2, DIFF_V_DIM, s)
        oc_df = None
        if with_ctx:
            oc = ctx_attn(_ctx_heads(qc, 2 * DIFF_HEADS), _ctx_heads(kc, 2 * DIFF_HEADS),
                          _ctx_heads(vc, DIFF_HEADS)).reshape(DIFF_HEADS, 2, n_ctx, DIFF_V_DIM)
            oc_df = _diff_post(jnp.transpose(oc[:, 0], (1, 0, 2)), jnp.transpose(oc[:, 1], (1, 0, 2)),
                               lam, diff_sub_gain[l], lambda_init).reshape(n_ctx, -1)

        qT, k, kc, vT, vcT, qc, vc = _mla_prep(
            p, pc, mla_qa_gain[l], mla_kva_gain[l], mla_w_uq[l], mla_w_ukv[l], mla_q_gain[l],
            mla_k_gain[l], rope_mla)
        u_ml = _score_bound(MLA_NOPE + MLA_ROPE, mla_q_gain[l], mla_k_gain[l],
                            (MLA_NOPE + MLA_ROPE) ** -0.5)
        oT_ml = flash(qT, k, kc, vT, vcT, u_ml, 1)
        oc_ml = (_merge(ctx_attn(_ctx_heads(qc, MLA_HEADS), _ctx_heads(kc, MLA_HEADS),
                                 _ctx_heads(vc, MLA_HEADS))) if with_ctx else None)

        xl, hf, lg = outproj_t(oT_na, oT_wa, oT_df, oT_ml,
                               diff_sub_gain[l].astype(F32) * (1.0 - lambda_init), lam,
                               xl, w_out_l, m_lat[2], g_f, 1.0 + m_lat[4], m_lat[3], wr)
        if with_ctx:
            oc_cat = jnp.concatenate([oc_na, oc_wa, oc_df, oc_ml], axis=-1).astype(BF16)
            xc, hfc, lgc = outproj(oc_cat, xc, w_out_l, m_ctx[2], g_f, 1.0 + m_ctx[4], m_ctx[3], wr)
            tokens = jnp.concatenate([hfc, hf], axis=0)
            logits = jnp.concatenate([lgc, lg], axis=0)
        else:
            tokens, logits = hf, lg
        y = _moe(tokens, logits, b_router, l, *moe_w, *sh_w)
        if with_ctx:
            xc = xc + m_ctx[5] * y[:n_ctx]
            y = y[n_ctx:]
        xl = xl + m_lat[5] * y
    return xl[None]
```

```python
import functools
import math

import jax
import jax.numpy as jnp
import numpy as np
from jax import lax
from jax.experimental import pallas as pl
from jax.experimental.pallas import tpu as pltpu

F32 = jnp.float32
BF16 = jnp.bfloat16

GRID_W = 64
HEAD_DIM = 64
ROPE_THETA = 10000.0
NORM_EPS = 1e-6
NEG_INF = -1e30
WINDOW = 128
NA_HEADS = 8
NA_KH = 8
NA_KW = 16
WA_HEADS = 8
WA_KV_HEADS = 2
DIFF_HEADS = 4
DIFF_DIM = 64
DIFF_V_DIM = 128
MLA_HEADS = 8
MLA_NOPE = 64
MLA_ROPE = 32
MLA_V = 64
MLA_Q_LORA = 384
MLA_KV_LORA = 128
N_EXPERTS = 16
N_EXPERT_GROUPS = 4
TOP_K = 2
ROUTED_SCALE = 1.0
MOE_BLOCK = 256

NA_COLS = 3 * NA_HEADS * HEAD_DIM
WA_COLS = (WA_HEADS + 2 * WA_KV_HEADS) * HEAD_DIM
DIFF_COLS = 3 * DIFF_HEADS * 2 * DIFF_DIM
MLA_COLS = MLA_Q_LORA + MLA_KV_LORA + MLA_ROPE
NA_OFF = 0
WA_OFF = NA_OFF + NA_COLS
DIFF_OFF = WA_OFF + WA_COLS
MLA_OFF = DIFF_OFF + DIFF_COLS
IN_COLS = MLA_OFF + MLA_COLS

LANE = 128
INPROJ_TN = 768
LOG2E = math.log2(math.e)
VMEM_LIMIT = 48 * 1024 * 1024
MOE_VMEM_LIMIT = 56 * 1024 * 1024


def _round_up(n, m):
    return (n + m - 1) // m * m


def _params(sem):
    return pltpu.CompilerParams(dimension_semantics=sem, vmem_limit_bytes=VMEM_LIMIT)


def _sigmoid(x):
    return 1.0 / (1.0 + jnp.exp(-x))


def _modvec_kernel(c_ref, w_ref, b_ref, o_ref):
    a = c_ref[...]
    a = a * _sigmoid(a)
    o_ref[0] = jnp.dot(a, w_ref[0], preferred_element_type=F32,
                       precision=lax.Precision.HIGHEST) + b_ref[0]


def modvec(cond8, w_ada, b_ada):
    depth, d, n = w_ada.shape
    tn = 1024
    return pl.pallas_call(
        _modvec_kernel,
        grid=(depth, n // tn),
        in_specs=[pl.BlockSpec((8, d), lambda l, j: (0, 0)),
                  pl.BlockSpec((1, d, tn), lambda l, j: (l, 0, j)),
                  pl.BlockSpec((1, 1, tn), lambda l, j: (l, 0, j))],
        out_specs=pl.BlockSpec((1, 8, tn), lambda l, j: (l, 0, j)),
        out_shape=jax.ShapeDtypeStruct((depth, 8, n), F32),
        compiler_params=_params(("parallel", "parallel")),
        name="modvec",
    )(cond8, w_ada, b_ada.reshape(depth, 1, n))


def _inproj_kernel(x_ref, g_ref, sc_ref, sh_ref, w_ref, o_ref, h_sc):
    @pl.when(pl.program_id(1) == 0)
    def _():
        x = x_ref[...]
        ms = jnp.mean(x * x, axis=-1, keepdims=True)
        y = x * lax.rsqrt(ms + NORM_EPS) * g_ref[...]
        h_sc[...] = (y * sc_ref[...] + sh_ref[...]).astype(BF16)

    o_ref[...] = jnp.dot(h_sc[...], w_ref[...], preferred_element_type=F32).astype(o_ref.dtype)


def inproj(x, g, sc1, sh, w, tn):
    r, d = x.shape
    n = w.shape[1]
    tm = min(r, 1024)
    vec = pl.BlockSpec((1, d), lambda i, j: (0, 0))
    return pl.pallas_call(
        _inproj_kernel,
        grid=(r // tm, n // tn),
        in_specs=[pl.BlockSpec((tm, d), lambda i, j: (i, 0)), vec, vec, vec,
                  pl.BlockSpec((d, tn), lambda i, j: (0, j))],
        out_specs=pl.BlockSpec((tm, tn), lambda i, j: (i, j)),
        out_shape=jax.ShapeDtypeStruct((r, n), BF16),
        scratch_shapes=[pltpu.VMEM((tm, d), BF16)],
        compiler_params=_params(("parallel", "arbitrary")),
        name="inproj",
    )(x, g, sc1, sh, w)


def _mm_kernel(x_ref, w_ref, o_ref):
    o_ref[...] = jnp.dot(x_ref[...], w_ref[...], preferred_element_type=F32)


def matmul(x, w):
    r, k = x.shape
    n = w.shape[1]
    tm = min(r, 1024)
    return pl.pallas_call(
        _mm_kernel,
        grid=(r // tm,),
        in_specs=[pl.BlockSpec((tm, k), lambda i: (i, 0)),
                  pl.BlockSpec((k, n), lambda i: (0, 0))],
        out_specs=pl.BlockSpec((tm, n), lambda i: (i, 0)),
        out_shape=jax.ShapeDtypeStruct((r, n), F32),
        compiler_params=_params(("parallel",)),
        name="matmul",
    )(x, w)


FLASH_SAFE_SHIFT = 60.0
FLASH_BQ = 1024
FLASH_CHAIN_Q = 512
FLASH_SCORE_BYTES = 17 * 1024 * 1024


def _flash_kernel(qT_ref, k_ref, kc_ref, vT_ref, vcT_ref, u_ref, o_ref, acc_sc, *, n_comp, n_split,
                  bk, n_kb, online):
    bq = qT_ref.shape[1] // n_split
    dv = vT_ref.shape[0]
    comp_rows = LANE // n_comp
    chains = [(c, h) for c in range(n_comp) for h in range(n_split)]
    u = u_ref[0:1, 0:1]
    row = lax.broadcasted_iota(jnp.int32, (LANE, bq), 0)
    shift_rows = jnp.where(row == 0, -u, 0.0).astype(BF16)
    q_ops = []
    for c, h in chains:
        qT = qT_ref[:, h * bq:(h + 1) * bq]
        if n_comp > 1:
            qT = jnp.where((row >= c * comp_rows) & (row < (c + 1) * comp_rows), qT,
                           jnp.zeros_like(qT))
        q_ops.append(jnp.concatenate([qT, shift_rows], axis=0))
    for i in range(len(chains)):
        acc_sc[i] = jnp.zeros(acc_sc.shape[1:], F32)

    def step(kb, vb, carry):
        n = kb.shape[0]
        ones_col = jnp.where(lax.broadcasted_iota(jnp.int32, (n, LANE), 1) == 0, 1.0, 0.0)
        ka = jnp.concatenate([kb, ones_col.astype(BF16)], axis=1)
        scores = [jnp.dot(ka, q, preferred_element_type=F32) for q in q_ops]
        out = []
        for i, sT in enumerate(scores):
            m, l = carry[i]
            if online:
                m_new = jnp.maximum(m, jnp.max(sT, axis=0, keepdims=True))
                alpha = jnp.exp2(m - m_new)
                pT = jnp.exp2(sT - m_new)
                acc_sc[i] = alpha * acc_sc[i] + jnp.dot(vb, pT.astype(BF16),
                                                        preferred_element_type=F32)
                out.append((m_new, alpha * l + jnp.sum(pT, axis=0, keepdims=True)))
            else:
                pT = jnp.exp2(sT)
                acc_sc[i] += jnp.dot(vb, pT.astype(BF16), preferred_element_type=F32)
                out.append((m, l + jnp.sum(pT, axis=0, keepdims=True)))
        return tuple(out)

    def body(j, carry):
        off = pl.multiple_of(j * bk, bk)
        return step(k_ref[pl.ds(off, bk), :], vT_ref[:, pl.ds(off, bk)], carry)

    init = tuple((jnp.full((1, bq), NEG_INF, F32), jnp.zeros((1, bq), F32)) for _ in chains)
    fin = lax.fori_loop(0, n_kb, body, step(kc_ref[...], vcT_ref[...], init), unroll=2)
    for i, (c, h) in enumerate(chains):
        o_ref[c * dv:(c + 1) * dv, h * bq:(h + 1) * bq] = acc_sc[i] * (1.0 / fin[i][1])


def _pick_bk(n, n_chains, bq):
    for bk in (4096, 2048, 1024, 512, 256, 128):
        if n % bk == 0 and n_chains * bk * bq * 4 <= FLASH_SCORE_BYTES:
            return bk
    raise ValueError(f"key count {n} must be a multiple of {LANE}")


def _flash_call(qT, k, kc, vT, vcT, u8, n_comp, online):
    hw, s = qT.shape
    n_heads = hw // LANE
    dv = vT.shape[0] // n_heads
    l = kc.shape[0]
    bq = min(FLASH_BQ, s)
    n_split = max(1, bq // FLASH_CHAIN_Q)
    bk = _pick_bk(s, n_comp * n_split, bq // n_split)
    kern = functools.partial(_flash_kernel, n_comp=n_comp, n_split=n_split, bk=bk, n_kb=s // bk,
                             online=online)
    return pl.pallas_call(
        kern,
        grid=(n_heads, s // bq),
        in_specs=[pl.BlockSpec((LANE, bq), lambda h, i: (h, i)),
                  pl.BlockSpec((s, LANE), lambda h, i: (0, h)),
                  pl.BlockSpec((l, LANE), lambda h, i: (0, h)),
                  pl.BlockSpec((dv, s), lambda h, i: (h, 0)),
                  pl.BlockSpec((dv, l), lambda h, i: (h, 0)),
                  pl.BlockSpec(u8.shape, lambda h, i: (0, 0))],
        out_specs=pl.BlockSpec((n_comp * dv, bq), lambda h, i: (h, i)),
        out_shape=jax.ShapeDtypeStruct((n_heads * n_comp * dv, s), F32),
        scratch_shapes=[pltpu.VMEM((n_comp * n_split, dv, bq // n_split), F32)],
        compiler_params=_params(("parallel", "arbitrary")),
        name="flash_online" if online else "flash",
    )(qT, k, kc, vT, vcT, u8)


def flash(qT, k, kc, vT, vcT, u, n_comp):
    u8 = jnp.full((8, LANE), u, F32)
    return lax.cond(u <= FLASH_SAFE_SHIFT,
                    lambda *a: _flash_call(*a, n_comp, False),
                    lambda *a: _flash_call(*a, n_comp, True),
                    qT, k, kc, vT, vcT, u8)


def _score_bound(d, q_gain, k_gain, scale):
    return (d * scale * LOG2E * 1.02 * jnp.max(jnp.abs(q_gain.astype(F32)))
            * jnp.max(jnp.abs(k_gain.astype(F32))) + 0.01)


def _ctx_kernel(q_ref, k_ref, v_ref, sink_ref, o_ref, *, use_sink):
    s = lax.dot_general(q_ref[0], k_ref[0], (((1,), (1,)), ((), ())),
                        preferred_element_type=F32)
    m = jnp.max(s, axis=-1, keepdims=True)
    if use_sink:
        sk = sink_ref[0, 0:1, 0:1]
        m = jnp.maximum(m, sk)
    p = jnp.exp2(s - m)
    l = jnp.sum(p, axis=-1, keepdims=True)
    if use_sink:
        l = l + jnp.exp2(sk - m)
    o = jnp.dot(p.astype(BF16), v_ref[0], preferred_element_type=F32)
    o_ref[0] = o * (1.0 / l)


def ctx_attn(q, k, v, sink=None):
    h, l, dk = q.shape
    hk, hv, dv = k.shape[0], v.shape[0], v.shape[2]
    use_sink = sink is not None
    if sink is None:
        sink = jnp.zeros((h,), F32)
    sink3 = jnp.broadcast_to(sink.astype(F32)[:, None, None], (h, 8, LANE))
    kern = functools.partial(_ctx_kernel, use_sink=use_sink)
    return pl.pallas_call(
        kern,
        grid=(h,),
        in_specs=[pl.BlockSpec((1, l, dk), lambda i: (i, 0, 0)),
                  pl.BlockSpec((1, l, dk), lambda i: (i // (h // hk), 0, 0)),
                  pl.BlockSpec((1, l, dv), lambda i: (i // (h // hv), 0, 0)),
                  pl.BlockSpec((1, 8, LANE), lambda i: (i, 0, 0))],
        out_specs=pl.BlockSpec((1, l, dv), lambda i: (i, 0, 0)),
        out_shape=jax.ShapeDtypeStruct((h, l, dv), F32),
        compiler_params=_params(("parallel",)),
        name="ctx_attn",
    )(q, k, v, sink3)


LOCAL_BQ = 512


def _pad_rows(qT_h, slot, n_slots):
    z = jnp.zeros_like(qT_h)
    return jnp.concatenate([qT_h if s == slot else z for s in range(n_slots)], axis=0)


def _softmax_pv(s_list, v_list, extra, exact_max):
    if exact_max:
        m = functools.reduce(jnp.maximum, [jnp.max(s, axis=0, keepdims=True) for s in s_list])
        if extra is not None:
            m = jnp.maximum(m, extra)
            extra = extra - m
        s_list = [s - m for s in s_list]
    p_list = [jnp.exp2(s) for s in s_list]
    l = functools.reduce(jnp.add, [jnp.sum(p, axis=0, keepdims=True) for p in p_list])
    if extra is not None:
        l = l + jnp.exp2(extra)
    acc = functools.reduce(jnp.add, [jnp.dot(v, p.astype(BF16), preferred_element_type=F32)
                                     for v, p in zip(v_list, p_list)])
    return acc, l


def _wa_kernel(qT_ref, k_ref, vT_ref, kc_ref, vcT_ref, sh_ref, o_ref, *, seq, n_heads, n_kv,
               exact_max):
    i = pl.program_id(0)
    bq = qT_ref.shape[1]
    win = bq + 2 * WINDOW
    grp = n_heads // n_kv
    start = pl.multiple_of(jnp.clip(i * bq - WINDOW, 0, seq - win), WINDOW)
    kw = k_ref[pl.ds(start, win), :]
    vwT = vT_ref[:, pl.ds(start, win)]
    kc = kc_ref[...]
    vcT = vcT_ref[...]
    u = sh_ref[n_heads:n_heads + 1, 0:1]
    kpos = start + lax.broadcasted_iota(jnp.int32, (win, bq), 0)
    qpos = i * bq + lax.broadcasted_iota(jnp.int32, (win, bq), 1)
    mask_shift = jnp.where(jnp.abs(qpos - kpos) <= WINDOW, -u, NEG_INF)
    for h in range(n_heads):
        g = h // grp
        qTp = _pad_rows(qT_ref[h * HEAD_DIM:(h + 1) * HEAD_DIM, :], g, n_kv)
        s_loc = jnp.dot(kw, qTp, preferred_element_type=F32) + mask_shift
        s_ctx = jnp.dot(kc, qTp, preferred_element_type=F32) - u
        acc, l = _softmax_pv([s_loc, s_ctx], [vwT, vcT], sh_ref[h:h + 1, 0:1], exact_max)
        o_ref[h * HEAD_DIM:(h + 1) * HEAD_DIM, :] = (acc[g * HEAD_DIM:(g + 1) * HEAD_DIM]
                                                     * (1.0 / l))


def _wa_call(qT, k, vT, kc, vcT, sh, exact_max):
    hd, s = qT.shape
    kw = k.shape[1]
    l = kc.shape[0]
    bq = min(LOCAL_BQ, s)
    assert s % bq == 0 and s >= bq + 2 * WINDOW
    kern = functools.partial(_wa_kernel, seq=s, n_heads=hd // HEAD_DIM, n_kv=kw // HEAD_DIM,
                             exact_max=exact_max)
    full = lambda i: (0, 0)
    return pl.pallas_call(
        kern,
        grid=(s // bq,),
        in_specs=[pl.BlockSpec((hd, bq), lambda i: (0, i)),
                  pl.BlockSpec((s, kw), full), pl.BlockSpec((kw, s), full),
                  pl.BlockSpec((l, kw), full), pl.BlockSpec((kw, l), full),
                  pl.BlockSpec(sh.shape, full)],
        out_specs=pl.BlockSpec((hd, bq), lambda i: (0, i)),
        out_shape=jax.ShapeDtypeStruct((hd, s), F32),
        compiler_params=_params(("parallel",)),
        name="wa_attn_max" if exact_max else "wa_attn",
    )(qT, k, vT, kc, vcT, sh)


def wa_attn(qT, k, vT, kc, vcT, sink, u):
    n_heads = qT.shape[0] // HEAD_DIM
    rows = jnp.concatenate([sink.astype(F32) - u, jnp.reshape(u, (1,)).astype(F32),
                            jnp.zeros((2 * 8 - n_heads - 1,), F32)])
    sh = jnp.broadcast_to(rows[:, None], (rows.shape[0], LANE))
    return lax.cond(u <= FLASH_SAFE_SHIFT,
                    lambda *a: _wa_call(*a, False), lambda *a: _wa_call(*a, True),
                    qT, k, vT, kc, vcT, sh)


NA_KEY_BLOCKS = 4
NA_KEY_ROWS = 2 * NA_KH


def _na_kernel(qT_ref, *refs, exact_max):
    k_refs, v_refs = refs[:NA_KEY_BLOCKS], refs[NA_KEY_BLOCKS:2 * NA_KEY_BLOCKS]
    kc_ref, vcT_ref, bias_ref, sh_ref, o_ref = refs[2 * NA_KEY_BLOCKS:]
    kwin = jnp.concatenate([r[...] for r in k_refs], axis=0)
    vwinT = jnp.concatenate([r[...] for r in v_refs], axis=1)
    kc = kc_ref[...]
    vcT = vcT_ref[...]
    u = sh_ref[0:1, 0:1]
    for hh in range(2):
        rows = slice(hh * HEAD_DIM, (hh + 1) * HEAD_DIM)
        qTp = _pad_rows(qT_ref[rows, :], hh, 2)
        s_loc = jnp.dot(kwin, qTp, preferred_element_type=F32) + bias_ref[0, hh]
        s_ctx = jnp.dot(kc, qTp, preferred_element_type=F32) - u
        acc, l = _softmax_pv([s_loc, s_ctx], [vwinT, vcT], None, exact_max)
        o_ref[rows, :] = acc[rows] * (1.0 / l)


def na_bias_table(rpb, rows, shift):
    n_h = rpb.shape[0]
    a = jnp.arange(NA_KH)
    b = jnp.arange(NA_KEY_ROWS)
    c = jnp.arange(GRID_W)
    kc = jnp.arange(GRID_W)
    c0 = jnp.clip(c - NA_KW // 2, 0, GRID_W - NA_KW)
    col_ok = (kc[None, :] >= c0[:, None]) & (kc[None, :] < c0[:, None] + NA_KW)
    col_rel = jnp.clip(kc[None, :] - c[:, None] + (NA_KW - 1), 0, 2 * NA_KW - 2)
    row_rel = jnp.clip(b[None, :] - a[:, None] + NA_KH // 2 - 1, 0, 2 * NA_KH - 2)

    def edge_ok(r_base):
        r0 = jnp.clip(r_base + a - NA_KH // 2, 0, rows - NA_KH)
        key_row = r_base - NA_KH // 2 + b
        return (key_row[None, :] >= r0[:, None]) & (key_row[None, :] < r0[:, None] + NA_KH)

    inner_ok = (b[None, :] >= a[:, None]) & (b[None, :] < a[:, None] + NA_KH)
    row_ok = jnp.stack([edge_ok(0), inner_ok, edge_ok(rows - NA_KH)])
    e_c = jax.nn.one_hot(col_rel, 2 * NA_KW - 1, dtype=F32)
    e_r = jax.nn.one_hot(row_rel, 2 * NA_KH - 1, dtype=F32)
    hi = lax.Precision.HIGHEST
    t = jnp.einsum('hrx,ckx->hrck', rpb.astype(F32), e_c, precision=hi)
    t = jnp.einsum('abr,hrck->habck', e_r, t, precision=hi) * LOG2E - shift
    ok = row_ok[:, None, :, :, None, None] & col_ok[None, None, None, None]
    t = jnp.where(ok, t[None], NEG_INF)
    t = jnp.transpose(t, (0, 1, 3, 5, 2, 4))
    return t.reshape(3, n_h, NA_KEY_ROWS * GRID_W, NA_KH * GRID_W)


def _na_call(qT, k, vT, kc, vcT, bias, sh, exact_max):
    hd, s = qT.shape
    l = kc.shape[0]
    bq = NA_KH * GRID_W
    nb = s // bq
    pw = 2 * HEAD_DIM
    assert s % bq == 0 and bq == LOCAL_BQ
    var = lambda i: jnp.where(i == 0, 0, jnp.where(i == nb - 1, 2, 1))
    kb = bq // 2
    blk = lambda i, t: jnp.clip(2 * i - 1 + t, 0, 2 * nb - 1)
    kspecs = [pl.BlockSpec((kb, pw), lambda p, i, t=t: (blk(i, t), p))
              for t in range(NA_KEY_BLOCKS)]
    vspecs = [pl.BlockSpec((pw, kb), lambda p, i, t=t: (p, blk(i, t)))
              for t in range(NA_KEY_BLOCKS)]
    kern = functools.partial(_na_kernel, exact_max=exact_max)
    return pl.pallas_call(
        kern,
        grid=(hd // pw, nb),
        in_specs=[pl.BlockSpec((pw, bq), lambda p, i: (p, i)), *kspecs, *vspecs,
                  pl.BlockSpec((l, pw), lambda p, i: (0, p)),
                  pl.BlockSpec((pw, l), lambda p, i: (p, 0)),
                  pl.BlockSpec((1, 2, NA_KEY_ROWS * GRID_W, bq), lambda p, i: (var(i), p, 0, 0)),
                  pl.BlockSpec(sh.shape, lambda p, i: (0, 0))],
        out_specs=pl.BlockSpec((pw, bq), lambda p, i: (p, i)),
        out_shape=jax.ShapeDtypeStruct((hd, s), F32),
        compiler_params=_params(("parallel", "arbitrary")),
        name="na_attn_max" if exact_max else "na_attn",
    )(qT, *([k] * NA_KEY_BLOCKS), *([vT] * NA_KEY_BLOCKS), kc, vcT, bias, sh)


def na_attn(qT, k, vT, kc, vcT, rpb, u_qk):
    s = qT.shape[1]
    u = u_qk + jnp.maximum(jnp.max(rpb.astype(F32)) * LOG2E, 0.0)
    bias = na_bias_table(rpb, s // GRID_W, u)
    sh = jnp.full((8, LANE), u, F32)
    return lax.cond(u <= FLASH_SAFE_SHIFT,
                    lambda *a: _na_call(*a, False), lambda *a: _na_call(*a, True),
                    qT, k, vT, kc, vcT, bias, sh)


def _outproj_kernel(o_ref, x_ref, w_ref, gate_ref, g_ref, sc_ref, sh_ref, wr_ref,
                    xo_ref, hf_ref, lg_ref):
    acc = jnp.dot(o_ref[...], w_ref[...], preferred_element_type=F32)
    xn = x_ref[...] + gate_ref[...] * acc
    xo_ref[...] = xn
    ms = jnp.mean(xn * xn, axis=-1, keepdims=True)
    y = xn * lax.rsqrt(ms + NORM_EPS) * g_ref[...]
    hf = (y * sc_ref[...] + sh_ref[...]).astype(BF16)
    hf_ref[...] = hf
    lg_ref[...] = jnp.dot(hf, wr_ref[...], preferred_element_type=F32)


def _outproj_t_kernel(na_ref, wa_ref, df_ref, ml_ref, dfg_ref, lam_ref, x_ref, w_ref, gate_ref,
                      g_ref, sc_ref, sh_ref, wr_ref, xo_ref, hf_ref, lg_ref):
    lam = lam_ref[0:1, 0:1]
    parts = [na_ref[...], wa_ref[...]]
    for h in range(df_ref.shape[0]):
        dd = df_ref[h, 0] - lam * df_ref[h, 1]
        ms = jnp.mean(dd * dd, axis=0, keepdims=True)
        parts.append(dd * lax.rsqrt(ms + NORM_EPS) * dfg_ref[...])
    parts.append(ml_ref[...])
    o = jnp.concatenate([jnp.transpose(t).astype(BF16) for t in parts], axis=1)
    acc = jnp.dot(o, w_ref[...], preferred_element_type=F32)
    xn = x_ref[...] + gate_ref[...] * acc
    xo_ref[...] = xn
    ms = jnp.mean(xn * xn, axis=-1, keepdims=True)
    y = xn * lax.rsqrt(ms + NORM_EPS) * g_ref[...]
    hf = (y * sc_ref[...] + sh_ref[...]).astype(BF16)
    hf_ref[...] = hf
    lg_ref[...] = jnp.dot(hf, wr_ref[...], preferred_element_type=F32)


def outproj_t(oT_na, oT_wa, oT_df, oT_ml, df_gain, lam, x, w, gate, g, sc1, sh, wr):
    r, d = x.shape
    tm = min(r, 256)
    nh, _, dv2, _ = oT_df.shape
    wdt = oT_na.shape[0]
    dfg = jnp.broadcast_to(df_gain.astype(F32)[:, None], (dv2, tm))
    lam8 = jnp.full((8, LANE), lam, F32)
    vec = pl.BlockSpec((1, d), lambda i: (0, 0))
    row = lambda n: pl.BlockSpec((tm, n), lambda i: (i, 0))
    colblk = pl.BlockSpec((wdt, tm), lambda i: (0, i))
    const = lambda shape: pl.BlockSpec(shape, lambda i: tuple(0 for _ in shape))
    return pl.pallas_call(
        _outproj_t_kernel,
        grid=(r // tm,),
        in_specs=[colblk, colblk, pl.BlockSpec((nh, 2, dv2, tm), lambda i: (0, 0, 0, i)), colblk,
                  const((dv2, tm)), const((8, LANE)), row(d), const(w.shape), vec, vec, vec, vec,
                  const((d, LANE))],
        out_specs=[row(d), row(d), row(LANE)],
        out_shape=[jax.ShapeDtypeStruct((r, d), F32), jax.ShapeDtypeStruct((r, d), BF16),
                   jax.ShapeDtypeStruct((r, LANE), F32)],
        compiler_params=_params(("parallel",)),
        name="outproj_t",
    )(oT_na, oT_wa, oT_df, oT_ml, dfg, lam8, x, w, gate, g, sc1, sh, wr)


def outproj(o, x, w, gate, g, sc1, sh, wr):
    r, d = x.shape
    k = o.shape[1]
    tm = min(r, 256)
    vec = pl.BlockSpec((1, d), lambda i: (0, 0))
    row = lambda n: pl.BlockSpec((tm, n), lambda i: (i, 0))
    return pl.pallas_call(
        _outproj_kernel,
        grid=(r // tm,),
        in_specs=[row(k), row(d), pl.BlockSpec((k, d), lambda i: (0, 0)), vec, vec, vec, vec,
                  pl.BlockSpec((d, LANE), lambda i: (0, 0))],
        out_specs=[row(d), row(d), row(LANE)],
        out_shape=[jax.ShapeDtypeStruct((r, d), F32), jax.ShapeDtypeStruct((r, d), BF16),
                   jax.ShapeDtypeStruct((r, LANE), F32)],
        compiler_params=_params(("parallel",)),
        name="outproj",
    )(o, x, w, gate, g, sc1, sh, wr)


def _moe_kernel(be_ref, nv_ref, x_ref, w1_ref, w3_ref, w2_ref, o_ref):
    i = pl.program_id(0)

    @pl.when(i < nv_ref[0])
    def _():
        x = x_ref[...].astype(w1_ref.dtype)
        a = jnp.dot(x, w1_ref[0], preferred_element_type=F32)
        b = jnp.dot(x, w3_ref[0], preferred_element_type=F32)
        hmid = (a * _sigmoid(a) * b).astype(w2_ref.dtype)
        o_ref[...] = jnp.dot(hmid, w2_ref[0], preferred_element_type=F32).astype(o_ref.dtype)

    @pl.when(i >= nv_ref[0])
    def _():
        o_ref[...] = jnp.zeros(o_ref.shape, o_ref.dtype)


def moe_blocks(x, w1, w3, w2, blk_e, n_valid):
    n, d = x.shape
    f = w1.shape[2]
    n_blk = n // MOE_BLOCK
    grid_spec = pltpu.PrefetchScalarGridSpec(
        num_scalar_prefetch=2,
        grid=(n_blk,),
        in_specs=[pl.BlockSpec((MOE_BLOCK, d), lambda i, be, nv: (i, 0)),
                  pl.BlockSpec((1, d, f), lambda i, be, nv: (be[i], 0, 0)),
                  pl.BlockSpec((1, d, f), lambda i, be, nv: (be[i], 0, 0)),
                  pl.BlockSpec((1, f, d), lambda i, be, nv: (be[i], 0, 0),
                               pipeline_mode=pl.Buffered(1))],
        out_specs=pl.BlockSpec((MOE_BLOCK, d), lambda i, be, nv: (i, 0)),
    )
    return pl.pallas_call(
        _moe_kernel,
        grid_spec=grid_spec,
        out_shape=jax.ShapeDtypeStruct((n, d), BF16),
        compiler_params=pltpu.CompilerParams(dimension_semantics=("arbitrary",),
                                             vmem_limit_bytes=MOE_VMEM_LIMIT),
        name="moe_blocks",
    )(blk_e, n_valid, x, w1, w3, w2)


def _rms(x, g):
    y = x * lax.rsqrt(jnp.mean(x * x, axis=-1, keepdims=True) + NORM_EPS)
    return y * g


def _rope_tables(seq_len, rot_dim):
    n = rot_dim // 4
    t = np.arange(seq_len)
    row = (t // GRID_W).astype(np.float32)[:, None]
    col = (t % GRID_W).astype(np.float32)[:, None]
    inv = (np.float32(ROPE_THETA) ** (-np.arange(n, dtype=np.float32) / np.float32(n)))
    inv = inv.astype(np.float32)
    return tuple(f(a * inv).astype(np.float32) for a in (row, col) for f in (np.cos, np.sin))


def _heads_first(t):
    return jnp.transpose(t, (1, 0, 2))


def _route(logits, b_router):
    n = logits.shape[0]
    per = N_EXPERTS // N_EXPERT_GROUPS
    scores = jax.nn.sigmoid(logits.astype(F32))
    sel = scores + b_router.astype(F32)
    grp = sel.reshape(n, N_EXPERT_GROUPS, per)
    gscore = None
    for a in range(per):
        for bb in range(a + 1, per):
            pair = grp[..., a] + grp[..., bb]
            gscore = pair if gscore is None else jnp.maximum(gscore, pair)
    gidx = jnp.argmax(gscore, axis=-1)
    eids = jnp.arange(N_EXPERTS)
    masked = jnp.where((eids // per)[None, :] == gidx[:, None], sel, -jnp.inf)
    e1 = jnp.argmax(masked, axis=-1)
    e2 = jnp.argmax(jnp.where(eids[None, :] == e1[:, None], -jnp.inf, masked), axis=-1)
    eidx = jnp.stack([e1, e2], axis=1).astype(jnp.int32)
    wts = jnp.take_along_axis(scores, eidx, axis=1)
    wts = wts / jnp.sum(wts, axis=-1, keepdims=True) * ROUTED_SCALE
    return eidx, wts


def _prefix_counts(onehot):
    n, e = onehot.shape
    ch = onehot.astype(F32).reshape(n // LANE, LANE, e)
    tri = jnp.tril(jnp.ones((LANE, LANE), F32))
    within = jnp.einsum('ij,cjk->cik', tri, ch)
    tot = within[:, -1, :]
    base = jnp.cumsum(tot, axis=0) - tot
    return (within + base[:, None, :]).reshape(n, e)


def _dispatch(eidx):
    n = eidx.shape[0]
    n_assign = n * TOP_K
    assert n_assign % LANE == 0
    e_flat = eidx.reshape(-1)
    tok = jnp.repeat(jnp.arange(n, dtype=jnp.int32), TOP_K)
    onehot = e_flat[:, None] == jnp.arange(N_EXPERTS)[None, :]
    csum = _prefix_counts(onehot)
    counts = csum[-1].astype(jnp.int32)
    rank = jnp.sum(jnp.where(onehot, csum, 0.0), axis=1).astype(jnp.int32) - 1
    padded = (counts + MOE_BLOCK - 1) // MOE_BLOCK * MOE_BLOCK
    pend = jnp.cumsum(padded)
    pstart = pend - padded
    dest = jnp.sum(jnp.where(onehot, pstart[None, :], 0), axis=1) + rank
    n_slots = (n_assign + N_EXPERTS * (MOE_BLOCK - 1) + MOE_BLOCK - 1) // MOE_BLOCK * MOE_BLOCK
    n_blk = n_slots // MOE_BLOCK
    slot_tok = jnp.full((n_slots,), n, dtype=jnp.int32).at[dest].set(tok)
    slot_of = dest.reshape(n, TOP_K)
    blk_start = jnp.arange(n_blk, dtype=jnp.int32) * MOE_BLOCK
    blk_e = jnp.minimum(jnp.sum(blk_start[:, None] >= pend[None, :], axis=1), N_EXPERTS - 1)
    n_valid = (pend[-1] // MOE_BLOCK).reshape(1)
    return slot_tok, slot_of, blk_e.astype(jnp.int32), n_valid.astype(jnp.int32)


PREP_TM = 4096


def _slab_prep_kernel(*refs, seg, count, rot, transposed, has_add, raw):
    refs = list(refs)
    o_ref = refs.pop()
    x = refs[0][...].astype(F32)
    if not raw:
        g_ref = refs[1]
        nxt = 2
        if has_add:
            x = x + refs[nxt][...]
            nxt += 1
        r = lax.broadcasted_iota(jnp.int32, (LANE, LANE), 0) // seg
        c = lax.broadcasted_iota(jnp.int32, (LANE, LANE), 1) // seg
        blk = jnp.where(r == c, 1.0, 0.0)
        ssum = jnp.dot(x * x, blk, preferred_element_type=F32)
        x = x * lax.rsqrt(ssum * (1.0 / count) + NORM_EPS) * g_ref[...]
        if rot:
            cos, s_next, s_prev = refs[nxt][...], refs[nxt + 1][...], refs[nxt + 2][...]
            x = (x * cos + pltpu.roll(x, LANE - rot, 1) * s_next + pltpu.roll(x, rot, 1) * s_prev)
    o_ref[...] = (jnp.transpose(x) if transposed else x).astype(o_ref.dtype)


def slab_prep(x, col0, n_slabs, gain=None, rope=None, rot=0, add=None, seg=HEAD_DIM,
              count=HEAD_DIM, scale=1.0, transposed=False):
    r = x.shape[0]
    tm = min(PREP_TM, r)
    raw = gain is None
    ops = [x]
    specs = [pl.BlockSpec((tm, LANE), lambda i, j: (i, col0 + j))]
    rowspec = pl.BlockSpec((tm, LANE), lambda i, j: (i, 0))
    if not raw:
        ops.append((gain.astype(F32) * scale).reshape(1, LANE))
        specs.append(pl.BlockSpec((1, LANE), lambda i, j: (0, 0)))
        if add is not None:
            ops.append(add)
            specs.append(rowspec)
        if rope is not None:
            ops += list(rope)
            specs += [rowspec] * 3
    kern = functools.partial(_slab_prep_kernel, seg=seg, count=count,
                             rot=rot if rope is not None else 0,
                             transposed=transposed, has_add=add is not None, raw=raw)
    if transposed:
        out_spec = pl.BlockSpec((LANE, tm), lambda i, j: (j, i))
        out_shape = jax.ShapeDtypeStruct((n_slabs * LANE, r), BF16)
    else:
        out_spec = pl.BlockSpec((tm, LANE), lambda i, j: (i, j))
        out_shape = jax.ShapeDtypeStruct((r, n_slabs * LANE), BF16)
    return pl.pallas_call(
        kern,
        grid=(r // tm, n_slabs),
        in_specs=specs,
        out_specs=out_spec,
        out_shape=out_shape,
        compiler_params=_params(("parallel", "arbitrary")),
        name="slab_prep",
    )(*ops)


def _rope_slab_tables(seq_len, width, offset, reps):
    n = width // 4
    cr, sr, cc, sc = (jnp.asarray(t) for t in _rope_tables(seq_len, width))
    one = jnp.ones((seq_len, offset), F32)
    zero = jnp.zeros((seq_len, offset), F32)
    zq = jnp.zeros((seq_len, n), F32)
    rest = LANE // reps - offset - width
    pad1 = jnp.ones((seq_len, rest), F32)
    pad0 = jnp.zeros((seq_len, rest), F32)
    cos = jnp.concatenate([one, cr, cr, cc, cc, pad1] * reps, axis=1)
    s_next = jnp.concatenate([zero, -sr, zq, -sc, zq, pad0] * reps, axis=1)
    s_prev = jnp.concatenate([zero, zq, sr, zq, sc, pad0] * reps, axis=1)
    return cos, s_next, s_prev


def _gain2(g):
    return jnp.concatenate([g, g]).astype(F32)


def _ctx_heads(t, n_heads):
    return _heads_first(t.reshape(t.shape[0], n_heads, -1)).astype(BF16)


def _qk_prep(p, pc, off, w, q_gain, k_gain, rope, wk=None):
    wk = w if wk is None else wk
    qs = HEAD_DIM ** -0.5 * LOG2E
    gq, gk = _gain2(q_gain), _gain2(k_gain)
    c0 = off // LANE
    rot = HEAD_DIM // 4
    qT = slab_prep(p, c0, w // LANE, gq, rope, rot, scale=qs, transposed=True)
    k = slab_prep(p, c0 + w // LANE, wk // LANE, gk, rope, rot)
    qc = slab_prep(pc, c0, w // LANE, gq, scale=qs)
    kc = slab_prep(pc, c0 + w // LANE, wk // LANE, gk)
    return qT, k, qc, kc


def _mla_prep(p, pc, qa_gain, kva_gain, w_uq, w_ukv, q_gain, k_gain, rope):
    hh, dn, dr, dv = MLA_HEADS, MLA_NOPE, MLA_ROPE, MLA_V
    dq = dn + dr
    qs = dq ** -0.5 * LOG2E
    padh = LANE - dq
    w_q = jnp.pad(w_uq.reshape(MLA_Q_LORA, hh, dq), ((0, 0), (0, 0), (0, padh)))
    w_kv = w_ukv.reshape(MLA_KV_LORA, hh, dn + dv)
    w_k = jnp.pad(w_kv[..., :dn], ((0, 0), (0, 0), (0, LANE - dn)))
    w_all = jnp.concatenate([w_k.reshape(MLA_KV_LORA, hh * LANE),
                             w_kv[..., dn:].reshape(MLA_KV_LORA, hh * dv)], axis=1).astype(BF16)
    w_q = w_q.reshape(MLA_Q_LORA, hh * LANE).astype(BF16)
    gq = jnp.pad(q_gain.astype(F32), (0, padh))
    gk = jnp.pad(k_gain.astype(F32), (0, padh))
    off = MLA_OFF

    def project(t, tabs):
        lora = t[:, off:off + MLA_Q_LORA + MLA_KV_LORA + dr].astype(F32)
        cq = _rms(lora[:, :MLA_Q_LORA], qa_gain).astype(BF16)
        ckv = _rms(lora[:, MLA_Q_LORA:MLA_Q_LORA + MLA_KV_LORA], kva_gain).astype(BF16)
        k_rope = jnp.pad(lora[:, MLA_Q_LORA + MLA_KV_LORA:], ((0, 0), (dn, padh)))
        kv = matmul(ckv, w_all)
        q = slab_prep(matmul(cq, w_q), 0, hh, gq, tabs, dr // 4, seg=LANE, count=dq, scale=qs,
                      transposed=tabs is not None)
        k = slab_prep(kv, 0, hh, gk, tabs, dr // 4, add=k_rope, seg=LANE, count=dq)
        return q, k, kv

    qT, k, kv = project(p, rope)
    qc, kc, kvc = project(pc, None)
    vT = slab_prep(kv, hh, hh * dv // LANE, transposed=True)
    vc = kvc[:, hh * LANE:].astype(BF16)
    return qT, k, kc, vT, vc.T, qc, vc


def _merge(o):
    return jnp.transpose(o, (1, 0, 2)).reshape(o.shape[1], -1)


def _diff_post(o1, o2, lam, sub_gain, lambda_init):
    return _rms(o1 - lam * o2, sub_gain) * (1.0 - lambda_init)


def _moe(tokens_bf16, logits, b_router, layer, w1, w3, w2, sw1, sw3, sw2):
    n, d = tokens_bf16.shape
    assert n % MOE_BLOCK == 0
    eidx, gate = _route(logits[:, :N_EXPERTS], b_router)
    slot_tok, slot_of, blk_e, n_valid = _dispatch(eidx)
    xb = tokens_bf16[jnp.minimum(slot_tok, n - 1)]
    n_sh = n // MOE_BLOCK
    shared = moe_blocks(tokens_bf16, sw1, sw3, sw2, jnp.full((n_sh,), layer, jnp.int32),
                        jnp.full((1,), n_sh, jnp.int32))
    xb, shared = lax.optimization_barrier((xb, shared))
    yb = moe_blocks(xb, w1, w3, w2, blk_e + layer * N_EXPERTS, n_valid)
    y0, y1 = lax.optimization_barrier((yb[slot_of[:, 0]], yb[slot_of[:, 1]]))
    routed = y0.astype(F32) * gate[:, 0:1] + y1.astype(F32) * gate[:, 1:2]
    return routed + shared.astype(F32)


def kernel(x, c, ctx, c_ctx, w_ada, b_ada, g_attn, g_ffn, w_in, w_out, na_q_gain, na_k_gain, na_rpb, wa_q_gain, wa_k_gain, wa_sink, diff_q_gain, diff_k_gain, diff_lq1, diff_lk1, diff_lq2, diff_lk2, diff_sub_gain, mla_qa_gain, mla_kva_gain, mla_w_uq, mla_w_ukv, mla_q_gain, mla_k_gain, w_router, b_router, moe_w1, moe_w3, moe_w2, sh_w1, sh_w3, sh_w2):
    b, s, d = x.shape
    assert b == 1
    n_ctx = ctx.shape[1]
    depth = w_ada.shape[0]
    xl = x[0]
    xc = ctx[0]
    rope_head = _rope_slab_tables(s, HEAD_DIM, 0, 2)
    rope_mla = _rope_slab_tables(s, MLA_ROPE, MLA_NOPE, 1)

    cond8 = jnp.zeros((8, d), F32).at[0].set(c[0]).at[1].set(c_ctx)
    mod = modvec(cond8, w_ada, b_ada)
    tn = INPROJ_TN
    in_pad = _round_up(IN_COLS, tn)
    wr = jnp.pad(w_router, ((0, 0), (0, LANE - N_EXPERTS))).astype(BF16)
    inv_sqrt_d = HEAD_DIM ** -0.5
    moe_w = [t.reshape((depth * N_EXPERTS,) + t.shape[2:]) for t in (moe_w1, moe_w3, moe_w2)]
    sh_w = [sh_w1, sh_w3, sh_w2]

    for l in range(depth):
        with_ctx = l < depth - 1
        m_lat = mod[l, 0].reshape(6, 1, d)
        m_ctx = mod[l, 1].reshape(6, 1, d)
        w_in_l = jnp.pad(w_in[l], ((0, 0), (0, in_pad - IN_COLS))).astype(BF16)
        w_out_l = w_out[l].astype(BF16)
        g_a = g_attn[l][None]
        g_f = g_ffn[l][None]

        p = inproj(xl, g_a, 1.0 + m_lat[1], m_lat[0], w_in_l, tn)
        pc = inproj(xc, g_a, 1.0 + m_ctx[1], m_ctx[0], w_in_l, tn)

        w = NA_HEADS * HEAD_DIM
        qT, k, qc, kc = _qk_prep(p, pc, NA_OFF, w, na_q_gain[l], na_k_gain[l], None)
        vT = slab_prep(p, (NA_OFF + 2 * w) // LANE, w // LANE, transposed=True)
        vc = pc[:, NA_OFF + 2 * w:WA_OFF]
        u_na = _score_bound(HEAD_DIM, na_q_gain[l], na_k_gain[l], inv_sqrt_d)
        oT_na = na_attn(qT, k, vT, kc, vc.T, na_rpb[l], u_na)
        oc_na = (_merge(ctx_attn(_ctx_heads(qc, NA_HEADS), _ctx_heads(kc, NA_HEADS),
                                 _ctx_heads(vc, NA_HEADS))) if with_ctx else None)

        w, wk = WA_HEADS * HEAD_DIM, WA_KV_HEADS * HEAD_DIM
        qT, k, qc, kc = _qk_prep(p, pc, WA_OFF, w, wa_q_gain[l], wa_k_gain[l], rope_head, wk)
        vT = slab_prep(p, (WA_OFF + w + wk) // LANE, wk // LANE, transposed=True)
        vc = pc[:, WA_OFF + w + wk:DIFF_OFF]
        u_wa = _score_bound(HEAD_DIM, wa_q_gain[l], wa_k_gain[l], inv_sqrt_d)
        sink2 = wa_sink[l].astype(F32) * LOG2E
        oT_wa = wa_attn(qT, k, vT, kc, vc.T, sink2, u_wa)
        oc_wa = (_merge(ctx_attn(_ctx_heads(qc, WA_HEADS), _ctx_heads(kc, WA_KV_HEADS),
                                 _ctx_heads(vc, WA_KV_HEADS), sink2)) if with_ctx else None)

        lambda_init = 0.8 - 0.6 * math.exp(-0.3 * l)
        lam = (jnp.exp(jnp.sum(diff_lq1[l].astype(F32) * diff_lk1[l].astype(F32)))
               - jnp.exp(jnp.sum(diff_lq2[l].astype(F32) * diff_lk2[l].astype(F32))) + lambda_init)
        w = DIFF_HEADS * 2 * DIFF_DIM
        qT, k, qc, kc = _qk_prep(p, pc, DIFF_OFF, w, diff_q_gain[l], diff_k_gain[l], rope_head)
        vT = slab_prep(p, (DIFF_OFF + 2 * w) // LANE, w // LANE, transposed=True)
        vc = pc[:, DIFF_OFF + 2 * w:MLA_OFF]
        u_df = _score_bound(DIFF_DIM, diff_q_gain[l], diff_k_gain[l], DIFF_DIM ** -0.5)
        oT_df = flash(qT, k, kc, vT, vc.T, u_df, 2).reshape(DIFF_HEADS, 2, DIFF_V_DIM, s)
        oc_df = None
        if with_ctx:
            oc = ctx_attn(_ctx_heads(qc, 2 * DIFF_HEADS), _ctx_heads(kc, 2 * DIFF_HEADS),
                          _ctx_heads(vc, DIFF_HEADS)).reshape(DIFF_HEADS, 2, n_ctx, DIFF_V_DIM)
            oc_df = _diff_post(jnp.transpose(oc[:, 0], (1, 0, 2)), jnp.transpose(oc[:, 1], (1, 0, 2)),
                               lam, diff_sub_gain[l], lambda_init).reshape(n_ctx, -1)

        qT, k, kc, vT, vcT, qc, vc = _mla_prep(
            p, pc, mla_qa_gain[l], mla_kva_gain[l], mla_w_uq[l], mla_w_ukv[l], mla_q_gain[l],
            mla_k_gain[l], rope_mla)
        u_ml = _score_bound(MLA_NOPE + MLA_ROPE, mla_q_gain[l], mla_k_gain[l],
                            (MLA_NOPE + MLA_ROPE) ** -0.5)
        oT_ml = flash(qT, k, kc, vT, vcT, u_ml, 1)
        oc_ml = (_merge(ctx_attn(_ctx_heads(qc, MLA_HEADS), _ctx_heads(kc, MLA_HEADS),
                                 _ctx_heads(vc, MLA_HEADS))) if with_ctx else None)

        xl, hf, lg = outproj_t(oT_na, oT_wa, oT_df, oT_ml,
                               diff_sub_gain[l].astype(F32) * (1.0 - lambda_init), lam,
                               xl, w_out_l, m_lat[2], g_f, 1.0 + m_lat[4], m_lat[3], wr)
        if with_ctx:
            oc_cat = jnp.concatenate([oc_na, oc_wa, oc_df, oc_ml], axis=-1).astype(BF16)
            xc, hfc, lgc = outproj(oc_cat, xc, w_out_l, m_ctx[2], g_f, 1.0 + m_ctx[4], m_ctx[3], wr)
            tokens = jnp.concatenate([hfc, hf], axis=0)
            logits = jnp.concatenate([lgc, lg], axis=0)
        else:
            tokens, logits = hf, lg
        y = _moe(tokens, logits, b_router, l, *moe_w, *sh_w)
        if with_ctx:
            xc = xc + m_ctx[5] * y[:n_ctx]
            y = y[n_ctx:]
        xl = xl + m_lat[5] * y
    return xl[None]
```
